```python
import math
import jax
import jax.numpy as jnp
from jax import lax
import numpy as np

D_MODEL = 1024
BATCH = 4
SEQ = 8192
DEPTH = 1

HEAD_DIM = 64
Q_BLOCK = 128
DIL_GROUPS = ((128, 1), (512, 4), (2048, 16))
A_HEADS_PER_GROUP = 4
A_HEADS = A_HEADS_PER_GROUP * len(DIL_GROUPS)
A_WIDTH = A_HEADS * HEAD_DIM
A_OUT = A_HEADS_PER_GROUP * HEAD_DIM
B_HEADS = 8
B_KV_HEADS = 2
B_GQA = B_HEADS // B_KV_HEADS
B_WIDTH = B_HEADS * HEAD_DIM
B_KV_WIDTH = B_KV_HEADS * HEAD_DIM
CMP_LEN = 32
CMP_STRIDE = 16
CMP_HIDDEN = 256
SLC_LEN = 64
SLC_TOPK = 16
WIN = 512
N_GROUPS = 4
EXP_PER_GROUP = 8
N_EXPERTS = N_GROUPS * EXP_PER_GROUP
D_EXPERT = 512
MOE_TOP_K = 2
MOE_BLOCK = 128
IN_SIZES = (3 * A_WIDTH, B_WIDTH, 6 * B_KV_WIDTH, 3 * B_HEADS, 2 * D_MODEL)
IN_COLS = sum(IN_SIZES)
ALPHA = (2.0 * DEPTH) ** 0.25
BETA = (8.0 * DEPTH) ** -0.25
LN_EPS = 1e-5
FORCE_BONUS = 1e4
TINY = 1e-30
ATTN_SCALE = HEAD_DIM ** -0.5

kernel_name = 'hybrid_dilated_nsa_hmoe_deepnorm'


def layer_norm(x, gain, bias):
    xf = x.astype(jnp.float32)
    mu = jnp.mean(xf, axis=-1, keepdims=True)
    var = jnp.mean(jnp.square(xf - mu), axis=-1, keepdims=True)
    y = (xf - mu) * lax.rsqrt(var + LN_EPS) * gain + bias
    return y.astype(x.dtype)


def alibi_slopes(n):
    return jnp.exp2(-8.0 * jnp.arange(1, n + 1, dtype=jnp.float32) / n)


def masked_softmax(s, mask):
    s = jnp.where(mask, s.astype(jnp.float32), -jnp.inf)
    m = jnp.max(s, axis=-1, keepdims=True)
    m = jnp.where(jnp.isfinite(m), m, 0.0)
    p = jnp.exp(s - m)
    l = jnp.sum(p, axis=-1, keepdims=True)
    return p / jnp.maximum(l, TINY), m + jnp.log(l)


def dilated_group(q, k, v, window, dil, slopes):
    B_, S_, H, E = q.shape
    L = S_ // dil
    nb = -(-L // Q_BLOCK)
    Lp = nb * Q_BLOCK
    n_back = window // dil

    def to_sub(t):
        t = t.reshape(B_, L, dil, H, E).transpose(0, 2, 3, 1, 4)
        t = jnp.pad(t, ((0, 0), (0, 0), (0, 0), (0, Lp - L), (0, 0)))
        return t.reshape(B_, dil, H, nb, Q_BLOCK, E)

    def with_prev(t):
        prev = jnp.pad(t, ((0, 0), (0, 0), (0, 0), (1, 0), (0, 0), (0, 0)))[:, :, :, :-1]
        return jnp.concatenate([prev, t], axis=4)

    qs = to_sub(q)
    kk = with_prev(to_sub(k))
    vv = with_prev(to_sub(v))
    qi = Q_BLOCK + jnp.arange(Q_BLOCK)
    kj = jnp.arange(2 * Q_BLOCK)
    delta = qi[:, None] - kj[None, :]
    key_idx = jnp.arange(nb)[:, None] * Q_BLOCK - Q_BLOCK + kj[None, :]
    mask = ((delta >= 0) & (delta <= n_back))[None] & (key_idx >= 0)[:, None, :]
    s = jnp.einsum('bdhnqe,bdhnke->bdhnqk', qs, kk).astype(jnp.float32) * ATTN_SCALE
    s = s - slopes[:, None, None, None] * (delta * dil).astype(jnp.float32)
    probs, lse = masked_softmax(s, mask)
    o = jnp.einsum('bdhnqk,bdhnke->bdhnqe', probs, vv)
    o = o.reshape(B_, dil, H, Lp, E)[:, :, :, :L].transpose(0, 3, 1, 2, 4).reshape(B_, S_, H, E)
    lse = lse[..., 0].reshape(B_, dil, H, Lp)[:, :, :, :L].transpose(0, 3, 1, 2).reshape(B_, S_, H)
    return o, lse


def dilated_mixer(a_qkv):
    B_, S_, _ = a_qkv.shape
    n_g = len(DIL_GROUPS)
    qkv = a_qkv.reshape(B_, S_, 3, n_g, A_HEADS_PER_GROUP, HEAD_DIM)
    slopes = alibi_slopes(A_HEADS).reshape(n_g, A_HEADS_PER_GROUP)
    outs, lses = [], []
    for g, (window, dil) in enumerate(DIL_GROUPS):
        o, lse = dilated_group(qkv[:, :, 0, g], qkv[:, :, 1, g], qkv[:, :, 2, g], window, dil, slopes[g])
        outs.append(o)
        lses.append(lse)
    w = jax.nn.softmax(jnp.stack(lses), axis=0)
    o = jnp.sum(w[..., None] * jnp.stack(outs), axis=0)
    return o.reshape(B_, S_, A_OUT)


def compress_tokens(kv, pos_emb, w1, w2):
    B_, S_, G, E = kv.shape
    n_chunk = S_ // CMP_STRIDE
    ratio = CMP_LEN // CMP_STRIDE
    n_cmp = n_chunk - ratio + 1
    chunks = kv.reshape(B_, n_chunk, CMP_STRIDE, G, E)
    blocks = jnp.concatenate([chunks[:, r:r + n_cmp] for r in range(ratio)], axis=2)
    blocks = blocks + pos_emb[:, None, :]
    flat = blocks.transpose(0, 1, 3, 2, 4).reshape(B_, n_cmp, G, CMP_LEN * E)
    return jax.nn.gelu(flat @ w1) @ w2


def nsa_mixer(b_q, b_kv, b_gate, cmp_pos_k, cmp_w1_k, cmp_w2_k, cmp_pos_v, cmp_w1_v, cmp_w2_v):
    B_, S_, _ = b_q.shape
    G, R, E = B_KV_HEADS, B_GQA, HEAD_DIM
    q = b_q.reshape(B_, S_, G, R, E)
    kv = b_kv.reshape(B_, S_, 6, G, E)
    k_cmp = compress_tokens(kv[:, :, 0], cmp_pos_k, cmp_w1_k, cmp_w2_k)
    v_cmp = compress_tokens(kv[:, :, 1], cmp_pos_v, cmp_w1_v, cmp_w2_v)
    n_cmp = k_cmp.shape[1]
    n_slc = S_ // SLC_LEN
    n_sel = min(SLC_TOPK, n_slc)
    k_blk = kv[:, :, 2].reshape(B_, n_slc, SLC_LEN, G, E).transpose(0, 3, 1, 2, 4)
    v_blk = kv[:, :, 3].reshape(B_, n_slc, SLC_LEN, G, E).transpose(0, 3, 1, 2, 4)
    pad = ((0, 0), (WIN, 0), (0, 0), (0, 0))
    k_win = jnp.pad(kv[:, :, 4], pad)
    v_win = jnp.pad(kv[:, :, 5], pad)
    gates = jax.nn.sigmoid(b_gate).reshape(B_, S_, G, R, 3)
    slopes = alibi_slopes(B_HEADS).reshape(G, R)
    c_start = jnp.arange(n_cmp) * CMP_STRIDE
    cmp_end = c_start + (CMP_LEN - 1)
    s_start = jnp.arange(n_slc) * SLC_LEN
    overlap = ((c_start[:, None] < s_start[None, :] + SLC_LEN) &
               (c_start[:, None] + CMP_LEN > s_start[None, :])).astype(jnp.float32)
    blk_ids = jnp.arange(n_slc)
    bi = jnp.arange(B_)[:, None, None, None]
    gi = jnp.arange(G)[None, :, None, None]

    def one_block(n):
        t0 = n * Q_BLOCK
        t = t0 + jnp.arange(Q_BLOCK)
        qb = lax.dynamic_slice_in_dim(q, t0, Q_BLOCK, axis=1).transpose(0, 2, 3, 1, 4)
        gb = lax.dynamic_slice_in_dim(gates, t0, Q_BLOCK, axis=1).transpose(0, 2, 3, 1, 4)
        s_c = jnp.einsum('bgrqe,bcge->bgrqc', qb, k_cmp).astype(jnp.float32) * ATTN_SCALE
        p_c, _ = masked_softmax(s_c, cmp_end[None, :] <= t[:, None])
        o_c = jnp.einsum('bgrqc,bcge->bgrqe', p_c, v_cmp)
        imp = jnp.einsum('bgrqc,cj->bgqj', p_c, overlap)
        cur = t // SLC_LEN
        valid = blk_ids[None, :] * SLC_LEN <= t[:, None]
        forced = (blk_ids[None, :] == 0) | (blk_ids[None, :] == cur[:, None]) | (blk_ids[None, :] == cur[:, None] - 1)
        score = jnp.where(valid, imp + FORCE_BONUS * forced.astype(jnp.float32), -jnp.inf)
        top_s, idx = lax.top_k(score, n_sel)
        k_sel = k_blk[bi, gi, idx]
        v_sel = v_blk[bi, gi, idx]
        pos = idx[..., None] * SLC_LEN + jnp.arange(SLC_LEN)
        dist = t[:, None, None] - pos
        mask_s = (jnp.isfinite(top_s)[..., None] & (dist >= 0))[:, :, None]
        s_s = jnp.einsum('bgrqe,bgqkle->bgrqkl', qb, k_sel).astype(jnp.float32) * ATTN_SCALE
        s_s = s_s - slopes[:, :, None, None, None] * dist[:, :, None].astype(jnp.float32)
        nk = n_sel * SLC_LEN
        p_s, _ = masked_softmax(s_s.reshape(B_, G, R, Q_BLOCK, nk), mask_s.reshape(B_, G, 1, Q_BLOCK, nk))
        o_s = jnp.einsum('bgrqn,bgqne->bgrqe', p_s, v_sel.reshape(B_, G, Q_BLOCK, nk, E))
        kw = lax.dynamic_slice_in_dim(k_win, t0, Q_BLOCK + WIN, axis=1)
        vw = lax.dynamic_slice_in_dim(v_win, t0, Q_BLOCK + WIN, axis=1)
        kpos = t0 - WIN + jnp.arange(Q_BLOCK + WIN)
        dw = t[:, None] - kpos[None, :]
        mask_w = (dw >= 0) & (dw < WIN) & (kpos[None, :] >= 0)
        s_w = jnp.einsum('bgrqe,bkge->bgrqk', qb, kw).astype(jnp.float32) * ATTN_SCALE
        s_w = s_w - slopes[:, :, None, None] * dw.astype(jnp.float32)
        p_w, _ = masked_softmax(s_w, mask_w)
        o_w = jnp.einsum('bgrqk,bkge->bgrqe', p_w, vw)
        o = gb[..., 0:1] * o_c + gb[..., 1:2] * o_s + gb[..., 2:3] * o_w
        return o.transpose(0, 3, 1, 2, 4).reshape(B_, Q_BLOCK, B_WIDTH)

    out = lax.map(one_block, jnp.arange(S_ // Q_BLOCK))
    return out.transpose(1, 0, 2, 3).reshape(B_, S_, B_WIDTH)


def hybrid_mixer(x, w_in, cmp_pos_k, cmp_w1_k, cmp_w2_k, cmp_pos_v, cmp_w1_v, cmp_w2_v,
                 w_branch_a, w_branch_b, w_out):
    u = x @ w_in
    split_idx = np.cumsum(IN_SIZES)[:-1].tolist()
    a_qkv, b_q, b_kv, b_gate, m_gate = jnp.split(u, split_idx, axis=-1)
    y_a = dilated_mixer(a_qkv)
    y_b = nsa_mixer(b_q, b_kv, b_gate, cmp_pos_k, cmp_w1_k, cmp_w2_k, cmp_pos_v, cmp_w1_v, cmp_w2_v)
    g_a, g_b = jnp.split(jax.nn.sigmoid(m_gate), 2, axis=-1)
    merged = g_a * (y_a @ w_branch_a) + g_b * (y_b @ w_branch_b)
    return merged @ w_out


def hier_moe(h, w_coarse, b_coarse, w_fine, b_fine, w_gate_up, w_down):
    B_, S_, D = h.shape
    T = B_ * S_
    xt = h.reshape(T, D)
    lg = (xt @ w_coarse).astype(jnp.float32) + b_coarse
    grp = jnp.argmax(lg, axis=-1)
    p_grp = jnp.take_along_axis(jax.nn.softmax(lg, axis=-1), grp[:, None], axis=-1)
    lf = jnp.einsum('td,gde->tge', xt, w_fine).astype(jnp.float32) + b_fine
    lf = jnp.take_along_axis(lf, grp[:, None, None], axis=1)[:, 0]
    top_v, top_i = lax.top_k(lf, MOE_TOP_K)
    weights = p_grp * jax.nn.softmax(top_v, axis=-1)
    expert = grp[:, None] * EXP_PER_GROUP + top_i
    n_assign = T * MOE_TOP_K
    e_flat = expert.reshape(n_assign)
    w_flat = weights.reshape(n_assign)
    tok = jnp.arange(n_assign) // MOE_TOP_K
    order = jnp.argsort(e_flat)
    e_s, tok_s, w_s = e_flat[order], tok[order], w_flat[order]
    counts = jnp.zeros((N_EXPERTS,), jnp.int32).at[e_flat].add(1)
    starts = jnp.cumsum(counts) - counts
    pcounts = (counts + MOE_BLOCK - 1) // MOE_BLOCK * MOE_BLOCK
    pends = jnp.cumsum(pcounts)
    pstarts = pends - pcounts
    dest = pstarts[e_s] + (jnp.arange(n_assign) - starts[e_s])
    P = -(-(n_assign + N_EXPERTS * (MOE_BLOCK - 1)) // MOE_BLOCK) * MOE_BLOCK
    nb = P // MOE_BLOCK
    tok_buf = jnp.full((P,), T, jnp.int32).at[dest].set(tok_s.astype(jnp.int32))
    w_buf = jnp.zeros((P,), jnp.float32).at[dest].set(w_s)
    blk_e = jnp.minimum(jnp.searchsorted(pends, jnp.arange(nb) * MOE_BLOCK, side='right'), N_EXPERTS - 1)
    x_pad = jnp.concatenate([xt, jnp.zeros((1, D), xt.dtype)], axis=0)
    xb = x_pad[tok_buf].reshape(nb, MOE_BLOCK, D)

    def expert_block(args):
        xblk, e = args
        g, u = jnp.split(xblk @ w_gate_up[e], 2, axis=-1)
        return (jax.nn.silu(g) * u) @ w_down[e]

    yb = lax.map(expert_block, (xb, blk_e)).reshape(P, D)
    y = jax.ops.segment_sum(yb * w_buf[:, None], tok_buf, num_segments=T + 1)[:T]
    return y.reshape(B_, S_, D).astype(h.dtype)


def setup_inputs(seed: int = 0) -> dict:
    key = jax.random.key(seed)
    ks = jax.random.split(key, 22)
    L, D, E = DEPTH, D_MODEL, HEAD_DIM

    def nrm(k, shape, scale):
        return jax.random.normal(k, shape, jnp.float32) * scale

    col_scale = np.ones((IN_COLS,), np.float32)
    col_scale[2 * A_WIDTH:3 * A_WIDTH] = BETA
    kv_off = 3 * A_WIDTH + B_WIDTH
    for i in (1, 3, 5):
        col_scale[kv_off + i * B_KV_WIDTH:kv_off + (i + 1) * B_KV_WIDTH] = BETA
    return {
        'x': nrm(ks[0], (BATCH, SEQ, D), 1.0),
        'w_in': nrm(ks[1], (L, D, IN_COLS), D ** -0.5) * jnp.asarray(col_scale),
        'cmp_pos_k': nrm(ks[2], (L, CMP_LEN, E), 0.5),
        'cmp_w1_k': nrm(ks[3], (L, CMP_LEN * E, CMP_HIDDEN), (CMP_LEN * E) ** -0.5),
        'cmp_w2_k': nrm(ks[4], (L, CMP_HIDDEN, E), CMP_HIDDEN ** -0.5),
        'cmp_pos_v': nrm(ks[5], (L, CMP_LEN, E), 0.5),
        'cmp_w1_v': nrm(ks[6], (L, CMP_LEN * E, CMP_HIDDEN), (CMP_LEN * E) ** -0.5),
        'cmp_w2_v': nrm(ks[7], (L, CMP_HIDDEN, E), CMP_HIDDEN ** -0.5),
        'w_branch_a': nrm(ks[8], (L, A_OUT, D), A_OUT ** -0.5),
        'w_branch_b': nrm(ks[9], (L, B_WIDTH, D), B_WIDTH ** -0.5),
        'w_out': nrm(ks[10], (L, D, D), BETA * D ** -0.5),
        'ln1_g': 1.0 + nrm(ks[11], (L, D), 0.02),
        'ln1_b': nrm(ks[12], (L, D), 0.02),
        'w_coarse': nrm(ks[13], (L, D, N_GROUPS), D ** -0.5),
        'b_coarse': nrm(ks[14], (L, N_GROUPS), 0.01),
        'w_fine': nrm(ks[15], (L, N_GROUPS, D, EXP_PER_GROUP), D ** -0.5),
        'b_fine': nrm(ks[16], (L, N_GROUPS, EXP_PER_GROUP), 0.01),
        'w_gate_up': nrm(ks[17], (L, N_EXPERTS, D, 2 * D_EXPERT), D ** -0.5),
        'w_down': nrm(ks[18], (L, N_EXPERTS, D_EXPERT, D), BETA * D_EXPERT ** -0.5),
        'ln2_g': 1.0 + nrm(ks[19], (L, D), 0.02),
        'ln2_b': nrm(ks[20], (L, D), 0.02),
    }


def reference(x, w_in, cmp_pos_k, cmp_w1_k, cmp_w2_k, cmp_pos_v, cmp_w1_v, cmp_w2_v,
              w_branch_a, w_branch_b, w_out, ln1_g, ln1_b,
              w_coarse, b_coarse, w_fine, b_fine, w_gate_up, w_down, ln2_g, ln2_b):
    h = x
    for l in range(DEPTH):
        mix = hybrid_mixer(h, w_in[l], cmp_pos_k[l], cmp_w1_k[l], cmp_w2_k[l],
                           cmp_pos_v[l], cmp_w1_v[l], cmp_w2_v[l],
                           w_branch_a[l], w_branch_b[l], w_out[l])
        h = layer_norm(ALPHA * h + mix, ln1_g[l], ln1_b[l])
        ffn = hier_moe(h, w_coarse[l], b_coarse[l], w_fine[l], b_fine[l], w_gate_up[l], w_down[l])
        h = layer_norm(ALPHA * h + ffn, ln2_g[l], ln2_b[l])
    return h
```

```python
import functools

import numpy as np
import jax
import jax.numpy as jnp
from jax import lax
from jax.experimental import pallas as pl
from jax.experimental.pallas import tpu as pltpu

D_MODEL = 1024
HEAD_DIM = 64
Q_BLOCK = 128
DIL_GROUPS = ((128, 1), (512, 4), (2048, 16))
N_DIL = len(DIL_GROUPS)
A_HEADS_PER_GROUP = 4
A_HEADS = A_HEADS_PER_GROUP * N_DIL
A_WIDTH = A_HEADS * HEAD_DIM
A_OUT = A_HEADS_PER_GROUP * HEAD_DIM
B_HEADS = 8
B_KV_HEADS = 2
B_GQA = B_HEADS // B_KV_HEADS
B_WIDTH = B_HEADS * HEAD_DIM
B_KV_WIDTH = B_KV_HEADS * HEAD_DIM
CMP_LEN = 32
CMP_STRIDE = 16
CMP_HIDDEN = 256
SLC_LEN = 64
SLC_TOPK = 16
WIN = 512
N_GROUPS = 4
EXP_PER_GROUP = 8
N_EXPERTS = N_GROUPS * EXP_PER_GROUP
D_EXPERT = 512
MOE_TOP_K = 2
DEPTH = 1
ALPHA = (2.0 * DEPTH) ** 0.25
LN_EPS = 1e-5
FORCE_BONUS = 1e4
TINY = 1e-30
ATTN_SCALE = HEAD_DIM ** -0.5

LANES = 128
VMEM_LIMIT = 48 * 1024 * 1024

BF_Q_A = 0
BF_K_A = N_DIL
BF_V_A = 2 * N_DIL
BF_Q_B = 3 * N_DIL
BF_SLC = (3 * A_WIDTH + B_WIDTH) // LANES
BF_WIN = BF_SLC + B_KV_HEADS
BF_COLS = 3 * A_WIDTH + B_WIDTH + 4 * B_KV_WIDTH
F_MGATE = 0
F_CMP = 2 * D_MODEL
F_GATE = (2 * D_MODEL + 2 * B_KV_WIDTH) // LANES
F_COLS = 2 * D_MODEL + 2 * B_KV_WIDTH + B_KV_HEADS * LANES


def _params(semantics):
    return pltpu.CompilerParams(dimension_semantics=semantics, vmem_limit_bytes=VMEM_LIMIT)


def _in_proj_columns():
    kv_off = 3 * A_WIDTH + B_WIDTH
    gate_off = kv_off + 6 * B_KV_WIDTH
    mg_off = gate_off + 3 * B_HEADS
    bf = list(range(0, 3 * A_WIDTH + B_WIDTH))
    for first in (2, 4):
        for g in range(B_KV_HEADS):
            for i in (first, first + 1):
                base = kv_off + i * B_KV_WIDTH + g * HEAD_DIM
                bf += list(range(base, base + HEAD_DIM))
    f32 = list(range(mg_off, mg_off + 2 * D_MODEL))
    f32 += list(range(kv_off, kv_off + 2 * B_KV_WIDTH))
    per_head = 3 * B_GQA
    for g in range(B_KV_HEADS):
        f32 += list(range(gate_off + g * per_head, gate_off + (g + 1) * per_head))
        f32 += [-1] * (LANES - per_head)
    assert len(bf) == BF_COLS and len(f32) == F_COLS
    return np.asarray(bf), np.asarray(f32)


def _permute_columns(w, cols):
    taken = jnp.take(w, jnp.asarray(np.maximum(cols, 0)), axis=1)
    return jnp.where(jnp.asarray(cols >= 0)[None, :], taken, 0.0)


def _proj_kernel(x_ref, w_ref, o_ref, *, chunk):
    xb = x_ref[...].astype(jnp.bfloat16)
    n = o_ref.shape[1]
    for c0 in range(0, n, chunk):
        o_ref[:, c0:c0 + chunk] = jnp.dot(
            xb, w_ref[:, c0:c0 + chunk], preferred_element_type=jnp.float32).astype(o_ref.dtype)


def _project(x2d, w, out_dtype, tm=512, chunk=256):
    t, d = x2d.shape
    n = w.shape[1]
    assert t % tm == 0 and n % chunk == 0
    return pl.pallas_call(
        functools.partial(_proj_kernel, chunk=chunk),
        grid=(t // tm,),
        in_specs=[pl.BlockSpec((tm, d), lambda i: (i, 0)),
                  pl.BlockSpec((d, n), lambda i: (0, 0))],
        out_specs=pl.BlockSpec((tm, n), lambda i: (i, 0)),
        out_shape=jax.ShapeDtypeStruct((t, n), out_dtype),
        compiler_params=_params(("parallel",)),
    )(x2d, w)


def _dilated_kernel(*refs, nbs, dils, slopes):
    ins, outs = refs[:5 * N_DIL], refs[5 * N_DIL:]
    i = pl.program_id(1)
    qi = lax.broadcasted_iota(jnp.int32, (Q_BLOCK, Q_BLOCK), 0)
    kj = lax.broadcasted_iota(jnp.int32, (Q_BLOCK, Q_BLOCK), 1)
    d_prev = Q_BLOCK + qi - kj
    d_cur = qi - kj
    dn = (((1,), (1,)), ((), ()))
    for g in range(N_DIL):
        q_ref, kp_ref, kc_ref, vp_ref, vc_ref = ins[5 * g:5 * g + 5]
        o_ref, lse_ref = outs[2 * g:2 * g + 2]
        m_prev = d_prev <= jnp.where((i % nbs[g]) > 0, Q_BLOCK, 0)
        m_cur = d_cur >= 0
        for h in range(A_HEADS_PER_GROUP):
            hs = slice(h * HEAD_DIM, (h + 1) * HEAD_DIM)
            q = q_ref[0, :, hs]
            slope = slopes[g][h] * dils[g]
            s_p = lax.dot_general(q, kp_ref[0, :, hs], dn, preferred_element_type=jnp.float32) * ATTN_SCALE
            s_c = lax.dot_general(q, kc_ref[0, :, hs], dn, preferred_element_type=jnp.float32) * ATTN_SCALE
            s_p = jnp.where(m_prev, s_p - slope * d_prev.astype(jnp.float32), -jnp.inf)
            s_c = jnp.where(m_cur, s_c - slope * d_cur.astype(jnp.float32), -jnp.inf)
            m = jnp.maximum(jnp.max(s_p, axis=1, keepdims=True), jnp.max(s_c, axis=1, keepdims=True))
            p_p = jnp.exp(s_p - m)
            p_c = jnp.exp(s_c - m)
            l = jnp.sum(p_p, axis=1, keepdims=True) + jnp.sum(p_c, axis=1, keepdims=True)
            acc = jnp.dot(p_p.astype(jnp.bfloat16), vp_ref[0, :, hs], preferred_element_type=jnp.float32)
            acc += jnp.dot(p_c.astype(jnp.bfloat16), vc_ref[0, :, hs], preferred_element_type=jnp.float32)
            o_ref[0, :, hs] = acc / l
            lse_ref[0, :, hs] = jnp.broadcast_to(m + jnp.log(l), (Q_BLOCK, HEAD_DIM))


def _dilated(u_bf, batch, seq):
    n_col = BF_COLS // A_OUT
    steps = seq // Q_BLOCK
    in_arrays, in_specs, out_specs, out_shapes, nbs, dils = [], [], [], [], [], []
    for g, (_, dil) in enumerate(DIL_GROUPS):
        length = seq // dil
        assert length % Q_BLOCK == 0
        nb = length // Q_BLOCK
        nbs.append(nb)
        dils.append(dil)
        view = u_bf.reshape(batch, length, dil * BF_COLS)

        def cur(b, i, nb=nb, col=0):
            return (b, i % nb, (i // nb) * n_col + col)

        def prev(b, i, nb=nb, col=0):
            return (b, jnp.maximum(i % nb - 1, 0), (i // nb) * n_col + col)

        blk = (1, Q_BLOCK, A_OUT)
        for fn, col in ((cur, BF_Q_A + g), (prev, BF_K_A + g), (cur, BF_K_A + g),
                        (prev, BF_V_A + g), (cur, BF_V_A + g)):
            in_arrays.append(view)
            in_specs.append(pl.BlockSpec(blk, functools.partial(fn, col=col)))

        def out_map(b, i, nb=nb):
            return (b, i % nb, i // nb)

        for _ in range(2):
            out_specs.append(pl.BlockSpec(blk, out_map))
            out_shapes.append(jax.ShapeDtypeStruct((batch, length, dil * A_OUT), jnp.float32))
    slopes = np.exp2(-8.0 * np.arange(1, A_HEADS + 1, dtype=np.float64) / A_HEADS).reshape(N_DIL, A_HEADS_PER_GROUP)
    outs = pl.pallas_call(
        functools.partial(_dilated_kernel, nbs=tuple(nbs), dils=tuple(dils),
                          slopes=tuple(tuple(float(s) for s in row) for row in slopes)),
        grid=(batch, steps),
        in_specs=in_specs,
        out_specs=out_specs,
        out_shape=out_shapes,
        compiler_params=_params(("parallel", "parallel")),
    )(*in_arrays)
    return [o.reshape(batch * seq, A_OUT) for o in outs]


def _compress_kernel(x_ref, pos_ref, w1_ref, w2_ref, o_ref):
    half = CMP_STRIDE * HEAD_DIM
    x = x_ref[...]
    n_chunk = x.shape[0]
    lo = (x + pos_ref[0]).astype(jnp.bfloat16)
    hi = (x + pos_ref[1]).astype(jnp.bfloat16)
    h_lo = jnp.dot(lo, w1_ref[0:half, :], preferred_element_type=jnp.float32)
    h_hi = jnp.dot(hi, w1_ref[half:2 * half, :], preferred_element_type=jnp.float32)
    h = h_lo + pltpu.roll(h_hi, n_chunk - 1, 0)
    act = jax.nn.gelu(h, approximate=True)
    o_ref[...] = jnp.dot(act.astype(jnp.bfloat16), w2_ref[...],
                         preferred_element_type=jnp.float32).astype(o_ref.dtype)


def _compress(u_f32, batch, seq, pos, w1, w2):
    n_chunk = seq // CMP_STRIDE
    half = CMP_STRIDE * HEAD_DIM
    ckv = u_f32[:, F_CMP:F_CMP + 2 * B_KV_WIDTH]
    ckv = ckv.reshape(batch, n_chunk, CMP_STRIDE, 2, B_KV_HEADS, HEAD_DIM)
    ckv = ckv.transpose(3, 0, 4, 1, 2, 5).reshape(2, batch, B_KV_HEADS, n_chunk, half)
    return pl.pallas_call(
        _compress_kernel,
        grid=(2, batch, B_KV_HEADS),
        in_specs=[pl.BlockSpec((None, None, None, n_chunk, half), lambda s, b, g: (s, b, g, 0, 0)),
                  pl.BlockSpec((None, 2, 1, half), lambda s, b, g: (s, 0, 0, 0)),
                  pl.BlockSpec((None, 2 * half, CMP_HIDDEN), lambda s, b, g: (s, 0, 0)),
                  pl.BlockSpec((None, CMP_HIDDEN, HEAD_DIM), lambda s, b, g: (s, 0, 0))],
        out_specs=pl.BlockSpec((None, None, None, n_chunk, HEAD_DIM), lambda s, b, g: (s, b, g, 0, 0)),
        out_shape=jax.ShapeDtypeStruct((2, batch, B_KV_HEADS, n_chunk, HEAD_DIM), jnp.bfloat16),
        compiler_params=_params(("parallel", "parallel", "parallel")),
    )(ckv, pos, w1, w2)


def _softmax_parts(s):
    m = jnp.max(s, axis=1, keepdims=True)
    m = jnp.where(m == -jnp.inf, 0.0, m)
    p = jnp.exp(s - m)
    l = jnp.sum(p, axis=1, keepdims=True)
    return p, jnp.maximum(l, TINY)


def _nsa_kernel(q_ref, kc_ref, vc_ref, skv_ref, wkv_ref, gate_ref, ov_ref, e_ref, slope_ref, o_ref,
                m_sc, l_sc, acc_sc, *, n_slc, n_sel, tk):
    n = pl.program_id(2)
    t0 = n * Q_BLOCK
    rows = B_GQA * Q_BLOCK
    dn = (((1,), (1,)), ((), ()))
    q = q_ref[...]
    qs = jnp.concatenate([q[:, r * HEAD_DIM:(r + 1) * HEAD_DIM] for r in range(B_GQA)], axis=0)
    row = lax.broadcasted_iota(jnp.int32, (rows, 1), 0)
    tq = t0 + (row & (Q_BLOCK - 1))
    slope = slope_ref[...]

    n_cmp_pad = kc_ref.shape[0]
    s = lax.dot_general(qs, kc_ref[...], dn, preferred_element_type=jnp.float32) * ATTN_SCALE
    cmp_end = lax.broadcasted_iota(jnp.int32, (1, n_cmp_pad), 1) * CMP_STRIDE + (CMP_LEN - 1)
    s = jnp.where(cmp_end <= tq, s, -jnp.inf)
    p, l = _softmax_parts(s)
    p_c = (p / l).astype(jnp.bfloat16)
    o_c = jnp.dot(p_c, vc_ref[...], preferred_element_type=jnp.float32)

    imp = jnp.dot(p_c[0:Q_BLOCK], ov_ref[...], preferred_element_type=jnp.float32)
    for r in range(1, B_GQA):
        imp += jnp.dot(p_c[r * Q_BLOCK:(r + 1) * Q_BLOCK], ov_ref[...], preferred_element_type=jnp.float32)
    blk = lax.broadcasted_iota(jnp.int32, (Q_BLOCK, n_slc), 1)
    t = t0 + lax.broadcasted_iota(jnp.int32, (Q_BLOCK, 1), 0)
    cur = t >> 6
    forced = (blk == 0) | (blk == cur) | (blk == cur - 1)
    score = jnp.where(blk * SLC_LEN <= t, imp + jnp.where(forced, FORCE_BONUS, 0.0), -jnp.inf)

    def pick(_, carry):
        work, sel = carry
        best = jnp.max(work, axis=1, keepdims=True)
        idx = jnp.min(jnp.where(work == best, blk, n_slc), axis=1, keepdims=True)
        hit = blk == idx
        finite = jnp.where(best > -jnp.inf, 1.0, 0.0)
        sel = jnp.where(hit, jnp.maximum(sel, finite), sel)
        return jnp.where(hit, -jnp.inf, work), sel

    _, sel = lax.fori_loop(0, n_sel, pick, (score, jnp.zeros((Q_BLOCK, n_slc), jnp.float32)))
    sel = sel.astype(jnp.bfloat16)

    m_sc[...] = jnp.full(m_sc.shape, -jnp.inf, jnp.float32)
    l_sc[...] = jnp.zeros(l_sc.shape, jnp.float32)
    acc_sc[...] = jnp.zeros(acc_sc.shape, jnp.float32)

    def tile(j, carry):
        k0 = pl.multiple_of(j * tk, tk)
        kt = skv_ref[pl.ds(k0, tk), 0:HEAD_DIM]
        vt = skv_ref[pl.ds(k0, tk), HEAD_DIM:2 * HEAD_DIM]
        s = lax.dot_general(qs, kt, dn, preferred_element_type=jnp.float32) * ATTN_SCALE
        dist = tq - (k0 + lax.broadcasted_iota(jnp.int32, (1, tk), 1))
        keep = jnp.dot(sel, e_ref[j], preferred_element_type=jnp.float32)
        keep = jnp.concatenate([keep] * B_GQA, axis=0)
        s = jnp.where((keep > 0.5) & (dist >= 0), s - slope * dist.astype(jnp.float32), -jnp.inf)
        m_old = m_sc[...]
        m_new = jnp.maximum(m_old, jnp.max(s, axis=1, keepdims=True))
        m_safe = jnp.where(m_new == -jnp.inf, 0.0, m_new)
        alpha = jnp.exp(m_old - m_safe)
        p = jnp.exp(s - m_safe)
        l_sc[...] = alpha * l_sc[...] + jnp.sum(p, axis=1, keepdims=True)
        acc_sc[...] = alpha * acc_sc[...] + jnp.dot(p.astype(jnp.bfloat16), vt, preferred_element_type=jnp.float32)
        m_sc[...] = m_new
        return carry

    lax.fori_loop(0, (t0 + Q_BLOCK + tk - 1) // tk, tile, 0)
    o_s = acc_sc[...] / jnp.maximum(l_sc[...], TINY)

    span = min(WIN + Q_BLOCK, skv_ref.shape[0])
    start = pl.multiple_of(jnp.maximum(t0 + Q_BLOCK - span, 0), Q_BLOCK)
    kw = wkv_ref[pl.ds(start, span), 0:HEAD_DIM]
    vw = wkv_ref[pl.ds(start, span), HEAD_DIM:2 * HEAD_DIM]
    s = lax.dot_general(qs, kw, dn, preferred_element_type=jnp.float32) * ATTN_SCALE
    dw = tq - (start + lax.broadcasted_iota(jnp.int32, (1, span), 1))
    s = jnp.where((dw >= 0) & (dw < WIN), s - slope * dw.astype(jnp.float32), -jnp.inf)
    p, l = _softmax_parts(s)
    o_w = jnp.dot(p.astype(jnp.bfloat16), vw, preferred_element_type=jnp.float32) / l

    gates = 1.0 / (1.0 + jnp.exp(-gate_ref[...]))
    for r in range(B_GQA):
        rs = slice(r * Q_BLOCK, (r + 1) * Q_BLOCK)
        out = (gates[:, 3 * r:3 * r + 1] * o_c[rs] + gates[:, 3 * r + 1:3 * r + 2] * o_s[rs]
               + gates[:, 3 * r + 2:3 * r + 3] * o_w[rs])
        o_ref[:, r * HEAD_DIM:(r + 1) * HEAD_DIM] = out.astype(o_ref.dtype)


def _nsa(u_bf, u_f32, cmp_kv, batch, seq):
    nq = seq // Q_BLOCK
    n_chunk = seq // CMP_STRIDE
    n_cmp = n_chunk - CMP_LEN // CMP_STRIDE + 1
    n_slc = seq // SLC_LEN
    n_sel = min(SLC_TOPK, n_slc)
    tk = min(512, seq)
    c_start = np.arange(n_chunk) * CMP_STRIDE
    s_start = np.arange(n_slc) * SLC_LEN
    overlap = ((c_start[:, None] < s_start[None, :] + SLC_LEN) & (c_start[:, None] + CMP_LEN > s_start[None, :])
               & (np.arange(n_chunk)[:, None] < n_cmp))
    expand = (np.arange(n_slc)[None, :, None] ==
              (np.arange(seq) // SLC_LEN).reshape(seq // tk, 1, tk))
    rows = B_GQA * Q_BLOCK
    slopes = jnp.exp2(-8.0 * jnp.arange(1, B_HEADS + 1, dtype=jnp.float32) / B_HEADS)
    slopes = jnp.repeat(slopes.reshape(B_KV_HEADS, B_GQA), Q_BLOCK, axis=1).reshape(B_KV_HEADS, rows, 1)
    kv_rows = pl.BlockSpec((seq, LANES), lambda b, g, n: (b, BF_SLC + g))
    win_rows = pl.BlockSpec((seq, LANES), lambda b, g, n: (b, BF_WIN + g))
    cmp_spec = lambda s: pl.BlockSpec((None, None, None, n_chunk, HEAD_DIM), lambda b, g, n: (s, b, g, 0, 0))
    return pl.pallas_call(
        functools.partial(_nsa_kernel, n_slc=n_slc, n_sel=n_sel, tk=tk),
        grid=(batch, B_KV_HEADS, nq),
        in_specs=[pl.BlockSpec((Q_BLOCK, B_GQA * HEAD_DIM), lambda b, g, n: (b * nq + n, BF_Q_B + g)),
                  cmp_spec(0), cmp_spec(1), kv_rows, win_rows,
                  pl.BlockSpec((Q_BLOCK, LANES), lambda b, g, n: (b * nq + n, F_GATE + g)),
                  pl.BlockSpec((n_chunk, n_slc), lambda b, g, n: (0, 0)),
                  pl.BlockSpec((seq // tk, n_slc, tk), lambda b, g, n: (0, 0, 0)),
                  pl.BlockSpec((None, rows, 1), lambda b, g, n: (g, 0, 0))],
        out_specs=pl.BlockSpec((Q_BLOCK, B_GQA * HEAD_DIM), lambda b, g, n: (b * nq + n, g)),
        out_shape=jax.ShapeDtypeStruct((batch * seq, B_WIDTH), jnp.bfloat16),
        scratch_shapes=[pltpu.VMEM((rows, 1), jnp.float32), pltpu.VMEM((rows, 1), jnp.float32),
                        pltpu.VMEM((rows, HEAD_DIM), jnp.float32)],
        compiler_params=_params(("parallel", "parallel", "arbitrary")),
    )(u_bf, cmp_kv, cmp_kv, u_bf, u_bf, u_f32, jnp.asarray(overlap, jnp.bfloat16), jnp.asarray(expand, jnp.bfloat16),
      slopes)


def _layer_norm(v, gain, bias):
    mu = jnp.mean(v, axis=1, keepdims=True)
    c = v - mu
    var = jnp.mean(c * c, axis=1, keepdims=True)
    return c * lax.rsqrt(var + LN_EPS) * gain + bias


def _sigmoid(v):
    return 1.0 / (1.0 + jnp.exp(-v))


def _merge_kernel(o0, o1, o2, l0, l1, l2, yb_ref, mga_ref, mgb_ref, x_ref, wa_ref, wb_ref, wo_ref,
                  g_ref, b_ref, wr_hi_ref, wr_lo_ref, br_ref, h_ref, rw_ref, re_ref):
    f32 = jnp.float32
    lse = [l0[...], l1[...], l2[...]]
    top = jnp.maximum(jnp.maximum(lse[0], lse[1]), lse[2])
    e = [jnp.exp(v - top) for v in lse]
    y_a = (e[0] * o0[...] + e[1] * o1[...] + e[2] * o2[...]) / (e[0] + e[1] + e[2])
    br_a = jnp.dot(y_a.astype(jnp.bfloat16), wa_ref[...], preferred_element_type=f32)
    br_b = jnp.dot(yb_ref[...], wb_ref[...], preferred_element_type=f32)
    merged = _sigmoid(mga_ref[...]) * br_a + _sigmoid(mgb_ref[...]) * br_b
    mix = jnp.dot(merged.astype(jnp.bfloat16), wo_ref[...], preferred_element_type=f32)
    h = _layer_norm(ALPHA * x_ref[...] + mix, g_ref[...], b_ref[...])
    h_ref[...] = h

    h_hi = h.astype(jnp.bfloat16)
    h_lo = (h - h_hi.astype(f32)).astype(jnp.bfloat16)
    logits = (jnp.dot(h_hi, wr_hi_ref[...], preferred_element_type=f32)
              + jnp.dot(h_lo, wr_hi_ref[...], preferred_element_type=f32)
              + jnp.dot(h_hi, wr_lo_ref[...], preferred_element_type=f32)) + br_ref[...]
    lane = lax.broadcasted_iota(jnp.int32, logits.shape, 1)
    coarse = jnp.where(lane < N_GROUPS, logits, -jnp.inf)
    c_max = jnp.max(coarse, axis=1, keepdims=True)
    grp = jnp.min(jnp.where(coarse == c_max, lane, LANES), axis=1, keepdims=True)
    p_grp = 1.0 / jnp.sum(jnp.exp(coarse - c_max), axis=1, keepdims=True)
    lo_lane = N_GROUPS + EXP_PER_GROUP * grp
    fine = jnp.where((lane >= lo_lane) & (lane < lo_lane + EXP_PER_GROUP), logits, -jnp.inf)
    v1 = jnp.max(fine, axis=1, keepdims=True)
    i1 = jnp.min(jnp.where(fine == v1, lane, LANES), axis=1, keepdims=True)
    fine = jnp.where(lane == i1, -jnp.inf, fine)
    v2 = jnp.max(fine, axis=1, keepdims=True)
    i2 = jnp.min(jnp.where(fine == v2, lane, LANES), axis=1, keepdims=True)
    e2 = jnp.exp(v2 - v1)
    w1 = p_grp / (1.0 + e2)
    w2 = p_grp * e2 / (1.0 + e2)
    rw_ref[...] = jnp.where(lane == 0, w1, jnp.where(lane == 1, w2, 0.0))
    re_ref[...] = jnp.where(lane == 0, i1 - N_GROUPS, jnp.where(lane == 1, i2 - N_GROUPS, 0))


def _merge(dil_outs, y_b, u_f32, x2d, wa, wb, wo, ln_g, ln_b, wr_hi, wr_lo, b_r, tm=256):
    t = x2d.shape[0]
    assert t % tm == 0
    row = lambda w: pl.BlockSpec((tm, w), lambda i: (i, 0))
    full = lambda a: pl.BlockSpec(a.shape, lambda i: (0,) * a.ndim)
    o0, l0, o1, l1, o2, l2 = dil_outs
    return pl.pallas_call(
        _merge_kernel,
        grid=(t // tm,),
        in_specs=[row(A_OUT)] * 6 + [row(B_WIDTH),
                  pl.BlockSpec((tm, D_MODEL), lambda i: (i, F_MGATE)),
                  pl.BlockSpec((tm, D_MODEL), lambda i: (i, F_MGATE + 1)),
                  row(D_MODEL), full(wa), full(wb), full(wo), full(ln_g), full(ln_b),
                  full(wr_hi), full(wr_lo), full(b_r)],
        out_specs=[row(D_MODEL), row(LANES), row(LANES)],
        out_shape=[jax.ShapeDtypeStruct((t, D_MODEL), jnp.float32),
                   jax.ShapeDtypeStruct((t, LANES), jnp.float32),
                   jax.ShapeDtypeStruct((t, LANES), jnp.int32)],
        compiler_params=_params(("parallel",)),
    )(o0, o1, o2, l0, l1, l2, y_b, u_f32, u_f32, x2d, wa, wb, wo, ln_g, ln_b, wr_hi, wr_lo, b_r)


def _gather_rows(idx_ref, count, src_hbm, dst, sem, src_of):
    def row_copy(r):
        src_row = idx_ref[0, 0, src_of(r)]
        return pltpu.make_async_copy(src_hbm.at[pl.ds(src_row, 1), :], dst.at[pl.ds(r, 1), :], sem)

    def start(r, carry):
        row_copy(r).start()
        return carry

    def wait(r, carry):
        row_copy(r).wait()
        return carry

    lax.fori_loop(0, count, start, 0)
    lax.fori_loop(0, count, wait, 0)


def _expert_kernel(blk_e_ref, tok_ref, h_hbm, wgu_ref, wd_ref, y_ref, xbuf, sem):
    bm = xbuf.shape[0]
    _gather_rows(tok_ref, bm, h_hbm, xbuf, sem, lambda r: r)
    xb = xbuf[...].astype(jnp.bfloat16)
    gu = jnp.dot(xb, wgu_ref[...], preferred_element_type=jnp.float32)
    gate, up = gu[:, :D_EXPERT], gu[:, D_EXPERT:]
    act = (gate * _sigmoid(gate)) * up
    y_ref[...] = jnp.dot(act.astype(jnp.bfloat16), wd_ref[...], preferred_element_type=jnp.float32)


def _experts(h, tok_buf, blk_e, wgu, wd, bm):
    p = tok_buf.shape[0]
    nb = p // bm
    return pl.pallas_call(
        _expert_kernel,
        grid_spec=pltpu.PrefetchScalarGridSpec(
            num_scalar_prefetch=1,
            grid=(nb,),
            in_specs=[pl.BlockSpec((1, 1, bm), lambda i, e: (i, 0, 0), memory_space=pltpu.SMEM),
                      pl.BlockSpec(memory_space=pl.ANY),
                      pl.BlockSpec((None, D_MODEL, 2 * D_EXPERT), lambda i, e: (e[i], 0, 0)),
                      pl.BlockSpec((None, D_EXPERT, D_MODEL), lambda i, e: (e[i], 0, 0))],
            out_specs=pl.BlockSpec((bm, D_MODEL), lambda i, e: (i, 0)),
            scratch_shapes=[pltpu.VMEM((bm, D_MODEL), jnp.float32), pltpu.SemaphoreType.DMA(())],
        ),
        out_shape=jax.ShapeDtypeStruct((p, D_MODEL), jnp.float32),
        compiler_params=_params(("arbitrary",)),
    )(blk_e, tok_buf.reshape(nb, 1, bm), h, wgu, wd)


def _combine_kernel(dest_ref, y_hbm, h_ref, rw_ref, g_ref, b_ref, o_ref, ybuf, sem):
    tm = h_ref.shape[0]
    _gather_rows(dest_ref, tm, y_hbm, ybuf.at[0], sem, lambda r: 2 * r)
    _gather_rows(dest_ref, tm, y_hbm, ybuf.at[1], sem, lambda r: 2 * r + 1)
    rw = rw_ref[...]
    ffn = rw[:, 0:1] * ybuf[0] + rw[:, 1:2] * ybuf[1]
    o_ref[...] = _layer_norm(ALPHA * h_ref[...] + ffn, g_ref[...], b_ref[...])


def _combine(dest, y_sorted, h, rw, ln_g, ln_b, tm=256):
    t = h.shape[0]
    assert t % tm == 0
    return pl.pallas_call(
        _combine_kernel,
        grid=(t // tm,),
        in_specs=[pl.BlockSpec((1, 1, MOE_TOP_K * tm), lambda i: (i, 0, 0), memory_space=pltpu.SMEM),
                  pl.BlockSpec(memory_space=pl.ANY),
                  pl.BlockSpec((tm, D_MODEL), lambda i: (i, 0)),
                  pl.BlockSpec((tm, LANES), lambda i: (i, 0)),
                  pl.BlockSpec((1, D_MODEL), lambda i: (0, 0)),
                  pl.BlockSpec((1, D_MODEL), lambda i: (0, 0))],
        out_specs=pl.BlockSpec((tm, D_MODEL), lambda i: (i, 0)),
        out_shape=jax.ShapeDtypeStruct((t, D_MODEL), jnp.float32),
        scratch_shapes=[pltpu.VMEM((MOE_TOP_K, tm, D_MODEL), jnp.float32), pltpu.SemaphoreType.DMA(())],
        compiler_params=_params(("arbitrary",)),
    )(dest.reshape(t // tm, 1, MOE_TOP_K * tm), y_sorted, h, rw, ln_g, ln_b)


def _dispatch(expert, bm):
    t = expert.shape[0]
    n_assign = t * MOE_TOP_K
    e_flat = expert.reshape(n_assign)
    order = jnp.argsort(e_flat)
    e_s = e_flat[order]
    counts = jnp.zeros((N_EXPERTS,), jnp.int32).at[e_flat].add(1)
    starts = jnp.cumsum(counts) - counts
    pcounts = (counts + bm - 1) // bm * bm
    pends = jnp.cumsum(pcounts)
    pstarts = pends - pcounts
    dest_s = pstarts[e_s] + (jnp.arange(n_assign, dtype=jnp.int32) - starts[e_s])
    p = -(-(n_assign + N_EXPERTS * (bm - 1)) // bm) * bm
    tok_buf = jnp.zeros((p,), jnp.int32).at[dest_s].set((order // MOE_TOP_K).astype(jnp.int32))
    dest = jnp.zeros((n_assign,), jnp.int32).at[order].set(dest_s.astype(jnp.int32))
    blk_e = jnp.minimum(jnp.searchsorted(pends, jnp.arange(p // bm, dtype=jnp.int32) * bm, side='right'),
                        N_EXPERTS - 1).astype(jnp.int32)
    return tok_buf, dest, blk_e


def _split_bf16(w):
    hi = w.astype(jnp.bfloat16)
    return hi, (w - hi.astype(jnp.float32)).astype(jnp.bfloat16)


def kernel(x, w_in, cmp_pos_k, cmp_w1_k, cmp_w2_k, cmp_pos_v, cmp_w1_v, cmp_w2_v, w_branch_a, w_branch_b, w_out, ln1_g, ln1_b, w_coarse, b_coarse, w_fine, b_fine, w_gate_up, w_down, ln2_g, ln2_b):
    batch, seq, d = x.shape
    bf16 = jnp.bfloat16
    h = x.reshape(batch * seq, d)
    cols_bf, cols_f32 = _in_proj_columns()
    half = CMP_STRIDE * HEAD_DIM
    for l in range(DEPTH):
        u_bf = _project(h, _permute_columns(w_in[l], cols_bf).astype(bf16), bf16)
        u_f32 = _project(h, _permute_columns(w_in[l], cols_f32).astype(bf16), jnp.float32)
        dil_outs = _dilated(u_bf, batch, seq)
        pos = jnp.stack([cmp_pos_k[l], cmp_pos_v[l]]).reshape(2, 2, 1, half)
        w1 = jnp.stack([cmp_w1_k[l], cmp_w1_v[l]]).astype(bf16)
        w2 = jnp.stack([cmp_w2_k[l], cmp_w2_v[l]]).astype(bf16)
        cmp_kv = _compress(u_f32, batch, seq, pos, w1, w2)
        y_b = _nsa(u_bf, u_f32, cmp_kv, batch, seq)
        w_r = jnp.concatenate([w_coarse[l], w_fine[l].transpose(1, 0, 2).reshape(d, N_EXPERTS)], axis=1)
        w_r = jnp.pad(w_r, ((0, 0), (0, LANES - w_r.shape[1])))
        b_r = jnp.pad(jnp.concatenate([b_coarse[l], b_fine[l].reshape(N_EXPERTS)]),
                      (0, LANES - N_GROUPS - N_EXPERTS)).reshape(1, LANES)
        wr_hi, wr_lo = _split_bf16(w_r)
        h1, rw, re = _merge(dil_outs, y_b, u_f32, h, w_branch_a[l].astype(bf16), w_branch_b[l].astype(bf16),
                            w_out[l].astype(bf16), ln1_g[l].reshape(1, d), ln1_b[l].reshape(1, d),
                            wr_hi, wr_lo, b_r)
        bm = 256
        tok_buf, dest, blk_e = _dispatch(re[:, :MOE_TOP_K], bm)
        y_sorted = _experts(h1, tok_buf, blk_e, w_gate_up[l].astype(bf16), w_down[l].astype(bf16), bm)
        h = _combine(dest, y_sorted, h1, rw, ln2_g[l].reshape(1, d), ln2_b[l].reshape(1, d))
    return h.reshape(batch, seq, d)
```

```python
import functools

import numpy as np
import jax
import jax.numpy as jnp
from jax import lax
from jax.experimental import pallas as pl
from jax.experimental.pallas import tpu as pltpu

D_MODEL = 1024
HEAD_DIM = 64
Q_BLOCK = 128
DIL_GROUPS = ((128, 1), (512, 4), (2048, 16))
N_DIL = len(DIL_GROUPS)
A_HEADS_PER_GROUP = 4
A_HEADS = A_HEADS_PER_GROUP * N_DIL
A_WIDTH = A_HEADS * HEAD_DIM
A_OUT = A_HEADS_PER_GROUP * HEAD_DIM
B_HEADS = 8
B_KV_HEADS = 2
B_GQA = B_HEADS // B_KV_HEADS
B_WIDTH = B_HEADS * HEAD_DIM
B_KV_WIDTH = B_KV_HEADS * HEAD_DIM
CMP_LEN = 32
CMP_STRIDE = 16
CMP_HIDDEN = 256
SLC_LEN = 64
SLC_TOPK = 16
WIN = 512
N_GROUPS = 4
EXP_PER_GROUP = 8
N_EXPERTS = N_GROUPS * EXP_PER_GROUP
D_EXPERT = 512
MOE_TOP_K = 2
DEPTH = 1
ALPHA = (2.0 * DEPTH) ** 0.25
LN_EPS = 1e-5
FORCE_BONUS = 1e4
TINY = 1e-30
ATTN_SCALE = HEAD_DIM ** -0.5

LANES = 128
VMEM_LIMIT = 48 * 1024 * 1024

BF_Q_A = 0
BF_K_A = N_DIL
BF_V_A = 2 * N_DIL
BF_Q_B = 3 * N_DIL
BF_SLC = (3 * A_WIDTH + B_WIDTH) // LANES
BF_WIN = BF_SLC + B_KV_HEADS
BF_COLS = 3 * A_WIDTH + B_WIDTH + 4 * B_KV_WIDTH
F_MGATE = 0
F_CMP = 2 * D_MODEL
F_GATE = (2 * D_MODEL + 2 * B_KV_WIDTH) // LANES
F_COLS = 2 * D_MODEL + 2 * B_KV_WIDTH + B_KV_HEADS * LANES


def _params(semantics):
    return pltpu.CompilerParams(dimension_semantics=semantics, vmem_limit_bytes=VMEM_LIMIT)


def _in_proj_columns():
    kv_off = 3 * A_WIDTH + B_WIDTH
    gate_off = kv_off + 6 * B_KV_WIDTH
    mg_off = gate_off + 3 * B_HEADS
    bf = list(range(0, 3 * A_WIDTH + B_WIDTH))
    for first in (2, 4):
        for g in range(B_KV_HEADS):
            for i in (first, first + 1):
                base = kv_off + i * B_KV_WIDTH + g * HEAD_DIM
                bf += list(range(base, base + HEAD_DIM))
    f32 = list(range(mg_off, mg_off + 2 * D_MODEL))
    f32 += list(range(kv_off, kv_off + 2 * B_KV_WIDTH))
    per_head = 3 * B_GQA
    for g in range(B_KV_HEADS):
        f32 += list(range(gate_off + g * per_head, gate_off + (g + 1) * per_head))
        f32 += [-1] * (LANES - per_head)
    assert len(bf) == BF_COLS and len(f32) == F_COLS
    scale = np.ones((BF_COLS,), np.float32)
    scale[0:A_WIDTH] = ATTN_SCALE
    scale[3 * A_WIDTH:3 * A_WIDTH + B_WIDTH] = ATTN_SCALE
    return np.asarray(bf), np.asarray(f32), scale


def _permute_columns(w, cols):
    taken = jnp.take(w, jnp.asarray(np.maximum(cols, 0)), axis=1)
    return jnp.where(jnp.asarray(cols >= 0)[None, :], taken, 0.0)


def _proj_kernel(x_ref, w_ref, o_ref, *, chunk):
    xb = x_ref[...].astype(jnp.bfloat16)
    n = o_ref.shape[1]
    for c0 in range(0, n, chunk):
        o_ref[:, c0:c0 + chunk] = jnp.dot(
            xb, w_ref[:, c0:c0 + chunk], preferred_element_type=jnp.float32).astype(o_ref.dtype)


def _project(x2d, w, out_dtype, tm=512, chunk=256):
    t, d = x2d.shape
    n = w.shape[1]
    assert t % tm == 0 and n % chunk == 0
    return pl.pallas_call(
        functools.partial(_proj_kernel, chunk=chunk),
        grid=(t // tm,),
        in_specs=[pl.BlockSpec((tm, d), lambda i: (i, 0)),
                  pl.BlockSpec((d, n), lambda i: (0, 0))],
        out_specs=pl.BlockSpec((tm, n), lambda i: (i, 0)),
        out_shape=jax.ShapeDtypeStruct((t, n), out_dtype),
        compiler_params=_params(("parallel",)),
    )(x2d, w)


def _dilated_kernel(*refs, nbs, dils, slopes):
    ins, outs = refs[:5 * N_DIL], refs[5 * N_DIL:]
    i = pl.program_id(1)
    qi = lax.broadcasted_iota(jnp.int32, (Q_BLOCK, Q_BLOCK), 0)
    kj = lax.broadcasted_iota(jnp.int32, (Q_BLOCK, Q_BLOCK), 1)
    d_prev = Q_BLOCK + qi - kj
    d_cur = qi - kj
    dn = (((1,), (1,)), ((), ()))
    for g in range(N_DIL):
        q_ref, kp_ref, kc_ref, vp_ref, vc_ref = ins[5 * g:5 * g + 5]
        o_ref, lse_ref = outs[2 * g:2 * g + 2]
        m_prev = d_prev <= jnp.where((i % nbs[g]) > 0, Q_BLOCK, 0)
        m_cur = d_cur >= 0
        for h in range(A_HEADS_PER_GROUP):
            hs = slice(h * HEAD_DIM, (h + 1) * HEAD_DIM)
            q = q_ref[0, :, hs]
            slope = slopes[g][h] * dils[g]
            s_p = lax.dot_general(q, kp_ref[0, :, hs], dn, preferred_element_type=jnp.float32)
            s_c = lax.dot_general(q, kc_ref[0, :, hs], dn, preferred_element_type=jnp.float32)
            s_p = jnp.where(m_prev, s_p - slope * d_prev.astype(jnp.float32), -jnp.inf)
            s_c = jnp.where(m_cur, s_c - slope * d_cur.astype(jnp.float32), -jnp.inf)
            m = jnp.maximum(jnp.max(s_p, axis=1, keepdims=True), jnp.max(s_c, axis=1, keepdims=True))
            p_p = jnp.exp(s_p - m)
            p_c = jnp.exp(s_c - m)
            l = jnp.sum(p_p, axis=1, keepdims=True) + jnp.sum(p_c, axis=1, keepdims=True)
            acc = jnp.dot(p_p.astype(jnp.bfloat16), vp_ref[0, :, hs], preferred_element_type=jnp.float32)
            acc += jnp.dot(p_c.astype(jnp.bfloat16), vc_ref[0, :, hs], preferred_element_type=jnp.float32)
            o_ref[0, :, hs] = acc / l
            lse_ref[0, :, hs] = jnp.broadcast_to(m + jnp.log(l), (Q_BLOCK, HEAD_DIM))


def _dilated(u_bf, batch, seq):
    n_col = BF_COLS // A_OUT
    steps = seq // Q_BLOCK
    in_arrays, in_specs, out_specs, out_shapes, nbs, dils = [], [], [], [], [], []
    for g, (_, dil) in enumerate(DIL_GROUPS):
        length = seq // dil
        assert length % Q_BLOCK == 0
        nb = length // Q_BLOCK
        nbs.append(nb)
        dils.append(dil)
        view = u_bf.reshape(batch, length, dil * BF_COLS)

        def cur(b, i, nb=nb, col=0):
            return (b, i % nb, (i // nb) * n_col + col)

        def prev(b, i, nb=nb, col=0):
            return (b, jnp.maximum(i % nb - 1, 0), (i // nb) * n_col + col)

        blk = (1, Q_BLOCK, A_OUT)
        for fn, col in ((cur, BF_Q_A + g), (prev, BF_K_A + g), (cur, BF_K_A + g),
                        (prev, BF_V_A + g), (cur, BF_V_A + g)):
            in_arrays.append(view)
            in_specs.append(pl.BlockSpec(blk, functools.partial(fn, col=col)))

        def out_map(b, i, nb=nb):
            return (b, i % nb, i // nb)

        for _ in range(2):
            out_specs.append(pl.BlockSpec(blk, out_map))
            out_shapes.append(jax.ShapeDtypeStruct((batch, length, dil * A_OUT), jnp.float32))
    slopes = np.exp2(-8.0 * np.arange(1, A_HEADS + 1, dtype=np.float64) / A_HEADS).reshape(N_DIL, A_HEADS_PER_GROUP)
    outs = pl.pallas_call(
        functools.partial(_dilated_kernel, nbs=tuple(nbs), dils=tuple(dils),
                          slopes=tuple(tuple(float(s) for s in row) for row in slopes)),
        grid=(batch, steps),
        in_specs=in_specs,
        out_specs=out_specs,
        out_shape=out_shapes,
        compiler_params=_params(("parallel", "parallel")),
    )(*in_arrays)
    return [o.reshape(batch * seq, A_OUT) for o in outs]


def _compress_kernel(x_ref, pos_ref, w1_ref, w2_ref, o_ref):
    half = CMP_STRIDE * HEAD_DIM
    x = x_ref[...]
    n_chunk = x.shape[0]
    lo = (x + pos_ref[0]).astype(jnp.bfloat16)
    hi = (x + pos_ref[1]).astype(jnp.bfloat16)
    h_lo = jnp.dot(lo, w1_ref[0:half, :], preferred_element_type=jnp.float32)
    h_hi = jnp.dot(hi, w1_ref[half:2 * half, :], preferred_element_type=jnp.float32)
    h = h_lo + pltpu.roll(h_hi, n_chunk - 1, 0)
    act = jax.nn.gelu(h, approximate=True)
    o_ref[...] = jnp.dot(act.astype(jnp.bfloat16), w2_ref[...],
                         preferred_element_type=jnp.float32).astype(o_ref.dtype)


def _compress(u_f32, batch, seq, pos, w1, w2):
    n_chunk = seq // CMP_STRIDE
    half = CMP_STRIDE * HEAD_DIM
    ckv = u_f32[:, F_CMP:F_CMP + 2 * B_KV_WIDTH]
    ckv = ckv.reshape(batch, n_chunk, CMP_STRIDE, 2, B_KV_HEADS, HEAD_DIM)
    ckv = ckv.transpose(3, 0, 4, 1, 2, 5).reshape(2, batch, B_KV_HEADS, n_chunk, half)
    return pl.pallas_call(
        _compress_kernel,
        grid=(2, batch, B_KV_HEADS),
        in_specs=[pl.BlockSpec((None, None, None, n_chunk, half), lambda s, b, g: (s, b, g, 0, 0)),
                  pl.BlockSpec((None, 2, 1, half), lambda s, b, g: (s, 0, 0, 0)),
                  pl.BlockSpec((None, 2 * half, CMP_HIDDEN), lambda s, b, g: (s, 0, 0)),
                  pl.BlockSpec((None, CMP_HIDDEN, HEAD_DIM), lambda s, b, g: (s, 0, 0))],
        out_specs=pl.BlockSpec((None, None, None, n_chunk, HEAD_DIM), lambda s, b, g: (s, b, g, 0, 0)),
        out_shape=jax.ShapeDtypeStruct((2, batch, B_KV_HEADS, n_chunk, HEAD_DIM), jnp.bfloat16),
        compiler_params=_params(("parallel", "parallel", "parallel")),
    )(ckv, pos, w1, w2)


NEG = -2.0 ** 60
N_FEAT = HEAD_DIM


def _nsa_kernel(q_ref, kc_ref, vc_ref, skv_ref, wkv_ref, gate_ref, ovt_ref, kconst_ref, slope_ref, o_ref,
                kaug_sc, vaug_sc, kwin_sc, vwin_sc, m_sc, acc_sc, *, n_sel, tk):
    n = pl.program_id(2)
    t0 = n * Q_BLOCK
    rows = B_GQA * Q_BLOCK
    seq = skv_ref.shape[0]
    dn = (((1,), (1,)), ((), ()))
    f32, bf16 = jnp.float32, jnp.bfloat16
    E = HEAD_DIM

    @pl.when(n == 0)
    def _build_keys():
        kaug_sc[:, 0:LANES] = kconst_ref[:, 0:LANES]
        kaug_sc[:, LANES:LANES + E] = skv_ref[:, 0:E]
        kaug_sc[:, LANES + E:2 * LANES] = kconst_ref[:, LANES:LANES + N_FEAT]
        vaug_sc[:, 0:E] = skv_ref[:, E:2 * E]
        vaug_sc[:, E:LANES] = kconst_ref[:, LANES + N_FEAT:2 * LANES]
        kwin_sc[:, 0:E] = wkv_ref[:, 0:E]
        kwin_sc[:, E:LANES] = kconst_ref[:, LANES:LANES + N_FEAT]
        vwin_sc[:, 0:E] = wkv_ref[:, E:2 * E]
        vwin_sc[:, E:LANES] = kconst_ref[:, LANES + N_FEAT:2 * LANES]

    q = q_ref[...]
    qs = jnp.concatenate([q[:, r * E:(r + 1) * E] for r in range(B_GQA)], axis=0)
    row = lax.broadcasted_iota(jnp.int32, (rows, 1), 0)
    within = row & (Q_BLOCK - 1)
    tq = t0 + within
    slope = slope_ref[...]
    lane = lax.broadcasted_iota(jnp.int32, (rows, N_FEAT), 1)
    feat = jnp.where(lane == 0, slope * SLC_LEN,
           jnp.where(lane == 1, slope,
           jnp.where(lane == 2, -slope * t0.astype(f32),
           jnp.where(lane == 3, -slope * within.astype(f32), 0.0))))
    q_feat = jnp.concatenate([qs, feat.astype(bf16)], axis=1)

    n_chunk = kc_ref.shape[0]
    s = lax.dot_general(qs, kc_ref[...], dn, preferred_element_type=f32)
    cmp_end = lax.broadcasted_iota(jnp.int32, (1, n_chunk), 1) * CMP_STRIDE + (CMP_LEN - 1)
    s = jnp.where(cmp_end <= tq, s, -jnp.inf)
    m = jnp.max(s, axis=1, keepdims=True)
    m = jnp.where(m == -jnp.inf, 0.0, m)
    p = jnp.exp(s - m)
    l = jnp.maximum(jnp.sum(p, axis=1, keepdims=True), TINY)
    p_c = (p / l).astype(bf16)
    o_c = jnp.dot(p_c, vc_ref[...], preferred_element_type=f32)

    imp = lax.dot_general(ovt_ref[...], p_c[0:Q_BLOCK], dn, preferred_element_type=f32)
    for r in range(1, B_GQA):
        imp += lax.dot_general(ovt_ref[...], p_c[r * Q_BLOCK:(r + 1) * Q_BLOCK], dn, preferred_element_type=f32)
    blk = lax.broadcasted_iota(jnp.int32, (LANES, Q_BLOCK), 0)
    t = t0 + lax.broadcasted_iota(jnp.int32, (LANES, Q_BLOCK), 1)
    cur = t >> 6
    forced = (blk == 0) | (blk == cur) | (blk == cur - 1)
    score = jnp.where(blk * SLC_LEN <= t, imp + jnp.where(forced, FORCE_BONUS, 0.0), -jnp.inf)

    def pick(_, carry):
        work, sel = carry
        best = jnp.max(work, axis=0, keepdims=True)
        idx = jnp.min(jnp.where(work == best, blk, LANES), axis=0, keepdims=True)
        hit = blk == idx
        finite = jnp.where(best > -jnp.inf, 1.0, 0.0)
        sel = jnp.where(hit, jnp.maximum(sel, finite), sel)
        return jnp.where(hit, -jnp.inf, work), sel

    _, sel = lax.fori_loop(0, n_sel, pick, (score, jnp.zeros((LANES, Q_BLOCK), f32)))
    bias = jnp.where(sel.T > 0.5, 0.0, NEG).astype(bf16)
    q_aug = jnp.concatenate([jnp.concatenate([bias] * B_GQA, axis=0), q_feat], axis=1)

    m_sc[...] = jnp.full(m_sc.shape, -jnp.inf, f32)
    acc_sc[...] = jnp.zeros(acc_sc.shape, f32)

    def tile(j, diagonal):
        k0 = pl.multiple_of(j * tk, tk)
        s = lax.dot_general(q_aug, kaug_sc[pl.ds(k0, tk), :], dn, preferred_element_type=f32)
        if diagonal:
            s = jnp.where(k0 + lax.broadcasted_iota(jnp.int32, (1, tk), 1) <= tq, s, -jnp.inf)
        m_old = m_sc[...]
        m_new = jnp.maximum(m_old, jnp.broadcast_to(jnp.max(s, axis=1, keepdims=True), m_old.shape))
        p = jnp.exp(s - jnp.concatenate([m_new] * (tk // LANES), axis=1)).astype(bf16)
        pv = jnp.dot(p, vaug_sc[pl.ds(k0, tk), :], preferred_element_type=f32)
        acc_sc[...] = jnp.exp(m_old - m_new) * acc_sc[...] + pv
        m_sc[...] = m_new

    j_last = t0 // tk

    def full_tile(j, carry):
        tile(j, False)
        return carry

    lax.fori_loop(0, j_last, full_tile, 0)
    tile(j_last, True)
    acc = acc_sc[...]
    o_s = acc[:, 0:E] / jnp.maximum(acc[:, E:E + 1], TINY)

    span = min(WIN + Q_BLOCK, seq)
    start = pl.multiple_of(jnp.maximum(t0 + Q_BLOCK - span, 0), Q_BLOCK)
    s = lax.dot_general(q_feat, kwin_sc[pl.ds(start, span), :], dn, preferred_element_type=f32)
    dw = tq - (start + lax.broadcasted_iota(jnp.int32, (1, span), 1))
    s = jnp.where((dw >= 0) & (dw < WIN), s, -jnp.inf)
    m = jnp.max(s, axis=1, keepdims=True)
    p = jnp.exp(s - m).astype(bf16)
    acc = jnp.dot(p, vwin_sc[pl.ds(start, span), :], preferred_element_type=f32)
    o_w = acc[:, 0:E] / jnp.maximum(acc[:, E:E + 1], TINY)

    gates = 1.0 / (1.0 + jnp.exp(-gate_ref[...]))
    for r in range(B_GQA):
        rs = slice(r * Q_BLOCK, (r + 1) * Q_BLOCK)
        out = (gates[:, 3 * r:3 * r + 1] * o_c[rs] + gates[:, 3 * r + 1:3 * r + 2] * o_s[rs]
               + gates[:, 3 * r + 2:3 * r + 3] * o_w[rs])
        o_ref[:, r * E:(r + 1) * E] = out.astype(o_ref.dtype)


def _nsa(u_bf, u_f32, cmp_kv, batch, seq):
    nq = seq // Q_BLOCK
    n_chunk = seq // CMP_STRIDE
    n_cmp = n_chunk - CMP_LEN // CMP_STRIDE + 1
    n_slc = seq // SLC_LEN
    n_sel = min(SLC_TOPK, n_slc)
    tk = min(512, seq)
    assert n_slc <= LANES and B_HEADS == 8 and seq % tk == 0
    c_start = np.arange(n_chunk) * CMP_STRIDE
    s_start = np.arange(LANES) * SLC_LEN
    overlap_t = ((c_start[None, :] < s_start[:, None] + SLC_LEN) & (c_start[None, :] + CMP_LEN > s_start[:, None])
                 & (np.arange(n_chunk)[None, :] < n_cmp) & (np.arange(LANES)[:, None] < n_slc))
    pos = np.arange(seq)
    kconst = np.zeros((seq, 2 * LANES), np.float32)
    kconst[pos, pos // SLC_LEN] = 1.0
    kconst[:, LANES + 0] = pos // SLC_LEN
    kconst[:, LANES + 1] = pos % SLC_LEN
    kconst[:, LANES + 2] = 1.0
    kconst[:, LANES + 3] = 1.0
    kconst[:, LANES + N_FEAT] = 1.0
    rows = B_GQA * Q_BLOCK
    slopes = jnp.exp2(-8.0 * jnp.arange(1, B_HEADS + 1, dtype=jnp.float32) / B_HEADS)
    slopes = jnp.repeat(slopes.reshape(B_KV_HEADS, B_GQA), Q_BLOCK, axis=1).reshape(B_KV_HEADS, rows, 1)
    kv_rows = pl.BlockSpec((seq, LANES), lambda b, g, n: (b, BF_SLC + g))
    win_rows = pl.BlockSpec((seq, LANES), lambda b, g, n: (b, BF_WIN + g))
    cmp_spec = lambda s: pl.BlockSpec((None, None, None, n_chunk, HEAD_DIM), lambda b, g, n: (s, b, g, 0, 0))
    return pl.pallas_call(
        functools.partial(_nsa_kernel, n_sel=n_sel, tk=tk),
        grid=(batch, B_KV_HEADS, nq),
        in_specs=[pl.BlockSpec((Q_BLOCK, B_GQA * HEAD_DIM), lambda b, g, n: (b * nq + n, BF_Q_B + g)),
                  cmp_spec(0), cmp_spec(1), kv_rows, win_rows,
                  pl.BlockSpec((Q_BLOCK, LANES), lambda b, g, n: (b * nq + n, F_GATE + g)),
                  pl.BlockSpec((LANES, n_chunk), lambda b, g, n: (0, 0)),
                  pl.BlockSpec((seq, 2 * LANES), lambda b, g, n: (0, 0)),
                  pl.BlockSpec((None, rows, 1), lambda b, g, n: (g, 0, 0))],
        out_specs=pl.BlockSpec((Q_BLOCK, B_GQA * HEAD_DIM), lambda b, g, n: (b * nq + n, g)),
        out_shape=jax.ShapeDtypeStruct((batch * seq, B_WIDTH), jnp.bfloat16),
        scratch_shapes=[pltpu.VMEM((seq, 2 * LANES), jnp.bfloat16), pltpu.VMEM((seq, LANES), jnp.bfloat16),
                        pltpu.VMEM((seq, LANES), jnp.bfloat16), pltpu.VMEM((seq, LANES), jnp.bfloat16),
                        pltpu.VMEM((rows, LANES), jnp.float32), pltpu.VMEM((rows, LANES), jnp.float32)],
        compiler_params=_params(("arbitrary", "arbitrary", "arbitrary")),
    )(u_bf, cmp_kv, cmp_kv, u_bf, u_bf, u_f32, jnp.asarray(overlap_t, jnp.bfloat16), jnp.asarray(kconst, jnp.bfloat16),
      slopes)


def _layer_norm(v, gain, bias):
    mu = jnp.mean(v, axis=1, keepdims=True)
    c = v - mu
    var = jnp.mean(c * c, axis=1, keepdims=True)
    return c * lax.rsqrt(var + LN_EPS) * gain + bias


def _sigmoid(v):
    return 1.0 / (1.0 + jnp.exp(-v))


def _merge_kernel(o0, o1, o2, l0, l1, l2, yb_ref, mga_ref, mgb_ref, x_ref, wa_ref, wb_ref, wo_ref,
                  g_ref, b_ref, wr_hi_ref, wr_lo_ref, br_ref, h_ref, rw_ref, re_ref):
    f32 = jnp.float32
    lse = [l0[...], l1[...], l2[...]]
    top = jnp.maximum(jnp.maximum(lse[0], lse[1]), lse[2])
    e = [jnp.exp(v - top) for v in lse]
    y_a = (e[0] * o0[...] + e[1] * o1[...] + e[2] * o2[...]) / (e[0] + e[1] + e[2])
    br_a = jnp.dot(y_a.astype(jnp.bfloat16), wa_ref[...], preferred_element_type=f32)
    br_b = jnp.dot(yb_ref[...], wb_ref[...], preferred_element_type=f32)
    merged = _sigmoid(mga_ref[...]) * br_a + _sigmoid(mgb_ref[...]) * br_b
    mix = jnp.dot(merged.astype(jnp.bfloat16), wo_ref[...], preferred_element_type=f32)
    h = _layer_norm(ALPHA * x_ref[...] + mix, g_ref[...], b_ref[...])
    h_ref[...] = h

    h_hi = h.astype(jnp.bfloat16)
    h_lo = (h - h_hi.astype(f32)).astype(jnp.bfloat16)
    logits = (jnp.dot(h_hi, wr_hi_ref[...], preferred_element_type=f32)
              + jnp.dot(h_lo, wr_hi_ref[...], preferred_element_type=f32)
              + jnp.dot(h_hi, wr_lo_ref[...], preferred_element_type=f32)) + br_ref[...]
    lane = lax.broadcasted_iota(jnp.int32, logits.shape, 1)
    coarse = jnp.where(lane < N_GROUPS, logits, -jnp.inf)
    c_max = jnp.max(coarse, axis=1, keepdims=True)
    grp = jnp.min(jnp.where(coarse == c_max, lane, LANES), axis=1, keepdims=True)
    p_grp = 1.0 / jnp.sum(jnp.exp(coarse - c_max), axis=1, keepdims=True)
    lo_lane = N_GROUPS + EXP_PER_GROUP * grp
    fine = jnp.where((lane >= lo_lane) & (lane < lo_lane + EXP_PER_GROUP), logits, -jnp.inf)
    v1 = jnp.max(fine, axis=1, keepdims=True)
    i1 = jnp.min(jnp.where(fine == v1, lane, LANES), axis=1, keepdims=True)
    fine = jnp.where(lane == i1, -jnp.inf, fine)
    v2 = jnp.max(fine, axis=1, keepdims=True)
    i2 = jnp.min(jnp.where(fine == v2, lane, LANES), axis=1, keepdims=True)
    e2 = jnp.exp(v2 - v1)
    w1 = p_grp / (1.0 + e2)
    w2 = p_grp * e2 / (1.0 + e2)
    rw_ref[...] = jnp.where(lane == 0, w1, jnp.where(lane == 1, w2, 0.0))
    re_ref[...] = jnp.where(lane == 0, i1 - N_GROUPS, jnp.where(lane == 1, i2 - N_GROUPS, 0))


def _merge(dil_outs, y_b, u_f32, x2d, wa, wb, wo, ln_g, ln_b, wr_hi, wr_lo, b_r, tm=256):
    t = x2d.shape[0]
    assert t % tm == 0
    row = lambda w: pl.BlockSpec((tm, w), lambda i: (i, 0))
    full = lambda a: pl.BlockSpec(a.shape, lambda i: (0,) * a.ndim)
    o0, l0, o1, l1, o2, l2 = dil_outs
    return pl.pallas_call(
        _merge_kernel,
        grid=(t // tm,),
        in_specs=[row(A_OUT)] * 6 + [row(B_WIDTH),
                  pl.BlockSpec((tm, D_MODEL), lambda i: (i, F_MGATE)),
                  pl.BlockSpec((tm, D_MODEL), lambda i: (i, F_MGATE + 1)),
                  row(D_MODEL), full(wa), full(wb), full(wo), full(ln_g), full(ln_b),
                  full(wr_hi), full(wr_lo), full(b_r)],
        out_specs=[row(D_MODEL), row(LANES), row(LANES)],
        out_shape=[jax.ShapeDtypeStruct((t, D_MODEL), jnp.float32),
                   jax.ShapeDtypeStruct((t, LANES), jnp.float32),
                   jax.ShapeDtypeStruct((t, LANES), jnp.int32)],
        compiler_params=_params(("parallel",)),
    )(o0, o1, o2, l0, l1, l2, y_b, u_f32, u_f32, x2d, wa, wb, wo, ln_g, ln_b, wr_hi, wr_lo, b_r)


def _gather_rows(idx_ref, count, src_hbm, dst, sem, src_of):
    def row_copy(r):
        src_row = idx_ref[0, 0, src_of(r)]
        return pltpu.make_async_copy(src_hbm.at[pl.ds(src_row, 1), :], dst.at[pl.ds(r, 1), :], sem)

    def start(r, carry):
        row_copy(r).start()
        return carry

    def wait(r, carry):
        row_copy(r).wait()
        return carry

    lax.fori_loop(0, count, start, 0)
    lax.fori_loop(0, count, wait, 0)


def _expert_kernel(blk_e_ref, tok_ref, h_hbm, wgu_ref, wd_ref, y_ref, xbuf, sem):
    bm = xbuf.shape[0]
    _gather_rows(tok_ref, bm, h_hbm, xbuf, sem, lambda r: r)
    xb = xbuf[...].astype(jnp.bfloat16)
    gu = jnp.dot(xb, wgu_ref[...], preferred_element_type=jnp.float32)
    gate, up = gu[:, :D_EXPERT], gu[:, D_EXPERT:]
    act = (gate * _sigmoid(gate)) * up
    y_ref[...] = jnp.dot(act.astype(jnp.bfloat16), wd_ref[...], preferred_element_type=jnp.float32)


def _experts(h, tok_buf, blk_e, wgu, wd, bm):
    p = tok_buf.shape[0]
    nb = p // bm
    return pl.pallas_call(
        _expert_kernel,
        grid_spec=pltpu.PrefetchScalarGridSpec(
            num_scalar_prefetch=1,
            grid=(nb,),
            in_specs=[pl.BlockSpec((1, 1, bm), lambda i, e: (i, 0, 0), memory_space=pltpu.SMEM),
                      pl.BlockSpec(memory_space=pl.ANY),
                      pl.BlockSpec((None, D_MODEL, 2 * D_EXPERT), lambda i, e: (e[i], 0, 0)),
                      pl.BlockSpec((None, D_EXPERT, D_MODEL), lambda i, e: (e[i], 0, 0))],
            out_specs=pl.BlockSpec((bm, D_MODEL), lambda i, e: (i, 0)),
            scratch_shapes=[pltpu.VMEM((bm, D_MODEL), jnp.float32), pltpu.SemaphoreType.DMA(())],
        ),
        out_shape=jax.ShapeDtypeStruct((p, D_MODEL), jnp.float32),
        compiler_params=_params(("arbitrary",)),
    )(blk_e, tok_buf.reshape(nb, 1, bm), h, wgu, wd)


def _combine_kernel(dest_ref, y_hbm, h_ref, rw_ref, g_ref, b_ref, o_ref, ybuf, sem):
    tm = h_ref.shape[0]
    _gather_rows(dest_ref, tm, y_hbm, ybuf.at[0], sem, lambda r: 2 * r)
    _gather_rows(dest_ref, tm, y_hbm, ybuf.at[1], sem, lambda r: 2 * r + 1)
    rw = rw_ref[...]
    ffn = rw[:, 0:1] * ybuf[0] + rw[:, 1:2] * ybuf[1]
    o_ref[...] = _layer_norm(ALPHA * h_ref[...] + ffn, g_ref[...], b_ref[...])


def _combine(dest, y_sorted, h, rw, ln_g, ln_b, tm=256):
    t = h.shape[0]
    assert t % tm == 0
    return pl.pallas_call(
        _combine_kernel,
        grid=(t // tm,),
        in_specs=[pl.BlockSpec((1, 1, MOE_TOP_K * tm), lambda i: (i, 0, 0), memory_space=pltpu.SMEM),
                  pl.BlockSpec(memory_space=pl.ANY),
                  pl.BlockSpec((tm, D_MODEL), lambda i: (i, 0)),
                  pl.BlockSpec((tm, LANES), lambda i: (i, 0)),
                  pl.BlockSpec((1, D_MODEL), lambda i: (0, 0)),
                  pl.BlockSpec((1, D_MODEL), lambda i: (0, 0))],
        out_specs=pl.BlockSpec((tm, D_MODEL), lambda i: (i, 0)),
        out_shape=jax.ShapeDtypeStruct((t, D_MODEL), jnp.float32),
        scratch_shapes=[pltpu.VMEM((MOE_TOP_K, tm, D_MODEL), jnp.float32), pltpu.SemaphoreType.DMA(())],
        compiler_params=_params(("arbitrary",)),
    )(dest.reshape(t // tm, 1, MOE_TOP_K * tm), y_sorted, h, rw, ln_g, ln_b)


def _dispatch(expert, bm):
    t = expert.shape[0]
    n_assign = t * MOE_TOP_K
    e_flat = expert.reshape(n_assign)
    order = jnp.argsort(e_flat)
    e_s = e_flat[order]
    counts = jnp.zeros((N_EXPERTS,), jnp.int32).at[e_flat].add(1)
    starts = jnp.cumsum(counts) - counts
    pcounts = (counts + bm - 1) // bm * bm
    pends = jnp.cumsum(pcounts)
    pstarts = pends - pcounts
    dest_s = pstarts[e_s] + (jnp.arange(n_assign, dtype=jnp.int32) - starts[e_s])
    p = -(-(n_assign + N_EXPERTS * (bm - 1)) // bm) * bm
    tok_buf = jnp.zeros((p,), jnp.int32).at[dest_s].set((order // MOE_TOP_K).astype(jnp.int32))
    dest = jnp.zeros((n_assign,), jnp.int32).at[order].set(dest_s.astype(jnp.int32))
    blk_e = jnp.minimum(jnp.searchsorted(pends, jnp.arange(p // bm, dtype=jnp.int32) * bm, side='right'),
                        N_EXPERTS - 1).astype(jnp.int32)
    return tok_buf, dest, blk_e


def _split_bf16(w):
    hi = w.astype(jnp.bfloat16)
    return hi, (w - hi.astype(jnp.float32)).astype(jnp.bfloat16)


def kernel(x, w_in, cmp_pos_k, cmp_w1_k, cmp_w2_k, cmp_pos_v, cmp_w1_v, cmp_w2_v, w_branch_a, w_branch_b, w_out, ln1_g, ln1_b, w_coarse, b_coarse, w_fine, b_fine, w_gate_up, w_down, ln2_g, ln2_b):
    batch, seq, d = x.shape
    bf16 = jnp.bfloat16
    h = x.reshape(batch * seq, d)
    cols_bf, cols_f32, scale_bf = _in_proj_columns()
    half = CMP_STRIDE * HEAD_DIM
    for l in range(DEPTH):
        u_bf = _project(h, (_permute_columns(w_in[l], cols_bf) * scale_bf).astype(bf16), bf16)
        u_f32 = _project(h, _permute_columns(w_in[l], cols_f32).astype(bf16), jnp.float32)
        dil_outs = _dilated(u_bf, batch, seq)
        pos = jnp.stack([cmp_pos_k[l], cmp_pos_v[l]]).reshape(2, 2, 1, half)
        w1 = jnp.stack([cmp_w1_k[l], cmp_w1_v[l]]).astype(bf16)
        w2 = jnp.stack([cmp_w2_k[l], cmp_w2_v[l]]).astype(bf16)
        cmp_kv = _compress(u_f32, batch, seq, pos, w1, w2)
        y_b = _nsa(u_bf, u_f32, cmp_kv, batch, seq)
        w_r = jnp.concatenate([w_coarse[l], w_fine[l].transpose(1, 0, 2).reshape(d, N_EXPERTS)], axis=1)
        w_r = jnp.pad(w_r, ((0, 0), (0, LANES - w_r.shape[1])))
        b_r = jnp.pad(jnp.concatenate([b_coarse[l], b_fine[l].reshape(N_EXPERTS)]),
                      (0, LANES - N_GROUPS - N_EXPERTS)).reshape(1, LANES)
        wr_hi, wr_lo = _split_bf16(w_r)
        h1, rw, re = _merge(dil_outs, y_b, u_f32, h, w_branch_a[l].astype(bf16), w_branch_b[l].astype(bf16),
                            w_out[l].astype(bf16), ln1_g[l].reshape(1, d), ln1_b[l].reshape(1, d),
                            wr_hi, wr_lo, b_r)
        bm = 256
        tok_buf, dest, blk_e = _dispatch(re[:, :MOE_TOP_K], bm)
        y_sorted = _experts(h1, tok_buf, blk_e, w_gate_up[l].astype(bf16), w_down[l].astype(bf16), bm)
        h = _combine(dest, y_sorted, h1, rw, ln2_g[l].reshape(1, d), ln2_b[l].reshape(1, d))
    return h.reshape(batch, seq, d)
```

```python
import functools

import numpy as np
import jax
import jax.numpy as jnp
from jax import lax
from jax.experimental import pallas as pl
from jax.experimental.pallas import tpu as pltpu

D_MODEL = 1024
HEAD_DIM = 64
Q_BLOCK = 128
DIL_GROUPS = ((128, 1), (512, 4), (2048, 16))
N_DIL = len(DIL_GROUPS)
A_HEADS_PER_GROUP = 4
A_HEADS = A_HEADS_PER_GROUP * N_DIL
A_WIDTH = A_HEADS * HEAD_DIM
A_OUT = A_HEADS_PER_GROUP * HEAD_DIM
B_HEADS = 8
B_KV_HEADS = 2
B_GQA = B_HEADS // B_KV_HEADS
B_WIDTH = B_HEADS * HEAD_DIM
B_KV_WIDTH = B_KV_HEADS * HEAD_DIM
CMP_LEN = 32
CMP_STRIDE = 16
CMP_HIDDEN = 256
SLC_LEN = 64
SLC_TOPK = 16
WIN = 512
N_GROUPS = 4
EXP_PER_GROUP = 8
N_EXPERTS = N_GROUPS * EXP_PER_GROUP
D_EXPERT = 512
MOE_TOP_K = 2
DEPTH = 1
ALPHA = (2.0 * DEPTH) ** 0.25
LN_EPS = 1e-5
FORCE_BONUS = 1e4
TINY = 1e-30
ATTN_SCALE = HEAD_DIM ** -0.5

LANES = 128
VMEM_LIMIT = 48 * 1024 * 1024

QKV_COLS = 3 * A_OUT
NAT_Q_B = QKV_COLS // A_OUT
NAT_SLC = (QKV_COLS + B_WIDTH) // LANES
NAT_WIN = NAT_SLC + B_KV_HEADS
NAT_COLS = QKV_COLS + B_WIDTH + 4 * B_KV_WIDTH
BF_COLS = NAT_COLS + (N_DIL - 1) * QKV_COLS
F_MGATE = 0
F_CMP = 2 * D_MODEL
F_GATE = (2 * D_MODEL + 2 * B_KV_WIDTH) // LANES
F_COLS = 2 * D_MODEL + 2 * B_KV_WIDTH + B_KV_HEADS * LANES


def _params(semantics):
    return pltpu.CompilerParams(dimension_semantics=semantics, vmem_limit_bytes=VMEM_LIMIT)


def _in_proj_columns():
    kv_off = 3 * A_WIDTH + B_WIDTH
    gate_off = kv_off + 6 * B_KV_WIDTH
    mg_off = gate_off + 3 * B_HEADS
    def qkv_cols(g):
        return [part * A_WIDTH + g * A_OUT + c for part in range(3) for c in range(A_OUT)]

    bf = qkv_cols(0) + list(range(3 * A_WIDTH, 3 * A_WIDTH + B_WIDTH))
    for first in (2, 4):
        for g in range(B_KV_HEADS):
            for i in (first, first + 1):
                base = kv_off + i * B_KV_WIDTH + g * HEAD_DIM
                bf += list(range(base, base + HEAD_DIM))
    for g in range(1, N_DIL):
        bf += qkv_cols(g)
    f32 = list(range(mg_off, mg_off + 2 * D_MODEL))
    f32 += list(range(kv_off, kv_off + 2 * B_KV_WIDTH))
    per_head = 3 * B_GQA
    for g in range(B_KV_HEADS):
        f32 += list(range(gate_off + g * per_head, gate_off + (g + 1) * per_head))
        f32 += [-1] * (LANES - per_head)
    assert len(bf) == BF_COLS and len(f32) == F_COLS
    bf = np.asarray(bf)
    is_q = (bf < A_WIDTH) | ((bf >= 3 * A_WIDTH) & (bf < 3 * A_WIDTH + B_WIDTH))
    scale = np.where(is_q, ATTN_SCALE, 1.0).astype(np.float32)
    return bf, np.asarray(f32), scale


def _permute_columns(w, cols):
    taken = jnp.take(w, jnp.asarray(np.maximum(cols, 0)), axis=1)
    return jnp.where(jnp.asarray(cols >= 0)[None, :], taken, 0.0)


def _proj_kernel(x_ref, w_ref, o_ref, *, chunk):
    xb = x_ref[...].astype(jnp.bfloat16)
    n = o_ref.shape[1]
    for c0 in range(0, n, chunk):
        o_ref[:, c0:c0 + chunk] = jnp.dot(
            xb, w_ref[:, c0:c0 + chunk], preferred_element_type=jnp.float32).astype(o_ref.dtype)


def _project(x2d, w, out_dtype, tm=512, chunk=256):
    t, d = x2d.shape
    n = w.shape[1]
    assert t % tm == 0 and n % chunk == 0
    return pl.pallas_call(
        functools.partial(_proj_kernel, chunk=chunk),
        grid=(t // tm,),
        in_specs=[pl.BlockSpec((tm, d), lambda i: (i, 0)),
                  pl.BlockSpec((d, n), lambda i: (0, 0))],
        out_specs=pl.BlockSpec((tm, n), lambda i: (i, 0)),
        out_shape=jax.ShapeDtypeStruct((t, n), out_dtype),
        compiler_params=_params(("parallel",)),
    )(x2d, w)


def _proj_bf_kernel(x_ref, w_ref, nat_ref, *rest, dils):
    group_refs, stage = rest[:-1], rest[-1]
    xb = x_ref[...].astype(jnp.bfloat16)
    tm = x_ref.shape[0]
    for c0 in range(0, NAT_COLS, A_OUT):
        nat_ref[:, c0:c0 + A_OUT] = jnp.dot(
            xb, w_ref[:, c0:c0 + A_OUT], preferred_element_type=jnp.float32).astype(nat_ref.dtype)
    col = NAT_COLS
    for o_ref, dil in zip(group_refs, dils):
        for part in range(3):
            res = jnp.dot(xb, w_ref[:, col:col + A_OUT], preferred_element_type=jnp.float32)
            for j in range(A_OUT // LANES):
                stage[j] = res[:, j * LANES:(j + 1) * LANES]
                c0 = part * A_OUT + j * LANES
                for r in range(dil):
                    o_ref[r, :, c0:c0 + LANES] = stage[j, pl.ds(r, tm // dil, stride=dil), :].astype(o_ref.dtype)
            col += A_OUT


def _project_bf(x2d, w, batch, seq, tm=512):
    t, d = x2d.shape
    dils = tuple(dil for _, dil in DIL_GROUPS[1:])
    assert seq % tm == 0 and all(tm % (dil * 16) == 0 for dil in dils) and w.shape[1] == BF_COLS
    per_batch = seq // tm
    out_specs = [pl.BlockSpec((tm, NAT_COLS), lambda i: (i, 0))]
    out_shapes = [jax.ShapeDtypeStruct((t, NAT_COLS), jnp.bfloat16)]
    for dil in dils:
        out_specs.append(pl.BlockSpec((None, dil, tm // dil, QKV_COLS),
                                      lambda i: (i // per_batch, 0, i % per_batch, 0)))
        out_shapes.append(jax.ShapeDtypeStruct((batch, dil, seq // dil, QKV_COLS), jnp.bfloat16))
    return pl.pallas_call(
        functools.partial(_proj_bf_kernel, dils=dils),
        grid=(t // tm,),
        in_specs=[pl.BlockSpec((tm, d), lambda i: (i, 0)),
                  pl.BlockSpec((d, BF_COLS), lambda i: (0, 0))],
        out_specs=out_specs,
        out_shape=out_shapes,
        scratch_shapes=[pltpu.VMEM((A_OUT // LANES, tm, LANES), jnp.float32)],
        compiler_params=_params(("parallel",)),
    )(x2d, w)


def _dilated_kernel(*refs, nbs):
    bias_ref, ins, outs = refs[0], refs[1:1 + 5 * N_DIL], refs[1 + 5 * N_DIL:]
    i = pl.program_id(1)
    dn = (((1,), (1,)), ((), ()))
    for g in range(N_DIL):
        q_ref, kp_ref, kc_ref, vp_ref, vc_ref = ins[5 * g:5 * g + 5]
        o_ref, lse_ref = outs[2 * g:2 * g + 2]
        no_prev = jnp.where((i % nbs[g]) > 0, 0.0, -jnp.inf)
        for h in range(A_HEADS_PER_GROUP):
            hs = slice(h * HEAD_DIM, (h + 1) * HEAD_DIM)
            q = q_ref[:, hs]
            head = g * A_HEADS_PER_GROUP + h
            s_p = lax.dot_general(q, kp_ref[:, hs], dn, preferred_element_type=jnp.float32)
            s_c = lax.dot_general(q, kc_ref[:, hs], dn, preferred_element_type=jnp.float32)
            s_p = s_p + bias_ref[head, :, 0:Q_BLOCK] + no_prev
            s_c = s_c + bias_ref[head, :, Q_BLOCK:2 * Q_BLOCK]
            m = jnp.maximum(jnp.max(s_p, axis=1, keepdims=True), jnp.max(s_c, axis=1, keepdims=True))
            p_p = jnp.exp(s_p - m)
            p_c = jnp.exp(s_c - m)
            l = jnp.sum(p_p, axis=1, keepdims=True) + jnp.sum(p_c, axis=1, keepdims=True)
            acc = jnp.dot(p_p.astype(jnp.bfloat16), vp_ref[:, hs], preferred_element_type=jnp.float32)
            acc += jnp.dot(p_c.astype(jnp.bfloat16), vc_ref[:, hs], preferred_element_type=jnp.float32)
            o_ref[:, hs] = acc / l
            lse_ref[:, hs] = jnp.broadcast_to(m + jnp.log(l), (Q_BLOCK, HEAD_DIM))


def _dilated_bias():
    qi = Q_BLOCK + jnp.arange(Q_BLOCK)
    kj = jnp.arange(2 * Q_BLOCK)
    delta = qi[:, None] - kj[None, :]
    slopes = jnp.exp2(-8.0 * jnp.arange(1, A_HEADS + 1, dtype=jnp.float32) / A_HEADS)
    tables = []
    for g, (window, dil) in enumerate(DIL_GROUPS):
        on_band = (delta >= 0) & (delta <= window // dil)
        dist = (delta * dil).astype(jnp.float32)
        for h in range(A_HEADS_PER_GROUP):
            tables.append(jnp.where(on_band, -slopes[g * A_HEADS_PER_GROUP + h] * dist, -jnp.inf))
    return jnp.stack(tables)


def _dilated(u_nat, groups, batch, seq):
    steps = seq // Q_BLOCK
    arrays = [u_nat.reshape(batch, 1, seq, NAT_COLS)] + list(groups)
    bias = _dilated_bias()
    in_arrays, in_specs = [bias], [pl.BlockSpec(bias.shape, lambda b, i: (0, 0, 0))]
    out_specs, out_shapes, nbs = [], [], []
    blk = (None, None, Q_BLOCK, A_OUT)
    for g, (window, dil) in enumerate(DIL_GROUPS):
        length = seq // dil
        assert length % Q_BLOCK == 0 and window // dil == Q_BLOCK
        nb = length // Q_BLOCK
        nbs.append(nb)

        def cur(b, i, nb=nb, col=0):
            return (b, i // nb, i % nb, col)

        def prev(b, i, nb=nb, col=0):
            return (b, i // nb, jnp.maximum(i % nb - 1, 0), col)

        for fn, col in ((cur, 0), (prev, 1), (cur, 1), (prev, 2), (cur, 2)):
            in_arrays.append(arrays[g])
            in_specs.append(pl.BlockSpec(blk, functools.partial(fn, col=col)))
        for _ in range(2):
            out_specs.append(pl.BlockSpec(blk, functools.partial(cur, col=0)))
            out_shapes.append(jax.ShapeDtypeStruct((batch, dil, length, A_OUT), jnp.float32))
    return pl.pallas_call(
        functools.partial(_dilated_kernel, nbs=tuple(nbs)),
        grid=(batch, steps),
        in_specs=in_specs,
        out_specs=out_specs,
        out_shape=out_shapes,
        compiler_params=_params(("parallel", "parallel")),
    )(*in_arrays)


def _compress_kernel(x_ref, pos_ref, w1_ref, w2_ref, o_ref):
    half = CMP_STRIDE * HEAD_DIM
    x = x_ref[...]
    n_chunk = x.shape[0]
    lo = (x + pos_ref[0]).astype(jnp.bfloat16)
    hi = (x + pos_ref[1]).astype(jnp.bfloat16)
    h_lo = jnp.dot(lo, w1_ref[0:half, :], preferred_element_type=jnp.float32)
    h_hi = jnp.dot(hi, w1_ref[half:2 * half, :], preferred_element_type=jnp.float32)
    h = h_lo + pltpu.roll(h_hi, n_chunk - 1, 0)
    act = jax.nn.gelu(h, approximate=True)
    o_ref[...] = jnp.dot(act.astype(jnp.bfloat16), w2_ref[...],
                         preferred_element_type=jnp.float32).astype(o_ref.dtype)


def _compress(u_f32, batch, seq, pos, w1, w2):
    n_chunk = seq // CMP_STRIDE
    half = CMP_STRIDE * HEAD_DIM
    ckv = u_f32[:, F_CMP:F_CMP + 2 * B_KV_WIDTH]
    ckv = ckv.reshape(batch, n_chunk, CMP_STRIDE, 2, B_KV_HEADS, HEAD_DIM)
    ckv = ckv.transpose(3, 0, 4, 1, 2, 5).reshape(2, batch, B_KV_HEADS, n_chunk, half)
    return pl.pallas_call(
        _compress_kernel,
        grid=(2, batch, B_KV_HEADS),
        in_specs=[pl.BlockSpec((None, None, None, n_chunk, half), lambda s, b, g: (s, b, g, 0, 0)),
                  pl.BlockSpec((None, 2, 1, half), lambda s, b, g: (s, 0, 0, 0)),
                  pl.BlockSpec((None, 2 * half, CMP_HIDDEN), lambda s, b, g: (s, 0, 0)),
                  pl.BlockSpec((None, CMP_HIDDEN, HEAD_DIM), lambda s, b, g: (s, 0, 0))],
        out_specs=pl.BlockSpec((None, None, None, n_chunk, HEAD_DIM), lambda s, b, g: (s, b, g, 0, 0)),
        out_shape=jax.ShapeDtypeStruct((2, batch, B_KV_HEADS, n_chunk, HEAD_DIM), jnp.bfloat16),
        compiler_params=_params(("parallel", "parallel", "parallel")),
    )(ckv, pos, w1, w2)


NEG = -2.0 ** 60
N_FEAT = HEAD_DIM


def _nsa_kernel(q_ref, kc_ref, vc_ref, skv_ref, wkv_ref, gate_ref, ovt_ref, kconst_ref, slope_ref, o_ref,
                kaug_sc, vaug_sc, kwin_sc, vwin_sc, m_sc, acc_sc, qaug_sc, sa_sc, sb_sc, *, n_sel, tk):
    n = pl.program_id(2)
    t0 = n * Q_BLOCK
    rows = B_GQA * Q_BLOCK
    seq = skv_ref.shape[0]
    dn = (((1,), (1,)), ((), ()))
    f32, bf16 = jnp.float32, jnp.bfloat16
    E = HEAD_DIM

    @pl.when(n == 0)
    def _build_keys():
        kaug_sc[:, 0:LANES] = kconst_ref[:, 0:LANES]
        kaug_sc[:, LANES:LANES + E] = skv_ref[:, 0:E]
        kaug_sc[:, LANES + E:2 * LANES] = kconst_ref[:, LANES:LANES + N_FEAT]
        vaug_sc[:, 0:E] = skv_ref[:, E:2 * E]
        vaug_sc[:, E:LANES] = kconst_ref[:, LANES + N_FEAT:2 * LANES]
        kwin_sc[:, 0:E] = wkv_ref[:, 0:E]
        kwin_sc[:, E:LANES] = kconst_ref[:, LANES:LANES + N_FEAT]
        vwin_sc[:, 0:E] = wkv_ref[:, E:2 * E]
        vwin_sc[:, E:LANES] = kconst_ref[:, LANES + N_FEAT:2 * LANES]

    q = q_ref[...]
    qs = jnp.concatenate([q[:, r * E:(r + 1) * E] for r in range(B_GQA)], axis=0)
    row = lax.broadcasted_iota(jnp.int32, (rows, 1), 0)
    within = row & (Q_BLOCK - 1)
    tq = t0 + within
    slope = slope_ref[...]
    lane = lax.broadcasted_iota(jnp.int32, (rows, N_FEAT), 1)
    feat = jnp.where(lane == 0, slope * SLC_LEN,
           jnp.where(lane == 1, slope,
           jnp.where(lane == 2, -slope * t0.astype(f32),
           jnp.where(lane == 3, -slope * within.astype(f32), 0.0))))
    q_feat = jnp.concatenate([qs, feat.astype(bf16)], axis=1)

    n_chunk = kc_ref.shape[0]
    s = lax.dot_general(qs, kc_ref[...], dn, preferred_element_type=f32)
    cmp_end = lax.broadcasted_iota(jnp.int32, (1, n_chunk), 1) * CMP_STRIDE + (CMP_LEN - 1)
    s = jnp.where(cmp_end <= tq, s, -jnp.inf)
    m = jnp.max(s, axis=1, keepdims=True)
    m = jnp.where(m == -jnp.inf, 0.0, m)
    p = jnp.exp(s - m)
    l = jnp.maximum(jnp.sum(p, axis=1, keepdims=True), TINY)
    p_c = (p / l).astype(bf16)
    o_c = jnp.dot(p_c, vc_ref[...], preferred_element_type=f32)

    imp = lax.dot_general(ovt_ref[...], p_c[0:Q_BLOCK], dn, preferred_element_type=f32)
    for r in range(1, B_GQA):
        imp += lax.dot_general(ovt_ref[...], p_c[r * Q_BLOCK:(r + 1) * Q_BLOCK], dn, preferred_element_type=f32)
    blk = lax.broadcasted_iota(jnp.int32, (LANES, Q_BLOCK), 0)
    t = t0 + lax.broadcasted_iota(jnp.int32, (LANES, Q_BLOCK), 1)
    cur = t >> 6
    forced = (blk == 0) | (blk == cur) | (blk == cur - 1)
    score = jnp.where(blk * SLC_LEN <= t, imp + jnp.where(forced, FORCE_BONUS, 0.0), -jnp.inf)

    def pick(_, carry):
        work, sel = carry
        best = jnp.max(work, axis=0, keepdims=True)
        idx = jnp.min(jnp.where(work == best, blk, LANES), axis=0, keepdims=True)
        hit = blk == idx
        finite = jnp.where(best > -jnp.inf, 1.0, 0.0)
        sel = jnp.where(hit, jnp.maximum(sel, finite), sel)
        return jnp.where(hit, -jnp.inf, work), sel

    _, sel = lax.fori_loop(0, n_sel, pick, (score, jnp.zeros((LANES, Q_BLOCK), f32)))
    bias = jnp.where(sel.T > 0.5, 0.0, NEG).astype(bf16)
    qaug_sc[...] = jnp.concatenate([jnp.concatenate([bias] * B_GQA, axis=0), q_feat], axis=1)

    m_sc[...] = jnp.full(m_sc.shape, -jnp.inf, f32)
    acc_sc[...] = jnp.zeros(acc_sc.shape, f32)

    def logits(j, s_ref):
        k0 = pl.multiple_of(j * tk, tk)
        s_ref[...] = lax.dot_general(qaug_sc[...], kaug_sc[pl.ds(k0, tk), :], dn, preferred_element_type=f32)

    def update(j, s_ref, diagonal):
        k0 = pl.multiple_of(j * tk, tk)
        s = s_ref[...]
        if diagonal:
            s = jnp.where(k0 + lax.broadcasted_iota(jnp.int32, (1, tk), 1) <= tq, s, -jnp.inf)
        m_old = m_sc[...]
        m_new = jnp.maximum(m_old, jnp.broadcast_to(jnp.max(s, axis=1, keepdims=True), m_old.shape))
        p = jnp.exp(s - jnp.concatenate([m_new] * (tk // LANES), axis=1)).astype(bf16)
        pv = jnp.dot(p, vaug_sc[pl.ds(k0, tk), :], preferred_element_type=f32)
        acc_sc[...] = jnp.exp(m_old - m_new) * acc_sc[...] + pv
        m_sc[...] = m_new

    j_last = t0 // tk
    logits(0, sa_sc)

    def tile_pair(i, carry):
        j = 2 * i
        logits(j + 1, sb_sc)
        update(j, sa_sc, False)
        logits(j + 2, sa_sc)
        update(j + 1, sb_sc, False)
        return carry

    lax.fori_loop(0, j_last // 2, tile_pair, 0)
    odd = (j_last % 2) == 1

    @pl.when(odd)
    def _two_left():
        logits(j_last, sb_sc)
        update(j_last - 1, sa_sc, False)
        update(j_last, sb_sc, True)

    @pl.when(jnp.logical_not(odd))
    def _one_left():
        update(j_last, sa_sc, True)

    acc = acc_sc[...]
    o_s = acc[:, 0:E] / jnp.maximum(acc[:, E:E + 1], TINY)

    span = min(WIN + Q_BLOCK, seq)
    start = pl.multiple_of(jnp.maximum(t0 + Q_BLOCK - span, 0), Q_BLOCK)
    s = lax.dot_general(q_feat, kwin_sc[pl.ds(start, span), :], dn, preferred_element_type=f32)
    dw = tq - (start + lax.broadcasted_iota(jnp.int32, (1, span), 1))
    s = jnp.where((dw >= 0) & (dw < WIN), s, -jnp.inf)
    m = jnp.max(s, axis=1, keepdims=True)
    p = jnp.exp(s - m).astype(bf16)
    acc = jnp.dot(p, vwin_sc[pl.ds(start, span), :], preferred_element_type=f32)
    o_w = acc[:, 0:E] / jnp.maximum(acc[:, E:E + 1], TINY)

    gates = 1.0 / (1.0 + jnp.exp(-gate_ref[...]))
    for r in range(B_GQA):
        rs = slice(r * Q_BLOCK, (r + 1) * Q_BLOCK)
        out = (gates[:, 3 * r:3 * r + 1] * o_c[rs] + gates[:, 3 * r + 1:3 * r + 2] * o_s[rs]
               + gates[:, 3 * r + 2:3 * r + 3] * o_w[rs])
        o_ref[:, r * E:(r + 1) * E] = out.astype(o_ref.dtype)


def _nsa(u_nat, u_f32, cmp_kv, batch, seq):
    nq = seq // Q_BLOCK
    n_chunk = seq // CMP_STRIDE
    n_cmp = n_chunk - CMP_LEN // CMP_STRIDE + 1
    n_slc = seq // SLC_LEN
    n_sel = min(SLC_TOPK, n_slc)
    tk = min(512, seq)
    assert n_slc <= LANES and B_HEADS == 8 and seq % tk == 0
    c_start = np.arange(n_chunk) * CMP_STRIDE
    s_start = np.arange(LANES) * SLC_LEN
    overlap_t = ((c_start[None, :] < s_start[:, None] + SLC_LEN) & (c_start[None, :] + CMP_LEN > s_start[:, None])
                 & (np.arange(n_chunk)[None, :] < n_cmp) & (np.arange(LANES)[:, None] < n_slc))
    pos = np.arange(seq)
    kconst = np.zeros((seq, 2 * LANES), np.float32)
    kconst[pos, pos // SLC_LEN] = 1.0
    kconst[:, LANES + 0] = pos // SLC_LEN
    kconst[:, LANES + 1] = pos % SLC_LEN
    kconst[:, LANES + 2] = 1.0
    kconst[:, LANES + 3] = 1.0
    kconst[:, LANES + N_FEAT] = 1.0
    rows = B_GQA * Q_BLOCK
    slopes = jnp.exp2(-8.0 * jnp.arange(1, B_HEADS + 1, dtype=jnp.float32) / B_HEADS)
    slopes = jnp.repeat(slopes.reshape(B_KV_HEADS, B_GQA), Q_BLOCK, axis=1).reshape(B_KV_HEADS, rows, 1)
    kv_rows = pl.BlockSpec((seq, LANES), lambda b, g, n: (b, NAT_SLC + g))
    win_rows = pl.BlockSpec((seq, LANES), lambda b, g, n: (b, NAT_WIN + g))
    cmp_spec = lambda s: pl.BlockSpec((None, None, None, n_chunk, HEAD_DIM), lambda b, g, n: (s, b, g, 0, 0))
    return pl.pallas_call(
        functools.partial(_nsa_kernel, n_sel=n_sel, tk=tk),
        grid=(batch, B_KV_HEADS, nq),
        in_specs=[pl.BlockSpec((Q_BLOCK, B_GQA * HEAD_DIM), lambda b, g, n: (b * nq + n, NAT_Q_B + g)),
                  cmp_spec(0), cmp_spec(1), kv_rows, win_rows,
                  pl.BlockSpec((Q_BLOCK, LANES), lambda b, g, n: (b * nq + n, F_GATE + g)),
                  pl.BlockSpec((LANES, n_chunk), lambda b, g, n: (0, 0)),
                  pl.BlockSpec((seq, 2 * LANES), lambda b, g, n: (0, 0)),
                  pl.BlockSpec((None, rows, 1), lambda b, g, n: (g, 0, 0))],
        out_specs=pl.BlockSpec((Q_BLOCK, B_GQA * HEAD_DIM), lambda b, g, n: (b * nq + n, g)),
        out_shape=jax.ShapeDtypeStruct((batch * seq, B_WIDTH), jnp.bfloat16),
        scratch_shapes=[pltpu.VMEM((seq, 2 * LANES), jnp.bfloat16), pltpu.VMEM((seq, LANES), jnp.bfloat16),
                        pltpu.VMEM((seq, LANES), jnp.bfloat16), pltpu.VMEM((seq, LANES), jnp.bfloat16),
                        pltpu.VMEM((rows, LANES), jnp.float32), pltpu.VMEM((rows, LANES), jnp.float32),
                        pltpu.VMEM((rows, 2 * LANES), jnp.bfloat16),
                        pltpu.VMEM((rows, tk), jnp.float32), pltpu.VMEM((rows, tk), jnp.float32)],
        compiler_params=_params(("arbitrary", "arbitrary", "arbitrary")),
    )(u_nat, cmp_kv, cmp_kv, u_nat, u_nat, u_f32, jnp.asarray(overlap_t, jnp.bfloat16), jnp.asarray(kconst, jnp.bfloat16),
      slopes)


def _layer_norm(v, gain, bias):
    mu = jnp.mean(v, axis=1, keepdims=True)
    c = v - mu
    var = jnp.mean(c * c, axis=1, keepdims=True)
    return c * lax.rsqrt(var + LN_EPS) * gain + bias


def _sigmoid(v):
    return 1.0 / (1.0 + jnp.exp(-v))


def _token_major(ref, stage):
    dil, n = ref.shape[0], ref.shape[1]
    parts = stage.shape[0]
    for r in range(dil):
        for j in range(parts):
            stage[j, pl.ds(r, n, stride=dil), :] = ref[r, :, j * LANES:(j + 1) * LANES]
    return jnp.concatenate([stage[j] for j in range(parts)], axis=1)


def _merge_kernel(*refs):
    dil_refs, refs = refs[:2 * N_DIL], refs[2 * N_DIL:]
    (yb_ref, mga_ref, mgb_ref, x_ref, wa_ref, wb_ref, wo_ref, g_ref, b_ref, wr_hi_ref, wr_lo_ref, br_ref,
     h_ref, rw_ref, re_ref, cnt_ref) = refs[:16]
    stages, cnt_sc = refs[16:16 + 2 * (N_DIL - 1)], refs[16 + 2 * (N_DIL - 1)]
    f32 = jnp.float32
    outs, lse = [dil_refs[0][...]], [dil_refs[1][...]]
    for g in range(1, N_DIL):
        outs.append(_token_major(dil_refs[2 * g], stages[2 * g - 2]))
        lse.append(_token_major(dil_refs[2 * g + 1], stages[2 * g - 1]))
    top = jnp.maximum(jnp.maximum(lse[0], lse[1]), lse[2])
    e = [jnp.exp(v - top) for v in lse]
    y_a = (e[0] * outs[0] + e[1] * outs[1] + e[2] * outs[2]) / (e[0] + e[1] + e[2])
    br_a = jnp.dot(y_a.astype(jnp.bfloat16), wa_ref[...], preferred_element_type=f32)
    br_b = jnp.dot(yb_ref[...], wb_ref[...], preferred_element_type=f32)
    merged = _sigmoid(mga_ref[...]) * br_a + _sigmoid(mgb_ref[...]) * br_b
    mix = jnp.dot(merged.astype(jnp.bfloat16), wo_ref[...], preferred_element_type=f32)
    h = _layer_norm(ALPHA * x_ref[...] + mix, g_ref[...], b_ref[...])
    h_ref[...] = h

    h_hi = h.astype(jnp.bfloat16)
    h_lo = (h - h_hi.astype(f32)).astype(jnp.bfloat16)
    logits = (jnp.dot(h_hi, wr_hi_ref[...], preferred_element_type=f32)
              + jnp.dot(h_lo, wr_hi_ref[...], preferred_element_type=f32)
              + jnp.dot(h_hi, wr_lo_ref[...], preferred_element_type=f32)) + br_ref[...]
    lane = lax.broadcasted_iota(jnp.int32, logits.shape, 1)
    coarse = jnp.where(lane < N_GROUPS, logits, -jnp.inf)
    c_max = jnp.max(coarse, axis=1, keepdims=True)
    grp = jnp.min(jnp.where(coarse == c_max, lane, LANES), axis=1, keepdims=True)
    p_grp = 1.0 / jnp.sum(jnp.exp(coarse - c_max), axis=1, keepdims=True)
    lo_lane = N_GROUPS + EXP_PER_GROUP * grp
    fine = jnp.where((lane >= lo_lane) & (lane < lo_lane + EXP_PER_GROUP), logits, -jnp.inf)
    v1 = jnp.max(fine, axis=1, keepdims=True)
    i1 = jnp.min(jnp.where(fine == v1, lane, LANES), axis=1, keepdims=True)
    fine = jnp.where(lane == i1, -jnp.inf, fine)
    v2 = jnp.max(fine, axis=1, keepdims=True)
    i2 = jnp.min(jnp.where(fine == v2, lane, LANES), axis=1, keepdims=True)
    e2 = jnp.exp(v2 - v1)
    w1 = p_grp / (1.0 + e2)
    w2 = p_grp * e2 / (1.0 + e2)
    e_1, e_2 = i1 - N_GROUPS, i2 - N_GROUPS
    rw_ref[...] = jnp.where(lane == 0, w1, jnp.where(lane == 1, w2, 0.0))

    @pl.when(pl.program_id(0) == 0)
    def _zero_counts():
        cnt_sc[...] = jnp.zeros(cnt_sc.shape, f32)

    tm = logits.shape[0]
    onehot = jnp.where((lane == e_1) | (lane == e_2), 1.0, 0.0)
    earlier = (lax.broadcasted_iota(jnp.int32, (tm, tm), 0) > lax.broadcasted_iota(jnp.int32, (tm, tm), 1))
    before = jnp.dot(jnp.where(earlier, 1.0, 0.0).astype(jnp.bfloat16), onehot.astype(jnp.bfloat16),
                     preferred_element_type=f32) + cnt_sc[...]
    r_1 = jnp.sum(jnp.where(lane == e_1, before, 0.0), axis=1, keepdims=True).astype(jnp.int32)
    r_2 = jnp.sum(jnp.where(lane == e_2, before, 0.0), axis=1, keepdims=True).astype(jnp.int32)
    cnt_sc[...] = cnt_sc[...] + jnp.sum(onehot, axis=0, keepdims=True)
    cnt_ref[...] = cnt_sc[...].astype(jnp.int32)
    re_ref[...] = jnp.where(lane == 0, e_1, jnp.where(lane == 1, e_2,
                            jnp.where(lane == 2, r_1, jnp.where(lane == 3, r_2, 0))))


def _merge(dil_outs, y_b, u_f32, x2d, seq, wa, wb, wo, ln_g, ln_b, wr_hi, wr_lo, b_r, tm=256):
    t = x2d.shape[0]
    per_batch = seq // tm
    assert seq % tm == 0 and all(tm % (dil * 8) == 0 for _, dil in DIL_GROUPS)
    row = lambda w: pl.BlockSpec((tm, w), lambda i: (i, 0))
    full = lambda a: pl.BlockSpec(a.shape, lambda i: (0,) * a.ndim)
    dil_specs = [pl.BlockSpec((None, None, tm, A_OUT), lambda i: (i // per_batch, 0, i % per_batch, 0))] * 2
    stages = []
    for _, dil in DIL_GROUPS[1:]:
        dil_specs += [pl.BlockSpec((None, dil, tm // dil, A_OUT),
                                   lambda i: (i // per_batch, 0, i % per_batch, 0))] * 2
        stages += [pltpu.VMEM((A_OUT // LANES, tm, LANES), jnp.float32)] * 2
    return pl.pallas_call(
        _merge_kernel,
        grid=(t // tm,),
        in_specs=dil_specs + [row(B_WIDTH),
                  pl.BlockSpec((tm, D_MODEL), lambda i: (i, F_MGATE)),
                  pl.BlockSpec((tm, D_MODEL), lambda i: (i, F_MGATE + 1)),
                  row(D_MODEL), full(wa), full(wb), full(wo), full(ln_g), full(ln_b),
                  full(wr_hi), full(wr_lo), full(b_r)],
        out_specs=[row(D_MODEL), row(LANES), row(LANES), pl.BlockSpec((1, LANES), lambda i: (0, 0))],
        out_shape=[jax.ShapeDtypeStruct((t, D_MODEL), jnp.float32),
                   jax.ShapeDtypeStruct((t, LANES), jnp.float32),
                   jax.ShapeDtypeStruct((t, LANES), jnp.int32),
                   jax.ShapeDtypeStruct((1, LANES), jnp.int32)],
        scratch_shapes=stages + [pltpu.VMEM((1, LANES), jnp.float32)],
        compiler_params=_params(("arbitrary",)),
    )(*dil_outs, y_b, u_f32, u_f32, x2d, wa, wb, wo, ln_g, ln_b, wr_hi, wr_lo, b_r)


GATHER_UNROLL = 8


def _start_row_gather(idx_ref, count, src_hbm, dst, sem, src_of):
    def start(r, carry):
        src_row = idx_ref[0, 0, src_of(r)]
        pltpu.make_async_copy(src_hbm.at[pl.ds(src_row, 1), :], dst.at[pl.ds(r, 1), :], sem).start()
        return carry

    lax.fori_loop(0, count, start, 0, unroll=GATHER_UNROLL)


def _wait_row_gather(src_hbm, dst, sem):
    pltpu.make_async_copy(src_hbm.at[pl.ds(0, dst.shape[0]), :], dst, sem).wait()


def _expert_kernel(blk_e_ref, tok_ref, tok_next_ref, h_hbm, wgu_ref, wd_ref, y_ref, xbuf, sems):
    i = pl.program_id(0)
    nb = pl.num_programs(0)
    bm = xbuf.shape[1]
    slot = i % 2

    @pl.when(i == 0)
    def _first_block():
        _start_row_gather(tok_ref, bm, h_hbm, xbuf.at[0], sems.at[0], lambda r: r)

    @pl.when(i + 1 < nb)
    def _next_block():
        _start_row_gather(tok_next_ref, bm, h_hbm, xbuf.at[1 - slot], sems.at[1 - slot], lambda r: r)

    _wait_row_gather(h_hbm, xbuf.at[slot], sems.at[slot])
    xb = xbuf[slot].astype(jnp.bfloat16)
    gu = jnp.dot(xb, wgu_ref[...], preferred_element_type=jnp.float32)
    gate, up = gu[:, :D_EXPERT], gu[:, D_EXPERT:]
    act = (gate * _sigmoid(gate)) * up
    y_ref[...] = jnp.dot(act.astype(jnp.bfloat16), wd_ref[...], preferred_element_type=jnp.float32)


def _experts(h, tok_buf, blk_e, wgu, wd, bm):
    p = tok_buf.shape[0]
    nb = p // bm
    tok = tok_buf.reshape(nb, 1, bm)
    return pl.pallas_call(
        _expert_kernel,
        grid_spec=pltpu.PrefetchScalarGridSpec(
            num_scalar_prefetch=1,
            grid=(nb,),
            in_specs=[pl.BlockSpec((1, 1, bm), lambda i, e: (i, 0, 0), memory_space=pltpu.SMEM),
                      pl.BlockSpec((1, 1, bm), lambda i, e: (jnp.minimum(i + 1, nb - 1), 0, 0),
                                   memory_space=pltpu.SMEM),
                      pl.BlockSpec(memory_space=pl.ANY),
                      pl.BlockSpec((None, D_MODEL, 2 * D_EXPERT), lambda i, e: (e[i], 0, 0)),
                      pl.BlockSpec((None, D_EXPERT, D_MODEL), lambda i, e: (e[i], 0, 0))],
            out_specs=pl.BlockSpec((bm, D_MODEL), lambda i, e: (i, 0)),
            scratch_shapes=[pltpu.VMEM((2, bm, D_MODEL), jnp.float32), pltpu.SemaphoreType.DMA((2,))],
        ),
        out_shape=jax.ShapeDtypeStruct((p, D_MODEL), jnp.float32),
        compiler_params=_params(("arbitrary",)),
    )(blk_e, tok, tok, h, wgu, wd)


def _combine_kernel(dest_ref, dest_next_ref, y_hbm, h_ref, rw_ref, g_ref, b_ref, o_ref, ybuf, sems):
    i = pl.program_id(0)
    tm = h_ref.shape[0]
    slot = i % 2

    def start(idx_ref, s):
        for k in range(MOE_TOP_K):
            _start_row_gather(idx_ref, tm, y_hbm, ybuf.at[s, k], sems.at[s], lambda r, k=k: MOE_TOP_K * r + k)

    @pl.when(i == 0)
    def _first_tile():
        start(dest_ref, 0)

    @pl.when(i + 1 < pl.num_programs(0))
    def _next_tile():
        start(dest_next_ref, 1 - slot)

    for k in range(MOE_TOP_K):
        _wait_row_gather(y_hbm, ybuf.at[slot, k], sems.at[slot])
    rw = rw_ref[...]
    ffn = rw[:, 0:1] * ybuf[slot, 0] + rw[:, 1:2] * ybuf[slot, 1]
    o_ref[...] = _layer_norm(ALPHA * h_ref[...] + ffn, g_ref[...], b_ref[...])


def _combine(dest, y_sorted, h, rw, ln_g, ln_b, tm=256):
    t = h.shape[0]
    assert t % tm == 0
    nt = t // tm
    idx = dest.reshape(nt, 1, MOE_TOP_K * tm)
    return pl.pallas_call(
        _combine_kernel,
        grid=(nt,),
        in_specs=[pl.BlockSpec((1, 1, MOE_TOP_K * tm), lambda i: (i, 0, 0), memory_space=pltpu.SMEM),
                  pl.BlockSpec((1, 1, MOE_TOP_K * tm), lambda i: (jnp.minimum(i + 1, nt - 1), 0, 0),
                               memory_space=pltpu.SMEM),
                  pl.BlockSpec(memory_space=pl.ANY),
                  pl.BlockSpec((tm, D_MODEL), lambda i: (i, 0)),
                  pl.BlockSpec((tm, LANES), lambda i: (i, 0)),
                  pl.BlockSpec((1, D_MODEL), lambda i: (0, 0)),
                  pl.BlockSpec((1, D_MODEL), lambda i: (0, 0))],
        out_specs=pl.BlockSpec((tm, D_MODEL), lambda i: (i, 0)),
        out_shape=jax.ShapeDtypeStruct((t, D_MODEL), jnp.float32),
        scratch_shapes=[pltpu.VMEM((2, MOE_TOP_K, tm, D_MODEL), jnp.float32), pltpu.SemaphoreType.DMA((2,))],
        compiler_params=_params(("arbitrary",)),
    )(idx, idx, y_sorted, h, rw, ln_g, ln_b)


def _dispatch(route, counts, bm):
    t = route.shape[0]
    n_assign = t * MOE_TOP_K
    expert, rank = route[:, 0:MOE_TOP_K], route[:, MOE_TOP_K:2 * MOE_TOP_K]
    starts = jnp.cumsum(counts) - counts
    pcounts = (counts + bm - 1) // bm * bm
    pends = jnp.cumsum(pcounts)
    pstarts = pends - pcounts
    ids = jnp.arange(N_EXPERTS, dtype=jnp.int32)
    dest = jnp.sum(jnp.where(expert[..., None] == ids, pstarts, 0), axis=-1) + rank
    p = -(-(n_assign + N_EXPERTS * (bm - 1)) // bm) * bm
    nb = p // bm
    blk_e = jnp.minimum(jnp.sum(pends[None, :] <= (jnp.arange(nb, dtype=jnp.int32) * bm)[:, None], axis=1),
                        N_EXPERTS - 1).astype(jnp.int32)
    order = jnp.argsort(expert.reshape(n_assign), stable=True).astype(jnp.int32)
    row_e = jnp.repeat(blk_e, bm)
    k = jnp.arange(p, dtype=jnp.int32) - pstarts[row_e]
    src = jnp.clip(starts[row_e] + k, 0, n_assign - 1)
    tok_buf = jnp.where(k < counts[row_e], order[src] // MOE_TOP_K, 0).astype(jnp.int32)
    return tok_buf, dest.astype(jnp.int32), blk_e


def _split_bf16(w):
    hi = w.astype(jnp.bfloat16)
    return hi, (w - hi.astype(jnp.float32)).astype(jnp.bfloat16)


def kernel(x, w_in, cmp_pos_k, cmp_w1_k, cmp_w2_k, cmp_pos_v, cmp_w1_v, cmp_w2_v, w_branch_a, w_branch_b, w_out, ln1_g, ln1_b, w_coarse, b_coarse, w_fine, b_fine, w_gate_up, w_down, ln2_g, ln2_b):
    batch, seq, d = x.shape
    bf16 = jnp.bfloat16
    h = x.reshape(batch * seq, d)
    cols_bf, cols_f32, scale_bf = _in_proj_columns()
    half = CMP_STRIDE * HEAD_DIM
    for l in range(DEPTH):
        u_nat, *groups = _project_bf(h, (_permute_columns(w_in[l], cols_bf) * scale_bf).astype(bf16), batch, seq)
        u_f32 = _project(h, _permute_columns(w_in[l], cols_f32).astype(bf16), jnp.float32)
        dil_outs = _dilated(u_nat, groups, batch, seq)
        pos = jnp.stack([cmp_pos_k[l], cmp_pos_v[l]]).reshape(2, 2, 1, half)
        w1 = jnp.stack([cmp_w1_k[l], cmp_w1_v[l]]).astype(bf16)
        w2 = jnp.stack([cmp_w2_k[l], cmp_w2_v[l]]).astype(bf16)
        cmp_kv = _compress(u_f32, batch, seq, pos, w1, w2)
        y_b = _nsa(u_nat, u_f32, cmp_kv, batch, seq)
        w_r = jnp.concatenate([w_coarse[l], w_fine[l].transpose(1, 0, 2).reshape(d, N_EXPERTS)], axis=1)
        w_r = jnp.pad(w_r, ((0, 0), (0, LANES - w_r.shape[1])))
        b_r = jnp.pad(jnp.concatenate([b_coarse[l], b_fine[l].reshape(N_EXPERTS)]),
                      (0, LANES - N_GROUPS - N_EXPERTS)).reshape(1, LANES)
        wr_hi, wr_lo = _split_bf16(w_r)
        h1, rw, route, counts = _merge(dil_outs, y_b, u_f32, h, seq, w_branch_a[l].astype(bf16),
                                       w_branch_b[l].astype(bf16), w_out[l].astype(bf16),
                                       ln1_g[l].reshape(1, d), ln1_b[l].reshape(1, d), wr_hi, wr_lo, b_r)
        bm = 256
        tok_buf, dest, blk_e = _dispatch(route, counts[0, :N_EXPERTS], bm)
        y_sorted = _experts(h1, tok_buf, blk_e, w_gate_up[l].astype(bf16), w_down[l].astype(bf16), bm)
        h = _combine(dest, y_sorted, h1, rw, ln2_g[l].reshape(1, d), ln2_b[l].reshape(1, d))
    return h.reshape(batch, seq, d)
```

```python
import functools

import numpy as np
import jax
import jax.numpy as jnp
from jax import lax
from jax.experimental import pallas as pl
from jax.experimental.pallas import tpu as pltpu

D_MODEL = 1024
HEAD_DIM = 64
Q_BLOCK = 128
DIL_GROUPS = ((128, 1), (512, 4), (2048, 16))
N_DIL = len(DIL_GROUPS)
A_HEADS_PER_GROUP = 4
A_HEADS = A_HEADS_PER_GROUP * N_DIL
A_WIDTH = A_HEADS * HEAD_DIM
A_OUT = A_HEADS_PER_GROUP * HEAD_DIM
B_HEADS = 8
B_KV_HEADS = 2
B_GQA = B_HEADS // B_KV_HEADS
B_WIDTH = B_HEADS * HEAD_DIM
B_KV_WIDTH = B_KV_HEADS * HEAD_DIM
CMP_LEN = 32
CMP_STRIDE = 16
CMP_HIDDEN = 256
SLC_LEN = 64
SLC_TOPK = 16
WIN = 512
N_GROUPS = 4
EXP_PER_GROUP = 8
N_EXPERTS = N_GROUPS * EXP_PER_GROUP
D_EXPERT = 512
MOE_TOP_K = 2
DEPTH = 1
ALPHA = (2.0 * DEPTH) ** 0.25
LN_EPS = 1e-5
FORCE_BONUS = 1e4
TINY = 1e-30
ATTN_SCALE = HEAD_DIM ** -0.5

LANES = 128
VMEM_LIMIT = 48 * 1024 * 1024

QKV_COLS = 3 * A_OUT
NAT_Q_B = QKV_COLS // A_OUT
NAT_SLC = (QKV_COLS + B_WIDTH) // LANES
NAT_WIN = NAT_SLC + B_KV_HEADS
NAT_COLS = QKV_COLS + B_WIDTH + 4 * B_KV_WIDTH
BF_COLS = NAT_COLS + (N_DIL - 1) * QKV_COLS
F_MGATE = 0
F_GATE = 2 * D_MODEL // LANES
F_NAT_COLS = 2 * D_MODEL + B_KV_HEADS * LANES
F_COLS = F_NAT_COLS + 2 * B_KV_WIDTH


def _params(semantics):
    return pltpu.CompilerParams(dimension_semantics=semantics, vmem_limit_bytes=VMEM_LIMIT)


def _in_proj_columns():
    kv_off = 3 * A_WIDTH + B_WIDTH
    gate_off = kv_off + 6 * B_KV_WIDTH
    mg_off = gate_off + 3 * B_HEADS
    def qkv_cols(g):
        return [part * A_WIDTH + g * A_OUT + c for part in range(3) for c in range(A_OUT)]

    bf = qkv_cols(0) + list(range(3 * A_WIDTH, 3 * A_WIDTH + B_WIDTH))
    for first in (2, 4):
        for g in range(B_KV_HEADS):
            for i in (first, first + 1):
                base = kv_off + i * B_KV_WIDTH + g * HEAD_DIM
                bf += list(range(base, base + HEAD_DIM))
    for g in range(1, N_DIL):
        bf += qkv_cols(g)
    f32 = list(range(mg_off, mg_off + 2 * D_MODEL))
    per_head = 3 * B_GQA
    for g in range(B_KV_HEADS):
        f32 += list(range(gate_off + g * per_head, gate_off + (g + 1) * per_head))
        f32 += [-1] * (LANES - per_head)
    f32 += list(range(kv_off, kv_off + 2 * B_KV_WIDTH))
    assert len(bf) == BF_COLS and len(f32) == F_COLS
    bf = np.asarray(bf)
    is_q = (bf < A_WIDTH) | ((bf >= 3 * A_WIDTH) & (bf < 3 * A_WIDTH + B_WIDTH))
    scale = np.where(is_q, ATTN_SCALE, 1.0).astype(np.float32)
    return bf, np.asarray(f32), scale


def _permute_columns(w, cols):
    taken = jnp.take(w, jnp.asarray(np.maximum(cols, 0)), axis=1)
    return jnp.where(jnp.asarray(cols >= 0)[None, :], taken, 0.0)


def _proj_f32_kernel(x_ref, w_ref, nat_ref, cmp_ref, stage):
    xb = x_ref[...].astype(jnp.bfloat16)
    tm = x_ref.shape[0]
    n_chunk = tm // CMP_STRIDE
    for c0 in range(0, F_NAT_COLS, A_OUT):
        nat_ref[:, c0:c0 + A_OUT] = jnp.dot(xb, w_ref[:, c0:c0 + A_OUT], preferred_element_type=jnp.float32)
    res = jnp.dot(xb, w_ref[:, F_NAT_COLS:F_COLS], preferred_element_type=jnp.float32)
    lower = lax.broadcasted_iota(jnp.int32, (n_chunk, LANES), 1) < HEAD_DIM
    for kv in range(2):
        stage[...] = res[:, kv * LANES:(kv + 1) * LANES]
        for p in range(0, CMP_STRIDE, 2):
            even = stage[pl.ds(p, n_chunk, stride=CMP_STRIDE), :]
            odd = stage[pl.ds(p + 1, n_chunk, stride=CMP_STRIDE), :]
            cols = slice(p * HEAD_DIM, (p + 2) * HEAD_DIM)
            cmp_ref[kv, 0, :, cols] = jnp.where(lower, even, pltpu.roll(odd, HEAD_DIM, 1))
            cmp_ref[kv, 1, :, cols] = jnp.where(lower, pltpu.roll(even, HEAD_DIM, 1), odd)


def _project_f32(x2d, w, batch, seq, tm=512):
    t, d = x2d.shape
    assert seq % tm == 0 and w.shape[1] == F_COLS and B_KV_HEADS * HEAD_DIM == LANES
    per_batch = seq // tm
    half = CMP_STRIDE * HEAD_DIM
    return pl.pallas_call(
        _proj_f32_kernel,
        grid=(t // tm,),
        in_specs=[pl.BlockSpec((tm, d), lambda i: (i, 0)),
                  pl.BlockSpec((d, F_COLS), lambda i: (0, 0))],
        out_specs=[pl.BlockSpec((tm, F_NAT_COLS), lambda i: (i, 0)),
                   pl.BlockSpec((2, None, B_KV_HEADS, tm // CMP_STRIDE, half),
                                lambda i: (0, i // per_batch, 0, i % per_batch, 0))],
        out_shape=[jax.ShapeDtypeStruct((t, F_NAT_COLS), jnp.float32),
                   jax.ShapeDtypeStruct((2, batch, B_KV_HEADS, seq // CMP_STRIDE, half), jnp.float32)],
        scratch_shapes=[pltpu.VMEM((tm, LANES), jnp.float32)],
        compiler_params=_params(("parallel",)),
    )(x2d, w)


def _proj_bf_kernel(x_ref, w_ref, nat_ref, *rest, dils):
    group_refs, stage = rest[:-1], rest[-1]
    xb = x_ref[...].astype(jnp.bfloat16)
    tm = x_ref.shape[0]
    for c0 in range(0, NAT_COLS, A_OUT):
        nat_ref[:, c0:c0 + A_OUT] = jnp.dot(
            xb, w_ref[:, c0:c0 + A_OUT], preferred_element_type=jnp.float32).astype(nat_ref.dtype)
    col = NAT_COLS
    for o_ref, dil in zip(group_refs, dils):
        for part in range(3):
            res = jnp.dot(xb, w_ref[:, col:col + A_OUT], preferred_element_type=jnp.float32)
            for j in range(A_OUT // LANES):
                stage[j] = res[:, j * LANES:(j + 1) * LANES]
                c0 = part * A_OUT + j * LANES
                for r in range(dil):
                    o_ref[r, :, c0:c0 + LANES] = stage[j, pl.ds(r, tm // dil, stride=dil), :].astype(o_ref.dtype)
            col += A_OUT


def _project_bf(x2d, w, batch, seq, tm=512):
    t, d = x2d.shape
    dils = tuple(dil for _, dil in DIL_GROUPS[1:])
    assert seq % tm == 0 and all(tm % (dil * 16) == 0 for dil in dils) and w.shape[1] == BF_COLS
    per_batch = seq // tm
    out_specs = [pl.BlockSpec((tm, NAT_COLS), lambda i: (i, 0))]
    out_shapes = [jax.ShapeDtypeStruct((t, NAT_COLS), jnp.bfloat16)]
    for dil in dils:
        out_specs.append(pl.BlockSpec((None, dil, tm // dil, QKV_COLS),
                                      lambda i: (i // per_batch, 0, i % per_batch, 0)))
        out_shapes.append(jax.ShapeDtypeStruct((batch, dil, seq // dil, QKV_COLS), jnp.bfloat16))
    return pl.pallas_call(
        functools.partial(_proj_bf_kernel, dils=dils),
        grid=(t // tm,),
        in_specs=[pl.BlockSpec((tm, d), lambda i: (i, 0)),
                  pl.BlockSpec((d, BF_COLS), lambda i: (0, 0))],
        out_specs=out_specs,
        out_shape=out_shapes,
        scratch_shapes=[pltpu.VMEM((A_OUT // LANES, tm, LANES), jnp.float32)],
        compiler_params=_params(("parallel",)),
    )(x2d, w)


def _dilated_kernel(*refs, nbs):
    bias_ref, ins, outs = refs[0], refs[1:1 + 5 * N_DIL], refs[1 + 5 * N_DIL:]
    i = pl.program_id(1)
    dn = (((1,), (1,)), ((), ()))
    f32, bf16 = jnp.float32, jnp.bfloat16
    lower = lax.broadcasted_iota(jnp.int32, (Q_BLOCK, LANES), 1) < HEAD_DIM
    heads = [(g, pair, half) for g in range(N_DIL) for pair in range(A_OUT // LANES) for half in range(2)]

    def col(ref, pair):
        return ref[:, pair * LANES:(pair + 1) * LANES]

    logits = []
    for g, pair, half in heads:
        q_ref, kp_ref, kc_ref = ins[5 * g:5 * g + 3]
        head = g * A_HEADS_PER_GROUP + 2 * pair + half
        q = col(q_ref, pair)
        q = jnp.where(lower if half == 0 else jnp.logical_not(lower), q, jnp.zeros_like(q))
        no_prev = jnp.where((i % nbs[g]) > 0, 0.0, -jnp.inf)
        s_p = lax.dot_general(q, col(kp_ref, pair), dn, preferred_element_type=f32)
        s_c = lax.dot_general(q, col(kc_ref, pair), dn, preferred_element_type=f32)
        logits.append((s_p + bias_ref[head, :, 0:Q_BLOCK] + no_prev, s_c + bias_ref[head, :, Q_BLOCK:2 * Q_BLOCK]))

    probs = []
    for s_p, s_c in logits:
        m = jnp.max(jnp.maximum(s_p, s_c), axis=1, keepdims=True)
        p_p, p_c = jnp.exp(s_p - m), jnp.exp(s_c - m)
        l = jnp.sum(p_p + p_c, axis=1, keepdims=True)
        probs.append((p_p.astype(bf16), p_c.astype(bf16), l, m + jnp.log(l)))

    for n_head in range(0, len(heads), 2):
        g, pair, _ = heads[n_head]
        vp_ref, vc_ref = ins[5 * g + 3:5 * g + 5]
        o_ref, lse_ref = outs[2 * g:2 * g + 2]
        out, lse = [], []
        for p_p, p_c, l, lse_h in probs[n_head:n_head + 2]:
            acc = jnp.dot(p_p, col(vp_ref, pair), preferred_element_type=f32)
            acc += jnp.dot(p_c, col(vc_ref, pair), preferred_element_type=f32)
            out.append(acc / l)
            lse.append(jnp.broadcast_to(lse_h, (Q_BLOCK, LANES)))
        o_ref[:, pair * LANES:(pair + 1) * LANES] = jnp.where(lower, out[0], out[1])
        lse_ref[:, pair * LANES:(pair + 1) * LANES] = jnp.where(lower, lse[0], lse[1])


def _dilated_bias():
    qi = Q_BLOCK + jnp.arange(Q_BLOCK)
    kj = jnp.arange(2 * Q_BLOCK)
    delta = qi[:, None] - kj[None, :]
    slopes = jnp.exp2(-8.0 * jnp.arange(1, A_HEADS + 1, dtype=jnp.float32) / A_HEADS)
    tables = []
    for g, (window, dil) in enumerate(DIL_GROUPS):
        on_band = (delta >= 0) & (delta <= window // dil)
        dist = (delta * dil).astype(jnp.float32)
        for h in range(A_HEADS_PER_GROUP):
            tables.append(jnp.where(on_band, -slopes[g * A_HEADS_PER_GROUP + h] * dist, -jnp.inf))
    return jnp.stack(tables)


def _dilated(u_nat, groups, batch, seq):
    steps = seq // Q_BLOCK
    arrays = [u_nat.reshape(batch, 1, seq, NAT_COLS)] + list(groups)
    bias = _dilated_bias()
    in_arrays, in_specs = [bias], [pl.BlockSpec(bias.shape, lambda b, i: (0, 0, 0))]
    out_specs, out_shapes, nbs = [], [], []
    blk = (None, None, Q_BLOCK, A_OUT)
    for g, (window, dil) in enumerate(DIL_GROUPS):
        length = seq // dil
        assert length % Q_BLOCK == 0 and window // dil == Q_BLOCK
        nb = length // Q_BLOCK
        nbs.append(nb)

        def cur(b, i, nb=nb, col=0):
            return (b, i // nb, i % nb, col)

        def prev(b, i, nb=nb, col=0):
            return (b, i // nb, jnp.maximum(i % nb - 1, 0), col)

        for fn, col in ((cur, 0), (prev, 1), (cur, 1), (prev, 2), (cur, 2)):
            in_arrays.append(arrays[g])
            in_specs.append(pl.BlockSpec(blk, functools.partial(fn, col=col)))
        for _ in range(2):
            out_specs.append(pl.BlockSpec(blk, functools.partial(cur, col=0)))
            out_shapes.append(jax.ShapeDtypeStruct((batch, dil, length, A_OUT), jnp.float32))
    return pl.pallas_call(
        functools.partial(_dilated_kernel, nbs=tuple(nbs)),
        grid=(batch, steps),
        in_specs=in_specs,
        out_specs=out_specs,
        out_shape=out_shapes,
        compiler_params=_params(("parallel", "parallel")),
    )(*in_arrays)


def _compress_kernel(x_ref, pos_ref, w1_ref, w2_ref, o_ref):
    half = CMP_STRIDE * HEAD_DIM
    x = x_ref[...]
    n_chunk = x.shape[0]
    lo = (x + pos_ref[0]).astype(jnp.bfloat16)
    hi = (x + pos_ref[1]).astype(jnp.bfloat16)
    h_lo = jnp.dot(lo, w1_ref[0:half, :], preferred_element_type=jnp.float32)
    h_hi = jnp.dot(hi, w1_ref[half:2 * half, :], preferred_element_type=jnp.float32)
    h = h_lo + pltpu.roll(h_hi, n_chunk - 1, 0)
    act = jax.nn.gelu(h, approximate=True)
    o_ref[...] = jnp.dot(act.astype(jnp.bfloat16), w2_ref[...],
                         preferred_element_type=jnp.float32).astype(o_ref.dtype)


def _compress(ckv, pos, w1, w2):
    _, batch, _, n_chunk, half = ckv.shape
    return pl.pallas_call(
        _compress_kernel,
        grid=(2, batch, B_KV_HEADS),
        in_specs=[pl.BlockSpec((None, None, None, n_chunk, half), lambda s, b, g: (s, b, g, 0, 0)),
                  pl.BlockSpec((None, 2, 1, half), lambda s, b, g: (s, 0, 0, 0)),
                  pl.BlockSpec((None, 2 * half, CMP_HIDDEN), lambda s, b, g: (s, 0, 0)),
                  pl.BlockSpec((None, CMP_HIDDEN, HEAD_DIM), lambda s, b, g: (s, 0, 0))],
        out_specs=pl.BlockSpec((None, None, None, n_chunk, HEAD_DIM), lambda s, b, g: (s, b, g, 0, 0)),
        out_shape=jax.ShapeDtypeStruct((2, batch, B_KV_HEADS, n_chunk, HEAD_DIM), jnp.bfloat16),
        compiler_params=_params(("parallel", "parallel", "parallel")),
    )(ckv, pos, w1, w2)


NEG = -2.0 ** 60
N_FEAT = HEAD_DIM


def _nsa_kernel(q_ref, kc_ref, vc_ref, skv_ref, wkv_ref, gate_ref, ovt_ref, kconst_ref, slope_ref, o_ref,
                kaug_sc, vaug_sc, kwin_sc, vwin_sc, m_sc, acc_sc, qaug_sc, sa_sc, sb_sc, *, n_sel, tk):
    n = pl.program_id(2)
    t0 = n * Q_BLOCK
    rows = B_GQA * Q_BLOCK
    seq = skv_ref.shape[0]
    dn = (((1,), (1,)), ((), ()))
    f32, bf16 = jnp.float32, jnp.bfloat16
    E = HEAD_DIM

    @pl.when(n == 0)
    def _build_keys():
        kaug_sc[:, 0:LANES] = kconst_ref[:, 0:LANES]
        kaug_sc[:, LANES:LANES + E] = skv_ref[:, 0:E]
        kaug_sc[:, LANES + E:2 * LANES] = kconst_ref[:, LANES:LANES + N_FEAT]
        vaug_sc[:, 0:E] = skv_ref[:, E:2 * E]
        vaug_sc[:, E:LANES] = kconst_ref[:, LANES + N_FEAT:2 * LANES]
        kwin_sc[:, 0:E] = wkv_ref[:, 0:E]
        kwin_sc[:, E:LANES] = kconst_ref[:, LANES:LANES + N_FEAT]
        vwin_sc[:, 0:E] = wkv_ref[:, E:2 * E]
        vwin_sc[:, E:LANES] = kconst_ref[:, LANES + N_FEAT:2 * LANES]

    q = q_ref[...]
    qs = jnp.concatenate([q[:, r * E:(r + 1) * E] for r in range(B_GQA)], axis=0)
    row = lax.broadcasted_iota(jnp.int32, (rows, 1), 0)
    within = row & (Q_BLOCK - 1)
    tq = t0 + within
    slope = slope_ref[...]
    lane = lax.broadcasted_iota(jnp.int32, (rows, N_FEAT), 1)
    feat = jnp.where(lane == 0, slope * SLC_LEN,
           jnp.where(lane == 1, slope,
           jnp.where(lane == 2, -slope * t0.astype(f32),
           jnp.where(lane == 3, -slope * within.astype(f32), 0.0))))
    q_feat = jnp.concatenate([qs, feat.astype(bf16)], axis=1)

    n_chunk = kc_ref.shape[0]
    s = lax.dot_general(qs, kc_ref[...], dn, preferred_element_type=f32)
    cmp_end = lax.broadcasted_iota(jnp.int32, (1, n_chunk), 1) * CMP_STRIDE + (CMP_LEN - 1)
    s = jnp.where(cmp_end <= tq, s, -jnp.inf)
    m = jnp.max(s, axis=1, keepdims=True)
    m = jnp.where(m == -jnp.inf, 0.0, m)
    p = jnp.exp(s - m)
    l = jnp.maximum(jnp.sum(p, axis=1, keepdims=True), TINY)
    p_c = (p / l).astype(bf16)
    o_c = jnp.dot(p_c, vc_ref[...], preferred_element_type=f32)

    span = min(WIN + Q_BLOCK, seq)
    start = pl.multiple_of(jnp.maximum(t0 + Q_BLOCK - span, 0), Q_BLOCK)
    s = lax.dot_general(q_feat, kwin_sc[pl.ds(start, span), :], dn, preferred_element_type=f32)
    dw = tq - (start + lax.broadcasted_iota(jnp.int32, (1, span), 1))
    s = jnp.where((dw >= 0) & (dw < WIN), s, -jnp.inf)
    m = jnp.max(s, axis=1, keepdims=True)
    p = jnp.exp(s - m).astype(bf16)
    acc = jnp.dot(p, vwin_sc[pl.ds(start, span), :], preferred_element_type=f32)
    o_w = acc[:, 0:E] / jnp.maximum(acc[:, E:E + 1], TINY)

    imp = lax.dot_general(ovt_ref[...], p_c[0:Q_BLOCK], dn, preferred_element_type=f32)
    for r in range(1, B_GQA):
        imp += lax.dot_general(ovt_ref[...], p_c[r * Q_BLOCK:(r + 1) * Q_BLOCK], dn, preferred_element_type=f32)
    blk = lax.broadcasted_iota(jnp.int32, (LANES, Q_BLOCK), 0)
    t = t0 + lax.broadcasted_iota(jnp.int32, (LANES, Q_BLOCK), 1)
    cur = t >> 6
    forced = (blk == 0) | (blk == cur) | (blk == cur - 1)
    score = jnp.where(blk * SLC_LEN <= t, imp + jnp.where(forced, FORCE_BONUS, 0.0), -jnp.inf)

    def pick(_, carry):
        work, sel = carry
        best = jnp.max(work, axis=0, keepdims=True)
        idx = jnp.min(jnp.where(work == best, blk, LANES), axis=0, keepdims=True)
        hit = blk == idx
        finite = jnp.where(best > -jnp.inf, 1.0, 0.0)
        sel = jnp.where(hit, jnp.maximum(sel, finite), sel)
        return jnp.where(hit, -jnp.inf, work), sel

    _, sel = lax.fori_loop(0, n_sel, pick, (score, jnp.zeros((LANES, Q_BLOCK), f32)))
    bias = jnp.where(sel.T > 0.5, 0.0, NEG).astype(bf16)
    qaug_sc[...] = jnp.concatenate([jnp.concatenate([bias] * B_GQA, axis=0), q_feat], axis=1)

    m_sc[...] = jnp.full(m_sc.shape, -jnp.inf, f32)
    acc_sc[...] = jnp.zeros(acc_sc.shape, f32)

    def logits(j, s_ref):
        k0 = pl.multiple_of(j * tk, tk)
        s_ref[...] = lax.dot_general(qaug_sc[...], kaug_sc[pl.ds(k0, tk), :], dn, preferred_element_type=f32)

    def update(j, s_ref, diagonal):
        k0 = pl.multiple_of(j * tk, tk)
        s = s_ref[...]
        if diagonal:
            s = jnp.where(k0 + lax.broadcasted_iota(jnp.int32, (1, tk), 1) <= tq, s, -jnp.inf)
        m_old = m_sc[...]
        m_new = jnp.maximum(m_old, jnp.broadcast_to(jnp.max(s, axis=1, keepdims=True), m_old.shape))
        p = jnp.exp(s - jnp.concatenate([m_new] * (tk // LANES), axis=1)).astype(bf16)
        pv = jnp.dot(p, vaug_sc[pl.ds(k0, tk), :], preferred_element_type=f32)
        acc_sc[...] = jnp.exp(m_old - m_new) * acc_sc[...] + pv
        m_sc[...] = m_new

    j_last = t0 // tk
    logits(0, sa_sc)

    def tile_pair(i, carry):
        j = 2 * i
        logits(j + 1, sb_sc)
        update(j, sa_sc, False)
        logits(j + 2, sa_sc)
        update(j + 1, sb_sc, False)
        return carry

    lax.fori_loop(0, j_last // 2, tile_pair, 0)
    odd = (j_last % 2) == 1

    @pl.when(odd)
    def _two_left():
        logits(j_last, sb_sc)
        update(j_last - 1, sa_sc, False)
        update(j_last, sb_sc, True)

    @pl.when(jnp.logical_not(odd))
    def _one_left():
        update(j_last, sa_sc, True)

    acc = acc_sc[...]
    o_s = acc[:, 0:E] / jnp.maximum(acc[:, E:E + 1], TINY)

    gates = 1.0 / (1.0 + jnp.exp(-gate_ref[...]))
    for r in range(B_GQA):
        rs = slice(r * Q_BLOCK, (r + 1) * Q_BLOCK)
        out = (gates[:, 3 * r:3 * r + 1] * o_c[rs] + gates[:, 3 * r + 1:3 * r + 2] * o_s[rs]
               + gates[:, 3 * r + 2:3 * r + 3] * o_w[rs])
        o_ref[:, r * E:(r + 1) * E] = out.astype(o_ref.dtype)


def _nsa(u_nat, u_f32, cmp_kv, batch, seq):
    nq = seq // Q_BLOCK
    n_chunk = seq // CMP_STRIDE
    n_cmp = n_chunk - CMP_LEN // CMP_STRIDE + 1
    n_slc = seq // SLC_LEN
    n_sel = min(SLC_TOPK, n_slc)
    tk = min(512, seq)
    assert n_slc <= LANES and B_HEADS == 8 and seq % tk == 0
    c_start = np.arange(n_chunk) * CMP_STRIDE
    s_start = np.arange(LANES) * SLC_LEN
    overlap_t = ((c_start[None, :] < s_start[:, None] + SLC_LEN) & (c_start[None, :] + CMP_LEN > s_start[:, None])
                 & (np.arange(n_chunk)[None, :] < n_cmp) & (np.arange(LANES)[:, None] < n_slc))
    pos = np.arange(seq)
    kconst = np.zeros((seq, 2 * LANES), np.float32)
    kconst[pos, pos // SLC_LEN] = 1.0
    kconst[:, LANES + 0] = pos // SLC_LEN
    kconst[:, LANES + 1] = pos % SLC_LEN
    kconst[:, LANES + 2] = 1.0
    kconst[:, LANES + 3] = 1.0
    kconst[:, LANES + N_FEAT] = 1.0
    rows = B_GQA * Q_BLOCK
    slopes = jnp.exp2(-8.0 * jnp.arange(1, B_HEADS + 1, dtype=jnp.float32) / B_HEADS)
    slopes = jnp.repeat(slopes.reshape(B_KV_HEADS, B_GQA), Q_BLOCK, axis=1).reshape(B_KV_HEADS, rows, 1)
    kv_rows = pl.BlockSpec((seq, LANES), lambda b, g, n: (b, NAT_SLC + g))
    win_rows = pl.BlockSpec((seq, LANES), lambda b, g, n: (b, NAT_WIN + g))
    cmp_spec = lambda s: pl.BlockSpec((None, None, None, n_chunk, HEAD_DIM), lambda b, g, n: (s, b, g, 0, 0))
    return pl.pallas_call(
        functools.partial(_nsa_kernel, n_sel=n_sel, tk=tk),
        grid=(batch, B_KV_HEADS, nq),
        in_specs=[pl.BlockSpec((Q_BLOCK, B_GQA * HEAD_DIM), lambda b, g, n: (b * nq + n, NAT_Q_B + g)),
                  cmp_spec(0), cmp_spec(1), kv_rows, win_rows,
                  pl.BlockSpec((Q_BLOCK, LANES), lambda b, g, n: (b * nq + n, F_GATE + g)),
                  pl.BlockSpec((LANES, n_chunk), lambda b, g, n: (0, 0)),
                  pl.BlockSpec((seq, 2 * LANES), lambda b, g, n: (0, 0)),
                  pl.BlockSpec((None, rows, 1), lambda b, g, n: (g, 0, 0))],
        out_specs=pl.BlockSpec((Q_BLOCK, B_GQA * HEAD_DIM), lambda b, g, n: (b * nq + n, g)),
        out_shape=jax.ShapeDtypeStruct((batch * seq, B_WIDTH), jnp.bfloat16),
        scratch_shapes=[pltpu.VMEM((seq, 2 * LANES), jnp.bfloat16), pltpu.VMEM((seq, LANES), jnp.bfloat16),
                        pltpu.VMEM((seq, LANES), jnp.bfloat16), pltpu.VMEM((seq, LANES), jnp.bfloat16),
                        pltpu.VMEM((rows, LANES), jnp.float32), pltpu.VMEM((rows, LANES), jnp.float32),
                        pltpu.VMEM((rows, 2 * LANES), jnp.bfloat16),
                        pltpu.VMEM((rows, tk), jnp.float32), pltpu.VMEM((rows, tk), jnp.float32)],
        compiler_params=_params(("arbitrary", "arbitrary", "arbitrary")),
    )(u_nat, cmp_kv, cmp_kv, u_nat, u_nat, u_f32, jnp.asarray(overlap_t, jnp.bfloat16), jnp.asarray(kconst, jnp.bfloat16),
      slopes)


def _layer_norm(v, gain, bias):
    mu = jnp.mean(v, axis=1, keepdims=True)
    c = v - mu
    var = jnp.mean(c * c, axis=1, keepdims=True)
    return c * lax.rsqrt(var + LN_EPS) * gain + bias


def _sigmoid(v):
    return 1.0 / (1.0 + jnp.exp(-v))


def _token_major(ref, stage):
    dil, n = ref.shape[0], ref.shape[1]
    parts = stage.shape[0]
    for r in range(dil):
        for j in range(parts):
            stage[j, pl.ds(r, n, stride=dil), :] = ref[r, :, j * LANES:(j + 1) * LANES]
    return jnp.concatenate([stage[j] for j in range(parts)], axis=1)


def _merge_kernel(*refs):
    dil_refs, refs = refs[:2 * N_DIL], refs[2 * N_DIL:]
    (yb_ref, mga_ref, mgb_ref, x_ref, wa_ref, wb_ref, wo_ref, g_ref, b_ref, wr_hi_ref, wr_lo_ref, br_ref,
     h_ref, rw_ref, re_ref, cnt_ref) = refs[:16]
    stages, cnt_sc = refs[16:16 + 2 * (N_DIL - 1)], refs[16 + 2 * (N_DIL - 1)]
    f32 = jnp.float32
    outs, lse = [dil_refs[0][...]], [dil_refs[1][...]]
    for g in range(1, N_DIL):
        outs.append(_token_major(dil_refs[2 * g], stages[2 * g - 2]))
        lse.append(_token_major(dil_refs[2 * g + 1], stages[2 * g - 1]))
    top = jnp.maximum(jnp.maximum(lse[0], lse[1]), lse[2])
    e = [jnp.exp(v - top) for v in lse]
    y_a = (e[0] * outs[0] + e[1] * outs[1] + e[2] * outs[2]) / (e[0] + e[1] + e[2])
    br_a = jnp.dot(y_a.astype(jnp.bfloat16), wa_ref[...], preferred_element_type=f32)
    br_b = jnp.dot(yb_ref[...], wb_ref[...], preferred_element_type=f32)
    merged = _sigmoid(mga_ref[...]) * br_a + _sigmoid(mgb_ref[...]) * br_b
    mix = jnp.dot(merged.astype(jnp.bfloat16), wo_ref[...], preferred_element_type=f32)
    h = _layer_norm(ALPHA * x_ref[...] + mix, g_ref[...], b_ref[...])
    h_ref[...] = h

    h_hi = h.astype(jnp.bfloat16)
    h_lo = (h - h_hi.astype(f32)).astype(jnp.bfloat16)
    logits = (jnp.dot(h_hi, wr_hi_ref[...], preferred_element_type=f32)
              + jnp.dot(h_lo, wr_hi_ref[...], preferred_element_type=f32)
              + jnp.dot(h_hi, wr_lo_ref[...], preferred_element_type=f32)) + br_ref[...]
    lane = lax.broadcasted_iota(jnp.int32, logits.shape, 1)
    coarse = jnp.where(lane < N_GROUPS, logits, -jnp.inf)
    c_max = jnp.max(coarse, axis=1, keepdims=True)
    grp = jnp.min(jnp.where(coarse == c_max, lane, LANES), axis=1, keepdims=True)
    p_grp = 1.0 / jnp.sum(jnp.exp(coarse - c_max), axis=1, keepdims=True)
    lo_lane = N_GROUPS + EXP_PER_GROUP * grp
    fine = jnp.where((lane >= lo_lane) & (lane < lo_lane + EXP_PER_GROUP), logits, -jnp.inf)
    v1 = jnp.max(fine, axis=1, keepdims=True)
    i1 = jnp.min(jnp.where(fine == v1, lane, LANES), axis=1, keepdims=True)
    fine = jnp.where(lane == i1, -jnp.inf, fine)
    v2 = jnp.max(fine, axis=1, keepdims=True)
    i2 = jnp.min(jnp.where(fine == v2, lane, LANES), axis=1, keepdims=True)
    e2 = jnp.exp(v2 - v1)
    w1 = p_grp / (1.0 + e2)
    w2 = p_grp * e2 / (1.0 + e2)
    e_1, e_2 = i1 - N_GROUPS, i2 - N_GROUPS
    rw_ref[...] = jnp.where(lane == 0, w1, jnp.where(lane == 1, w2, 0.0))

    @pl.when(pl.program_id(0) == 0)
    def _zero_counts():
        cnt_sc[...] = jnp.zeros(cnt_sc.shape, f32)

    tm = logits.shape[0]
    onehot = jnp.where((lane == e_1) | (lane == e_2), 1.0, 0.0)
    earlier = (lax.broadcasted_iota(jnp.int32, (tm, tm), 0) > lax.broadcasted_iota(jnp.int32, (tm, tm), 1))
    before = jnp.dot(jnp.where(earlier, 1.0, 0.0).astype(jnp.bfloat16), onehot.astype(jnp.bfloat16),
                     preferred_element_type=f32) + cnt_sc[...]
    r_1 = jnp.sum(jnp.where(lane == e_1, before, 0.0), axis=1, keepdims=True).astype(jnp.int32)
    r_2 = jnp.sum(jnp.where(lane == e_2, before, 0.0), axis=1, keepdims=True).astype(jnp.int32)
    cnt_sc[...] = cnt_sc[...] + jnp.sum(onehot, axis=0, keepdims=True)
    cnt_ref[...] = cnt_sc[...].astype(jnp.int32)
    re_ref[...] = jnp.where(lane == 0, e_1, jnp.where(lane == 1, e_2,
                            jnp.where(lane == 2, r_1, jnp.where(lane == 3, r_2, 0))))


def _merge(dil_outs, y_b, u_f32, x2d, seq, wa, wb, wo, ln_g, ln_b, wr_hi, wr_lo, b_r, tm=256):
    t = x2d.shape[0]
    per_batch = seq // tm
    assert seq % tm == 0 and all(tm % (dil * 8) == 0 for _, dil in DIL_GROUPS)
    row = lambda w: pl.BlockSpec((tm, w), lambda i: (i, 0))
    full = lambda a: pl.BlockSpec(a.shape, lambda i: (0,) * a.ndim)
    dil_specs = [pl.BlockSpec((None, None, tm, A_OUT), lambda i: (i // per_batch, 0, i % per_batch, 0))] * 2
    stages = []
    for _, dil in DIL_GROUPS[1:]:
        dil_specs += [pl.BlockSpec((None, dil, tm // dil, A_OUT),
                                   lambda i: (i // per_batch, 0, i % per_batch, 0))] * 2
        stages += [pltpu.VMEM((A_OUT // LANES, tm, LANES), jnp.float32)] * 2
    return pl.pallas_call(
        _merge_kernel,
        grid=(t // tm,),
        in_specs=dil_specs + [row(B_WIDTH),
                  pl.BlockSpec((tm, D_MODEL), lambda i: (i, F_MGATE)),
                  pl.BlockSpec((tm, D_MODEL), lambda i: (i, F_MGATE + 1)),
                  row(D_MODEL), full(wa), full(wb), full(wo), full(ln_g), full(ln_b),
                  full(wr_hi), full(wr_lo), full(b_r)],
        out_specs=[row(D_MODEL), row(LANES), row(LANES), pl.BlockSpec((1, LANES), lambda i: (0, 0))],
        out_shape=[jax.ShapeDtypeStruct((t, D_MODEL), jnp.float32),
                   jax.ShapeDtypeStruct((t, LANES), jnp.float32),
                   jax.ShapeDtypeStruct((t, LANES), jnp.int32),
                   jax.ShapeDtypeStruct((1, LANES), jnp.int32)],
        scratch_shapes=stages + [pltpu.VMEM((1, LANES), jnp.float32)],
        compiler_params=_params(("arbitrary",)),
    )(*dil_outs, y_b, u_f32, u_f32, x2d, wa, wb, wo, ln_g, ln_b, wr_hi, wr_lo, b_r)


SUBLANES = 8
GATHER_UNROLL = 8


def _row_copy(idx_ref, k, src3, dst3, r, sem):
    row = idx_ref[0, 0, k]
    return pltpu.make_async_copy(src3.at[row >> 3, pl.ds(row & (SUBLANES - 1), 1), :],
                                 dst3.at[r >> 3, pl.ds(r & (SUBLANES - 1), 1), :], sem)


def _start_row_gather(idx_ref, idx_of, src3, dst3, sem, unrolled):
    n = dst3.shape[0] * SUBLANES
    if unrolled:
        for r in range(n):
            _row_copy(idx_ref, idx_of(r), src3, dst3, r, sem).start()
    else:
        def start(r, carry):
            _row_copy(idx_ref, idx_of(r), src3, dst3, r, sem).start()
            return carry

        lax.fori_loop(0, n, start, 0, unroll=GATHER_UNROLL)


def _wait_row_gather(src3, dst3, sem):
    pltpu.make_async_copy(src3.at[pl.ds(0, dst3.shape[0])], dst3, sem).wait()


def _expert_kernel(blk_e_ref, tok_ref, tok_next_ref, h3_hbm, wgu_ref, wd_ref, y_ref, xbuf_a, xbuf_b, sems):
    i = pl.program_id(0)
    last = pl.num_programs(0) - 1
    same = lambda r: r

    @pl.when(i == 0)
    def _first_block():
        _start_row_gather(tok_ref, same, h3_hbm, xbuf_a, sems.at[0], unrolled=False)

    def step(cur, nxt, sem_cur, sem_nxt):
        _wait_row_gather(h3_hbm, cur, sem_cur)
        _start_row_gather(tok_next_ref, same, h3_hbm, nxt, sem_nxt, unrolled=True)
        xb = cur[...].reshape(y_ref.shape).astype(jnp.bfloat16)
        gu = jnp.dot(xb, wgu_ref[...], preferred_element_type=jnp.float32)
        gate, up = gu[:, :D_EXPERT], gu[:, D_EXPERT:]
        act = (gate * _sigmoid(gate)) * up
        y_ref[...] = jnp.dot(act.astype(jnp.bfloat16), wd_ref[...], preferred_element_type=jnp.float32)

        @pl.when(i == last)
        def _drain():
            _wait_row_gather(h3_hbm, nxt, sem_nxt)

    @pl.when(i % 2 == 0)
    def _even():
        step(xbuf_a, xbuf_b, sems.at[0], sems.at[1])

    @pl.when(i % 2 == 1)
    def _odd():
        step(xbuf_b, xbuf_a, sems.at[1], sems.at[0])


def _experts(h, tok_buf, blk_e, wgu, wd, bm):
    p = tok_buf.shape[0]
    nb = p // bm
    tok = tok_buf.reshape(nb, 1, bm)
    buf = pltpu.VMEM((bm // SUBLANES, SUBLANES, D_MODEL), jnp.float32)
    return pl.pallas_call(
        _expert_kernel,
        grid_spec=pltpu.PrefetchScalarGridSpec(
            num_scalar_prefetch=1,
            grid=(nb,),
            in_specs=[pl.BlockSpec((1, 1, bm), lambda i, e: (i, 0, 0), memory_space=pltpu.SMEM),
                      pl.BlockSpec((1, 1, bm), lambda i, e: (jnp.minimum(i + 1, nb - 1), 0, 0),
                                   memory_space=pltpu.SMEM),
                      pl.BlockSpec(memory_space=pl.ANY),
                      pl.BlockSpec((None, D_MODEL, 2 * D_EXPERT), lambda i, e: (e[i], 0, 0)),
                      pl.BlockSpec((None, D_EXPERT, D_MODEL), lambda i, e: (e[i], 0, 0))],
            out_specs=pl.BlockSpec((bm, D_MODEL), lambda i, e: (i, 0)),
            scratch_shapes=[buf, buf, pltpu.SemaphoreType.DMA((2,))],
        ),
        out_shape=jax.ShapeDtypeStruct((p, D_MODEL), jnp.float32),
        compiler_params=_params(("arbitrary",)),
    )(blk_e, tok, tok, h.reshape(-1, SUBLANES, D_MODEL), wgu, wd)


def _combine_kernel(dest_ref, dest_next_ref, y3_hbm, h_ref, rw_ref, g_ref, b_ref, o_ref, ybuf_a, ybuf_b, sems):
    i = pl.program_id(0)
    last = pl.num_programs(0) - 1

    def start(idx_ref, buf, sem, unrolled):
        for k in range(MOE_TOP_K):
            _start_row_gather(idx_ref, lambda r, k=k: MOE_TOP_K * r + k, y3_hbm, buf.at[k], sem, unrolled)

    def wait(buf, sem):
        for k in range(MOE_TOP_K):
            _wait_row_gather(y3_hbm, buf.at[k], sem)

    @pl.when(i == 0)
    def _first_tile():
        start(dest_ref, ybuf_a, sems.at[0], unrolled=False)

    def step(cur, nxt, sem_cur, sem_nxt):
        wait(cur, sem_cur)
        start(dest_next_ref, nxt, sem_nxt, unrolled=True)
        rw = rw_ref[...]
        ffn = rw[:, 0:1] * cur[0].reshape(h_ref.shape) + rw[:, 1:2] * cur[1].reshape(h_ref.shape)
        o_ref[...] = _layer_norm(ALPHA * h_ref[...] + ffn, g_ref[...], b_ref[...])

        @pl.when(i == last)
        def _drain():
            wait(nxt, sem_nxt)

    @pl.when(i % 2 == 0)
    def _even():
        step(ybuf_a, ybuf_b, sems.at[0], sems.at[1])

    @pl.when(i % 2 == 1)
    def _odd():
        step(ybuf_b, ybuf_a, sems.at[1], sems.at[0])


def _combine(dest, y_sorted, h, rw, ln_g, ln_b, tm=256):
    t = h.shape[0]
    assert t % tm == 0
    nt = t // tm
    idx = dest.reshape(nt, 1, MOE_TOP_K * tm)
    buf = pltpu.VMEM((MOE_TOP_K, tm // SUBLANES, SUBLANES, D_MODEL), jnp.float32)
    return pl.pallas_call(
        _combine_kernel,
        grid=(nt,),
        in_specs=[pl.BlockSpec((1, 1, MOE_TOP_K * tm), lambda i: (i, 0, 0), memory_space=pltpu.SMEM),
                  pl.BlockSpec((1, 1, MOE_TOP_K * tm), lambda i: (jnp.minimum(i + 1, nt - 1), 0, 0),
                               memory_space=pltpu.SMEM),
                  pl.BlockSpec(memory_space=pl.ANY),
                  pl.BlockSpec((tm, D_MODEL), lambda i: (i, 0)),
                  pl.BlockSpec((tm, LANES), lambda i: (i, 0)),
                  pl.BlockSpec((1, D_MODEL), lambda i: (0, 0)),
                  pl.BlockSpec((1, D_MODEL), lambda i: (0, 0))],
        out_specs=pl.BlockSpec((tm, D_MODEL), lambda i: (i, 0)),
        out_shape=jax.ShapeDtypeStruct((t, D_MODEL), jnp.float32),
        scratch_shapes=[buf, buf, pltpu.SemaphoreType.DMA((2,))],
        compiler_params=_params(("arbitrary",)),
    )(idx, idx, y_sorted.reshape(-1, SUBLANES, D_MODEL), h, rw, ln_g, ln_b)


def _dispatch(route, counts, bm):
    t = route.shape[0]
    n_assign = t * MOE_TOP_K
    expert, rank = route[:, 0:MOE_TOP_K], route[:, MOE_TOP_K:2 * MOE_TOP_K]
    starts = jnp.cumsum(counts) - counts
    pcounts = (counts + bm - 1) // bm * bm
    pends = jnp.cumsum(pcounts)
    pstarts = pends - pcounts
    ids = jnp.arange(N_EXPERTS, dtype=jnp.int32)
    dest = jnp.sum(jnp.where(expert[..., None] == ids, pstarts, 0), axis=-1) + rank
    p = -(-(n_assign + N_EXPERTS * (bm - 1)) // bm) * bm
    nb = p // bm
    blk_e = jnp.minimum(jnp.sum(pends[None, :] <= (jnp.arange(nb, dtype=jnp.int32) * bm)[:, None], axis=1),
                        N_EXPERTS - 1).astype(jnp.int32)
    order = jnp.argsort(expert.reshape(n_assign), stable=True).astype(jnp.int32)
    row_e = jnp.repeat(blk_e, bm)
    k = jnp.arange(p, dtype=jnp.int32) - pstarts[row_e]
    src = jnp.clip(starts[row_e] + k, 0, n_assign - 1)
    tok_buf = jnp.where(k < counts[row_e], order[src] // MOE_TOP_K, 0).astype(jnp.int32)
    return tok_buf, dest.astype(jnp.int32), blk_e


def _split_bf16(w):
    hi = w.astype(jnp.bfloat16)
    return hi, (w - hi.astype(jnp.float32)).astype(jnp.bfloat16)


def kernel(x, w_in, cmp_pos_k, cmp_w1_k, cmp_w2_k, cmp_pos_v, cmp_w1_v, cmp_w2_v, w_branch_a, w_branch_b, w_out, ln1_g, ln1_b, w_coarse, b_coarse, w_fine, b_fine, w_gate_up, w_down, ln2_g, ln2_b):
    batch, seq, d = x.shape
    bf16 = jnp.bfloat16
    h = x.reshape(batch * seq, d)
    cols_bf, cols_f32, scale_bf = _in_proj_columns()
    half = CMP_STRIDE * HEAD_DIM
    for l in range(DEPTH):
        u_nat, *groups = _project_bf(h, (_permute_columns(w_in[l], cols_bf) * scale_bf).astype(bf16), batch, seq)
        u_f32, ckv = _project_f32(h, _permute_columns(w_in[l], cols_f32).astype(bf16), batch, seq)
        dil_outs = _dilated(u_nat, groups, batch, seq)
        pos = jnp.stack([cmp_pos_k[l], cmp_pos_v[l]]).reshape(2, 2, 1, half)
        w1 = jnp.stack([cmp_w1_k[l], cmp_w1_v[l]]).astype(bf16)
        w2 = jnp.stack([cmp_w2_k[l], cmp_w2_v[l]]).astype(bf16)
        cmp_kv = _compress(ckv, pos, w1, w2)
        y_b = _nsa(u_nat, u_f32, cmp_kv, batch, seq)
        w_r = jnp.concatenate([w_coarse[l], w_fine[l].transpose(1, 0, 2).reshape(d, N_EXPERTS)], axis=1)
        w_r = jnp.pad(w_r, ((0, 0), (0, LANES - w_r.shape[1])))
        b_r = jnp.pad(jnp.concatenate([b_coarse[l], b_fine[l].reshape(N_EXPERTS)]),
                      (0, LANES - N_GROUPS - N_EXPERTS)).reshape(1, LANES)
        wr_hi, wr_lo = _split_bf16(w_r)
        h1, rw, route, counts = _merge(dil_outs, y_b, u_f32, h, seq, w_branch_a[l].astype(bf16),
                                       w_branch_b[l].astype(bf16), w_out[l].astype(bf16),
                                       ln1_g[l].reshape(1, d), ln1_b[l].reshape(1, d), wr_hi, wr_lo, b_r)
        bm = 256
        tok_buf, dest, blk_e = _dispatch(route, counts[0, :N_EXPERTS], bm)
        y_sorted = _experts(h1, tok_buf, blk_e, w_gate_up[l].astype(bf16), w_down[l].astype(bf16), bm)
        h = _combine(dest, y_sorted, h1, rw, ln2_g[l].reshape(1, d), ln2_b[l].reshape(1, d))
    return h.reshape(batch, seq, d)
```

```python
import functools

import numpy as np
import jax
import jax.numpy as jnp
from jax import lax
from jax.experimental import pallas as pl
from jax.experimental.pallas import tpu as pltpu

D_MODEL = 1024
HEAD_DIM = 64
Q_BLOCK = 128
DIL_GROUPS = ((128, 1), (512, 4), (2048, 16))
N_DIL = len(DIL_GROUPS)
A_HEADS_PER_GROUP = 4
A_HEADS = A_HEADS_PER_GROUP * N_DIL
A_WIDTH = A_HEADS * HEAD_DIM
A_OUT = A_HEADS_PER_GROUP * HEAD_DIM
B_HEADS = 8
B_KV_HEADS = 2
B_GQA = B_HEADS // B_KV_HEADS
B_WIDTH = B_HEADS * HEAD_DIM
B_KV_WIDTH = B_KV_HEADS * HEAD_DIM
CMP_LEN = 32
CMP_STRIDE = 16
CMP_HIDDEN = 256
SLC_LEN = 64
SLC_TOPK = 16
WIN = 512
N_GROUPS = 4
EXP_PER_GROUP = 8
N_EXPERTS = N_GROUPS * EXP_PER_GROUP
D_EXPERT = 512
MOE_TOP_K = 2
DEPTH = 1
ALPHA = (2.0 * DEPTH) ** 0.25
LN_EPS = 1e-5
FORCE_BONUS = 1e4
TINY = 1e-30
ATTN_SCALE = HEAD_DIM ** -0.5

LANES = 128
VMEM_LIMIT = 48 * 1024 * 1024

QKV_COLS = 3 * A_OUT
NAT_Q_B = QKV_COLS // A_OUT
NAT_SLC = (QKV_COLS + B_WIDTH) // LANES
NAT_WIN = NAT_SLC + B_KV_HEADS
NAT_COLS = QKV_COLS + B_WIDTH + 4 * B_KV_WIDTH
BF_COLS = NAT_COLS + (N_DIL - 1) * QKV_COLS
F_MGATE = 0
F_GATE = 2 * D_MODEL // LANES
F_NAT_COLS = 2 * D_MODEL + B_KV_HEADS * LANES
F_COLS = F_NAT_COLS + 2 * B_KV_WIDTH


def _params(semantics):
    return pltpu.CompilerParams(dimension_semantics=semantics, vmem_limit_bytes=VMEM_LIMIT)


def _in_proj_columns():
    kv_off = 3 * A_WIDTH + B_WIDTH
    gate_off = kv_off + 6 * B_KV_WIDTH
    mg_off = gate_off + 3 * B_HEADS
    def qkv_cols(g):
        return [part * A_WIDTH + g * A_OUT + c for part in range(3) for c in range(A_OUT)]

    bf = qkv_cols(0) + list(range(3 * A_WIDTH, 3 * A_WIDTH + B_WIDTH))
    for first in (2, 4):
        for g in range(B_KV_HEADS):
            for i in (first, first + 1):
                base = kv_off + i * B_KV_WIDTH + g * HEAD_DIM
                bf += list(range(base, base + HEAD_DIM))
    for g in range(1, N_DIL):
        bf += qkv_cols(g)
    f32 = list(range(mg_off, mg_off + 2 * D_MODEL))
    per_head = 3 * B_GQA
    for g in range(B_KV_HEADS):
        f32 += list(range(gate_off + g * per_head, gate_off + (g + 1) * per_head))
        f32 += [-1] * (LANES - per_head)
    f32 += list(range(kv_off, kv_off + 2 * B_KV_WIDTH))
    assert len(bf) == BF_COLS and len(f32) == F_COLS
    bf = np.asarray(bf)
    is_q = (bf < A_WIDTH) | ((bf >= 3 * A_WIDTH) & (bf < 3 * A_WIDTH + B_WIDTH))
    scale = np.where(is_q, ATTN_SCALE, 1.0).astype(np.float32)
    return bf, np.asarray(f32), scale


def _permute_columns(w, cols):
    taken = jnp.take(w, jnp.asarray(np.maximum(cols, 0)), axis=1)
    return jnp.where(jnp.asarray(cols >= 0)[None, :], taken, 0.0)


def _proj_f32_kernel(x_ref, w_ref, nat_ref, cmp_ref, stage):
    xb = x_ref[...].astype(jnp.bfloat16)
    tm = x_ref.shape[0]
    n_chunk = tm // CMP_STRIDE
    for c0 in range(0, F_NAT_COLS, A_OUT):
        nat_ref[:, c0:c0 + A_OUT] = jnp.dot(xb, w_ref[:, c0:c0 + A_OUT], preferred_element_type=jnp.float32)
    res = jnp.dot(xb, w_ref[:, F_NAT_COLS:F_COLS], preferred_element_type=jnp.float32)
    lower = lax.broadcasted_iota(jnp.int32, (n_chunk, LANES), 1) < HEAD_DIM
    for kv in range(2):
        stage[...] = res[:, kv * LANES:(kv + 1) * LANES]
        for p in range(0, CMP_STRIDE, 2):
            even = stage[pl.ds(p, n_chunk, stride=CMP_STRIDE), :]
            odd = stage[pl.ds(p + 1, n_chunk, stride=CMP_STRIDE), :]
            cols = slice(p * HEAD_DIM, (p + 2) * HEAD_DIM)
            cmp_ref[kv, 0, :, cols] = jnp.where(lower, even, pltpu.roll(odd, HEAD_DIM, 1))
            cmp_ref[kv, 1, :, cols] = jnp.where(lower, pltpu.roll(even, HEAD_DIM, 1), odd)


def _project_f32(x2d, w, batch, seq, tm=512):
    t, d = x2d.shape
    assert seq % tm == 0 and w.shape[1] == F_COLS and B_KV_HEADS * HEAD_DIM == LANES
    per_batch = seq // tm
    half = CMP_STRIDE * HEAD_DIM
    return pl.pallas_call(
        _proj_f32_kernel,
        grid=(t // tm,),
        in_specs=[pl.BlockSpec((tm, d), lambda i: (i, 0)),
                  pl.BlockSpec((d, F_COLS), lambda i: (0, 0))],
        out_specs=[pl.BlockSpec((tm, F_NAT_COLS), lambda i: (i, 0)),
                   pl.BlockSpec((2, None, B_KV_HEADS, tm // CMP_STRIDE, half),
                                lambda i: (0, i // per_batch, 0, i % per_batch, 0))],
        out_shape=[jax.ShapeDtypeStruct((t, F_NAT_COLS), jnp.float32),
                   jax.ShapeDtypeStruct((2, batch, B_KV_HEADS, seq // CMP_STRIDE, half), jnp.float32)],
        scratch_shapes=[pltpu.VMEM((tm, LANES), jnp.float32)],
        compiler_params=_params(("parallel",)),
    )(x2d, w)


def _proj_bf_kernel(x_ref, w_ref, nat_ref, *rest, dils):
    group_refs, stage = rest[:-1], rest[-1]
    xb = x_ref[...].astype(jnp.bfloat16)
    tm = x_ref.shape[0]
    for c0 in range(0, NAT_COLS, A_OUT):
        nat_ref[:, c0:c0 + A_OUT] = jnp.dot(
            xb, w_ref[:, c0:c0 + A_OUT], preferred_element_type=jnp.float32).astype(nat_ref.dtype)
    col = NAT_COLS
    for o_ref, dil in zip(group_refs, dils):
        for part in range(3):
            res = jnp.dot(xb, w_ref[:, col:col + A_OUT], preferred_element_type=jnp.float32)
            for j in range(A_OUT // LANES):
                stage[j] = res[:, j * LANES:(j + 1) * LANES]
                c0 = part * A_OUT + j * LANES
                for r in range(dil):
                    o_ref[r, :, c0:c0 + LANES] = stage[j, pl.ds(r, tm // dil, stride=dil), :].astype(o_ref.dtype)
            col += A_OUT


def _project_bf(x2d, w, batch, seq, tm=512):
    t, d = x2d.shape
    dils = tuple(dil for _, dil in DIL_GROUPS[1:])
    assert seq % tm == 0 and all(tm % (dil * 16) == 0 for dil in dils) and w.shape[1] == BF_COLS
    per_batch = seq // tm
    out_specs = [pl.BlockSpec((tm, NAT_COLS), lambda i: (i, 0))]
    out_shapes = [jax.ShapeDtypeStruct((t, NAT_COLS), jnp.bfloat16)]
    for dil in dils:
        out_specs.append(pl.BlockSpec((None, dil, tm // dil, QKV_COLS),
                                      lambda i: (i // per_batch, 0, i % per_batch, 0)))
        out_shapes.append(jax.ShapeDtypeStruct((batch, dil, seq // dil, QKV_COLS), jnp.bfloat16))
    return pl.pallas_call(
        functools.partial(_proj_bf_kernel, dils=dils),
        grid=(t // tm,),
        in_specs=[pl.BlockSpec((tm, d), lambda i: (i, 0)),
                  pl.BlockSpec((d, BF_COLS), lambda i: (0, 0))],
        out_specs=out_specs,
        out_shape=out_shapes,
        scratch_shapes=[pltpu.VMEM((A_OUT // LANES, tm, LANES), jnp.float32)],
        compiler_params=_params(("parallel",)),
    )(x2d, w)


def _dilated_kernel(*refs, nbs):
    bias_ref, ins, outs = refs[0], refs[1:1 + 5 * N_DIL], refs[1 + 5 * N_DIL:]
    i = pl.program_id(1)
    dn = (((1,), (1,)), ((), ()))
    f32, bf16 = jnp.float32, jnp.bfloat16
    lower = lax.broadcasted_iota(jnp.int32, (Q_BLOCK, LANES), 1) < HEAD_DIM
    heads = [(g, pair, half) for g in range(N_DIL) for pair in range(A_OUT // LANES) for half in range(2)]

    def col(ref, pair):
        return ref[:, pair * LANES:(pair + 1) * LANES]

    logits = []
    for g, pair, half in heads:
        q_ref, kp_ref, kc_ref = ins[5 * g:5 * g + 3]
        head = g * A_HEADS_PER_GROUP + 2 * pair + half
        q = col(q_ref, pair)
        q = jnp.where(lower if half == 0 else jnp.logical_not(lower), q, jnp.zeros_like(q))
        no_prev = jnp.where((i % nbs[g]) > 0, 0.0, -jnp.inf)
        s_p = lax.dot_general(q, col(kp_ref, pair), dn, preferred_element_type=f32)
        s_c = lax.dot_general(q, col(kc_ref, pair), dn, preferred_element_type=f32)
        logits.append((s_p + bias_ref[head, :, 0:Q_BLOCK] + no_prev, s_c + bias_ref[head, :, Q_BLOCK:2 * Q_BLOCK]))

    probs = []
    for s_p, s_c in logits:
        m = jnp.max(jnp.maximum(s_p, s_c), axis=1, keepdims=True)
        p_p, p_c = jnp.exp(s_p - m), jnp.exp(s_c - m)
        l = jnp.sum(p_p + p_c, axis=1, keepdims=True)
        probs.append((p_p.astype(bf16), p_c.astype(bf16), l, m + jnp.log(l)))

    for n_head in range(0, len(heads), 2):
        g, pair, _ = heads[n_head]
        vp_ref, vc_ref = ins[5 * g + 3:5 * g + 5]
        o_ref, lse_ref = outs[2 * g:2 * g + 2]
        out, lse = [], []
        for p_p, p_c, l, lse_h in probs[n_head:n_head + 2]:
            acc = jnp.dot(p_p, col(vp_ref, pair), preferred_element_type=f32)
            acc += jnp.dot(p_c, col(vc_ref, pair), preferred_element_type=f32)
            out.append(acc / l)
            lse.append(jnp.broadcast_to(lse_h, (Q_BLOCK, LANES)))
        o_ref[:, pair * LANES:(pair + 1) * LANES] = jnp.where(lower, out[0], out[1])
        lse_ref[:, pair * LANES:(pair + 1) * LANES] = jnp.where(lower, lse[0], lse[1])


def _dilated_bias():
    qi = Q_BLOCK + jnp.arange(Q_BLOCK)
    kj = jnp.arange(2 * Q_BLOCK)
    delta = qi[:, None] - kj[None, :]
    slopes = jnp.exp2(-8.0 * jnp.arange(1, A_HEADS + 1, dtype=jnp.float32) / A_HEADS)
    tables = []
    for g, (window, dil) in enumerate(DIL_GROUPS):
        on_band = (delta >= 0) & (delta <= window // dil)
        dist = (delta * dil).astype(jnp.float32)
        for h in range(A_HEADS_PER_GROUP):
            tables.append(jnp.where(on_band, -slopes[g * A_HEADS_PER_GROUP + h] * dist, -jnp.inf))
    return jnp.stack(tables)


def _dilated(u_nat, groups, batch, seq):
    steps = seq // Q_BLOCK
    arrays = [u_nat.reshape(batch, 1, seq, NAT_COLS)] + list(groups)
    bias = _dilated_bias()
    in_arrays, in_specs = [bias], [pl.BlockSpec(bias.shape, lambda b, i: (0, 0, 0))]
    out_specs, out_shapes, nbs = [], [], []
    blk = (None, None, Q_BLOCK, A_OUT)
    for g, (window, dil) in enumerate(DIL_GROUPS):
        length = seq // dil
        assert length % Q_BLOCK == 0 and window // dil == Q_BLOCK
        nb = length // Q_BLOCK
        nbs.append(nb)

        def cur(b, i, nb=nb, col=0):
            return (b, i // nb, i % nb, col)

        def prev(b, i, nb=nb, col=0):
            return (b, i // nb, jnp.maximum(i % nb - 1, 0), col)

        for fn, col in ((cur, 0), (prev, 1), (cur, 1), (prev, 2), (cur, 2)):
            in_arrays.append(arrays[g])
            in_specs.append(pl.BlockSpec(blk, functools.partial(fn, col=col)))
        for _ in range(2):
            out_specs.append(pl.BlockSpec(blk, functools.partial(cur, col=0)))
            out_shapes.append(jax.ShapeDtypeStruct((batch, dil, length, A_OUT), jnp.float32))
    return pl.pallas_call(
        functools.partial(_dilated_kernel, nbs=tuple(nbs)),
        grid=(batch, steps),
        in_specs=in_specs,
        out_specs=out_specs,
        out_shape=out_shapes,
        compiler_params=_params(("parallel", "parallel")),
    )(*in_arrays)


def _compress_kernel(x_ref, pos_ref, w1_ref, w2_ref, o_ref):
    half = CMP_STRIDE * HEAD_DIM
    x = x_ref[...]
    n_chunk = x.shape[0]
    lo = (x + pos_ref[0]).astype(jnp.bfloat16)
    hi = (x + pos_ref[1]).astype(jnp.bfloat16)
    h_lo = jnp.dot(lo, w1_ref[0:half, :], preferred_element_type=jnp.float32)
    h_hi = jnp.dot(hi, w1_ref[half:2 * half, :], preferred_element_type=jnp.float32)
    h = h_lo + pltpu.roll(h_hi, n_chunk - 1, 0)
    act = jax.nn.gelu(h, approximate=True)
    o_ref[...] = jnp.dot(act.astype(jnp.bfloat16), w2_ref[...],
                         preferred_element_type=jnp.float32).astype(o_ref.dtype)


def _compress(ckv, pos, w1, w2):
    _, batch, _, n_chunk, half = ckv.shape
    return pl.pallas_call(
        _compress_kernel,
        grid=(2, batch, B_KV_HEADS),
        in_specs=[pl.BlockSpec((None, None, None, n_chunk, half), lambda s, b, g: (s, b, g, 0, 0)),
                  pl.BlockSpec((None, 2, 1, half), lambda s, b, g: (s, 0, 0, 0)),
                  pl.BlockSpec((None, 2 * half, CMP_HIDDEN), lambda s, b, g: (s, 0, 0)),
                  pl.BlockSpec((None, CMP_HIDDEN, HEAD_DIM), lambda s, b, g: (s, 0, 0))],
        out_specs=pl.BlockSpec((None, None, None, n_chunk, HEAD_DIM), lambda s, b, g: (s, b, g, 0, 0)),
        out_shape=jax.ShapeDtypeStruct((2, batch, B_KV_HEADS, n_chunk, HEAD_DIM), jnp.bfloat16),
        compiler_params=_params(("parallel", "parallel", "parallel")),
    )(ckv, pos, w1, w2)


NEG = -2.0 ** 60
N_FEAT = HEAD_DIM


def _nsa_kernel(q_ref, kc_ref, vc_ref, skv_ref, wkv_ref, gate_ref, ovt_ref, kconst_ref, slope_ref, o_ref,
                kaug_sc, vaug_sc, kwin_sc, vwin_sc, m_sc, acc_sc, qaug_sc, sa_sc, sb_sc, *, n_sel, tk):
    n = pl.program_id(2)
    t0 = n * Q_BLOCK
    rows = B_GQA * Q_BLOCK
    seq = skv_ref.shape[0]
    dn = (((1,), (1,)), ((), ()))
    f32, bf16 = jnp.float32, jnp.bfloat16
    E = HEAD_DIM

    @pl.when(n == 0)
    def _build_keys():
        kaug_sc[:, 0:LANES] = kconst_ref[:, 0:LANES]
        kaug_sc[:, LANES:LANES + E] = skv_ref[:, 0:E]
        kaug_sc[:, LANES + E:2 * LANES] = kconst_ref[:, LANES:LANES + N_FEAT]
        vaug_sc[:, 0:E] = skv_ref[:, E:2 * E]
        vaug_sc[:, E:LANES] = kconst_ref[:, LANES + N_FEAT:2 * LANES]
        kwin_sc[:, 0:E] = wkv_ref[:, 0:E]
        kwin_sc[:, E:LANES] = kconst_ref[:, LANES:LANES + N_FEAT]
        vwin_sc[:, 0:E] = wkv_ref[:, E:2 * E]
        vwin_sc[:, E:LANES] = kconst_ref[:, LANES + N_FEAT:2 * LANES]

    q = q_ref[...]
    qs = jnp.concatenate([q[:, r * E:(r + 1) * E] for r in range(B_GQA)], axis=0)
    row = lax.broadcasted_iota(jnp.int32, (rows, 1), 0)
    within = row & (Q_BLOCK - 1)
    tq = t0 + within
    slope = slope_ref[...]
    lane = lax.broadcasted_iota(jnp.int32, (rows, N_FEAT), 1)
    feat = jnp.where(lane == 0, slope * SLC_LEN,
           jnp.where(lane == 1, slope,
           jnp.where(lane == 2, -slope * t0.astype(f32),
           jnp.where(lane == 3, -slope * within.astype(f32), 0.0))))
    q_feat = jnp.concatenate([qs, feat.astype(bf16)], axis=1)

    n_chunk = kc_ref.shape[0]
    s = lax.dot_general(qs, kc_ref[...], dn, preferred_element_type=f32)
    cmp_end = lax.broadcasted_iota(jnp.int32, (1, n_chunk), 1) * CMP_STRIDE + (CMP_LEN - 1)
    s = jnp.where(cmp_end <= tq, s, -jnp.inf)
    m = jnp.max(s, axis=1, keepdims=True)
    m = jnp.where(m == -jnp.inf, 0.0, m)
    p = jnp.exp(s - m)
    l = jnp.maximum(jnp.sum(p, axis=1, keepdims=True), TINY)
    p_c = (p / l).astype(bf16)
    o_c = jnp.dot(p_c, vc_ref[...], preferred_element_type=f32)

    span = min(WIN + Q_BLOCK, seq)
    start = pl.multiple_of(jnp.maximum(t0 + Q_BLOCK - span, 0), Q_BLOCK)
    s = lax.dot_general(q_feat, kwin_sc[pl.ds(start, span), :], dn, preferred_element_type=f32)
    dw = tq - (start + lax.broadcasted_iota(jnp.int32, (1, span), 1))
    s = jnp.where((dw >= 0) & (dw < WIN), s, -jnp.inf)
    m = jnp.max(s, axis=1, keepdims=True)
    p = jnp.exp(s - m).astype(bf16)
    acc = jnp.dot(p, vwin_sc[pl.ds(start, span), :], preferred_element_type=f32)
    o_w = acc[:, 0:E] / jnp.maximum(acc[:, E:E + 1], TINY)

    gates = 1.0 / (1.0 + jnp.exp(-gate_ref[...]))
    gate_of = lambda r, branch: jnp.broadcast_to(gates[:, 3 * r + branch:3 * r + branch + 1], (Q_BLOCK, E))
    gate_s = [gate_of(r, 1) for r in range(B_GQA)]
    gated_cw = [gate_of(r, 0) * o_c[r * Q_BLOCK:(r + 1) * Q_BLOCK] + gate_of(r, 2) * o_w[r * Q_BLOCK:(r + 1) * Q_BLOCK]
                for r in range(B_GQA)]

    imp = lax.dot_general(ovt_ref[...], p_c[0:Q_BLOCK], dn, preferred_element_type=f32)
    for r in range(1, B_GQA):
        imp += lax.dot_general(ovt_ref[...], p_c[r * Q_BLOCK:(r + 1) * Q_BLOCK], dn, preferred_element_type=f32)
    blk = lax.broadcasted_iota(jnp.int32, (LANES, Q_BLOCK), 0)
    t = t0 + lax.broadcasted_iota(jnp.int32, (LANES, Q_BLOCK), 1)
    cur = t >> 6
    forced = (blk == 0) | (blk == cur) | (blk == cur - 1)
    score = jnp.where(blk * SLC_LEN <= t, imp + jnp.where(forced, FORCE_BONUS, 0.0), -jnp.inf)

    def pick(_, carry):
        work, sel = carry
        best = jnp.max(work, axis=0, keepdims=True)
        idx = jnp.min(jnp.where(work == best, blk, LANES), axis=0, keepdims=True)
        hit = blk == idx
        finite = jnp.where(best > -jnp.inf, 1.0, 0.0)
        sel = jnp.where(hit, jnp.maximum(sel, finite), sel)
        return jnp.where(hit, -jnp.inf, work), sel

    _, sel = lax.fori_loop(0, n_sel, pick, (score, jnp.zeros((LANES, Q_BLOCK), f32)), unroll=True)
    bias = jnp.where(sel.T > 0.5, 0.0, NEG).astype(bf16)
    qaug_sc[...] = jnp.concatenate([jnp.concatenate([bias] * B_GQA, axis=0), q_feat], axis=1)

    m_sc[...] = jnp.full(m_sc.shape, -jnp.inf, f32)
    acc_sc[...] = jnp.zeros(acc_sc.shape, f32)

    def logits(j, s_ref):
        k0 = pl.multiple_of(j * tk, tk)
        s_ref[...] = lax.dot_general(qaug_sc[...], kaug_sc[pl.ds(k0, tk), :], dn, preferred_element_type=f32)

    def update(j, s_ref, diagonal):
        k0 = pl.multiple_of(j * tk, tk)
        s = s_ref[...]
        if diagonal:
            s = jnp.where(k0 + lax.broadcasted_iota(jnp.int32, (1, tk), 1) <= tq, s, -jnp.inf)
        m_old = m_sc[...]
        m_new = jnp.maximum(m_old, jnp.broadcast_to(jnp.max(s, axis=1, keepdims=True), m_old.shape))
        p = jnp.exp(s - jnp.concatenate([m_new] * (tk // LANES), axis=1)).astype(bf16)
        pv = jnp.dot(p, vaug_sc[pl.ds(k0, tk), :], preferred_element_type=f32)
        acc_sc[...] = jnp.exp(m_old - m_new) * acc_sc[...] + pv
        m_sc[...] = m_new

    j_last = t0 // tk
    logits(0, sa_sc)

    def tile_pair(i, carry):
        j = 2 * i
        logits(j + 1, sb_sc)
        update(j, sa_sc, False)
        logits(j + 2, sa_sc)
        update(j + 1, sb_sc, False)
        return carry

    lax.fori_loop(0, j_last // 2, tile_pair, 0)
    odd = (j_last % 2) == 1

    @pl.when(odd)
    def _two_left():
        logits(j_last, sb_sc)
        update(j_last - 1, sa_sc, False)
        update(j_last, sb_sc, True)

    @pl.when(jnp.logical_not(odd))
    def _one_left():
        update(j_last, sa_sc, True)

    acc = acc_sc[...]
    o_s = acc[:, 0:E] / jnp.maximum(acc[:, E:E + 1], TINY)

    for r in range(B_GQA):
        out = gated_cw[r] + gate_s[r] * o_s[r * Q_BLOCK:(r + 1) * Q_BLOCK]
        o_ref[:, r * E:(r + 1) * E] = out.astype(o_ref.dtype)


def _nsa(u_nat, u_f32, cmp_kv, batch, seq):
    nq = seq // Q_BLOCK
    n_chunk = seq // CMP_STRIDE
    n_cmp = n_chunk - CMP_LEN // CMP_STRIDE + 1
    n_slc = seq // SLC_LEN
    n_sel = min(SLC_TOPK, n_slc)
    tk = min(512, seq)
    assert n_slc <= LANES and B_HEADS == 8 and seq % tk == 0
    c_start = np.arange(n_chunk) * CMP_STRIDE
    s_start = np.arange(LANES) * SLC_LEN
    overlap_t = ((c_start[None, :] < s_start[:, None] + SLC_LEN) & (c_start[None, :] + CMP_LEN > s_start[:, None])
                 & (np.arange(n_chunk)[None, :] < n_cmp) & (np.arange(LANES)[:, None] < n_slc))
    pos = np.arange(seq)
    kconst = np.zeros((seq, 2 * LANES), np.float32)
    kconst[pos, pos // SLC_LEN] = 1.0
    kconst[:, LANES + 0] = pos // SLC_LEN
    kconst[:, LANES + 1] = pos % SLC_LEN
    kconst[:, LANES + 2] = 1.0
    kconst[:, LANES + 3] = 1.0
    kconst[:, LANES + N_FEAT] = 1.0
    rows = B_GQA * Q_BLOCK
    slopes = jnp.exp2(-8.0 * jnp.arange(1, B_HEADS + 1, dtype=jnp.float32) / B_HEADS)
    slopes = jnp.repeat(slopes.reshape(B_KV_HEADS, B_GQA), Q_BLOCK, axis=1).reshape(B_KV_HEADS, rows, 1)
    kv_rows = pl.BlockSpec((seq, LANES), lambda b, g, n: (b, NAT_SLC + g))
    win_rows = pl.BlockSpec((seq, LANES), lambda b, g, n: (b, NAT_WIN + g))
    cmp_spec = lambda s: pl.BlockSpec((None, None, None, n_chunk, HEAD_DIM), lambda b, g, n: (s, b, g, 0, 0))
    return pl.pallas_call(
        functools.partial(_nsa_kernel, n_sel=n_sel, tk=tk),
        grid=(batch, B_KV_HEADS, nq),
        in_specs=[pl.BlockSpec((Q_BLOCK, B_GQA * HEAD_DIM), lambda b, g, n: (b * nq + n, NAT_Q_B + g)),
                  cmp_spec(0), cmp_spec(1), kv_rows, win_rows,
                  pl.BlockSpec((Q_BLOCK, LANES), lambda b, g, n: (b * nq + n, F_GATE + g)),
                  pl.BlockSpec((LANES, n_chunk), lambda b, g, n: (0, 0)),
                  pl.BlockSpec((seq, 2 * LANES), lambda b, g, n: (0, 0)),
                  pl.BlockSpec((None, rows, 1), lambda b, g, n: (g, 0, 0))],
        out_specs=pl.BlockSpec((Q_BLOCK, B_GQA * HEAD_DIM), lambda b, g, n: (b * nq + n, g)),
        out_shape=jax.ShapeDtypeStruct((batch * seq, B_WIDTH), jnp.bfloat16),
        scratch_shapes=[pltpu.VMEM((seq, 2 * LANES), jnp.bfloat16), pltpu.VMEM((seq, LANES), jnp.bfloat16),
                        pltpu.VMEM((seq, LANES), jnp.bfloat16), pltpu.VMEM((seq, LANES), jnp.bfloat16),
                        pltpu.VMEM((rows, LANES), jnp.float32), pltpu.VMEM((rows, LANES), jnp.float32),
                        pltpu.VMEM((rows, 2 * LANES), jnp.bfloat16),
                        pltpu.VMEM((rows, tk), jnp.float32), pltpu.VMEM((rows, tk), jnp.float32)],
        compiler_params=_params(("arbitrary", "arbitrary", "arbitrary")),
    )(u_nat, cmp_kv, cmp_kv, u_nat, u_nat, u_f32, jnp.asarray(overlap_t, jnp.bfloat16), jnp.asarray(kconst, jnp.bfloat16),
      slopes)


def _layer_norm(v, gain, bias):
    mu = jnp.mean(v, axis=1, keepdims=True)
    c = v - mu
    var = jnp.mean(c * c, axis=1, keepdims=True)
    return c * lax.rsqrt(var + LN_EPS) * gain + bias


def _sigmoid(v):
    return 1.0 / (1.0 + jnp.exp(-v))


MERGE_SPLIT = 2


def _token_major(ref, stage):
    dil, n = ref.shape[0], ref.shape[1]
    parts = stage.shape[0]
    for r in range(dil):
        for j in range(parts):
            stage[j, pl.ds(r, n, stride=dil), :] = ref[r, :, j * LANES:(j + 1) * LANES]
    return jnp.concatenate([stage[j] for j in range(parts)], axis=1)


def _merge_kernel(*refs):
    dil_refs, refs = refs[:2 * N_DIL], refs[2 * N_DIL:]
    (yb_ref, mga_ref, mgb_ref, x_ref, wa_ref, wb_ref, wo_ref, g_ref, b_ref, wr_hi_ref, wr_lo_ref, br_ref,
     h_ref, ht_ref, rw_ref, re_ref, cnt_ref) = refs[:17]
    stages, cnt_sc = refs[17:17 + 2 * (N_DIL - 1)], refs[17 + 2 * (N_DIL - 1)]
    f32 = jnp.float32
    outs, lse = [dil_refs[0][...]], [dil_refs[1][...]]
    for g in range(1, N_DIL):
        outs.append(_token_major(dil_refs[2 * g], stages[2 * g - 2]))
        lse.append(_token_major(dil_refs[2 * g + 1], stages[2 * g - 1]))
    def mixed_rows(rs):
        lse_r = [v[rs] for v in lse]
        top = jnp.maximum(jnp.maximum(lse_r[0], lse_r[1]), lse_r[2])
        e = [jnp.exp(v - top) for v in lse_r]
        y_a = (e[0] * outs[0][rs] + e[1] * outs[1][rs] + e[2] * outs[2][rs]) / (e[0] + e[1] + e[2])
        br_a = jnp.dot(y_a.astype(jnp.bfloat16), wa_ref[...], preferred_element_type=f32)
        br_b = jnp.dot(yb_ref[rs, :], wb_ref[...], preferred_element_type=f32)
        merged = _sigmoid(mga_ref[rs, :]) * br_a + _sigmoid(mgb_ref[rs, :]) * br_b
        mix = jnp.dot(merged.astype(jnp.bfloat16), wo_ref[...], preferred_element_type=f32)
        return _layer_norm(ALPHA * x_ref[rs, :] + mix, g_ref[...], b_ref[...])

    tm = x_ref.shape[0]
    part = tm // MERGE_SPLIT
    h = jnp.concatenate([mixed_rows(slice(k * part, (k + 1) * part)) for k in range(MERGE_SPLIT)], axis=0)
    h_ref[...] = h
    _to_token_tiles(ht_ref, h)

    h_hi = h.astype(jnp.bfloat16)
    h_lo = (h - h_hi.astype(f32)).astype(jnp.bfloat16)
    logits = (jnp.dot(h_hi, wr_hi_ref[...], preferred_element_type=f32)
              + jnp.dot(h_lo, wr_hi_ref[...], preferred_element_type=f32)
              + jnp.dot(h_hi, wr_lo_ref[...], preferred_element_type=f32)) + br_ref[...]
    lane = lax.broadcasted_iota(jnp.int32, logits.shape, 1)
    coarse = jnp.where(lane < N_GROUPS, logits, -jnp.inf)
    c_max = jnp.max(coarse, axis=1, keepdims=True)
    grp = jnp.min(jnp.where(coarse == c_max, lane, LANES), axis=1, keepdims=True)
    p_grp = 1.0 / jnp.sum(jnp.exp(coarse - c_max), axis=1, keepdims=True)
    lo_lane = N_GROUPS + EXP_PER_GROUP * grp
    fine = jnp.where((lane >= lo_lane) & (lane < lo_lane + EXP_PER_GROUP), logits, -jnp.inf)
    v1 = jnp.max(fine, axis=1, keepdims=True)
    i1 = jnp.min(jnp.where(fine == v1, lane, LANES), axis=1, keepdims=True)
    fine = jnp.where(lane == i1, -jnp.inf, fine)
    v2 = jnp.max(fine, axis=1, keepdims=True)
    i2 = jnp.min(jnp.where(fine == v2, lane, LANES), axis=1, keepdims=True)
    e2 = jnp.exp(v2 - v1)
    w1 = p_grp / (1.0 + e2)
    w2 = p_grp * e2 / (1.0 + e2)
    e_1, e_2 = i1 - N_GROUPS, i2 - N_GROUPS
    rw_ref[...] = jnp.where(lane == 0, w1, jnp.where(lane == 1, w2, 0.0))

    @pl.when(pl.program_id(0) == 0)
    def _zero_counts():
        cnt_sc[...] = jnp.zeros(cnt_sc.shape, f32)

    tm = logits.shape[0]
    onehot = jnp.where((lane == e_1) | (lane == e_2), 1.0, 0.0)
    earlier = (lax.broadcasted_iota(jnp.int32, (tm, tm), 0) > lax.broadcasted_iota(jnp.int32, (tm, tm), 1))
    before = jnp.dot(jnp.where(earlier, 1.0, 0.0).astype(jnp.bfloat16), onehot.astype(jnp.bfloat16),
                     preferred_element_type=f32) + cnt_sc[...]
    r_1 = jnp.sum(jnp.where(lane == e_1, before, 0.0), axis=1, keepdims=True).astype(jnp.int32)
    r_2 = jnp.sum(jnp.where(lane == e_2, before, 0.0), axis=1, keepdims=True).astype(jnp.int32)
    cnt_sc[...] = cnt_sc[...] + jnp.sum(onehot, axis=0, keepdims=True)
    cnt_ref[...] = cnt_sc[...].astype(jnp.int32)
    re_ref[...] = jnp.where(lane == 0, e_1, jnp.where(lane == 1, e_2,
                            jnp.where(lane == 2, r_1, jnp.where(lane == 3, r_2, 0))))


def _merge(dil_outs, y_b, u_f32, x2d, seq, wa, wb, wo, ln_g, ln_b, wr_hi, wr_lo, b_r, tm=256):
    t = x2d.shape[0]
    per_batch = seq // tm
    assert seq % tm == 0 and all(tm % (dil * 8) == 0 for _, dil in DIL_GROUPS)
    row = lambda w: pl.BlockSpec((tm, w), lambda i: (i, 0))
    full = lambda a: pl.BlockSpec(a.shape, lambda i: (0,) * a.ndim)
    dil_specs = [pl.BlockSpec((None, None, tm, A_OUT), lambda i: (i // per_batch, 0, i % per_batch, 0))] * 2
    stages = []
    for _, dil in DIL_GROUPS[1:]:
        dil_specs += [pl.BlockSpec((None, dil, tm // dil, A_OUT),
                                   lambda i: (i // per_batch, 0, i % per_batch, 0))] * 2
        stages += [pltpu.VMEM((A_OUT // LANES, tm, LANES), jnp.float32)] * 2
    return pl.pallas_call(
        _merge_kernel,
        grid=(t // tm,),
        in_specs=dil_specs + [row(B_WIDTH),
                  pl.BlockSpec((tm, D_MODEL), lambda i: (i, F_MGATE)),
                  pl.BlockSpec((tm, D_MODEL), lambda i: (i, F_MGATE + 1)),
                  row(D_MODEL), full(wa), full(wb), full(wo), full(ln_g), full(ln_b),
                  full(wr_hi), full(wr_lo), full(b_r)],
        out_specs=[row(D_MODEL), pl.BlockSpec((tm * CHUNKS, LANES), lambda i: (i, 0)), row(LANES), row(LANES),
                   pl.BlockSpec((1, LANES), lambda i: (0, 0))],
        out_shape=[jax.ShapeDtypeStruct((t, D_MODEL), jnp.float32),
                   jax.ShapeDtypeStruct((t * CHUNKS, LANES), jnp.float32),
                   jax.ShapeDtypeStruct((t, LANES), jnp.float32),
                   jax.ShapeDtypeStruct((t, LANES), jnp.int32),
                   jax.ShapeDtypeStruct((1, LANES), jnp.int32)],
        scratch_shapes=stages + [pltpu.VMEM((1, LANES), jnp.float32)],
        compiler_params=_params(("arbitrary",)),
    )(*dil_outs, y_b, u_f32, u_f32, x2d, wa, wb, wo, ln_g, ln_b, wr_hi, wr_lo, b_r)


CHUNKS = D_MODEL // LANES
GATHER_UNROLL = 8


def _to_token_tiles(ref2d, value):
    n = value.shape[0]
    for c in range(CHUNKS):
        ref2d[pl.ds(c, n, stride=CHUNKS), :] = value[:, c * LANES:(c + 1) * LANES]


def _from_token_tiles(ref2d):
    n = ref2d.shape[0] // CHUNKS
    return jnp.concatenate([ref2d[pl.ds(c, n, stride=CHUNKS), :] for c in range(CHUNKS)], axis=1)


def _token_copy(idx_ref, k, src2d, dst2d, r, sem):
    src_row = pl.multiple_of(idx_ref[0, 0, k] * CHUNKS, CHUNKS)
    dst_row = r * CHUNKS if isinstance(r, int) else pl.multiple_of(r * CHUNKS, CHUNKS)
    return pltpu.make_async_copy(src2d.at[pl.ds(src_row, CHUNKS), :], dst2d.at[pl.ds(dst_row, CHUNKS), :], sem)


def _start_token_gather(idx_ref, idx_of, src2d, dst2d, sem, unrolled):
    n = dst2d.shape[0] // CHUNKS
    if unrolled:
        for r in range(n):
            _token_copy(idx_ref, idx_of(r), src2d, dst2d, r, sem).start()
    else:
        def start(r, carry):
            _token_copy(idx_ref, idx_of(r), src2d, dst2d, r, sem).start()
            return carry

        lax.fori_loop(0, n, start, 0, unroll=GATHER_UNROLL)


def _wait_token_gather(src2d, dst2d, sem):
    pltpu.make_async_copy(src2d.at[pl.ds(0, dst2d.shape[0]), :], dst2d, sem).wait()


def _expert_kernel(blk_e_ref, tok_ref, tok_next_ref, ht_hbm, wgu_ref, wd_ref, y_ref, xbuf_a, xbuf_b, sems):
    i = pl.program_id(0)
    last = pl.num_programs(0) - 1
    same = lambda r: r

    @pl.when(i == 0)
    def _first_block():
        _start_token_gather(tok_ref, same, ht_hbm, xbuf_a, sems.at[0], unrolled=False)

    def step(cur, nxt, sem_cur, sem_nxt):
        _wait_token_gather(ht_hbm, cur, sem_cur)
        _start_token_gather(tok_next_ref, same, ht_hbm, nxt, sem_nxt, unrolled=True)
        xb = _from_token_tiles(cur).astype(jnp.bfloat16)
        gu = jnp.dot(xb, wgu_ref[...], preferred_element_type=jnp.float32)
        gate, up = gu[:, :D_EXPERT], gu[:, D_EXPERT:]
        act = (gate * _sigmoid(gate)) * up
        _to_token_tiles(y_ref, jnp.dot(act.astype(jnp.bfloat16), wd_ref[...], preferred_element_type=jnp.float32))

        @pl.when(i == last)
        def _drain():
            _wait_token_gather(ht_hbm, nxt, sem_nxt)

    @pl.when(i % 2 == 0)
    def _even():
        step(xbuf_a, xbuf_b, sems.at[0], sems.at[1])

    @pl.when(i % 2 == 1)
    def _odd():
        step(xbuf_b, xbuf_a, sems.at[1], sems.at[0])


def _experts(h_tiles, tok_buf, blk_e, wgu, wd, bm):
    p = tok_buf.shape[0]
    nb = p // bm
    tok = tok_buf.reshape(nb, 1, bm)
    buf = pltpu.VMEM((bm * CHUNKS, LANES), jnp.float32)
    return pl.pallas_call(
        _expert_kernel,
        grid_spec=pltpu.PrefetchScalarGridSpec(
            num_scalar_prefetch=1,
            grid=(nb,),
            in_specs=[pl.BlockSpec((1, 1, bm), lambda i, e: (i, 0, 0), memory_space=pltpu.SMEM),
                      pl.BlockSpec((1, 1, bm), lambda i, e: (jnp.minimum(i + 1, nb - 1), 0, 0),
                                   memory_space=pltpu.SMEM),
                      pl.BlockSpec(memory_space=pl.ANY),
                      pl.BlockSpec((None, D_MODEL, 2 * D_EXPERT), lambda i, e: (e[i], 0, 0)),
                      pl.BlockSpec((None, D_EXPERT, D_MODEL), lambda i, e: (e[i], 0, 0))],
            out_specs=pl.BlockSpec((bm * CHUNKS, LANES), lambda i, e: (i, 0)),
            scratch_shapes=[buf, buf, pltpu.SemaphoreType.DMA((2,))],
        ),
        out_shape=jax.ShapeDtypeStruct((p * CHUNKS, LANES), jnp.float32),
        compiler_params=_params(("arbitrary",)),
    )(blk_e, tok, tok, h_tiles, wgu, wd)


def _combine_kernel(dest_ref, dest_next_ref, yt_hbm, h_ref, rw_ref, g_ref, b_ref, o_ref, ybuf_a, ybuf_b, sems):
    i = pl.program_id(0)
    last = pl.num_programs(0) - 1

    def start(idx_ref, buf, sem, unrolled):
        for k in range(MOE_TOP_K):
            _start_token_gather(idx_ref, lambda r, k=k: MOE_TOP_K * r + k, yt_hbm, buf.at[k], sem, unrolled)

    def wait(buf, sem):
        for k in range(MOE_TOP_K):
            _wait_token_gather(yt_hbm, buf.at[k], sem)

    @pl.when(i == 0)
    def _first_tile():
        start(dest_ref, ybuf_a, sems.at[0], unrolled=False)

    def step(cur, nxt, sem_cur, sem_nxt):
        wait(cur, sem_cur)
        start(dest_next_ref, nxt, sem_nxt, unrolled=True)
        rw = rw_ref[...]
        ffn = rw[:, 0:1] * _from_token_tiles(cur.at[0]) + rw[:, 1:2] * _from_token_tiles(cur.at[1])
        o_ref[...] = _layer_norm(ALPHA * h_ref[...] + ffn, g_ref[...], b_ref[...])

        @pl.when(i == last)
        def _drain():
            wait(nxt, sem_nxt)

    @pl.when(i % 2 == 0)
    def _even():
        step(ybuf_a, ybuf_b, sems.at[0], sems.at[1])

    @pl.when(i % 2 == 1)
    def _odd():
        step(ybuf_b, ybuf_a, sems.at[1], sems.at[0])


def _combine(dest, y_tiles, h, rw, ln_g, ln_b, tm=256):
    t = h.shape[0]
    assert t % tm == 0
    nt = t // tm
    idx = dest.reshape(nt, 1, MOE_TOP_K * tm)
    buf = pltpu.VMEM((MOE_TOP_K, tm * CHUNKS, LANES), jnp.float32)
    return pl.pallas_call(
        _combine_kernel,
        grid=(nt,),
        in_specs=[pl.BlockSpec((1, 1, MOE_TOP_K * tm), lambda i: (i, 0, 0), memory_space=pltpu.SMEM),
                  pl.BlockSpec((1, 1, MOE_TOP_K * tm), lambda i: (jnp.minimum(i + 1, nt - 1), 0, 0),
                               memory_space=pltpu.SMEM),
                  pl.BlockSpec(memory_space=pl.ANY),
                  pl.BlockSpec((tm, D_MODEL), lambda i: (i, 0)),
                  pl.BlockSpec((tm, LANES), lambda i: (i, 0)),
                  pl.BlockSpec((1, D_MODEL), lambda i: (0, 0)),
                  pl.BlockSpec((1, D_MODEL), lambda i: (0, 0))],
        out_specs=pl.BlockSpec((tm, D_MODEL), lambda i: (i, 0)),
        out_shape=jax.ShapeDtypeStruct((t, D_MODEL), jnp.float32),
        scratch_shapes=[buf, buf, pltpu.SemaphoreType.DMA((2,))],
        compiler_params=_params(("arbitrary",)),
    )(idx, idx, y_tiles, h, rw, ln_g, ln_b)


def _dispatch(route, counts, bm):
    t = route.shape[0]
    n_assign = t * MOE_TOP_K
    expert, rank = route[:, 0:MOE_TOP_K], route[:, MOE_TOP_K:2 * MOE_TOP_K]
    starts = jnp.cumsum(counts) - counts
    pcounts = (counts + bm - 1) // bm * bm
    pends = jnp.cumsum(pcounts)
    pstarts = pends - pcounts
    ids = jnp.arange(N_EXPERTS, dtype=jnp.int32)
    dest = jnp.sum(jnp.where(expert[..., None] == ids, pstarts, 0), axis=-1) + rank
    p = -(-(n_assign + N_EXPERTS * (bm - 1)) // bm) * bm
    nb = p // bm
    blk_e = jnp.minimum(jnp.sum(pends[None, :] <= (jnp.arange(nb, dtype=jnp.int32) * bm)[:, None], axis=1),
                        N_EXPERTS - 1).astype(jnp.int32)
    order = jnp.argsort(expert.reshape(n_assign), stable=True).astype(jnp.int32)
    row_e = jnp.repeat(blk_e, bm)
    k = jnp.arange(p, dtype=jnp.int32) - pstarts[row_e]
    src = jnp.clip(starts[row_e] + k, 0, n_assign - 1)
    tok_buf = jnp.where(k < counts[row_e], order[src] // MOE_TOP_K, 0).astype(jnp.int32)
    return tok_buf, dest.astype(jnp.int32), blk_e


def _split_bf16(w):
    hi = w.astype(jnp.bfloat16)
    return hi, (w - hi.astype(jnp.float32)).astype(jnp.bfloat16)


def kernel(x, w_in, cmp_pos_k, cmp_w1_k, cmp_w2_k, cmp_pos_v, cmp_w1_v, cmp_w2_v, w_branch_a, w_branch_b, w_out, ln1_g, ln1_b, w_coarse, b_coarse, w_fine, b_fine, w_gate_up, w_down, ln2_g, ln2_b):
    batch, seq, d = x.shape
    bf16 = jnp.bfloat16
    h = x.reshape(batch * seq, d)
    cols_bf, cols_f32, scale_bf = _in_proj_columns()
    half = CMP_STRIDE * HEAD_DIM
    for l in range(DEPTH):
        u_nat, *groups = _project_bf(h, (_permute_columns(w_in[l], cols_bf) * scale_bf).astype(bf16), batch, seq)
        u_f32, ckv = _project_f32(h, _permute_columns(w_in[l], cols_f32).astype(bf16), batch, seq)
        dil_outs = _dilated(u_nat, groups, batch, seq)
        pos = jnp.stack([cmp_pos_k[l], cmp_pos_v[l]]).reshape(2, 2, 1, half)
        w1 = jnp.stack([cmp_w1_k[l], cmp_w1_v[l]]).astype(bf16)
        w2 = jnp.stack([cmp_w2_k[l], cmp_w2_v[l]]).astype(bf16)
        cmp_kv = _compress(ckv, pos, w1, w2)
        y_b = _nsa(u_nat, u_f32, cmp_kv, batch, seq)
        w_r = jnp.concatenate([w_coarse[l], w_fine[l].transpose(1, 0, 2).reshape(d, N_EXPERTS)], axis=1)
        w_r = jnp.pad(w_r, ((0, 0), (0, LANES - w_r.shape[1])))
        b_r = jnp.pad(jnp.concatenate([b_coarse[l], b_fine[l].reshape(N_EXPERTS)]),
                      (0, LANES - N_GROUPS - N_EXPERTS)).reshape(1, LANES)
        wr_hi, wr_lo = _split_bf16(w_r)
        h1, h1_tiles, rw, route, counts = _merge(dil_outs, y_b, u_f32, h, seq, w_branch_a[l].astype(bf16),
                                       w_branch_b[l].astype(bf16), w_out[l].astype(bf16),
                                       ln1_g[l].reshape(1, d), ln1_b[l].reshape(1, d), wr_hi, wr_lo, b_r)
        bm = 256
        tok_buf, dest, blk_e = _dispatch(route, counts[0, :N_EXPERTS], bm)
        y_tiles = _experts(h1_tiles, tok_buf, blk_e, w_gate_up[l].astype(bf16), w_down[l].astype(bf16), bm)
        h = _combine(dest, y_tiles, h1, rw, ln2_g[l].reshape(1, d), ln2_b[l].reshape(1, d))
    return h.reshape(batch, seq, d)
```

```python
import functools

import numpy as np
import jax
import jax.numpy as jnp
from jax import lax
from jax.experimental import pallas as pl
from jax.experimental.pallas import tpu as pltpu

D_MODEL = 1024
HEAD_DIM = 64
Q_BLOCK = 128
DIL_GROUPS = ((128, 1), (512, 4), (2048, 16))
N_DIL = len(DIL_GROUPS)
A_HEADS_PER_GROUP = 4
A_HEADS = A_HEADS_PER_GROUP * N_DIL
A_WIDTH = A_HEADS * HEAD_DIM
A_OUT = A_HEADS_PER_GROUP * HEAD_DIM
B_HEADS = 8
B_KV_HEADS = 2
B_GQA = B_HEADS // B_KV_HEADS
B_WIDTH = B_HEADS * HEAD_DIM
B_KV_WIDTH = B_KV_HEADS * HEAD_DIM
CMP_LEN = 32
CMP_STRIDE = 16
CMP_HIDDEN = 256
SLC_LEN = 64
SLC_TOPK = 16
WIN = 512
N_GROUPS = 4
EXP_PER_GROUP = 8
N_EXPERTS = N_GROUPS * EXP_PER_GROUP
D_EXPERT = 512
MOE_TOP_K = 2
DEPTH = 1
ALPHA = (2.0 * DEPTH) ** 0.25
LN_EPS = 1e-5
FORCE_BONUS = 1e4
TINY = 1e-30
ATTN_SCALE = HEAD_DIM ** -0.5

LANES = 128
VMEM_LIMIT = 48 * 1024 * 1024

QKV_COLS = 3 * A_OUT
NAT_Q_B = QKV_COLS // A_OUT
NAT_SLC = (QKV_COLS + B_WIDTH) // LANES
NAT_WIN = NAT_SLC + B_KV_HEADS
NAT_COLS = QKV_COLS + B_WIDTH + 4 * B_KV_WIDTH
BF_COLS = NAT_COLS + (N_DIL - 1) * QKV_COLS
F_MGATE = 0
F_GATE = 2 * D_MODEL // LANES
F_NAT_COLS = 2 * D_MODEL + B_KV_HEADS * LANES
F_COLS = F_NAT_COLS + 2 * B_KV_WIDTH


def _params(semantics):
    return pltpu.CompilerParams(dimension_semantics=semantics, vmem_limit_bytes=VMEM_LIMIT)


def _in_proj_columns():
    kv_off = 3 * A_WIDTH + B_WIDTH
    gate_off = kv_off + 6 * B_KV_WIDTH
    mg_off = gate_off + 3 * B_HEADS
    def qkv_cols(g):
        return [part * A_WIDTH + g * A_OUT + c for part in range(3) for c in range(A_OUT)]

    bf = qkv_cols(0) + list(range(3 * A_WIDTH, 3 * A_WIDTH + B_WIDTH))
    for first in (2, 4):
        for g in range(B_KV_HEADS):
            for i in (first, first + 1):
                base = kv_off + i * B_KV_WIDTH + g * HEAD_DIM
                bf += list(range(base, base + HEAD_DIM))
    for g in range(1, N_DIL):
        bf += qkv_cols(g)
    f32 = list(range(mg_off, mg_off + 2 * D_MODEL))
    per_head = 3 * B_GQA
    for g in range(B_KV_HEADS):
        f32 += list(range(gate_off + g * per_head, gate_off + (g + 1) * per_head))
        f32 += [-1] * (LANES - per_head)
    f32 += list(range(kv_off, kv_off + 2 * B_KV_WIDTH))
    assert len(bf) == BF_COLS and len(f32) == F_COLS
    bf = np.asarray(bf)
    is_q = (bf < A_WIDTH) | ((bf >= 3 * A_WIDTH) & (bf < 3 * A_WIDTH + B_WIDTH))
    scale = np.where(is_q, ATTN_SCALE, 1.0).astype(np.float32)
    return bf, np.asarray(f32), scale


def _permute_columns(w, cols):
    taken = jnp.take(w, jnp.asarray(np.maximum(cols, 0)), axis=1)
    return jnp.where(jnp.asarray(cols >= 0)[None, :], taken, 0.0)


def _proj_f32_kernel(x_ref, w_ref, nat_ref, cmp_ref, stage):
    xb = x_ref[...].astype(jnp.bfloat16)
    tm = x_ref.shape[0]
    n_chunk = tm // CMP_STRIDE
    for c0 in range(0, F_NAT_COLS, A_OUT):
        nat_ref[:, c0:c0 + A_OUT] = jnp.dot(xb, w_ref[:, c0:c0 + A_OUT], preferred_element_type=jnp.float32)
    res = jnp.dot(xb, w_ref[:, F_NAT_COLS:F_COLS], preferred_element_type=jnp.float32)
    lower = lax.broadcasted_iota(jnp.int32, (n_chunk, LANES), 1) < HEAD_DIM
    for kv in range(2):
        stage[...] = res[:, kv * LANES:(kv + 1) * LANES]
        for p in range(0, CMP_STRIDE, 2):
            even = stage[pl.ds(p, n_chunk, stride=CMP_STRIDE), :]
            odd = stage[pl.ds(p + 1, n_chunk, stride=CMP_STRIDE), :]
            cols = slice(p * HEAD_DIM, (p + 2) * HEAD_DIM)
            cmp_ref[kv, 0, :, cols] = jnp.where(lower, even, pltpu.roll(odd, HEAD_DIM, 1))
            cmp_ref[kv, 1, :, cols] = jnp.where(lower, pltpu.roll(even, HEAD_DIM, 1), odd)


def _project_f32(x2d, w, batch, seq, tm=512):
    t, d = x2d.shape
    assert seq % tm == 0 and w.shape[1] == F_COLS and B_KV_HEADS * HEAD_DIM == LANES
    per_batch = seq // tm
    half = CMP_STRIDE * HEAD_DIM
    return pl.pallas_call(
        _proj_f32_kernel,
        grid=(t // tm,),
        in_specs=[pl.BlockSpec((tm, d), lambda i: (i, 0)),
                  pl.BlockSpec((d, F_COLS), lambda i: (0, 0))],
        out_specs=[pl.BlockSpec((tm, F_NAT_COLS), lambda i: (i, 0)),
                   pl.BlockSpec((2, None, B_KV_HEADS, tm // CMP_STRIDE, half),
                                lambda i: (0, i // per_batch, 0, i % per_batch, 0))],
        out_shape=[jax.ShapeDtypeStruct((t, F_NAT_COLS), jnp.float32),
                   jax.ShapeDtypeStruct((2, batch, B_KV_HEADS, seq // CMP_STRIDE, half), jnp.float32)],
        scratch_shapes=[pltpu.VMEM((tm, LANES), jnp.float32)],
        compiler_params=_params(("parallel",)),
    )(x2d, w)


def _proj_bf_kernel(x_ref, w_ref, nat_ref, *rest, dils):
    group_refs, stage = rest[:-1], rest[-1]
    xb = x_ref[...].astype(jnp.bfloat16)
    tm = x_ref.shape[0]
    for c0 in range(0, NAT_COLS, A_OUT):
        nat_ref[:, c0:c0 + A_OUT] = jnp.dot(
            xb, w_ref[:, c0:c0 + A_OUT], preferred_element_type=jnp.float32).astype(nat_ref.dtype)
    col = NAT_COLS
    for o_ref, dil in zip(group_refs, dils):
        for part in range(3):
            res = jnp.dot(xb, w_ref[:, col:col + A_OUT], preferred_element_type=jnp.float32)
            for j in range(A_OUT // LANES):
                stage[j] = res[:, j * LANES:(j + 1) * LANES]
                c0 = part * A_OUT + j * LANES
                for r in range(dil):
                    o_ref[r, :, c0:c0 + LANES] = stage[j, pl.ds(r, tm // dil, stride=dil), :].astype(o_ref.dtype)
            col += A_OUT


def _project_bf(x2d, w, batch, seq, tm=512):
    t, d = x2d.shape
    dils = tuple(dil for _, dil in DIL_GROUPS[1:])
    assert seq % tm == 0 and all(tm % (dil * 16) == 0 for dil in dils) and w.shape[1] == BF_COLS
    per_batch = seq // tm
    out_specs = [pl.BlockSpec((tm, NAT_COLS), lambda i: (i, 0))]
    out_shapes = [jax.ShapeDtypeStruct((t, NAT_COLS), jnp.bfloat16)]
    for dil in dils:
        out_specs.append(pl.BlockSpec((None, dil, tm // dil, QKV_COLS),
                                      lambda i: (i // per_batch, 0, i % per_batch, 0)))
        out_shapes.append(jax.ShapeDtypeStruct((batch, dil, seq // dil, QKV_COLS), jnp.bfloat16))
    return pl.pallas_call(
        functools.partial(_proj_bf_kernel, dils=dils),
        grid=(t // tm,),
        in_specs=[pl.BlockSpec((tm, d), lambda i: (i, 0)),
                  pl.BlockSpec((d, BF_COLS), lambda i: (0, 0))],
        out_specs=out_specs,
        out_shape=out_shapes,
        scratch_shapes=[pltpu.VMEM((A_OUT // LANES, tm, LANES), jnp.float32)],
        compiler_params=_params(("parallel",)),
    )(x2d, w)


def _dilated_kernel(*refs, nbs):
    bias_ref, ins, outs = refs[0], refs[1:1 + 5 * N_DIL], refs[1 + 5 * N_DIL:]
    i = pl.program_id(1)
    dn = (((1,), (1,)), ((), ()))
    f32, bf16 = jnp.float32, jnp.bfloat16
    lower = lax.broadcasted_iota(jnp.int32, (Q_BLOCK, LANES), 1) < HEAD_DIM
    heads = [(g, pair, half) for g in range(N_DIL) for pair in range(A_OUT // LANES) for half in range(2)]

    def col(ref, pair):
        return ref[:, pair * LANES:(pair + 1) * LANES]

    logits = []
    for g, pair, half in heads:
        q_ref, kp_ref, kc_ref = ins[5 * g:5 * g + 3]
        head = g * A_HEADS_PER_GROUP + 2 * pair + half
        q = col(q_ref, pair)
        q = jnp.where(lower if half == 0 else jnp.logical_not(lower), q, jnp.zeros_like(q))
        no_prev = jnp.where((i % nbs[g]) > 0, 0.0, -jnp.inf)
        s_p = lax.dot_general(q, col(kp_ref, pair), dn, preferred_element_type=f32)
        s_c = lax.dot_general(q, col(kc_ref, pair), dn, preferred_element_type=f32)
        logits.append((s_p + bias_ref[head, :, 0:Q_BLOCK] + no_prev, s_c + bias_ref[head, :, Q_BLOCK:2 * Q_BLOCK]))

    probs = []
    for s_p, s_c in logits:
        m = jnp.max(jnp.maximum(s_p, s_c), axis=1, keepdims=True)
        p_p, p_c = jnp.exp(s_p - m), jnp.exp(s_c - m)
        l = jnp.sum(p_p + p_c, axis=1, keepdims=True)
        probs.append((p_p.astype(bf16), p_c.astype(bf16), l, m + jnp.log(l)))

    for n_head in range(0, len(heads), 2):
        g, pair, _ = heads[n_head]
        vp_ref, vc_ref = ins[5 * g + 3:5 * g + 5]
        o_ref, lse_ref = outs[2 * g:2 * g + 2]
        out, lse = [], []
        for p_p, p_c, l, lse_h in probs[n_head:n_head + 2]:
            acc = jnp.dot(p_p, col(vp_ref, pair), preferred_element_type=f32)
            acc += jnp.dot(p_c, col(vc_ref, pair), preferred_element_type=f32)
            out.append(acc / l)
            lse.append(jnp.broadcast_to(lse_h, (Q_BLOCK, LANES)))
        o_ref[:, pair * LANES:(pair + 1) * LANES] = jnp.where(lower, out[0], out[1])
        lse_ref[:, pair * LANES:(pair + 1) * LANES] = jnp.where(lower, lse[0], lse[1])


def _dilated_bias():
    qi = Q_BLOCK + jnp.arange(Q_BLOCK)
    kj = jnp.arange(2 * Q_BLOCK)
    delta = qi[:, None] - kj[None, :]
    slopes = jnp.exp2(-8.0 * jnp.arange(1, A_HEADS + 1, dtype=jnp.float32) / A_HEADS)
    tables = []
    for g, (window, dil) in enumerate(DIL_GROUPS):
        on_band = (delta >= 0) & (delta <= window // dil)
        dist = (delta * dil).astype(jnp.float32)
        for h in range(A_HEADS_PER_GROUP):
            tables.append(jnp.where(on_band, -slopes[g * A_HEADS_PER_GROUP + h] * dist, -jnp.inf))
    return jnp.stack(tables)


def _dilated(u_nat, groups, batch, seq):
    steps = seq // Q_BLOCK
    arrays = [u_nat.reshape(batch, 1, seq, NAT_COLS)] + list(groups)
    bias = _dilated_bias()
    in_arrays, in_specs = [bias], [pl.BlockSpec(bias.shape, lambda b, i: (0, 0, 0))]
    out_specs, out_shapes, nbs = [], [], []
    blk = (None, None, Q_BLOCK, A_OUT)
    for g, (window, dil) in enumerate(DIL_GROUPS):
        length = seq // dil
        assert length % Q_BLOCK == 0 and window // dil == Q_BLOCK
        nb = length // Q_BLOCK
        nbs.append(nb)

        def cur(b, i, nb=nb, col=0):
            return (b, i // nb, i % nb, col)

        def prev(b, i, nb=nb, col=0):
            return (b, i // nb, jnp.maximum(i % nb - 1, 0), col)

        for fn, col in ((cur, 0), (prev, 1), (cur, 1), (prev, 2), (cur, 2)):
            in_arrays.append(arrays[g])
            in_specs.append(pl.BlockSpec(blk, functools.partial(fn, col=col)))
        for _ in range(2):
            out_specs.append(pl.BlockSpec(blk, functools.partial(cur, col=0)))
            out_shapes.append(jax.ShapeDtypeStruct((batch, dil, length, A_OUT), jnp.float32))
    return pl.pallas_call(
        functools.partial(_dilated_kernel, nbs=tuple(nbs)),
        grid=(batch, steps),
        in_specs=in_specs,
        out_specs=out_specs,
        out_shape=out_shapes,
        compiler_params=_params(("parallel", "parallel")),
    )(*in_arrays)


def _compress_kernel(x_ref, pos_ref, w1_ref, w2_ref, o_ref):
    half = CMP_STRIDE * HEAD_DIM
    x = x_ref[...]
    n_chunk = x.shape[0]
    lo = (x + pos_ref[0]).astype(jnp.bfloat16)
    hi = (x + pos_ref[1]).astype(jnp.bfloat16)
    h_lo = jnp.dot(lo, w1_ref[0:half, :], preferred_element_type=jnp.float32)
    h_hi = jnp.dot(hi, w1_ref[half:2 * half, :], preferred_element_type=jnp.float32)
    h = h_lo + pltpu.roll(h_hi, n_chunk - 1, 0)
    act = jax.nn.gelu(h, approximate=True)
    o_ref[...] = jnp.dot(act.astype(jnp.bfloat16), w2_ref[...],
                         preferred_element_type=jnp.float32).astype(o_ref.dtype)


def _compress(ckv, pos, w1, w2):
    _, batch, _, n_chunk, half = ckv.shape
    return pl.pallas_call(
        _compress_kernel,
        grid=(2, batch, B_KV_HEADS),
        in_specs=[pl.BlockSpec((None, None, None, n_chunk, half), lambda s, b, g: (s, b, g, 0, 0)),
                  pl.BlockSpec((None, 2, 1, half), lambda s, b, g: (s, 0, 0, 0)),
                  pl.BlockSpec((None, 2 * half, CMP_HIDDEN), lambda s, b, g: (s, 0, 0)),
                  pl.BlockSpec((None, CMP_HIDDEN, HEAD_DIM), lambda s, b, g: (s, 0, 0))],
        out_specs=pl.BlockSpec((None, None, None, n_chunk, HEAD_DIM), lambda s, b, g: (s, b, g, 0, 0)),
        out_shape=jax.ShapeDtypeStruct((2, batch, B_KV_HEADS, n_chunk, HEAD_DIM), jnp.bfloat16),
        compiler_params=_params(("parallel", "parallel", "parallel")),
    )(ckv, pos, w1, w2)


NEG = -2.0 ** 60
N_FEAT = HEAD_DIM


def _nsa_kernel(q_ref, kc_ref, vc_ref, skv_ref, wkv_ref, gate_ref, ovt_ref, kconst_ref, slope_ref, o_ref,
                kaug_sc, vaug_sc, kwin_sc, vwin_sc, m_sc, acc_sc, qaug_sc, sa_sc, sb_sc, *, n_sel, tk):
    n = pl.program_id(2)
    t0 = n * Q_BLOCK
    rows = B_GQA * Q_BLOCK
    seq = skv_ref.shape[0]
    dn = (((1,), (1,)), ((), ()))
    f32, bf16 = jnp.float32, jnp.bfloat16
    E = HEAD_DIM

    @pl.when(n == 0)
    def _build_keys():
        kaug_sc[:, 0:LANES] = kconst_ref[:, 0:LANES]
        kaug_sc[:, LANES:LANES + E] = skv_ref[:, 0:E]
        kaug_sc[:, LANES + E:2 * LANES] = kconst_ref[:, LANES:LANES + N_FEAT]
        vaug_sc[:, 0:E] = skv_ref[:, E:2 * E]
        vaug_sc[:, E:LANES] = kconst_ref[:, LANES + N_FEAT:2 * LANES]
        kwin_sc[:, 0:E] = wkv_ref[:, 0:E]
        kwin_sc[:, E:LANES] = kconst_ref[:, LANES:LANES + N_FEAT]
        vwin_sc[:, 0:E] = wkv_ref[:, E:2 * E]
        vwin_sc[:, E:LANES] = kconst_ref[:, LANES + N_FEAT:2 * LANES]

    q = q_ref[...]
    qs = jnp.concatenate([q[:, r * E:(r + 1) * E] for r in range(B_GQA)], axis=0)
    row = lax.broadcasted_iota(jnp.int32, (rows, 1), 0)
    within = row & (Q_BLOCK - 1)
    tq = t0 + within
    slope = slope_ref[...]
    lane = lax.broadcasted_iota(jnp.int32, (rows, N_FEAT), 1)
    feat = jnp.where(lane == 0, slope * SLC_LEN,
           jnp.where(lane == 1, slope,
           jnp.where(lane == 2, -slope * t0.astype(f32),
           jnp.where(lane == 3, -slope * within.astype(f32), 0.0))))
    q_feat = jnp.concatenate([qs, feat.astype(bf16)], axis=1)

    n_chunk = kc_ref.shape[0]
    s = lax.dot_general(qs, kc_ref[...], dn, preferred_element_type=f32)
    cmp_end = lax.broadcasted_iota(jnp.int32, (1, n_chunk), 1) * CMP_STRIDE + (CMP_LEN - 1)
    s = jnp.where(cmp_end <= tq, s, -jnp.inf)
    m = jnp.max(s, axis=1, keepdims=True)
    m = jnp.where(m == -jnp.inf, 0.0, m)
    p = jnp.exp(s - m)
    l = jnp.maximum(jnp.sum(p, axis=1, keepdims=True), TINY)
    p_c = (p / l).astype(bf16)
    o_c = jnp.dot(p_c, vc_ref[...], preferred_element_type=f32)

    span = min(WIN + Q_BLOCK, seq)
    start = pl.multiple_of(jnp.maximum(t0 + Q_BLOCK - span, 0), Q_BLOCK)
    s = lax.dot_general(q_feat, kwin_sc[pl.ds(start, span), :], dn, preferred_element_type=f32)
    dw = tq - (start + lax.broadcasted_iota(jnp.int32, (1, span), 1))
    s = jnp.where((dw >= 0) & (dw < WIN), s, -jnp.inf)
    m = jnp.max(s, axis=1, keepdims=True)
    p = jnp.exp(s - m).astype(bf16)
    acc = jnp.dot(p, vwin_sc[pl.ds(start, span), :], preferred_element_type=f32)
    o_w = acc[:, 0:E] / jnp.maximum(acc[:, E:E + 1], TINY)

    gates = 1.0 / (1.0 + jnp.exp(-gate_ref[...]))
    gate_of = lambda r, branch: jnp.broadcast_to(gates[:, 3 * r + branch:3 * r + branch + 1], (Q_BLOCK, E))
    gate_s = [gate_of(r, 1) for r in range(B_GQA)]
    gated_cw = [gate_of(r, 0) * o_c[r * Q_BLOCK:(r + 1) * Q_BLOCK] + gate_of(r, 2) * o_w[r * Q_BLOCK:(r + 1) * Q_BLOCK]
                for r in range(B_GQA)]

    imp = lax.dot_general(ovt_ref[...], p_c[0:Q_BLOCK], dn, preferred_element_type=f32)
    for r in range(1, B_GQA):
        imp += lax.dot_general(ovt_ref[...], p_c[r * Q_BLOCK:(r + 1) * Q_BLOCK], dn, preferred_element_type=f32)
    blk = lax.broadcasted_iota(jnp.int32, (LANES, Q_BLOCK), 0)
    t = t0 + lax.broadcasted_iota(jnp.int32, (LANES, Q_BLOCK), 1)
    cur = t >> 6
    forced = (blk == 0) | (blk == cur) | (blk == cur - 1)
    score = jnp.where(blk * SLC_LEN <= t, imp + jnp.where(forced, FORCE_BONUS, 0.0), -jnp.inf)

    def pick(_, carry):
        work, sel = carry
        best = jnp.max(work, axis=0, keepdims=True)
        idx = jnp.min(jnp.where(work == best, blk, LANES), axis=0, keepdims=True)
        hit = blk == idx
        finite = jnp.where(best > -jnp.inf, 1.0, 0.0)
        sel = jnp.where(hit, jnp.maximum(sel, finite), sel)
        return jnp.where(hit, -jnp.inf, work), sel

    _, sel = lax.fori_loop(0, n_sel, pick, (score, jnp.zeros((LANES, Q_BLOCK), f32)), unroll=True)
    bias = jnp.where(sel.T > 0.5, 0.0, NEG).astype(bf16)
    qaug_sc[...] = jnp.concatenate([jnp.concatenate([bias] * B_GQA, axis=0), q_feat], axis=1)

    m_sc[...] = jnp.full(m_sc.shape, -jnp.inf, f32)
    acc_sc[...] = jnp.zeros(acc_sc.shape, f32)

    def logits(j, s_ref):
        k0 = pl.multiple_of(j * tk, tk)
        s_ref[...] = lax.dot_general(qaug_sc[...], kaug_sc[pl.ds(k0, tk), :], dn, preferred_element_type=f32)

    def update(j, s_ref, diagonal):
        k0 = pl.multiple_of(j * tk, tk)
        s = s_ref[...]
        if diagonal:
            s = jnp.where(k0 + lax.broadcasted_iota(jnp.int32, (1, tk), 1) <= tq, s, -jnp.inf)
        m_old = m_sc[...]
        m_new = jnp.maximum(m_old, jnp.broadcast_to(jnp.max(s, axis=1, keepdims=True), m_old.shape))
        p = jnp.exp(s - jnp.concatenate([m_new] * (tk // LANES), axis=1)).astype(bf16)
        pv = jnp.dot(p, vaug_sc[pl.ds(k0, tk), :], preferred_element_type=f32)
        acc_sc[...] = jnp.exp(m_old - m_new) * acc_sc[...] + pv
        m_sc[...] = m_new

    j_last = t0 // tk
    logits(0, sa_sc)

    def tile_pairs(first, pairs):
        for k in range(pairs):
            j = first + 2 * k
            logits(j + 1, sb_sc)
            update(j, sa_sc, False)
            logits(j + 2, sa_sc)
            update(j + 1, sb_sc, False)

    n_quads = j_last // 4

    def four_tiles(i, carry):
        tile_pairs(4 * i, 2)
        return carry

    lax.fori_loop(0, n_quads, four_tiles, 0)

    @pl.when(j_last - 4 * n_quads >= 2)
    def _pair_left():
        tile_pairs(4 * n_quads, 1)

    odd = (j_last % 2) == 1

    @pl.when(odd)
    def _two_left():
        logits(j_last, sb_sc)
        update(j_last - 1, sa_sc, False)
        update(j_last, sb_sc, True)

    @pl.when(jnp.logical_not(odd))
    def _one_left():
        update(j_last, sa_sc, True)

    acc = acc_sc[...]
    o_s = acc[:, 0:E] / jnp.maximum(acc[:, E:E + 1], TINY)

    for r in range(B_GQA):
        out = gated_cw[r] + gate_s[r] * o_s[r * Q_BLOCK:(r + 1) * Q_BLOCK]
        o_ref[:, r * E:(r + 1) * E] = out.astype(o_ref.dtype)


def _nsa(u_nat, u_f32, cmp_kv, batch, seq):
    nq = seq // Q_BLOCK
    n_chunk = seq // CMP_STRIDE
    n_cmp = n_chunk - CMP_LEN // CMP_STRIDE + 1
    n_slc = seq // SLC_LEN
    n_sel = min(SLC_TOPK, n_slc)
    tk = min(512, seq)
    assert n_slc <= LANES and B_HEADS == 8 and seq % tk == 0
    c_start = np.arange(n_chunk) * CMP_STRIDE
    s_start = np.arange(LANES) * SLC_LEN
    overlap_t = ((c_start[None, :] < s_start[:, None] + SLC_LEN) & (c_start[None, :] + CMP_LEN > s_start[:, None])
                 & (np.arange(n_chunk)[None, :] < n_cmp) & (np.arange(LANES)[:, None] < n_slc))
    pos = np.arange(seq)
    kconst = np.zeros((seq, 2 * LANES), np.float32)
    kconst[pos, pos // SLC_LEN] = 1.0
    kconst[:, LANES + 0] = pos // SLC_LEN
    kconst[:, LANES + 1] = pos % SLC_LEN
    kconst[:, LANES + 2] = 1.0
    kconst[:, LANES + 3] = 1.0
    kconst[:, LANES + N_FEAT] = 1.0
    rows = B_GQA * Q_BLOCK
    slopes = jnp.exp2(-8.0 * jnp.arange(1, B_HEADS + 1, dtype=jnp.float32) / B_HEADS)
    slopes = jnp.repeat(slopes.reshape(B_KV_HEADS, B_GQA), Q_BLOCK, axis=1).reshape(B_KV_HEADS, rows, 1)
    kv_rows = pl.BlockSpec((seq, LANES), lambda b, g, n: (b, NAT_SLC + g))
    win_rows = pl.BlockSpec((seq, LANES), lambda b, g, n: (b, NAT_WIN + g))
    cmp_spec = lambda s: pl.BlockSpec((None, None, None, n_chunk, HEAD_DIM), lambda b, g, n: (s, b, g, 0, 0))
    return pl.pallas_call(
        functools.partial(_nsa_kernel, n_sel=n_sel, tk=tk),
        grid=(batch, B_KV_HEADS, nq),
        in_specs=[pl.BlockSpec((Q_BLOCK, B_GQA * HEAD_DIM), lambda b, g, n: (b * nq + n, NAT_Q_B + g)),
                  cmp_spec(0), cmp_spec(1), kv_rows, win_rows,
                  pl.BlockSpec((Q_BLOCK, LANES), lambda b, g, n: (b * nq + n, F_GATE + g)),
                  pl.BlockSpec((LANES, n_chunk), lambda b, g, n: (0, 0)),
                  pl.BlockSpec((seq, 2 * LANES), lambda b, g, n: (0, 0)),
                  pl.BlockSpec((None, rows, 1), lambda b, g, n: (g, 0, 0))],
        out_specs=pl.BlockSpec((Q_BLOCK, B_GQA * HEAD_DIM), lambda b, g, n: (b * nq + n, g)),
        out_shape=jax.ShapeDtypeStruct((batch * seq, B_WIDTH), jnp.bfloat16),
        scratch_shapes=[pltpu.VMEM((seq, 2 * LANES), jnp.bfloat16), pltpu.VMEM((seq, LANES), jnp.bfloat16),
                        pltpu.VMEM((seq, LANES), jnp.bfloat16), pltpu.VMEM((seq, LANES), jnp.bfloat16),
                        pltpu.VMEM((rows, LANES), jnp.float32), pltpu.VMEM((rows, LANES), jnp.float32),
                        pltpu.VMEM((rows, 2 * LANES), jnp.bfloat16),
                        pltpu.VMEM((rows, tk), jnp.float32), pltpu.VMEM((rows, tk), jnp.float32)],
        compiler_params=_params(("arbitrary", "arbitrary", "arbitrary")),
    )(u_nat, cmp_kv, cmp_kv, u_nat, u_nat, u_f32, jnp.asarray(overlap_t, jnp.bfloat16), jnp.asarray(kconst, jnp.bfloat16),
      slopes)


def _layer_norm(v, gain, bias):
    mu = jnp.mean(v, axis=1, keepdims=True)
    c = v - mu
    var = jnp.mean(c * c, axis=1, keepdims=True)
    return c * lax.rsqrt(var + LN_EPS) * gain + bias


def _sigmoid(v):
    return 1.0 / (1.0 + jnp.exp(-v))


MERGE_SPLIT = 2


def _token_major(ref, stage):
    dil, n = ref.shape[0], ref.shape[1]
    parts = stage.shape[0]
    for r in range(dil):
        for j in range(parts):
            stage[j, pl.ds(r, n, stride=dil), :] = ref[r, :, j * LANES:(j + 1) * LANES]
    return jnp.concatenate([stage[j] for j in range(parts)], axis=1)


def _merge_kernel(*refs):
    dil_refs, refs = refs[:2 * N_DIL], refs[2 * N_DIL:]
    (yb_ref, mga_ref, mgb_ref, x_ref, wa_ref, wb_ref, wo_ref, g_ref, b_ref, wr_hi_ref, wr_lo_ref, br_ref,
     h_ref, ht_ref, rw_ref, re_ref, cnt_ref) = refs[:17]
    stages, cnt_sc = refs[17:17 + 2 * (N_DIL - 1)], refs[17 + 2 * (N_DIL - 1)]
    f32 = jnp.float32
    outs, lse = [dil_refs[0][...]], [dil_refs[1][...]]
    for g in range(1, N_DIL):
        outs.append(_token_major(dil_refs[2 * g], stages[2 * g - 2]))
        lse.append(_token_major(dil_refs[2 * g + 1], stages[2 * g - 1]))
    def mixed_rows(rs):
        lse_r = [v[rs] for v in lse]
        top = jnp.maximum(jnp.maximum(lse_r[0], lse_r[1]), lse_r[2])
        e = [jnp.exp(v - top) for v in lse_r]
        y_a = (e[0] * outs[0][rs] + e[1] * outs[1][rs] + e[2] * outs[2][rs]) / (e[0] + e[1] + e[2])
        br_a = jnp.dot(y_a.astype(jnp.bfloat16), wa_ref[...], preferred_element_type=f32)
        br_b = jnp.dot(yb_ref[rs, :], wb_ref[...], preferred_element_type=f32)
        merged = _sigmoid(mga_ref[rs, :]) * br_a + _sigmoid(mgb_ref[rs, :]) * br_b
        mix = jnp.dot(merged.astype(jnp.bfloat16), wo_ref[...], preferred_element_type=f32)
        return _layer_norm(ALPHA * x_ref[rs, :] + mix, g_ref[...], b_ref[...])

    tm = x_ref.shape[0]
    part = tm // MERGE_SPLIT
    h = jnp.concatenate([mixed_rows(slice(k * part, (k + 1) * part)) for k in range(MERGE_SPLIT)], axis=0)
    h_ref[...] = h
    _to_token_tiles(ht_ref, h)

    h_hi = h.astype(jnp.bfloat16)
    h_lo = (h - h_hi.astype(f32)).astype(jnp.bfloat16)
    logits = (jnp.dot(h_hi, wr_hi_ref[...], preferred_element_type=f32)
              + jnp.dot(h_lo, wr_hi_ref[...], preferred_element_type=f32)
              + jnp.dot(h_hi, wr_lo_ref[...], preferred_element_type=f32)) + br_ref[...]
    lane = lax.broadcasted_iota(jnp.int32, logits.shape, 1)
    coarse = jnp.where(lane < N_GROUPS, logits, -jnp.inf)
    c_max = jnp.max(coarse, axis=1, keepdims=True)
    grp = jnp.min(jnp.where(coarse == c_max, lane, LANES), axis=1, keepdims=True)
    p_grp = 1.0 / jnp.sum(jnp.exp(coarse - c_max), axis=1, keepdims=True)
    lo_lane = N_GROUPS + EXP_PER_GROUP * grp
    fine = jnp.where((lane >= lo_lane) & (lane < lo_lane + EXP_PER_GROUP), logits, -jnp.inf)
    v1 = jnp.max(fine, axis=1, keepdims=True)
    i1 = jnp.min(jnp.where(fine == v1, lane, LANES), axis=1, keepdims=True)
    fine = jnp.where(lane == i1, -jnp.inf, fine)
    v2 = jnp.max(fine, axis=1, keepdims=True)
    i2 = jnp.min(jnp.where(fine == v2, lane, LANES), axis=1, keepdims=True)
    e2 = jnp.exp(v2 - v1)
    w1 = p_grp / (1.0 + e2)
    w2 = p_grp * e2 / (1.0 + e2)
    e_1, e_2 = i1 - N_GROUPS, i2 - N_GROUPS
    rw_ref[...] = jnp.where(lane == 0, w1, jnp.where(lane == 1, w2, 0.0))

    @pl.when(pl.program_id(0) == 0)
    def _zero_counts():
        cnt_sc[...] = jnp.zeros(cnt_sc.shape, f32)

    tm = logits.shape[0]
    onehot = jnp.where((lane == e_1) | (lane == e_2), 1.0, 0.0)
    earlier = (lax.broadcasted_iota(jnp.int32, (tm, tm), 0) > lax.broadcasted_iota(jnp.int32, (tm, tm), 1))
    before = jnp.dot(jnp.where(earlier, 1.0, 0.0).astype(jnp.bfloat16), onehot.astype(jnp.bfloat16),
                     preferred_element_type=f32) + cnt_sc[...]
    r_1 = jnp.sum(jnp.where(lane == e_1, before, 0.0), axis=1, keepdims=True).astype(jnp.int32)
    r_2 = jnp.sum(jnp.where(lane == e_2, before, 0.0), axis=1, keepdims=True).astype(jnp.int32)
    cnt_sc[...] = cnt_sc[...] + jnp.sum(onehot, axis=0, keepdims=True)
    cnt_ref[...] = cnt_sc[...].astype(jnp.int32)
    re_ref[...] = jnp.where(lane == 0, e_1, jnp.where(lane == 1, e_2,
                            jnp.where(lane == 2, r_1, jnp.where(lane == 3, r_2, 0))))


def _merge(dil_outs, y_b, u_f32, x2d, seq, wa, wb, wo, ln_g, ln_b, wr_hi, wr_lo, b_r, tm=256):
    t = x2d.shape[0]
    per_batch = seq // tm
    assert seq % tm == 0 and all(tm % (dil * 8) == 0 for _, dil in DIL_GROUPS)
    row = lambda w: pl.BlockSpec((tm, w), lambda i: (i, 0))
    full = lambda a: pl.BlockSpec(a.shape, lambda i: (0,) * a.ndim)
    dil_specs = [pl.BlockSpec((None, None, tm, A_OUT), lambda i: (i // per_batch, 0, i % per_batch, 0))] * 2
    stages = []
    for _, dil in DIL_GROUPS[1:]:
        dil_specs += [pl.BlockSpec((None, dil, tm // dil, A_OUT),
                                   lambda i: (i // per_batch, 0, i % per_batch, 0))] * 2
        stages += [pltpu.VMEM((A_OUT // LANES, tm, LANES), jnp.float32)] * 2
    return pl.pallas_call(
        _merge_kernel,
        grid=(t // tm,),
        in_specs=dil_specs + [row(B_WIDTH),
                  pl.BlockSpec((tm, D_MODEL), lambda i: (i, F_MGATE)),
                  pl.BlockSpec((tm, D_MODEL), lambda i: (i, F_MGATE + 1)),
                  row(D_MODEL), full(wa), full(wb), full(wo), full(ln_g), full(ln_b),
                  full(wr_hi), full(wr_lo), full(b_r)],
        out_specs=[row(D_MODEL), pl.BlockSpec((tm * CHUNKS, LANES), lambda i: (i, 0)), row(LANES), row(LANES),
                   pl.BlockSpec((1, LANES), lambda i: (0, 0))],
        out_shape=[jax.ShapeDtypeStruct((t, D_MODEL), jnp.float32),
                   jax.ShapeDtypeStruct((t * CHUNKS, LANES), jnp.float32),
                   jax.ShapeDtypeStruct((t, LANES), jnp.float32),
                   jax.ShapeDtypeStruct((t, LANES), jnp.int32),
                   jax.ShapeDtypeStruct((1, LANES), jnp.int32)],
        scratch_shapes=stages + [pltpu.VMEM((1, LANES), jnp.float32)],
        compiler_params=_params(("arbitrary",)),
    )(*dil_outs, y_b, u_f32, u_f32, x2d, wa, wb, wo, ln_g, ln_b, wr_hi, wr_lo, b_r)


CHUNKS = D_MODEL // LANES
GATHER_UNROLL = 8


def _to_token_tiles(ref2d, value):
    n = value.shape[0]
    for c in range(CHUNKS):
        ref2d[pl.ds(c, n, stride=CHUNKS), :] = value[:, c * LANES:(c + 1) * LANES]


def _from_token_tiles(ref2d):
    n = ref2d.shape[0] // CHUNKS
    return jnp.concatenate([ref2d[pl.ds(c, n, stride=CHUNKS), :] for c in range(CHUNKS)], axis=1)


def _token_copy(idx_ref, k, src2d, dst2d, r, sem):
    src_row = pl.multiple_of(idx_ref[0, 0, k] * CHUNKS, CHUNKS)
    dst_row = r * CHUNKS if isinstance(r, int) else pl.multiple_of(r * CHUNKS, CHUNKS)
    return pltpu.make_async_copy(src2d.at[pl.ds(src_row, CHUNKS), :], dst2d.at[pl.ds(dst_row, CHUNKS), :], sem)


def _start_token_gather(idx_ref, idx_of, src2d, dst2d, sem, unrolled, priority_of=lambda r: 0):
    n = dst2d.shape[0] // CHUNKS
    if unrolled:
        for r in range(n):
            _token_copy(idx_ref, idx_of(r), src2d, dst2d, r, sem).start(priority=priority_of(r))
    else:
        def start(r, carry):
            _token_copy(idx_ref, idx_of(r), src2d, dst2d, r, sem).start()
            return carry

        lax.fori_loop(0, n, start, 0, unroll=GATHER_UNROLL)


def _wait_token_gather(src2d, dst2d, sem):
    pltpu.make_async_copy(src2d.at[pl.ds(0, dst2d.shape[0]), :], dst2d, sem).wait()


def _expert_kernel(blk_e_ref, tok_ref, tok_next_ref, ht_hbm, wgu_ref, wd_ref, y_ref, xbuf_a, xbuf_b, sems):
    i = pl.program_id(0)
    last = pl.num_programs(0) - 1
    same = lambda r: r

    @pl.when(i == 0)
    def _first_block():
        _start_token_gather(tok_ref, same, ht_hbm, xbuf_a, sems.at[0], unrolled=False)

    def step(cur, nxt, sem_cur, sem_nxt):
        _wait_token_gather(ht_hbm, cur, sem_cur)
        _start_token_gather(tok_next_ref, same, ht_hbm, nxt, sem_nxt, unrolled=True, priority_of=lambda r: 1)
        xb = _from_token_tiles(cur).astype(jnp.bfloat16)
        gu = jnp.dot(xb, wgu_ref[...], preferred_element_type=jnp.float32)
        gate, up = gu[:, :D_EXPERT], gu[:, D_EXPERT:]
        act = (gate * _sigmoid(gate)) * up
        _to_token_tiles(y_ref, jnp.dot(act.astype(jnp.bfloat16), wd_ref[...], preferred_element_type=jnp.float32))

        @pl.when(i == last)
        def _drain():
            _wait_token_gather(ht_hbm, nxt, sem_nxt)

    @pl.when(i % 2 == 0)
    def _even():
        step(xbuf_a, xbuf_b, sems.at[0], sems.at[1])

    @pl.when(i % 2 == 1)
    def _odd():
        step(xbuf_b, xbuf_a, sems.at[1], sems.at[0])


def _experts(h_tiles, tok_buf, blk_e, wgu, wd, bm):
    p = tok_buf.shape[0]
    nb = p // bm
    tok = tok_buf.reshape(nb, 1, bm)
    buf = pltpu.VMEM((bm * CHUNKS, LANES), jnp.float32)
    return pl.pallas_call(
        _expert_kernel,
        grid_spec=pltpu.PrefetchScalarGridSpec(
            num_scalar_prefetch=1,
            grid=(nb,),
            in_specs=[pl.BlockSpec((1, 1, bm), lambda i, e: (i, 0, 0), memory_space=pltpu.SMEM),
                      pl.BlockSpec((1, 1, bm), lambda i, e: (jnp.minimum(i + 1, nb - 1), 0, 0),
                                   memory_space=pltpu.SMEM),
                      pl.BlockSpec(memory_space=pl.ANY),
                      pl.BlockSpec((None, D_MODEL, 2 * D_EXPERT), lambda i, e: (e[i], 0, 0)),
                      pl.BlockSpec((None, D_EXPERT, D_MODEL), lambda i, e: (e[i], 0, 0))],
            out_specs=pl.BlockSpec((bm * CHUNKS, LANES), lambda i, e: (i, 0)),
            scratch_shapes=[buf, buf, pltpu.SemaphoreType.DMA((2,))],
        ),
        out_shape=jax.ShapeDtypeStruct((p * CHUNKS, LANES), jnp.float32),
        compiler_params=_params(("arbitrary",)),
    )(blk_e, tok, tok, h_tiles, wgu, wd)


def _combine_kernel(dest_ref, dest_next_ref, yt_hbm, h_ref, rw_ref, g_ref, b_ref, o_ref, ybuf_a, ybuf_b, sems):
    i = pl.program_id(0)
    last = pl.num_programs(0) - 1

    def start(idx_ref, buf, sem, unrolled):
        for k in range(MOE_TOP_K):
            _start_token_gather(idx_ref, lambda r, k=k: MOE_TOP_K * r + k, yt_hbm, buf.at[k], sem, unrolled,
                                priority_of=lambda r: r % 2)

    def wait(buf, sem):
        for k in range(MOE_TOP_K):
            _wait_token_gather(yt_hbm, buf.at[k], sem)

    @pl.when(i == 0)
    def _first_tile():
        start(dest_ref, ybuf_a, sems.at[0], unrolled=False)

    def step(cur, nxt, sem_cur, sem_nxt):
        wait(cur, sem_cur)
        start(dest_next_ref, nxt, sem_nxt, unrolled=True)
        rw = rw_ref[...]
        ffn = rw[:, 0:1] * _from_token_tiles(cur.at[0]) + rw[:, 1:2] * _from_token_tiles(cur.at[1])
        o_ref[...] = _layer_norm(ALPHA * h_ref[...] + ffn, g_ref[...], b_ref[...])

        @pl.when(i == last)
        def _drain():
            wait(nxt, sem_nxt)

    @pl.when(i % 2 == 0)
    def _even():
        step(ybuf_a, ybuf_b, sems.at[0], sems.at[1])

    @pl.when(i % 2 == 1)
    def _odd():
        step(ybuf_b, ybuf_a, sems.at[1], sems.at[0])


def _combine(dest, y_tiles, h, rw, ln_g, ln_b, tm=256):
    t = h.shape[0]
    assert t % tm == 0
    nt = t // tm
    idx = dest.reshape(nt, 1, MOE_TOP_K * tm)
    buf = pltpu.VMEM((MOE_TOP_K, tm * CHUNKS, LANES), jnp.float32)
    return pl.pallas_call(
        _combine_kernel,
        grid=(nt,),
        in_specs=[pl.BlockSpec((1, 1, MOE_TOP_K * tm), lambda i: (i, 0, 0), memory_space=pltpu.SMEM),
                  pl.BlockSpec((1, 1, MOE_TOP_K * tm), lambda i: (jnp.minimum(i + 1, nt - 1), 0, 0),
                               memory_space=pltpu.SMEM),
                  pl.BlockSpec(memory_space=pl.ANY),
                  pl.BlockSpec((tm, D_MODEL), lambda i: (i, 0)),
                  pl.BlockSpec((tm, LANES), lambda i: (i, 0)),
                  pl.BlockSpec((1, D_MODEL), lambda i: (0, 0)),
                  pl.BlockSpec((1, D_MODEL), lambda i: (0, 0))],
        out_specs=pl.BlockSpec((tm, D_MODEL), lambda i: (i, 0)),
        out_shape=jax.ShapeDtypeStruct((t, D_MODEL), jnp.float32),
        scratch_shapes=[buf, buf, pltpu.SemaphoreType.DMA((2,))],
        compiler_params=_params(("arbitrary",)),
    )(idx, idx, y_tiles, h, rw, ln_g, ln_b)


def _dispatch(route, counts, bm):
    t = route.shape[0]
    n_assign = t * MOE_TOP_K
    expert, rank = route[:, 0:MOE_TOP_K], route[:, MOE_TOP_K:2 * MOE_TOP_K]
    starts = jnp.cumsum(counts) - counts
    pcounts = (counts + bm - 1) // bm * bm
    pends = jnp.cumsum(pcounts)
    pstarts = pends - pcounts
    ids = jnp.arange(N_EXPERTS, dtype=jnp.int32)
    dest = jnp.sum(jnp.where(expert[..., None] == ids, pstarts, 0), axis=-1) + rank
    p = -(-(n_assign + N_EXPERTS * (bm - 1)) // bm) * bm
    nb = p // bm
    blk_e = jnp.minimum(jnp.sum(pends[None, :] <= (jnp.arange(nb, dtype=jnp.int32) * bm)[:, None], axis=1),
                        N_EXPERTS - 1).astype(jnp.int32)
    order = jnp.argsort(expert.reshape(n_assign), stable=True).astype(jnp.int32)
    row_e = jnp.repeat(blk_e, bm)
    k = jnp.arange(p, dtype=jnp.int32) - pstarts[row_e]
    src = jnp.clip(starts[row_e] + k, 0, n_assign - 1)
    tok_buf = jnp.where(k < counts[row_e], order[src] // MOE_TOP_K, 0).astype(jnp.int32)
    return tok_buf, dest.astype(jnp.int32), blk_e


def _split_bf16(w):
    hi = w.astype(jnp.bfloat16)
    return hi, (w - hi.astype(jnp.float32)).astype(jnp.bfloat16)


def kernel(x, w_in, cmp_pos_k, cmp_w1_k, cmp_w2_k, cmp_pos_v, cmp_w1_v, cmp_w2_v, w_branch_a, w_branch_b, w_out, ln1_g, ln1_b, w_coarse, b_coarse, w_fine, b_fine, w_gate_up, w_down, ln2_g, ln2_b):
    batch, seq, d = x.shape
    bf16 = jnp.bfloat16
    h = x.reshape(batch * seq, d)
    cols_bf, cols_f32, scale_bf = _in_proj_columns()
    half = CMP_STRIDE * HEAD_DIM
    for l in range(DEPTH):
        u_nat, *groups = _project_bf(h, (_permute_columns(w_in[l], cols_bf) * scale_bf).astype(bf16), batch, seq)
        u_f32, ckv = _project_f32(h, _permute_columns(w_in[l], cols_f32).astype(bf16), batch, seq)
        dil_outs = _dilated(u_nat, groups, batch, seq)
        pos = jnp.stack([cmp_pos_k[l], cmp_pos_v[l]]).reshape(2, 2, 1, half)
        w1 = jnp.stack([cmp_w1_k[l], cmp_w1_v[l]]).astype(bf16)
        w2 = jnp.stack([cmp_w2_k[l], cmp_w2_v[l]]).astype(bf16)
        cmp_kv = _compress(ckv, pos, w1, w2)
        y_b = _nsa(u_nat, u_f32, cmp_kv, batch, seq)
        w_r = jnp.concatenate([w_coarse[l], w_fine[l].transpose(1, 0, 2).reshape(d, N_EXPERTS)], axis=1)
        w_r = jnp.pad(w_r, ((0, 0), (0, LANES - w_r.shape[1])))
        b_r = jnp.pad(jnp.concatenate([b_coarse[l], b_fine[l].reshape(N_EXPERTS)]),
                      (0, LANES - N_GROUPS - N_EXPERTS)).reshape(1, LANES)
        wr_hi, wr_lo = _split_bf16(w_r)
        h1, h1_tiles, rw, route, counts = _merge(dil_outs, y_b, u_f32, h, seq, w_branch_a[l].astype(bf16),
                                       w_branch_b[l].astype(bf16), w_out[l].astype(bf16),
                                       ln1_g[l].reshape(1, d), ln1_b[l].reshape(1, d), wr_hi, wr_lo, b_r)
        bm = 256
        tok_buf, dest, blk_e = _dispatch(route, counts[0, :N_EXPERTS], bm)
        y_tiles = _experts(h1_tiles, tok_buf, blk_e, w_gate_up[l].astype(bf16), w_down[l].astype(bf16), bm)
        h = _combine(dest, y_tiles, h1, rw, ln2_g[l].reshape(1, d), ln2_b[l].reshape(1, d))
    return h.reshape(batch, seq, d)
```

```python
import functools

import numpy as np
import jax
import jax.numpy as jnp
from jax import lax
from jax.experimental import pallas as pl
from jax.experimental.pallas import tpu as pltpu

D_MODEL = 1024
HEAD_DIM = 64
Q_BLOCK = 128
DIL_GROUPS = ((128, 1), (512, 4), (2048, 16))
N_DIL = len(DIL_GROUPS)
A_HEADS_PER_GROUP = 4
A_HEADS = A_HEADS_PER_GROUP * N_DIL
A_WIDTH = A_HEADS * HEAD_DIM
A_OUT = A_HEADS_PER_GROUP * HEAD_DIM
B_HEADS = 8
B_KV_HEADS = 2
B_GQA = B_HEADS // B_KV_HEADS
B_WIDTH = B_HEADS * HEAD_DIM
B_KV_WIDTH = B_KV_HEADS * HEAD_DIM
CMP_LEN = 32
CMP_STRIDE = 16
CMP_HIDDEN = 256
SLC_LEN = 64
SLC_TOPK = 16
WIN = 512
N_GROUPS = 4
EXP_PER_GROUP = 8
N_EXPERTS = N_GROUPS * EXP_PER_GROUP
D_EXPERT = 512
MOE_TOP_K = 2
DEPTH = 1
ALPHA = (2.0 * DEPTH) ** 0.25
LN_EPS = 1e-5
FORCE_BONUS = 1e4
TINY = 1e-30
ATTN_SCALE = HEAD_DIM ** -0.5

LANES = 128
VMEM_LIMIT = 48 * 1024 * 1024

QKV_COLS = 3 * A_OUT
NAT_Q_B = QKV_COLS // A_OUT
NAT_SLC = (QKV_COLS + B_WIDTH) // LANES
NAT_WIN = NAT_SLC + B_KV_HEADS
NAT_COLS = QKV_COLS + B_WIDTH + 4 * B_KV_WIDTH
BF_COLS = NAT_COLS + (N_DIL - 1) * QKV_COLS
F_MGATE = 0
F_GATE = 2 * D_MODEL // LANES
F_NAT_COLS = 2 * D_MODEL + B_KV_HEADS * LANES
F_COLS = F_NAT_COLS + 2 * B_KV_WIDTH


def _params(semantics):
    return pltpu.CompilerParams(dimension_semantics=semantics, vmem_limit_bytes=VMEM_LIMIT)


def _in_proj_columns():
    kv_off = 3 * A_WIDTH + B_WIDTH
    gate_off = kv_off + 6 * B_KV_WIDTH
    mg_off = gate_off + 3 * B_HEADS
    def qkv_cols(g):
        return [part * A_WIDTH + g * A_OUT + c for part in range(3) for c in range(A_OUT)]

    bf = qkv_cols(0) + list(range(3 * A_WIDTH, 3 * A_WIDTH + B_WIDTH))
    for first in (2, 4):
        for g in range(B_KV_HEADS):
            for i in (first, first + 1):
                base = kv_off + i * B_KV_WIDTH + g * HEAD_DIM
                bf += list(range(base, base + HEAD_DIM))
    for g in range(1, N_DIL):
        bf += qkv_cols(g)
    f32 = list(range(mg_off, mg_off + 2 * D_MODEL))
    per_head = 3 * B_GQA
    for g in range(B_KV_HEADS):
        f32 += list(range(gate_off + g * per_head, gate_off + (g + 1) * per_head))
        f32 += [-1] * (LANES - per_head)
    f32 += list(range(kv_off, kv_off + 2 * B_KV_WIDTH))
    assert len(bf) == BF_COLS and len(f32) == F_COLS
    bf = np.asarray(bf)
    is_q = (bf < A_WIDTH) | ((bf >= 3 * A_WIDTH) & (bf < 3 * A_WIDTH + B_WIDTH))
    scale = np.where(is_q, ATTN_SCALE, 1.0).astype(np.float32)
    return bf, np.asarray(f32), scale


def _permute_columns(w, cols):
    taken = jnp.take(w, jnp.asarray(np.maximum(cols, 0)), axis=1)
    return jnp.where(jnp.asarray(cols >= 0)[None, :], taken, 0.0)


def _proj_f32_kernel(x_ref, w_ref, nat_ref, cmp_ref, stage):
    xb = x_ref[...].astype(jnp.bfloat16)
    tm = x_ref.shape[0]
    n_chunk = tm // CMP_STRIDE
    for c0 in range(0, F_NAT_COLS, A_OUT):
        nat_ref[:, c0:c0 + A_OUT] = jnp.dot(xb, w_ref[:, c0:c0 + A_OUT], preferred_element_type=jnp.float32)
    res = jnp.dot(xb, w_ref[:, F_NAT_COLS:F_COLS], preferred_element_type=jnp.float32)
    lower = lax.broadcasted_iota(jnp.int32, (n_chunk, LANES), 1) < HEAD_DIM
    for kv in range(2):
        stage[...] = res[:, kv * LANES:(kv + 1) * LANES]
        for p in range(0, CMP_STRIDE, 2):
            even = stage[pl.ds(p, n_chunk, stride=CMP_STRIDE), :]
            odd = stage[pl.ds(p + 1, n_chunk, stride=CMP_STRIDE), :]
            cols = slice(p * HEAD_DIM, (p + 2) * HEAD_DIM)
            cmp_ref[kv, 0, :, cols] = jnp.where(lower, even, pltpu.roll(odd, HEAD_DIM, 1))
            cmp_ref[kv, 1, :, cols] = jnp.where(lower, pltpu.roll(even, HEAD_DIM, 1), odd)


def _project_f32(x2d, w, batch, seq, tm=512):
    t, d = x2d.shape
    assert seq % tm == 0 and w.shape[1] == F_COLS and B_KV_HEADS * HEAD_DIM == LANES
    per_batch = seq // tm
    half = CMP_STRIDE * HEAD_DIM
    return pl.pallas_call(
        _proj_f32_kernel,
        grid=(t // tm,),
        in_specs=[pl.BlockSpec((tm, d), lambda i: (i, 0)),
                  pl.BlockSpec((d, F_COLS), lambda i: (0, 0))],
        out_specs=[pl.BlockSpec((tm, F_NAT_COLS), lambda i: (i, 0)),
                   pl.BlockSpec((2, None, B_KV_HEADS, tm // CMP_STRIDE, half),
                                lambda i: (0, i // per_batch, 0, i % per_batch, 0))],
        out_shape=[jax.ShapeDtypeStruct((t, F_NAT_COLS), jnp.float32),
                   jax.ShapeDtypeStruct((2, batch, B_KV_HEADS, seq // CMP_STRIDE, half), jnp.float32)],
        scratch_shapes=[pltpu.VMEM((tm, LANES), jnp.float32)],
        compiler_params=_params(("parallel",)),
    )(x2d, w)


def _proj_bf_kernel(x_ref, w_ref, nat_ref, *rest, dils):
    group_refs, stage = rest[:-1], rest[-1]
    xb = x_ref[...].astype(jnp.bfloat16)
    tm = x_ref.shape[0]
    for c0 in range(0, NAT_COLS, A_OUT):
        nat_ref[:, c0:c0 + A_OUT] = jnp.dot(
            xb, w_ref[:, c0:c0 + A_OUT], preferred_element_type=jnp.float32).astype(nat_ref.dtype)
    col = NAT_COLS
    for o_ref, dil in zip(group_refs, dils):
        for part in range(3):
            res = jnp.dot(xb, w_ref[:, col:col + A_OUT], preferred_element_type=jnp.float32)
            for j in range(A_OUT // LANES):
                stage[j] = res[:, j * LANES:(j + 1) * LANES]
                c0 = part * A_OUT + j * LANES
                for r in range(dil):
                    o_ref[r, :, c0:c0 + LANES] = stage[j, pl.ds(r, tm // dil, stride=dil), :].astype(o_ref.dtype)
            col += A_OUT


def _project_bf(x2d, w, batch, seq, tm=512):
    t, d = x2d.shape
    dils = tuple(dil for _, dil in DIL_GROUPS[1:])
    assert seq % tm == 0 and all(tm % (dil * 16) == 0 for dil in dils) and w.shape[1] == BF_COLS
    per_batch = seq // tm
    out_specs = [pl.BlockSpec((tm, NAT_COLS), lambda i: (i, 0))]
    out_shapes = [jax.ShapeDtypeStruct((t, NAT_COLS), jnp.bfloat16)]
    for dil in dils:
        out_specs.append(pl.BlockSpec((None, dil, tm // dil, QKV_COLS),
                                      lambda i: (i // per_batch, 0, i % per_batch, 0)))
        out_shapes.append(jax.ShapeDtypeStruct((batch, dil, seq // dil, QKV_COLS), jnp.bfloat16))
    return pl.pallas_call(
        functools.partial(_proj_bf_kernel, dils=dils),
        grid=(t // tm,),
        in_specs=[pl.BlockSpec((tm, d), lambda i: (i, 0)),
                  pl.BlockSpec((d, BF_COLS), lambda i: (0, 0))],
        out_specs=out_specs,
        out_shape=out_shapes,
        scratch_shapes=[pltpu.VMEM((A_OUT // LANES, tm, LANES), jnp.float32)],
        compiler_params=_params(("parallel",)),
    )(x2d, w)


def _dilated_kernel(*refs, nbs):
    bias_ref, ins, outs = refs[0], refs[1:1 + 5 * N_DIL], refs[1 + 5 * N_DIL:]
    i = pl.program_id(1)
    dn = (((1,), (1,)), ((), ()))
    f32, bf16 = jnp.float32, jnp.bfloat16
    lower = lax.broadcasted_iota(jnp.int32, (Q_BLOCK, LANES), 1) < HEAD_DIM
    heads = [(g, pair, half) for g in range(N_DIL) for pair in range(A_OUT // LANES) for half in range(2)]

    def col(ref, pair):
        return ref[:, pair * LANES:(pair + 1) * LANES]

    logits = []
    for g, pair, half in heads:
        q_ref, kp_ref, kc_ref = ins[5 * g:5 * g + 3]
        head = g * A_HEADS_PER_GROUP + 2 * pair + half
        q = col(q_ref, pair)
        q = jnp.where(lower if half == 0 else jnp.logical_not(lower), q, jnp.zeros_like(q))
        no_prev = jnp.where((i % nbs[g]) > 0, 0.0, -jnp.inf)
        s_p = lax.dot_general(q, col(kp_ref, pair), dn, preferred_element_type=f32)
        s_c = lax.dot_general(q, col(kc_ref, pair), dn, preferred_element_type=f32)
        logits.append((s_p + bias_ref[head, :, 0:Q_BLOCK] + no_prev, s_c + bias_ref[head, :, Q_BLOCK:2 * Q_BLOCK]))

    probs = []
    for s_p, s_c in logits:
        m = jnp.max(jnp.maximum(s_p, s_c), axis=1, keepdims=True)
        p_p, p_c = jnp.exp(s_p - m), jnp.exp(s_c - m)
        l = jnp.sum(p_p + p_c, axis=1, keepdims=True)
        probs.append((p_p.astype(bf16), p_c.astype(bf16), l, m + jnp.log(l)))

    for n_head in range(0, len(heads), 2):
        g, pair, _ = heads[n_head]
        vp_ref, vc_ref = ins[5 * g + 3:5 * g + 5]
        o_ref, lse_ref = outs[2 * g:2 * g + 2]
        out, lse = [], []
        for p_p, p_c, l, lse_h in probs[n_head:n_head + 2]:
            acc = jnp.dot(p_p, col(vp_ref, pair), preferred_element_type=f32)
            acc += jnp.dot(p_c, col(vc_ref, pair), preferred_element_type=f32)
            out.append(acc / l)
            lse.append(jnp.broadcast_to(lse_h, (Q_BLOCK, LANES)))
        o_ref[:, pair * LANES:(pair + 1) * LANES] = jnp.where(lower, out[0], out[1])
        lse_ref[:, pair * LANES:(pair + 1) * LANES] = jnp.where(lower, lse[0], lse[1])


def _dilated_bias():
    qi = Q_BLOCK + jnp.arange(Q_BLOCK)
    kj = jnp.arange(2 * Q_BLOCK)
    delta = qi[:, None] - kj[None, :]
    slopes = jnp.exp2(-8.0 * jnp.arange(1, A_HEADS + 1, dtype=jnp.float32) / A_HEADS)
    tables = []
    for g, (window, dil) in enumerate(DIL_GROUPS):
        on_band = (delta >= 0) & (delta <= window // dil)
        dist = (delta * dil).astype(jnp.float32)
        for h in range(A_HEADS_PER_GROUP):
            tables.append(jnp.where(on_band, -slopes[g * A_HEADS_PER_GROUP + h] * dist, -jnp.inf))
    return jnp.stack(tables)


def _dilated(u_nat, groups, batch, seq):
    steps = seq // Q_BLOCK
    arrays = [u_nat.reshape(batch, 1, seq, NAT_COLS)] + list(groups)
    bias = _dilated_bias()
    in_arrays, in_specs = [bias], [pl.BlockSpec(bias.shape, lambda b, i: (0, 0, 0))]
    out_specs, out_shapes, nbs = [], [], []
    blk = (None, None, Q_BLOCK, A_OUT)
    for g, (window, dil) in enumerate(DIL_GROUPS):
        length = seq // dil
        assert length % Q_BLOCK == 0 and window // dil == Q_BLOCK
        nb = length // Q_BLOCK
        nbs.append(nb)

        def cur(b, i, nb=nb, col=0):
            return (b, i // nb, i % nb, col)

        def prev(b, i, nb=nb, col=0):
            return (b, i // nb, jnp.maximum(i % nb - 1, 0), col)

        for fn, col in ((cur, 0), (prev, 1), (cur, 1), (prev, 2), (cur, 2)):
            in_arrays.append(arrays[g])
            in_specs.append(pl.BlockSpec(blk, functools.partial(fn, col=col)))
        for _ in range(2):
            out_specs.append(pl.BlockSpec(blk, functools.partial(cur, col=0)))
            out_shapes.append(jax.ShapeDtypeStruct((batch, dil, length, A_OUT), jnp.float32))
    return pl.pallas_call(
        functools.partial(_dilated_kernel, nbs=tuple(nbs)),
        grid=(batch, steps),
        in_specs=in_specs,
        out_specs=out_specs,
        out_shape=out_shapes,
        compiler_params=_params(("parallel", "parallel")),
    )(*in_arrays)


def _compress_kernel(x_ref, pos_ref, w1_ref, w2_ref, o_ref):
    half = CMP_STRIDE * HEAD_DIM
    x = x_ref[...]
    n_chunk = x.shape[0]
    lo = (x + pos_ref[0]).astype(jnp.bfloat16)
    hi = (x + pos_ref[1]).astype(jnp.bfloat16)
    h_lo = jnp.dot(lo, w1_ref[0:half, :], preferred_element_type=jnp.float32)
    h_hi = jnp.dot(hi, w1_ref[half:2 * half, :], preferred_element_type=jnp.float32)
    h = h_lo + pltpu.roll(h_hi, n_chunk - 1, 0)
    act = jax.nn.gelu(h, approximate=True)
    o_ref[...] = jnp.dot(act.astype(jnp.bfloat16), w2_ref[...],
                         preferred_element_type=jnp.float32).astype(o_ref.dtype)


def _compress(ckv, pos, w1, w2):
    _, batch, _, n_chunk, half = ckv.shape
    return pl.pallas_call(
        _compress_kernel,
        grid=(2, batch, B_KV_HEADS),
        in_specs=[pl.BlockSpec((None, None, None, n_chunk, half), lambda s, b, g: (s, b, g, 0, 0)),
                  pl.BlockSpec((None, 2, 1, half), lambda s, b, g: (s, 0, 0, 0)),
                  pl.BlockSpec((None, 2 * half, CMP_HIDDEN), lambda s, b, g: (s, 0, 0)),
                  pl.BlockSpec((None, CMP_HIDDEN, HEAD_DIM), lambda s, b, g: (s, 0, 0))],
        out_specs=pl.BlockSpec((None, None, None, n_chunk, HEAD_DIM), lambda s, b, g: (s, b, g, 0, 0)),
        out_shape=jax.ShapeDtypeStruct((2, batch, B_KV_HEADS, n_chunk, HEAD_DIM), jnp.bfloat16),
        compiler_params=_params(("parallel", "parallel", "parallel")),
    )(ckv, pos, w1, w2)


NEG = -2.0 ** 60
N_FEAT = HEAD_DIM


def _nsa_kernel(q_ref, kc_ref, vc_ref, skv_ref, wkv_ref, gate_ref, ovt_ref, kconst_ref, slope_ref, o_ref,
                kaug_sc, vaug_sc, kwin_sc, vwin_sc, m_sc, acc_sc, qaug_sc, sa_sc, sb_sc, *, n_sel, tk):
    n = pl.program_id(2)
    t0 = n * Q_BLOCK
    rows = B_GQA * Q_BLOCK
    seq = skv_ref.shape[0]
    dn = (((1,), (1,)), ((), ()))
    f32, bf16 = jnp.float32, jnp.bfloat16
    E = HEAD_DIM

    @pl.when(n == 0)
    def _build_keys():
        kaug_sc[:, 0:LANES] = kconst_ref[:, 0:LANES]
        kaug_sc[:, LANES:LANES + E] = skv_ref[:, 0:E]
        kaug_sc[:, LANES + E:2 * LANES] = kconst_ref[:, LANES:LANES + N_FEAT]
        vaug_sc[:, 0:E] = skv_ref[:, E:2 * E]
        vaug_sc[:, E:LANES] = kconst_ref[:, LANES + N_FEAT:2 * LANES]
        kwin_sc[:, 0:E] = wkv_ref[:, 0:E]
        kwin_sc[:, E:LANES] = kconst_ref[:, LANES:LANES + N_FEAT]
        vwin_sc[:, 0:E] = wkv_ref[:, E:2 * E]
        vwin_sc[:, E:LANES] = kconst_ref[:, LANES + N_FEAT:2 * LANES]

    q = q_ref[...]
    qs = jnp.concatenate([q[:, r * E:(r + 1) * E] for r in range(B_GQA)], axis=0)
    row = lax.broadcasted_iota(jnp.int32, (rows, 1), 0)
    within = row & (Q_BLOCK - 1)
    tq = t0 + within
    slope = slope_ref[...]
    lane = lax.broadcasted_iota(jnp.int32, (rows, N_FEAT), 1)
    feat = jnp.where(lane == 0, slope * SLC_LEN,
           jnp.where(lane == 1, slope,
           jnp.where(lane == 2, -slope * t0.astype(f32),
           jnp.where(lane == 3, -slope * within.astype(f32), 0.0))))
    q_feat = jnp.concatenate([qs, feat.astype(bf16)], axis=1)

    n_chunk = kc_ref.shape[0]
    s = lax.dot_general(qs, kc_ref[...], dn, preferred_element_type=f32)
    cmp_end = lax.broadcasted_iota(jnp.int32, (1, n_chunk), 1) * CMP_STRIDE + (CMP_LEN - 1)
    s = jnp.where(cmp_end <= tq, s, -jnp.inf)
    m = jnp.max(s, axis=1, keepdims=True)
    m = jnp.where(m == -jnp.inf, 0.0, m)
    p = jnp.exp(s - m)
    l = jnp.maximum(jnp.sum(p, axis=1, keepdims=True), TINY)
    p_c = (p / l).astype(bf16)
    o_c = jnp.dot(p_c, vc_ref[...], preferred_element_type=f32)

    span = min(WIN + Q_BLOCK, seq)
    start = pl.multiple_of(jnp.maximum(t0 + Q_BLOCK - span, 0), Q_BLOCK)
    s = lax.dot_general(q_feat, kwin_sc[pl.ds(start, span), :], dn, preferred_element_type=f32)
    dw = tq - (start + lax.broadcasted_iota(jnp.int32, (1, span), 1))
    s = jnp.where((dw >= 0) & (dw < WIN), s, -jnp.inf)
    m = jnp.max(s, axis=1, keepdims=True)
    p = jnp.exp(s - m).astype(bf16)
    acc = jnp.dot(p, vwin_sc[pl.ds(start, span), :], preferred_element_type=f32)
    o_w = acc[:, 0:E] / jnp.maximum(acc[:, E:E + 1], TINY)

    gates = 1.0 / (1.0 + jnp.exp(-gate_ref[...]))
    gate_of = lambda r, branch: jnp.broadcast_to(gates[:, 3 * r + branch:3 * r + branch + 1], (Q_BLOCK, E))
    gate_s = [gate_of(r, 1) for r in range(B_GQA)]
    gated_cw = [gate_of(r, 0) * o_c[r * Q_BLOCK:(r + 1) * Q_BLOCK] + gate_of(r, 2) * o_w[r * Q_BLOCK:(r + 1) * Q_BLOCK]
                for r in range(B_GQA)]

    imp = lax.dot_general(ovt_ref[...], p_c[0:Q_BLOCK], dn, preferred_element_type=f32)
    for r in range(1, B_GQA):
        imp += lax.dot_general(ovt_ref[...], p_c[r * Q_BLOCK:(r + 1) * Q_BLOCK], dn, preferred_element_type=f32)
    blk = lax.broadcasted_iota(jnp.int32, (LANES, Q_BLOCK), 0)
    t = t0 + lax.broadcasted_iota(jnp.int32, (LANES, Q_BLOCK), 1)
    cur = t >> 6
    forced = (blk == 0) | (blk == cur) | (blk == cur - 1)
    score = jnp.where(blk * SLC_LEN <= t, imp + jnp.where(forced, FORCE_BONUS, 0.0), -jnp.inf)

    def pick(_, carry):
        work, sel = carry
        best = jnp.max(work, axis=0, keepdims=True)
        idx = jnp.min(jnp.where(work == best, blk, LANES), axis=0, keepdims=True)
        hit = blk == idx
        finite = jnp.where(best > -jnp.inf, 1.0, 0.0)
        sel = jnp.where(hit, jnp.maximum(sel, finite), sel)
        return jnp.where(hit, -jnp.inf, work), sel

    _, sel = lax.fori_loop(0, n_sel, pick, (score, jnp.zeros((LANES, Q_BLOCK), f32)), unroll=True)
    bias = jnp.where(sel.T > 0.5, 0.0, NEG).astype(bf16)
    qaug_sc[...] = jnp.concatenate([jnp.concatenate([bias] * B_GQA, axis=0), q_feat], axis=1)

    m_sc[...] = jnp.full(m_sc.shape, -jnp.inf, f32)
    acc_sc[...] = jnp.zeros(acc_sc.shape, f32)

    def logits(j, s_ref):
        k0 = pl.multiple_of(j * tk, tk)
        s_ref[...] = lax.dot_general(qaug_sc[...], kaug_sc[pl.ds(k0, tk), :], dn, preferred_element_type=f32)

    def update(j, s_ref, diagonal):
        k0 = pl.multiple_of(j * tk, tk)
        s = s_ref[...]
        if diagonal:
            s = jnp.where(k0 + lax.broadcasted_iota(jnp.int32, (1, tk), 1) <= tq, s, -jnp.inf)
        m_old = m_sc[...]
        m_new = jnp.maximum(m_old, jnp.broadcast_to(jnp.max(s, axis=1, keepdims=True), m_old.shape))
        p = jnp.exp(s - jnp.concatenate([m_new] * (tk // LANES), axis=1)).astype(bf16)
        pv = jnp.dot(p, vaug_sc[pl.ds(k0, tk), :], preferred_element_type=f32)
        acc_sc[...] = jnp.exp(m_old - m_new) * acc_sc[...] + pv
        m_sc[...] = m_new

    j_last = t0 // tk
    logits(0, sa_sc)

    def tile_pairs(first, pairs):
        for k in range(pairs):
            j = first + 2 * k
            logits(j + 1, sb_sc)
            update(j, sa_sc, False)
            logits(j + 2, sa_sc)
            update(j + 1, sb_sc, False)

    n_quads = j_last // 4

    def four_tiles(i, carry):
        tile_pairs(4 * i, 2)
        return carry

    lax.fori_loop(0, n_quads, four_tiles, 0)

    @pl.when(j_last - 4 * n_quads >= 2)
    def _pair_left():
        tile_pairs(4 * n_quads, 1)

    odd = (j_last % 2) == 1

    @pl.when(odd)
    def _two_left():
        logits(j_last, sb_sc)
        update(j_last - 1, sa_sc, False)
        update(j_last, sb_sc, True)

    @pl.when(jnp.logical_not(odd))
    def _one_left():
        update(j_last, sa_sc, True)

    acc = acc_sc[...]
    o_s = acc[:, 0:E] / jnp.maximum(acc[:, E:E + 1], TINY)

    for r in range(B_GQA):
        out = gated_cw[r] + gate_s[r] * o_s[r * Q_BLOCK:(r + 1) * Q_BLOCK]
        o_ref[:, r * E:(r + 1) * E] = out.astype(o_ref.dtype)


def _nsa(u_nat, u_f32, cmp_kv, batch, seq):
    nq = seq // Q_BLOCK
    n_chunk = seq // CMP_STRIDE
    n_cmp = n_chunk - CMP_LEN // CMP_STRIDE + 1
    n_slc = seq // SLC_LEN
    n_sel = min(SLC_TOPK, n_slc)
    tk = min(512, seq)
    assert n_slc <= LANES and B_HEADS == 8 and seq % tk == 0
    c_start = np.arange(n_chunk) * CMP_STRIDE
    s_start = np.arange(LANES) * SLC_LEN
    overlap_t = ((c_start[None, :] < s_start[:, None] + SLC_LEN) & (c_start[None, :] + CMP_LEN > s_start[:, None])
                 & (np.arange(n_chunk)[None, :] < n_cmp) & (np.arange(LANES)[:, None] < n_slc))
    pos = np.arange(seq)
    kconst = np.zeros((seq, 2 * LANES), np.float32)
    kconst[pos, pos // SLC_LEN] = 1.0
    kconst[:, LANES + 0] = pos // SLC_LEN
    kconst[:, LANES + 1] = pos % SLC_LEN
    kconst[:, LANES + 2] = 1.0
    kconst[:, LANES + 3] = 1.0
    kconst[:, LANES + N_FEAT] = 1.0
    rows = B_GQA * Q_BLOCK
    slopes = jnp.exp2(-8.0 * jnp.arange(1, B_HEADS + 1, dtype=jnp.float32) / B_HEADS)
    slopes = jnp.repeat(slopes.reshape(B_KV_HEADS, B_GQA), Q_BLOCK, axis=1).reshape(B_KV_HEADS, rows, 1)
    kv_rows = pl.BlockSpec((seq, LANES), lambda b, g, n: (b, NAT_SLC + g))
    win_rows = pl.BlockSpec((seq, LANES), lambda b, g, n: (b, NAT_WIN + g))
    cmp_spec = lambda s: pl.BlockSpec((None, None, None, n_chunk, HEAD_DIM), lambda b, g, n: (s, b, g, 0, 0))
    return pl.pallas_call(
        functools.partial(_nsa_kernel, n_sel=n_sel, tk=tk),
        grid=(batch, B_KV_HEADS, nq),
        in_specs=[pl.BlockSpec((Q_BLOCK, B_GQA * HEAD_DIM), lambda b, g, n: (b * nq + n, NAT_Q_B + g)),
                  cmp_spec(0), cmp_spec(1), kv_rows, win_rows,
                  pl.BlockSpec((Q_BLOCK, LANES), lambda b, g, n: (b * nq + n, F_GATE + g)),
                  pl.BlockSpec((LANES, n_chunk), lambda b, g, n: (0, 0)),
                  pl.BlockSpec((seq, 2 * LANES), lambda b, g, n: (0, 0)),
                  pl.BlockSpec((None, rows, 1), lambda b, g, n: (g, 0, 0))],
        out_specs=pl.BlockSpec((Q_BLOCK, B_GQA * HEAD_DIM), lambda b, g, n: (b * nq + n, g)),
        out_shape=jax.ShapeDtypeStruct((batch * seq, B_WIDTH), jnp.bfloat16),
        scratch_shapes=[pltpu.VMEM((seq, 2 * LANES), jnp.bfloat16), pltpu.VMEM((seq, LANES), jnp.bfloat16),
                        pltpu.VMEM((seq, LANES), jnp.bfloat16), pltpu.VMEM((seq, LANES), jnp.bfloat16),
                        pltpu.VMEM((rows, LANES), jnp.float32), pltpu.VMEM((rows, LANES), jnp.float32),
                        pltpu.VMEM((rows, 2 * LANES), jnp.bfloat16),
                        pltpu.VMEM((rows, tk), jnp.float32), pltpu.VMEM((rows, tk), jnp.float32)],
        compiler_params=_params(("arbitrary", "arbitrary", "arbitrary")),
    )(u_nat, cmp_kv, cmp_kv, u_nat, u_nat, u_f32, jnp.asarray(overlap_t, jnp.bfloat16), jnp.asarray(kconst, jnp.bfloat16),
      slopes)


def _layer_norm(v, gain, bias):
    mu = jnp.mean(v, axis=1, keepdims=True)
    c = v - mu
    var = jnp.mean(c * c, axis=1, keepdims=True)
    return c * lax.rsqrt(var + LN_EPS) * gain + bias


def _sigmoid(v):
    return 1.0 / (1.0 + jnp.exp(-v))


MERGE_SPLIT = 2


def _token_major(ref, stage):
    dil, n = ref.shape[0], ref.shape[1]
    parts = stage.shape[0]
    for r in range(dil):
        for j in range(parts):
            stage[j, pl.ds(r, n, stride=dil), :] = ref[r, :, j * LANES:(j + 1) * LANES]
    return jnp.concatenate([stage[j] for j in range(parts)], axis=1)


def _merge_kernel(*refs):
    dil_refs, refs = refs[:2 * N_DIL], refs[2 * N_DIL:]
    (yb_ref, mga_ref, mgb_ref, x_ref, wa_ref, wb_ref, wo_ref, g_ref, b_ref, wr_hi_ref, wr_lo_ref, br_ref,
     h_ref, ht_ref, rw_ref, re_ref, cnt_ref) = refs[:17]
    stages, cnt_sc = refs[17:17 + 2 * (N_DIL - 1)], refs[17 + 2 * (N_DIL - 1)]
    f32 = jnp.float32
    outs, lse = [dil_refs[0][...]], [dil_refs[1][...]]
    for g in range(1, N_DIL):
        outs.append(_token_major(dil_refs[2 * g], stages[2 * g - 2]))
        lse.append(_token_major(dil_refs[2 * g + 1], stages[2 * g - 1]))
    def mixed_rows(rs):
        lse_r = [v[rs] for v in lse]
        top = jnp.maximum(jnp.maximum(lse_r[0], lse_r[1]), lse_r[2])
        e = [jnp.exp(v - top) for v in lse_r]
        y_a = (e[0] * outs[0][rs] + e[1] * outs[1][rs] + e[2] * outs[2][rs]) / (e[0] + e[1] + e[2])
        br_a = jnp.dot(y_a.astype(jnp.bfloat16), wa_ref[...], preferred_element_type=f32)
        br_b = jnp.dot(yb_ref[rs, :], wb_ref[...], preferred_element_type=f32)
        merged = _sigmoid(mga_ref[rs, :]) * br_a + _sigmoid(mgb_ref[rs, :]) * br_b
        mix = jnp.dot(merged.astype(jnp.bfloat16), wo_ref[...], preferred_element_type=f32)
        return _layer_norm(ALPHA * x_ref[rs, :] + mix, g_ref[...], b_ref[...])

    tm = x_ref.shape[0]
    part = tm // MERGE_SPLIT
    h = jnp.concatenate([mixed_rows(slice(k * part, (k + 1) * part)) for k in range(MERGE_SPLIT)], axis=0)
    h_ref[...] = h
    _to_token_tiles(ht_ref, h)

    h_hi = h.astype(jnp.bfloat16)
    h_lo = (h - h_hi.astype(f32)).astype(jnp.bfloat16)
    logits = (jnp.dot(h_hi, wr_hi_ref[...], preferred_element_type=f32)
              + jnp.dot(h_lo, wr_hi_ref[...], preferred_element_type=f32)
              + jnp.dot(h_hi, wr_lo_ref[...], preferred_element_type=f32)) + br_ref[...]
    lane = lax.broadcasted_iota(jnp.int32, logits.shape, 1)
    coarse = jnp.where(lane < N_GROUPS, logits, -jnp.inf)
    c_max = jnp.max(coarse, axis=1, keepdims=True)
    grp = jnp.min(jnp.where(coarse == c_max, lane, LANES), axis=1, keepdims=True)
    p_grp = 1.0 / jnp.sum(jnp.exp(coarse - c_max), axis=1, keepdims=True)
    lo_lane = N_GROUPS + EXP_PER_GROUP * grp
    fine = jnp.where((lane >= lo_lane) & (lane < lo_lane + EXP_PER_GROUP), logits, -jnp.inf)
    v1 = jnp.max(fine, axis=1, keepdims=True)
    i1 = jnp.min(jnp.where(fine == v1, lane, LANES), axis=1, keepdims=True)
    fine = jnp.where(lane == i1, -jnp.inf, fine)
    v2 = jnp.max(fine, axis=1, keepdims=True)
    i2 = jnp.min(jnp.where(fine == v2, lane, LANES), axis=1, keepdims=True)
    e2 = jnp.exp(v2 - v1)
    w1 = p_grp / (1.0 + e2)
    w2 = p_grp * e2 / (1.0 + e2)
    e_1, e_2 = i1 - N_GROUPS, i2 - N_GROUPS
    rw_ref[...] = jnp.where(lane == 0, w1, jnp.where(lane == 1, w2, 0.0))

    @pl.when(pl.program_id(0) == 0)
    def _zero_counts():
        cnt_sc[...] = jnp.zeros(cnt_sc.shape, f32)

    tm = logits.shape[0]
    onehot = jnp.where((lane == e_1) | (lane == e_2), 1.0, 0.0)
    earlier = (lax.broadcasted_iota(jnp.int32, (tm, tm), 0) > lax.broadcasted_iota(jnp.int32, (tm, tm), 1))
    before = jnp.dot(jnp.where(earlier, 1.0, 0.0).astype(jnp.bfloat16), onehot.astype(jnp.bfloat16),
                     preferred_element_type=f32) + cnt_sc[...]
    r_1 = jnp.sum(jnp.where(lane == e_1, before, 0.0), axis=1, keepdims=True).astype(jnp.int32)
    r_2 = jnp.sum(jnp.where(lane == e_2, before, 0.0), axis=1, keepdims=True).astype(jnp.int32)
    cnt_sc[...] = cnt_sc[...] + jnp.sum(onehot, axis=0, keepdims=True)
    cnt_ref[...] = cnt_sc[...].astype(jnp.int32)
    re_ref[...] = jnp.where(lane == 0, e_1, jnp.where(lane == 1, e_2,
                            jnp.where(lane == 2, r_1, jnp.where(lane == 3, r_2, 0))))


def _merge(dil_outs, y_b, u_f32, x2d, seq, wa, wb, wo, ln_g, ln_b, wr_hi, wr_lo, b_r, tm=256):
    t = x2d.shape[0]
    per_batch = seq // tm
    assert seq % tm == 0 and all(tm % (dil * 8) == 0 for _, dil in DIL_GROUPS)
    row = lambda w: pl.BlockSpec((tm, w), lambda i: (i, 0))
    full = lambda a: pl.BlockSpec(a.shape, lambda i: (0,) * a.ndim)
    dil_specs = [pl.BlockSpec((None, None, tm, A_OUT), lambda i: (i // per_batch, 0, i % per_batch, 0))] * 2
    stages = []
    for _, dil in DIL_GROUPS[1:]:
        dil_specs += [pl.BlockSpec((None, dil, tm // dil, A_OUT),
                                   lambda i: (i // per_batch, 0, i % per_batch, 0))] * 2
        stages += [pltpu.VMEM((A_OUT // LANES, tm, LANES), jnp.float32)] * 2
    return pl.pallas_call(
        _merge_kernel,
        grid=(t // tm,),
        in_specs=dil_specs + [row(B_WIDTH),
                  pl.BlockSpec((tm, D_MODEL), lambda i: (i, F_MGATE)),
                  pl.BlockSpec((tm, D_MODEL), lambda i: (i, F_MGATE + 1)),
                  row(D_MODEL), full(wa), full(wb), full(wo), full(ln_g), full(ln_b),
                  full(wr_hi), full(wr_lo), full(b_r)],
        out_specs=[row(D_MODEL), pl.BlockSpec((tm * CHUNKS, LANES), lambda i: (i, 0)), row(LANES), row(LANES),
                   pl.BlockSpec((1, LANES), lambda i: (0, 0))],
        out_shape=[jax.ShapeDtypeStruct((t, D_MODEL), jnp.float32),
                   jax.ShapeDtypeStruct((t * CHUNKS, LANES), jnp.float32),
                   jax.ShapeDtypeStruct((t, LANES), jnp.float32),
                   jax.ShapeDtypeStruct((t, LANES), jnp.int32),
                   jax.ShapeDtypeStruct((1, LANES), jnp.int32)],
        scratch_shapes=stages + [pltpu.VMEM((1, LANES), jnp.float32)],
        compiler_params=_params(("arbitrary",)),
    )(*dil_outs, y_b, u_f32, u_f32, x2d, wa, wb, wo, ln_g, ln_b, wr_hi, wr_lo, b_r)


CHUNKS = D_MODEL // LANES
GATHER_UNROLL = 8


def _to_token_tiles(ref2d, value):
    n = value.shape[0]
    for c in range(CHUNKS):
        ref2d[pl.ds(c, n, stride=CHUNKS), :] = value[:, c * LANES:(c + 1) * LANES]


def _from_token_tiles(ref2d):
    n = ref2d.shape[0] // CHUNKS
    return jnp.concatenate([ref2d[pl.ds(c, n, stride=CHUNKS), :] for c in range(CHUNKS)], axis=1)


def _token_copy(idx_ref, k, src2d, dst2d, r, sem):
    src_row = pl.multiple_of(idx_ref[0, 0, k] * CHUNKS, CHUNKS)
    dst_row = r * CHUNKS if isinstance(r, int) else pl.multiple_of(r * CHUNKS, CHUNKS)
    return pltpu.make_async_copy(src2d.at[pl.ds(src_row, CHUNKS), :], dst2d.at[pl.ds(dst_row, CHUNKS), :], sem)


def _start_token_gather(idx_ref, idx_of, src2d, dst2d, sem, unrolled, priority_of=lambda r: 0):
    n = dst2d.shape[0] // CHUNKS
    if unrolled:
        for r in range(n):
            _token_copy(idx_ref, idx_of(r), src2d, dst2d, r, sem).start(priority=priority_of(r))
    else:
        def start(r, carry):
            _token_copy(idx_ref, idx_of(r), src2d, dst2d, r, sem).start()
            return carry

        lax.fori_loop(0, n, start, 0, unroll=GATHER_UNROLL)


def _wait_token_gather(src2d, dst2d, sem):
    pltpu.make_async_copy(src2d.at[pl.ds(0, dst2d.shape[0]), :], dst2d, sem).wait()


GATHER_BUFFERS = 3


def _gather_ring(i, first_refs, ahead_ref, start, wait, compute):
    last = pl.num_programs(0) - 1

    @pl.when(i == 0)
    def _prologue():
        for k, ref in enumerate(first_refs):
            start(ref, k, unrolled=False)

    def step(k):
        ahead = (k + GATHER_BUFFERS - 1) % GATHER_BUFFERS
        wait(k)
        start(ahead_ref, ahead, unrolled=True)
        compute(k)

        @pl.when(i == last)
        def _drain():
            for other in range(GATHER_BUFFERS):
                if other != k:
                    wait(other)

    for k in range(GATHER_BUFFERS):
        pl.when(i % GATHER_BUFFERS == k)(functools.partial(step, k))


def _expert_kernel(blk_e_ref, tok_ref, tok_1_ref, tok_2_ref, ht_hbm, wgu_ref, wd_ref, y_ref, *scratch):
    bufs, sems = scratch[:GATHER_BUFFERS], scratch[GATHER_BUFFERS]

    def start(idx_ref, k, unrolled):
        _start_token_gather(idx_ref, lambda r: r, ht_hbm, bufs[k], sems.at[k], unrolled, priority_of=lambda r: r % 2)

    def wait(k):
        _wait_token_gather(ht_hbm, bufs[k], sems.at[k])

    def compute(k):
        xb = _from_token_tiles(bufs[k]).astype(jnp.bfloat16)
        gu = jnp.dot(xb, wgu_ref[...], preferred_element_type=jnp.float32)
        gate, up = gu[:, :D_EXPERT], gu[:, D_EXPERT:]
        act = (gate * _sigmoid(gate)) * up
        _to_token_tiles(y_ref, jnp.dot(act.astype(jnp.bfloat16), wd_ref[...], preferred_element_type=jnp.float32))

    _gather_ring(pl.program_id(0), (tok_ref, tok_1_ref), tok_2_ref, start, wait, compute)


def _experts(h_tiles, tok_buf, blk_e, wgu, wd, bm):
    p = tok_buf.shape[0]
    nb = p // bm
    tok = tok_buf.reshape(nb, 1, bm)
    idx_spec = lambda ahead: pl.BlockSpec((1, 1, bm), lambda i, e: (jnp.minimum(i + ahead, nb - 1), 0, 0),
                                          memory_space=pltpu.SMEM)
    return pl.pallas_call(
        _expert_kernel,
        grid_spec=pltpu.PrefetchScalarGridSpec(
            num_scalar_prefetch=1,
            grid=(nb,),
            in_specs=[idx_spec(0), idx_spec(1), idx_spec(2),
                      pl.BlockSpec(memory_space=pl.ANY),
                      pl.BlockSpec((None, D_MODEL, 2 * D_EXPERT), lambda i, e: (e[i], 0, 0)),
                      pl.BlockSpec((None, D_EXPERT, D_MODEL), lambda i, e: (e[i], 0, 0))],
            out_specs=pl.BlockSpec((bm * CHUNKS, LANES), lambda i, e: (i, 0)),
            scratch_shapes=[pltpu.VMEM((bm * CHUNKS, LANES), jnp.float32)] * GATHER_BUFFERS
                           + [pltpu.SemaphoreType.DMA((GATHER_BUFFERS,))],
        ),
        out_shape=jax.ShapeDtypeStruct((p * CHUNKS, LANES), jnp.float32),
        compiler_params=_params(("arbitrary",)),
    )(blk_e, tok, tok, tok, h_tiles, wgu, wd)


def _combine_kernel(dest_ref, dest_1_ref, dest_2_ref, yt_hbm, h_ref, rw_ref, g_ref, b_ref, o_ref, *scratch):
    bufs, sems = scratch[:GATHER_BUFFERS], scratch[GATHER_BUFFERS]

    def start(idx_ref, k, unrolled):
        for slot in range(MOE_TOP_K):
            _start_token_gather(idx_ref, lambda r, slot=slot: MOE_TOP_K * r + slot, yt_hbm, bufs[k].at[slot],
                                sems.at[k], unrolled, priority_of=lambda r: r % 2)

    def wait(k):
        for slot in range(MOE_TOP_K):
            _wait_token_gather(yt_hbm, bufs[k].at[slot], sems.at[k])

    def compute(k):
        rw = rw_ref[...]
        ffn = rw[:, 0:1] * _from_token_tiles(bufs[k].at[0]) + rw[:, 1:2] * _from_token_tiles(bufs[k].at[1])
        o_ref[...] = _layer_norm(ALPHA * h_ref[...] + ffn, g_ref[...], b_ref[...])

    _gather_ring(pl.program_id(0), (dest_ref, dest_1_ref), dest_2_ref, start, wait, compute)


def _combine(dest, y_tiles, h, rw, ln_g, ln_b, tm=256):
    t = h.shape[0]
    assert t % tm == 0
    nt = t // tm
    idx = dest.reshape(nt, 1, MOE_TOP_K * tm)
    idx_spec = lambda ahead: pl.BlockSpec((1, 1, MOE_TOP_K * tm), lambda i: (jnp.minimum(i + ahead, nt - 1), 0, 0),
                                          memory_space=pltpu.SMEM)
    return pl.pallas_call(
        _combine_kernel,
        grid=(nt,),
        in_specs=[idx_spec(0), idx_spec(1), idx_spec(2),
                  pl.BlockSpec(memory_space=pl.ANY),
                  pl.BlockSpec((tm, D_MODEL), lambda i: (i, 0)),
                  pl.BlockSpec((tm, LANES), lambda i: (i, 0)),
                  pl.BlockSpec((1, D_MODEL), lambda i: (0, 0)),
                  pl.BlockSpec((1, D_MODEL), lambda i: (0, 0))],
        out_specs=pl.BlockSpec((tm, D_MODEL), lambda i: (i, 0)),
        out_shape=jax.ShapeDtypeStruct((t, D_MODEL), jnp.float32),
        scratch_shapes=[pltpu.VMEM((MOE_TOP_K, tm * CHUNKS, LANES), jnp.float32)] * GATHER_BUFFERS
                       + [pltpu.SemaphoreType.DMA((GATHER_BUFFERS,))],
        compiler_params=_params(("arbitrary",)),
    )(idx, idx, idx, y_tiles, h, rw, ln_g, ln_b)


def _dispatch(route, counts, bm):
    t = route.shape[0]
    n_assign = t * MOE_TOP_K
    expert, rank = route[:, 0:MOE_TOP_K], route[:, MOE_TOP_K:2 * MOE_TOP_K]
    starts = jnp.cumsum(counts) - counts
    pcounts = (counts + bm - 1) // bm * bm
    pends = jnp.cumsum(pcounts)
    pstarts = pends - pcounts
    ids = jnp.arange(N_EXPERTS, dtype=jnp.int32)
    dest = jnp.sum(jnp.where(expert[..., None] == ids, pstarts, 0), axis=-1) + rank
    p = -(-(n_assign + N_EXPERTS * (bm - 1)) // bm) * bm
    nb = p // bm
    blk_e = jnp.minimum(jnp.sum(pends[None, :] <= (jnp.arange(nb, dtype=jnp.int32) * bm)[:, None], axis=1),
                        N_EXPERTS - 1).astype(jnp.int32)
    order = jnp.argsort(expert.reshape(n_assign), stable=True).astype(jnp.int32)
    row_e = jnp.repeat(blk_e, bm)
    k = jnp.arange(p, dtype=jnp.int32) - pstarts[row_e]
    src = jnp.clip(starts[row_e] + k, 0, n_assign - 1)
    tok_buf = jnp.where(k < counts[row_e], order[src] // MOE_TOP_K, 0).astype(jnp.int32)
    return tok_buf, dest.astype(jnp.int32), blk_e


def _split_bf16(w):
    hi = w.astype(jnp.bfloat16)
    return hi, (w - hi.astype(jnp.float32)).astype(jnp.bfloat16)


def kernel(x, w_in, cmp_pos_k, cmp_w1_k, cmp_w2_k, cmp_pos_v, cmp_w1_v, cmp_w2_v, w_branch_a, w_branch_b, w_out, ln1_g, ln1_b, w_coarse, b_coarse, w_fine, b_fine, w_gate_up, w_down, ln2_g, ln2_b):
    batch, seq, d = x.shape
    bf16 = jnp.bfloat16
    h = x.reshape(batch * seq, d)
    cols_bf, cols_f32, scale_bf = _in_proj_columns()
    half = CMP_STRIDE * HEAD_DIM
    for l in range(DEPTH):
        u_nat, *groups = _project_bf(h, (_permute_columns(w_in[l], cols_bf) * scale_bf).astype(bf16), batch, seq)
        u_f32, ckv = _project_f32(h, _permute_columns(w_in[l], cols_f32).astype(bf16), batch, seq)
        dil_outs = _dilated(u_nat, groups, batch, seq)
        pos = jnp.stack([cmp_pos_k[l], cmp_pos_v[l]]).reshape(2, 2, 1, half)
        w1 = jnp.stack([cmp_w1_k[l], cmp_w1_v[l]]).astype(bf16)
        w2 = jnp.stack([cmp_w2_k[l], cmp_w2_v[l]]).astype(bf16)
        cmp_kv = _compress(ckv, pos, w1, w2)
        y_b = _nsa(u_nat, u_f32, cmp_kv, batch, seq)
        w_r = jnp.concatenate([w_coarse[l], w_fine[l].transpose(1, 0, 2).reshape(d, N_EXPERTS)], axis=1)
        w_r = jnp.pad(w_r, ((0, 0), (0, LANES - w_r.shape[1])))
        b_r = jnp.pad(jnp.concatenate([b_coarse[l], b_fine[l].reshape(N_EXPERTS)]),
                      (0, LANES - N_GROUPS - N_EXPERTS)).reshape(1, LANES)
        wr_hi, wr_lo = _split_bf16(w_r)
        h1, h1_tiles, rw, route, counts = _merge(dil_outs, y_b, u_f32, h, seq, w_branch_a[l].astype(bf16),
                                       w_branch_b[l].astype(bf16), w_out[l].astype(bf16),
                                       ln1_g[l].reshape(1, d), ln1_b[l].reshape(1, d), wr_hi, wr_lo, b_r)
        bm = 256
        tok_buf, dest, blk_e = _dispatch(route, counts[0, :N_EXPERTS], bm)
        y_tiles = _experts(h1_tiles, tok_buf, blk_e, w_gate_up[l].astype(bf16), w_down[l].astype(bf16), bm)
        h = _combine(dest, y_tiles, h1, rw, ln2_g[l].reshape(1, d), ln2_b[l].reshape(1, d))
    return h.reshape(batch, seq, d)
```

```python
import functools

import numpy as np
import jax
import jax.numpy as jnp
from jax import lax
from jax.experimental import pallas as pl
from jax.experimental.pallas import tpu as pltpu

D_MODEL = 1024
HEAD_DIM = 64
Q_BLOCK = 128
DIL_GROUPS = ((128, 1), (512, 4), (2048, 16))
N_DIL = len(DIL_GROUPS)
A_HEADS_PER_GROUP = 4
A_HEADS = A_HEADS_PER_GROUP * N_DIL
A_WIDTH = A_HEADS * HEAD_DIM
A_OUT = A_HEADS_PER_GROUP * HEAD_DIM
B_HEADS = 8
B_KV_HEADS = 2
B_GQA = B_HEADS // B_KV_HEADS
B_WIDTH = B_HEADS * HEAD_DIM
B_KV_WIDTH = B_KV_HEADS * HEAD_DIM
CMP_LEN = 32
CMP_STRIDE = 16
CMP_HIDDEN = 256
SLC_LEN = 64
SLC_TOPK = 16
WIN = 512
N_GROUPS = 4
EXP_PER_GROUP = 8
N_EXPERTS = N_GROUPS * EXP_PER_GROUP
D_EXPERT = 512
MOE_TOP_K = 2
DEPTH = 1
ALPHA = (2.0 * DEPTH) ** 0.25
LN_EPS = 1e-5
FORCE_BONUS = 1e4
TINY = 1e-30
ATTN_SCALE = HEAD_DIM ** -0.5

LANES = 128
VMEM_LIMIT = 48 * 1024 * 1024

QKV_COLS = 3 * A_OUT
NAT_Q_B = QKV_COLS // A_OUT
NAT_SLC = (QKV_COLS + B_WIDTH) // LANES
NAT_WIN = NAT_SLC + B_KV_HEADS
NAT_COLS = QKV_COLS + B_WIDTH + 4 * B_KV_WIDTH
BF_COLS = NAT_COLS + (N_DIL - 1) * QKV_COLS
F_MGATE = 0
F_GATE = 2 * D_MODEL // LANES
F_NAT_COLS = 2 * D_MODEL + B_KV_HEADS * LANES
F_COLS = F_NAT_COLS + 2 * B_KV_WIDTH


def _params(semantics):
    return pltpu.CompilerParams(dimension_semantics=semantics, vmem_limit_bytes=VMEM_LIMIT)


def _in_proj_columns():
    kv_off = 3 * A_WIDTH + B_WIDTH
    gate_off = kv_off + 6 * B_KV_WIDTH
    mg_off = gate_off + 3 * B_HEADS
    def qkv_cols(g):
        return [part * A_WIDTH + g * A_OUT + c for part in range(3) for c in range(A_OUT)]

    bf = qkv_cols(0) + list(range(3 * A_WIDTH, 3 * A_WIDTH + B_WIDTH))
    for first in (2, 4):
        for g in range(B_KV_HEADS):
            for i in (first, first + 1):
                base = kv_off + i * B_KV_WIDTH + g * HEAD_DIM
                bf += list(range(base, base + HEAD_DIM))
    for g in range(1, N_DIL):
        bf += qkv_cols(g)
    f32 = list(range(mg_off, mg_off + 2 * D_MODEL))
    per_head = 3 * B_GQA
    for g in range(B_KV_HEADS):
        f32 += list(range(gate_off + g * per_head, gate_off + (g + 1) * per_head))
        f32 += [-1] * (LANES - per_head)
    f32 += list(range(kv_off, kv_off + 2 * B_KV_WIDTH))
    assert len(bf) == BF_COLS and len(f32) == F_COLS
    bf = np.asarray(bf)
    is_q = (bf < A_WIDTH) | ((bf >= 3 * A_WIDTH) & (bf < 3 * A_WIDTH + B_WIDTH))
    scale = np.where(is_q, ATTN_SCALE, 1.0).astype(np.float32)
    return bf, np.asarray(f32), scale


def _permute_columns(w, cols):
    pieces, start = [], 0
    for end in range(1, len(cols) + 1):
        run_ends = end == len(cols) or (cols[end] != cols[end - 1] + 1 if cols[end - 1] >= 0 else cols[end] >= 0)
        if run_ends:
            first = int(cols[start])
            pieces.append(jnp.zeros((w.shape[0], end - start), w.dtype) if first < 0
                          else w[:, first:first + end - start])
            start = end
    return jnp.concatenate(pieces, axis=1)


def _proj_f32_kernel(x_ref, w_ref, nat_ref, cmp_ref, stage):
    xb = x_ref[...].astype(jnp.bfloat16)
    tm = x_ref.shape[0]
    n_chunk = tm // CMP_STRIDE
    for c0 in range(0, F_NAT_COLS, A_OUT):
        nat_ref[:, c0:c0 + A_OUT] = jnp.dot(xb, w_ref[:, c0:c0 + A_OUT], preferred_element_type=jnp.float32)
    res = jnp.dot(xb, w_ref[:, F_NAT_COLS:F_COLS], preferred_element_type=jnp.float32)
    lower = lax.broadcasted_iota(jnp.int32, (n_chunk, LANES), 1) < HEAD_DIM
    for kv in range(2):
        stage[...] = res[:, kv * LANES:(kv + 1) * LANES]
        for p in range(0, CMP_STRIDE, 2):
            even = stage[pl.ds(p, n_chunk, stride=CMP_STRIDE), :]
            odd = stage[pl.ds(p + 1, n_chunk, stride=CMP_STRIDE), :]
            cols = slice(p * HEAD_DIM, (p + 2) * HEAD_DIM)
            cmp_ref[kv, 0, :, cols] = jnp.where(lower, even, pltpu.roll(odd, HEAD_DIM, 1))
            cmp_ref[kv, 1, :, cols] = jnp.where(lower, pltpu.roll(even, HEAD_DIM, 1), odd)


def _project_f32(x2d, w, batch, seq, tm=512):
    t, d = x2d.shape
    assert seq % tm == 0 and w.shape[1] == F_COLS and B_KV_HEADS * HEAD_DIM == LANES
    per_batch = seq // tm
    half = CMP_STRIDE * HEAD_DIM
    return pl.pallas_call(
        _proj_f32_kernel,
        grid=(t // tm,),
        in_specs=[pl.BlockSpec((tm, d), lambda i: (i, 0)),
                  pl.BlockSpec((d, F_COLS), lambda i: (0, 0))],
        out_specs=[pl.BlockSpec((tm, F_NAT_COLS), lambda i: (i, 0)),
                   pl.BlockSpec((2, None, B_KV_HEADS, tm // CMP_STRIDE, half),
                                lambda i: (0, i // per_batch, 0, i % per_batch, 0))],
        out_shape=[jax.ShapeDtypeStruct((t, F_NAT_COLS), jnp.float32),
                   jax.ShapeDtypeStruct((2, batch, B_KV_HEADS, seq // CMP_STRIDE, half), jnp.float32)],
        scratch_shapes=[pltpu.VMEM((tm, LANES), jnp.float32)],
        compiler_params=_params(("parallel",)),
    )(x2d, w)


def _proj_bf_kernel(x_ref, w_ref, nat_ref, *rest, dils):
    group_refs, stage = rest[:-1], rest[-1]
    xb = x_ref[...].astype(jnp.bfloat16)
    tm = x_ref.shape[0]
    for c0 in range(0, NAT_COLS, A_OUT):
        nat_ref[:, c0:c0 + A_OUT] = jnp.dot(
            xb, w_ref[:, c0:c0 + A_OUT], preferred_element_type=jnp.float32).astype(nat_ref.dtype)
    col = NAT_COLS
    for o_ref, dil in zip(group_refs, dils):
        for part in range(3):
            res = jnp.dot(xb, w_ref[:, col:col + A_OUT], preferred_element_type=jnp.float32)
            for j in range(A_OUT // LANES):
                stage[j] = res[:, j * LANES:(j + 1) * LANES]
                c0 = part * A_OUT + j * LANES
                for r in range(dil):
                    o_ref[r, :, c0:c0 + LANES] = stage[j, pl.ds(r, tm // dil, stride=dil), :].astype(o_ref.dtype)
            col += A_OUT


def _project_bf(x2d, w, batch, seq, tm=512):
    t, d = x2d.shape
    dils = tuple(dil for _, dil in DIL_GROUPS[1:])
    assert seq % tm == 0 and all(tm % (dil * 16) == 0 for dil in dils) and w.shape[1] == BF_COLS
    per_batch = seq // tm
    out_specs = [pl.BlockSpec((tm, NAT_COLS), lambda i: (i, 0))]
    out_shapes = [jax.ShapeDtypeStruct((t, NAT_COLS), jnp.bfloat16)]
    for dil in dils:
        out_specs.append(pl.BlockSpec((None, dil, tm // dil, QKV_COLS),
                                      lambda i: (i // per_batch, 0, i % per_batch, 0)))
        out_shapes.append(jax.ShapeDtypeStruct((batch, dil, seq // dil, QKV_COLS), jnp.bfloat16))
    return pl.pallas_call(
        functools.partial(_proj_bf_kernel, dils=dils),
        grid=(t // tm,),
        in_specs=[pl.BlockSpec((tm, d), lambda i: (i, 0)),
                  pl.BlockSpec((d, BF_COLS), lambda i: (0, 0))],
        out_specs=out_specs,
        out_shape=out_shapes,
        scratch_shapes=[pltpu.VMEM((A_OUT // LANES, tm, LANES), jnp.float32)],
        compiler_params=_params(("parallel",)),
    )(x2d, w)


def _dilated_kernel(*refs, nbs):
    bias_ref, ins, outs = refs[0], refs[1:1 + 5 * N_DIL], refs[1 + 5 * N_DIL:]
    i = pl.program_id(1)
    dn = (((1,), (1,)), ((), ()))
    f32, bf16 = jnp.float32, jnp.bfloat16
    lower = lax.broadcasted_iota(jnp.int32, (Q_BLOCK, LANES), 1) < HEAD_DIM
    heads = [(g, pair, half) for g in range(N_DIL) for pair in range(A_OUT // LANES) for half in range(2)]

    def col(ref, pair):
        return ref[:, pair * LANES:(pair + 1) * LANES]

    logits = []
    for g, pair, half in heads:
        q_ref, kp_ref, kc_ref = ins[5 * g:5 * g + 3]
        head = g * A_HEADS_PER_GROUP + 2 * pair + half
        q = col(q_ref, pair)
        q = jnp.where(lower if half == 0 else jnp.logical_not(lower), q, jnp.zeros_like(q))
        no_prev = jnp.where((i % nbs[g]) > 0, 0.0, -jnp.inf)
        s_p = lax.dot_general(q, col(kp_ref, pair), dn, preferred_element_type=f32)
        s_c = lax.dot_general(q, col(kc_ref, pair), dn, preferred_element_type=f32)
        logits.append((s_p + bias_ref[head, :, 0:Q_BLOCK] + no_prev, s_c + bias_ref[head, :, Q_BLOCK:2 * Q_BLOCK]))

    probs = []
    for s_p, s_c in logits:
        m = jnp.max(jnp.maximum(s_p, s_c), axis=1, keepdims=True)
        p_p, p_c = jnp.exp(s_p - m), jnp.exp(s_c - m)
        l = jnp.sum(p_p + p_c, axis=1, keepdims=True)
        probs.append((p_p.astype(bf16), p_c.astype(bf16), l, m + jnp.log(l)))

    for n_head in range(0, len(heads), 2):
        g, pair, _ = heads[n_head]
        vp_ref, vc_ref = ins[5 * g + 3:5 * g + 5]
        o_ref, lse_ref = outs[2 * g:2 * g + 2]
        out, lse = [], []
        for p_p, p_c, l, lse_h in probs[n_head:n_head + 2]:
            acc = jnp.dot(p_p, col(vp_ref, pair), preferred_element_type=f32)
            acc += jnp.dot(p_c, col(vc_ref, pair), preferred_element_type=f32)
            out.append(acc / l)
            lse.append(jnp.broadcast_to(lse_h, (Q_BLOCK, LANES)))
        o_ref[:, pair * LANES:(pair + 1) * LANES] = jnp.where(lower, out[0], out[1])
        lse_ref[:, pair * LANES:(pair + 1) * LANES] = jnp.where(lower, lse[0], lse[1])


def _dilated_bias():
    qi = Q_BLOCK + jnp.arange(Q_BLOCK)
    kj = jnp.arange(2 * Q_BLOCK)
    delta = qi[:, None] - kj[None, :]
    slopes = jnp.exp2(-8.0 * jnp.arange(1, A_HEADS + 1, dtype=jnp.float32) / A_HEADS)
    tables = []
    for g, (window, dil) in enumerate(DIL_GROUPS):
        on_band = (delta >= 0) & (delta <= window // dil)
        dist = (delta * dil).astype(jnp.float32)
        for h in range(A_HEADS_PER_GROUP):
            tables.append(jnp.where(on_band, -slopes[g * A_HEADS_PER_GROUP + h] * dist, -jnp.inf))
    return jnp.stack(tables)


def _dilated(u_nat, groups, batch, seq):
    steps = seq // Q_BLOCK
    arrays = [u_nat.reshape(batch, 1, seq, NAT_COLS)] + list(groups)
    bias = _dilated_bias()
    in_arrays, in_specs = [bias], [pl.BlockSpec(bias.shape, lambda b, i: (0, 0, 0))]
    out_specs, out_shapes, nbs = [], [], []
    blk = (None, None, Q_BLOCK, A_OUT)
    for g, (window, dil) in enumerate(DIL_GROUPS):
        length = seq // dil
        assert length % Q_BLOCK == 0 and window // dil == Q_BLOCK
        nb = length // Q_BLOCK
        nbs.append(nb)

        def cur(b, i, nb=nb, col=0):
            return (b, i // nb, i % nb, col)

        def prev(b, i, nb=nb, col=0):
            return (b, i // nb, jnp.maximum(i % nb - 1, 0), col)

        for fn, col in ((cur, 0), (prev, 1), (cur, 1), (prev, 2), (cur, 2)):
            in_arrays.append(arrays[g])
            in_specs.append(pl.BlockSpec(blk, functools.partial(fn, col=col)))
        for _ in range(2):
            out_specs.append(pl.BlockSpec(blk, functools.partial(cur, col=0)))
            out_shapes.append(jax.ShapeDtypeStruct((batch, dil, length, A_OUT), jnp.float32))
    return pl.pallas_call(
        functools.partial(_dilated_kernel, nbs=tuple(nbs)),
        grid=(batch, steps),
        in_specs=in_specs,
        out_specs=out_specs,
        out_shape=out_shapes,
        compiler_params=_params(("parallel", "parallel")),
    )(*in_arrays)


def _compress_kernel(x_ref, pos_ref, w1_ref, w2_ref, o_ref):
    half = CMP_STRIDE * HEAD_DIM
    x = x_ref[...]
    n_chunk = x.shape[0]
    lo = (x + pos_ref[0]).astype(jnp.bfloat16)
    hi = (x + pos_ref[1]).astype(jnp.bfloat16)
    h_lo = jnp.dot(lo, w1_ref[0:half, :], preferred_element_type=jnp.float32)
    h_hi = jnp.dot(hi, w1_ref[half:2 * half, :], preferred_element_type=jnp.float32)
    h = h_lo + pltpu.roll(h_hi, n_chunk - 1, 0)
    act = jax.nn.gelu(h, approximate=True)
    o_ref[...] = jnp.dot(act.astype(jnp.bfloat16), w2_ref[...],
                         preferred_element_type=jnp.float32).astype(o_ref.dtype)


def _compress(ckv, pos, w1, w2):
    _, batch, _, n_chunk, half = ckv.shape
    return pl.pallas_call(
        _compress_kernel,
        grid=(2, batch, B_KV_HEADS),
        in_specs=[pl.BlockSpec((None, None, None, n_chunk, half), lambda s, b, g: (s, b, g, 0, 0)),
                  pl.BlockSpec((None, 2, 1, half), lambda s, b, g: (s, 0, 0, 0)),
                  pl.BlockSpec((None, 2 * half, CMP_HIDDEN), lambda s, b, g: (s, 0, 0)),
                  pl.BlockSpec((None, CMP_HIDDEN, HEAD_DIM), lambda s, b, g: (s, 0, 0))],
        out_specs=pl.BlockSpec((None, None, None, n_chunk, HEAD_DIM), lambda s, b, g: (s, b, g, 0, 0)),
        out_shape=jax.ShapeDtypeStruct((2, batch, B_KV_HEADS, n_chunk, HEAD_DIM), jnp.bfloat16),
        compiler_params=_params(("parallel", "parallel", "parallel")),
    )(ckv, pos, w1, w2)


NEG = -2.0 ** 60
N_FEAT = HEAD_DIM


def _nsa_kernel(q_ref, kc_ref, vc_ref, skv_ref, wkv_ref, gate_ref, ovt_ref, kconst_ref, slope_ref, o_ref,
                kaug_sc, vaug_sc, kwin_sc, vwin_sc, m_sc, acc_sc, qaug_sc, sa_sc, sb_sc, *, n_sel, tk):
    n = pl.program_id(2)
    t0 = n * Q_BLOCK
    rows = B_GQA * Q_BLOCK
    seq = skv_ref.shape[0]
    dn = (((1,), (1,)), ((), ()))
    f32, bf16 = jnp.float32, jnp.bfloat16
    E = HEAD_DIM

    @pl.when(n == 0)
    def _build_keys():
        kaug_sc[:, 0:LANES] = kconst_ref[:, 0:LANES]
        kaug_sc[:, LANES:LANES + E] = skv_ref[:, 0:E]
        kaug_sc[:, LANES + E:2 * LANES] = kconst_ref[:, LANES:LANES + N_FEAT]
        vaug_sc[:, 0:E] = skv_ref[:, E:2 * E]
        vaug_sc[:, E:LANES] = kconst_ref[:, LANES + N_FEAT:2 * LANES]
        kwin_sc[:, 0:E] = wkv_ref[:, 0:E]
        kwin_sc[:, E:LANES] = kconst_ref[:, LANES:LANES + N_FEAT]
        vwin_sc[:, 0:E] = wkv_ref[:, E:2 * E]
        vwin_sc[:, E:LANES] = kconst_ref[:, LANES + N_FEAT:2 * LANES]

    q = q_ref[...]
    qs = jnp.concatenate([q[:, r * E:(r + 1) * E] for r in range(B_GQA)], axis=0)
    row = lax.broadcasted_iota(jnp.int32, (rows, 1), 0)
    within = row & (Q_BLOCK - 1)
    tq = t0 + within
    slope = slope_ref[...]
    lane = lax.broadcasted_iota(jnp.int32, (rows, N_FEAT), 1)
    feat = jnp.where(lane == 0, slope * SLC_LEN,
           jnp.where(lane == 1, slope,
           jnp.where(lane == 2, -slope * t0.astype(f32),
           jnp.where(lane == 3, -slope * within.astype(f32), 0.0))))
    q_feat = jnp.concatenate([qs, feat.astype(bf16)], axis=1)

    n_chunk = kc_ref.shape[0]
    s = lax.dot_general(qs, kc_ref[...], dn, preferred_element_type=f32)
    cmp_end = lax.broadcasted_iota(jnp.int32, (1, n_chunk), 1) * CMP_STRIDE + (CMP_LEN - 1)
    s = jnp.where(cmp_end <= tq, s, -jnp.inf)
    m = jnp.max(s, axis=1, keepdims=True)
    m = jnp.where(m == -jnp.inf, 0.0, m)
    p = jnp.exp(s - m)
    l = jnp.maximum(jnp.sum(p, axis=1, keepdims=True), TINY)
    p_c = (p / l).astype(bf16)
    o_c = jnp.dot(p_c, vc_ref[...], preferred_element_type=f32)

    span = min(WIN + Q_BLOCK, seq)
    start = pl.multiple_of(jnp.maximum(t0 + Q_BLOCK - span, 0), Q_BLOCK)
    s = lax.dot_general(q_feat, kwin_sc[pl.ds(start, span), :], dn, preferred_element_type=f32)
    dw = tq - (start + lax.broadcasted_iota(jnp.int32, (1, span), 1))
    s = jnp.where((dw >= 0) & (dw < WIN), s, -jnp.inf)
    m = jnp.max(s, axis=1, keepdims=True)
    p = jnp.exp(s - m).astype(bf16)
    acc = jnp.dot(p, vwin_sc[pl.ds(start, span), :], preferred_element_type=f32)
    o_w = acc[:, 0:E] / jnp.maximum(acc[:, E:E + 1], TINY)

    gates = 1.0 / (1.0 + jnp.exp(-gate_ref[...]))
    gate_of = lambda r, branch: jnp.broadcast_to(gates[:, 3 * r + branch:3 * r + branch + 1], (Q_BLOCK, E))
    gate_s = [gate_of(r, 1) for r in range(B_GQA)]
    gated_cw = [gate_of(r, 0) * o_c[r * Q_BLOCK:(r + 1) * Q_BLOCK] + gate_of(r, 2) * o_w[r * Q_BLOCK:(r + 1) * Q_BLOCK]
                for r in range(B_GQA)]

    imp = lax.dot_general(ovt_ref[...], p_c[0:Q_BLOCK], dn, preferred_element_type=f32)
    for r in range(1, B_GQA):
        imp += lax.dot_general(ovt_ref[...], p_c[r * Q_BLOCK:(r + 1) * Q_BLOCK], dn, preferred_element_type=f32)
    blk = lax.broadcasted_iota(jnp.int32, (LANES, Q_BLOCK), 0)
    t = t0 + lax.broadcasted_iota(jnp.int32, (LANES, Q_BLOCK), 1)
    cur = t >> 6
    forced = (blk == 0) | (blk == cur) | (blk == cur - 1)
    score = jnp.where(blk * SLC_LEN <= t, imp + jnp.where(forced, FORCE_BONUS, 0.0), -jnp.inf)

    def pick(_, carry):
        work, sel = carry
        best = jnp.max(work, axis=0, keepdims=True)
        idx = jnp.min(jnp.where(work == best, blk, LANES), axis=0, keepdims=True)
        hit = blk == idx
        finite = jnp.where(best > -jnp.inf, 1.0, 0.0)
        sel = jnp.where(hit, jnp.maximum(sel, finite), sel)
        return jnp.where(hit, -jnp.inf, work), sel

    _, sel = lax.fori_loop(0, n_sel, pick, (score, jnp.zeros((LANES, Q_BLOCK), f32)), unroll=True)
    bias = jnp.where(sel.T > 0.5, 0.0, NEG).astype(bf16)
    qaug_sc[...] = jnp.concatenate([jnp.concatenate([bias] * B_GQA, axis=0), q_feat], axis=1)

    m_sc[...] = jnp.full(m_sc.shape, -jnp.inf, f32)
    acc_sc[...] = jnp.zeros(acc_sc.shape, f32)

    def logits(j, s_ref):
        k0 = pl.multiple_of(j * tk, tk)
        s_ref[...] = lax.dot_general(qaug_sc[...], kaug_sc[pl.ds(k0, tk), :], dn, preferred_element_type=f32)

    def update(j, s_ref, diagonal):
        k0 = pl.multiple_of(j * tk, tk)
        s = s_ref[...]
        if diagonal:
            s = jnp.where(k0 + lax.broadcasted_iota(jnp.int32, (1, tk), 1) <= tq, s, -jnp.inf)
        m_old = m_sc[...]
        m_new = jnp.maximum(m_old, jnp.broadcast_to(jnp.max(s, axis=1, keepdims=True), m_old.shape))
        p = jnp.exp(s - jnp.concatenate([m_new] * (tk // LANES), axis=1)).astype(bf16)
        pv = jnp.dot(p, vaug_sc[pl.ds(k0, tk), :], preferred_element_type=f32)
        acc_sc[...] = jnp.exp(m_old - m_new) * acc_sc[...] + pv
        m_sc[...] = m_new

    j_last = t0 // tk
    logits(0, sa_sc)

    def tile_pairs(first, pairs):
        for k in range(pairs):
            j = first + 2 * k
            logits(j + 1, sb_sc)
            update(j, sa_sc, False)
            logits(j + 2, sa_sc)
            update(j + 1, sb_sc, False)

    n_quads = j_last // 4

    def four_tiles(i, carry):
        tile_pairs(4 * i, 2)
        return carry

    lax.fori_loop(0, n_quads, four_tiles, 0)

    @pl.when(j_last - 4 * n_quads >= 2)
    def _pair_left():
        tile_pairs(4 * n_quads, 1)

    odd = (j_last % 2) == 1

    @pl.when(odd)
    def _two_left():
        logits(j_last, sb_sc)
        update(j_last - 1, sa_sc, False)
        update(j_last, sb_sc, True)

    @pl.when(jnp.logical_not(odd))
    def _one_left():
        update(j_last, sa_sc, True)

    acc = acc_sc[...]
    o_s = acc[:, 0:E] / jnp.maximum(acc[:, E:E + 1], TINY)

    for r in range(B_GQA):
        out = gated_cw[r] + gate_s[r] * o_s[r * Q_BLOCK:(r + 1) * Q_BLOCK]
        o_ref[:, r * E:(r + 1) * E] = out.astype(o_ref.dtype)


def _nsa(u_nat, u_f32, cmp_kv, batch, seq):
    nq = seq // Q_BLOCK
    n_chunk = seq // CMP_STRIDE
    n_cmp = n_chunk - CMP_LEN // CMP_STRIDE + 1
    n_slc = seq // SLC_LEN
    n_sel = min(SLC_TOPK, n_slc)
    tk = min(512, seq)
    assert n_slc <= LANES and B_HEADS == 8 and seq % tk == 0
    c_start = np.arange(n_chunk) * CMP_STRIDE
    s_start = np.arange(LANES) * SLC_LEN
    overlap_t = ((c_start[None, :] < s_start[:, None] + SLC_LEN) & (c_start[None, :] + CMP_LEN > s_start[:, None])
                 & (np.arange(n_chunk)[None, :] < n_cmp) & (np.arange(LANES)[:, None] < n_slc))
    pos = np.arange(seq)
    kconst = np.zeros((seq, 2 * LANES), np.float32)
    kconst[pos, pos // SLC_LEN] = 1.0
    kconst[:, LANES + 0] = pos // SLC_LEN
    kconst[:, LANES + 1] = pos % SLC_LEN
    kconst[:, LANES + 2] = 1.0
    kconst[:, LANES + 3] = 1.0
    kconst[:, LANES + N_FEAT] = 1.0
    rows = B_GQA * Q_BLOCK
    slopes = jnp.exp2(-8.0 * jnp.arange(1, B_HEADS + 1, dtype=jnp.float32) / B_HEADS)
    slopes = jnp.repeat(slopes.reshape(B_KV_HEADS, B_GQA), Q_BLOCK, axis=1).reshape(B_KV_HEADS, rows, 1)
    kv_rows = pl.BlockSpec((seq, LANES), lambda b, g, n: (b, NAT_SLC + g))
    win_rows = pl.BlockSpec((seq, LANES), lambda b, g, n: (b, NAT_WIN + g))
    cmp_spec = lambda s: pl.BlockSpec((None, None, None, n_chunk, HEAD_DIM), lambda b, g, n: (s, b, g, 0, 0))
    return pl.pallas_call(
        functools.partial(_nsa_kernel, n_sel=n_sel, tk=tk),
        grid=(batch, B_KV_HEADS, nq),
        in_specs=[pl.BlockSpec((Q_BLOCK, B_GQA * HEAD_DIM), lambda b, g, n: (b * nq + n, NAT_Q_B + g)),
                  cmp_spec(0), cmp_spec(1), kv_rows, win_rows,
                  pl.BlockSpec((Q_BLOCK, LANES), lambda b, g, n: (b * nq + n, F_GATE + g)),
                  pl.BlockSpec((LANES, n_chunk), lambda b, g, n: (0, 0)),
                  pl.BlockSpec((seq, 2 * LANES), lambda b, g, n: (0, 0)),
                  pl.BlockSpec((None, rows, 1), lambda b, g, n: (g, 0, 0))],
        out_specs=pl.BlockSpec((Q_BLOCK, B_GQA * HEAD_DIM), lambda b, g, n: (b * nq + n, g)),
        out_shape=jax.ShapeDtypeStruct((batch * seq, B_WIDTH), jnp.bfloat16),
        scratch_shapes=[pltpu.VMEM((seq, 2 * LANES), jnp.bfloat16), pltpu.VMEM((seq, LANES), jnp.bfloat16),
                        pltpu.VMEM((seq, LANES), jnp.bfloat16), pltpu.VMEM((seq, LANES), jnp.bfloat16),
                        pltpu.VMEM((rows, LANES), jnp.float32), pltpu.VMEM((rows, LANES), jnp.float32),
                        pltpu.VMEM((rows, 2 * LANES), jnp.bfloat16),
                        pltpu.VMEM((rows, tk), jnp.float32), pltpu.VMEM((rows, tk), jnp.float32)],
        compiler_params=_params(("arbitrary", "arbitrary", "arbitrary")),
    )(u_nat, cmp_kv, cmp_kv, u_nat, u_nat, u_f32, jnp.asarray(overlap_t, jnp.bfloat16), jnp.asarray(kconst, jnp.bfloat16),
      slopes)


def _layer_norm(v, gain, bias):
    mu = jnp.mean(v, axis=1, keepdims=True)
    c = v - mu
    var = jnp.mean(c * c, axis=1, keepdims=True)
    return c * lax.rsqrt(var + LN_EPS) * gain + bias


def _sigmoid(v):
    return 1.0 / (1.0 + jnp.exp(-v))


MERGE_SPLIT = 2


def _token_major(ref, stage):
    dil, n = ref.shape[0], ref.shape[1]
    parts = stage.shape[0]
    for r in range(dil):
        for j in range(parts):
            stage[j, pl.ds(r, n, stride=dil), :] = ref[r, :, j * LANES:(j + 1) * LANES]
    return jnp.concatenate([stage[j] for j in range(parts)], axis=1)


def _merge_kernel(*refs):
    dil_refs, refs = refs[:2 * N_DIL], refs[2 * N_DIL:]
    (yb_ref, mga_ref, mgb_ref, x_ref, wa_ref, wb_ref, wo_ref, g_ref, b_ref, wr_hi_ref, wr_lo_ref, br_ref,
     h_ref, ht_ref, rw_ref, re_ref, cnt_ref) = refs[:17]
    stages, cnt_sc = refs[17:17 + 2 * (N_DIL - 1)], refs[17 + 2 * (N_DIL - 1)]
    f32 = jnp.float32
    outs, lse = [dil_refs[0][...]], [dil_refs[1][...]]
    for g in range(1, N_DIL):
        outs.append(_token_major(dil_refs[2 * g], stages[2 * g - 2]))
        lse.append(_token_major(dil_refs[2 * g + 1], stages[2 * g - 1]))
    def mixed_rows(rs):
        lse_r = [v[rs] for v in lse]
        top = jnp.maximum(jnp.maximum(lse_r[0], lse_r[1]), lse_r[2])
        e = [jnp.exp(v - top) for v in lse_r]
        y_a = (e[0] * outs[0][rs] + e[1] * outs[1][rs] + e[2] * outs[2][rs]) / (e[0] + e[1] + e[2])
        br_a = jnp.dot(y_a.astype(jnp.bfloat16), wa_ref[...], preferred_element_type=f32)
        br_b = jnp.dot(yb_ref[rs, :], wb_ref[...], preferred_element_type=f32)
        merged = _sigmoid(mga_ref[rs, :]) * br_a + _sigmoid(mgb_ref[rs, :]) * br_b
        mix = jnp.dot(merged.astype(jnp.bfloat16), wo_ref[...], preferred_element_type=f32)
        return _layer_norm(ALPHA * x_ref[rs, :] + mix, g_ref[...], b_ref[...])

    tm = x_ref.shape[0]
    part = tm // MERGE_SPLIT
    h = jnp.concatenate([mixed_rows(slice(k * part, (k + 1) * part)) for k in range(MERGE_SPLIT)], axis=0)
    h_ref[...] = h
    _to_token_tiles(ht_ref, h)

    h_hi = h.astype(jnp.bfloat16)
    h_lo = (h - h_hi.astype(f32)).astype(jnp.bfloat16)
    logits = (jnp.dot(h_hi, wr_hi_ref[...], preferred_element_type=f32)
              + jnp.dot(h_lo, wr_hi_ref[...], preferred_element_type=f32)
              + jnp.dot(h_hi, wr_lo_ref[...], preferred_element_type=f32)) + br_ref[...]
    lane = lax.broadcasted_iota(jnp.int32, logits.shape, 1)
    coarse = jnp.where(lane < N_GROUPS, logits, -jnp.inf)
    c_max = jnp.max(coarse, axis=1, keepdims=True)
    grp = jnp.min(jnp.where(coarse == c_max, lane, LANES), axis=1, keepdims=True)
    p_grp = 1.0 / jnp.sum(jnp.exp(coarse - c_max), axis=1, keepdims=True)
    lo_lane = N_GROUPS + EXP_PER_GROUP * grp
    fine = jnp.where((lane >= lo_lane) & (lane < lo_lane + EXP_PER_GROUP), logits, -jnp.inf)
    v1 = jnp.max(fine, axis=1, keepdims=True)
    i1 = jnp.min(jnp.where(fine == v1, lane, LANES), axis=1, keepdims=True)
    fine = jnp.where(lane == i1, -jnp.inf, fine)
    v2 = jnp.max(fine, axis=1, keepdims=True)
    i2 = jnp.min(jnp.where(fine == v2, lane, LANES), axis=1, keepdims=True)
    e2 = jnp.exp(v2 - v1)
    w1 = p_grp / (1.0 + e2)
    w2 = p_grp * e2 / (1.0 + e2)
    e_1, e_2 = i1 - N_GROUPS, i2 - N_GROUPS
    rw_ref[...] = jnp.where(lane == 0, w1, jnp.where(lane == 1, w2, 0.0))

    @pl.when(pl.program_id(0) == 0)
    def _zero_counts():
        cnt_sc[...] = jnp.zeros(cnt_sc.shape, f32)

    tm = logits.shape[0]
    onehot = jnp.where((lane == e_1) | (lane == e_2), 1.0, 0.0)
    earlier = (lax.broadcasted_iota(jnp.int32, (tm, tm), 0) > lax.broadcasted_iota(jnp.int32, (tm, tm), 1))
    before = jnp.dot(jnp.where(earlier, 1.0, 0.0).astype(jnp.bfloat16), onehot.astype(jnp.bfloat16),
                     preferred_element_type=f32) + cnt_sc[...]
    r_1 = jnp.sum(jnp.where(lane == e_1, before, 0.0), axis=1, keepdims=True).astype(jnp.int32)
    r_2 = jnp.sum(jnp.where(lane == e_2, before, 0.0), axis=1, keepdims=True).astype(jnp.int32)
    cnt_sc[...] = cnt_sc[...] + jnp.sum(onehot, axis=0, keepdims=True)
    cnt_ref[...] = cnt_sc[...].astype(jnp.int32)
    re_ref[...] = jnp.where(lane == 0, e_1, jnp.where(lane == 1, e_2,
                            jnp.where(lane == 2, r_1, jnp.where(lane == 3, r_2, 0))))


def _merge(dil_outs, y_b, u_f32, x2d, seq, wa, wb, wo, ln_g, ln_b, wr_hi, wr_lo, b_r, tm=256):
    t = x2d.shape[0]
    per_batch = seq // tm
    assert seq % tm == 0 and all(tm % (dil * 8) == 0 for _, dil in DIL_GROUPS)
    row = lambda w: pl.BlockSpec((tm, w), lambda i: (i, 0))
    full = lambda a: pl.BlockSpec(a.shape, lambda i: (0,) * a.ndim)
    dil_specs = [pl.BlockSpec((None, None, tm, A_OUT), lambda i: (i // per_batch, 0, i % per_batch, 0))] * 2
    stages = []
    for _, dil in DIL_GROUPS[1:]:
        dil_specs += [pl.BlockSpec((None, dil, tm // dil, A_OUT),
                                   lambda i: (i // per_batch, 0, i % per_batch, 0))] * 2
        stages += [pltpu.VMEM((A_OUT // LANES, tm, LANES), jnp.float32)] * 2
    return pl.pallas_call(
        _merge_kernel,
        grid=(t // tm,),
        in_specs=dil_specs + [row(B_WIDTH),
                  pl.BlockSpec((tm, D_MODEL), lambda i: (i, F_MGATE)),
                  pl.BlockSpec((tm, D_MODEL), lambda i: (i, F_MGATE + 1)),
                  row(D_MODEL), full(wa), full(wb), full(wo), full(ln_g), full(ln_b),
                  full(wr_hi), full(wr_lo), full(b_r)],
        out_specs=[row(D_MODEL), pl.BlockSpec((tm * CHUNKS, LANES), lambda i: (i, 0)), row(LANES), row(LANES),
                   pl.BlockSpec((1, LANES), lambda i: (0, 0))],
        out_shape=[jax.ShapeDtypeStruct((t, D_MODEL), jnp.float32),
                   jax.ShapeDtypeStruct((t * CHUNKS, LANES), jnp.float32),
                   jax.ShapeDtypeStruct((t, LANES), jnp.float32),
                   jax.ShapeDtypeStruct((t, LANES), jnp.int32),
                   jax.ShapeDtypeStruct((1, LANES), jnp.int32)],
        scratch_shapes=stages + [pltpu.VMEM((1, LANES), jnp.float32)],
        compiler_params=_params(("arbitrary",)),
    )(*dil_outs, y_b, u_f32, u_f32, x2d, wa, wb, wo, ln_g, ln_b, wr_hi, wr_lo, b_r)


CHUNKS = D_MODEL // LANES
GATHER_UNROLL = 8


def _to_token_tiles(ref2d, value):
    n = value.shape[0]
    for c in range(CHUNKS):
        ref2d[pl.ds(c, n, stride=CHUNKS), :] = value[:, c * LANES:(c + 1) * LANES]


def _from_token_tiles(ref2d):
    n = ref2d.shape[0] // CHUNKS
    return jnp.concatenate([ref2d[pl.ds(c, n, stride=CHUNKS), :] for c in range(CHUNKS)], axis=1)


def _token_copy(idx_ref, k, src2d, dst2d, r, sem):
    src_row = pl.multiple_of(idx_ref[0, 0, k] * CHUNKS, CHUNKS)
    dst_row = r * CHUNKS if isinstance(r, int) else pl.multiple_of(r * CHUNKS, CHUNKS)
    return pltpu.make_async_copy(src2d.at[pl.ds(src_row, CHUNKS), :], dst2d.at[pl.ds(dst_row, CHUNKS), :], sem)


def _start_token_gather(idx_ref, idx_of, src2d, dst2d, sem, unrolled, priority_of=lambda r: 0):
    n = dst2d.shape[0] // CHUNKS
    if unrolled:
        for r in range(n):
            _token_copy(idx_ref, idx_of(r), src2d, dst2d, r, sem).start(priority=priority_of(r))
    else:
        def start(r, carry):
            _token_copy(idx_ref, idx_of(r), src2d, dst2d, r, sem).start()
            return carry

        lax.fori_loop(0, n, start, 0, unroll=GATHER_UNROLL)


def _wait_token_gather(src2d, dst2d, sem):
    pltpu.make_async_copy(src2d.at[pl.ds(0, dst2d.shape[0]), :], dst2d, sem).wait()


GATHER_BUFFERS = 3


def _gather_ring(i, first_refs, ahead_ref, start, wait, compute):
    last = pl.num_programs(0) - 1

    @pl.when(i == 0)
    def _prologue():
        for k, ref in enumerate(first_refs):
            start(ref, k, unrolled=False)

    def step(k):
        ahead = (k + GATHER_BUFFERS - 1) % GATHER_BUFFERS
        wait(k)
        start(ahead_ref, ahead, unrolled=True)
        compute(k)

        @pl.when(i == last)
        def _drain():
            for other in range(GATHER_BUFFERS):
                if other != k:
                    wait(other)

    for k in range(GATHER_BUFFERS):
        pl.when(i % GATHER_BUFFERS == k)(functools.partial(step, k))


def _expert_kernel(blk_e_ref, tok_ref, tok_1_ref, tok_2_ref, ht_hbm, wgu_ref, wd_ref, y_ref, *scratch):
    bufs, sems = scratch[:GATHER_BUFFERS], scratch[GATHER_BUFFERS]

    def start(idx_ref, k, unrolled):
        _start_token_gather(idx_ref, lambda r: r, ht_hbm, bufs[k], sems.at[k], unrolled, priority_of=lambda r: r % 2)

    def wait(k):
        _wait_token_gather(ht_hbm, bufs[k], sems.at[k])

    def compute(k):
        xb = _from_token_tiles(bufs[k]).astype(jnp.bfloat16)
        gu = jnp.dot(xb, wgu_ref[...], preferred_element_type=jnp.float32)
        gate, up = gu[:, :D_EXPERT], gu[:, D_EXPERT:]
        act = (gate * _sigmoid(gate)) * up
        _to_token_tiles(y_ref, jnp.dot(act.astype(jnp.bfloat16), wd_ref[...], preferred_element_type=jnp.float32))

    _gather_ring(pl.program_id(0), (tok_ref, tok_1_ref), tok_2_ref, start, wait, compute)


def _experts(h_tiles, tok_buf, blk_e, wgu, wd, bm):
    p = tok_buf.shape[0]
    nb = p // bm
    tok = tok_buf.reshape(nb, 1, bm)
    idx_spec = lambda ahead: pl.BlockSpec((1, 1, bm), lambda i, e: (jnp.minimum(i + ahead, nb - 1), 0, 0),
                                          memory_space=pltpu.SMEM)
    return pl.pallas_call(
        _expert_kernel,
        grid_spec=pltpu.PrefetchScalarGridSpec(
            num_scalar_prefetch=1,
            grid=(nb,),
            in_specs=[idx_spec(0), idx_spec(1), idx_spec(2),
                      pl.BlockSpec(memory_space=pl.ANY),
                      pl.BlockSpec((None, D_MODEL, 2 * D_EXPERT), lambda i, e: (e[i], 0, 0)),
                      pl.BlockSpec((None, D_EXPERT, D_MODEL), lambda i, e: (e[i], 0, 0))],
            out_specs=pl.BlockSpec((bm * CHUNKS, LANES), lambda i, e: (i, 0)),
            scratch_shapes=[pltpu.VMEM((bm * CHUNKS, LANES), jnp.float32)] * GATHER_BUFFERS
                           + [pltpu.SemaphoreType.DMA((GATHER_BUFFERS,))],
        ),
        out_shape=jax.ShapeDtypeStruct((p * CHUNKS, LANES), jnp.float32),
        compiler_params=_params(("arbitrary",)),
    )(blk_e, tok, tok, tok, h_tiles, wgu, wd)


def _combine_kernel(dest_ref, dest_1_ref, dest_2_ref, yt_hbm, h_ref, rw_ref, g_ref, b_ref, o_ref, *scratch):
    bufs, sems = scratch[:GATHER_BUFFERS], scratch[GATHER_BUFFERS]

    def start(idx_ref, k, unrolled):
        for slot in range(MOE_TOP_K):
            _start_token_gather(idx_ref, lambda r, slot=slot: MOE_TOP_K * r + slot, yt_hbm, bufs[k].at[slot],
                                sems.at[k], unrolled, priority_of=lambda r: r % 2)

    def wait(k):
        for slot in range(MOE_TOP_K):
            _wait_token_gather(yt_hbm, bufs[k].at[slot], sems.at[k])

    def compute(k):
        rw = rw_ref[...]
        ffn = rw[:, 0:1] * _from_token_tiles(bufs[k].at[0]) + rw[:, 1:2] * _from_token_tiles(bufs[k].at[1])
        o_ref[...] = _layer_norm(ALPHA * h_ref[...] + ffn, g_ref[...], b_ref[...])

    _gather_ring(pl.program_id(0), (dest_ref, dest_1_ref), dest_2_ref, start, wait, compute)


def _combine(dest, y_tiles, h, rw, ln_g, ln_b, tm=256):
    t = h.shape[0]
    assert t % tm == 0
    nt = t // tm
    idx = dest.reshape(nt, 1, MOE_TOP_K * tm)
    idx_spec = lambda ahead: pl.BlockSpec((1, 1, MOE_TOP_K * tm), lambda i: (jnp.minimum(i + ahead, nt - 1), 0, 0),
                                          memory_space=pltpu.SMEM)
    return pl.pallas_call(
        _combine_kernel,
        grid=(nt,),
        in_specs=[idx_spec(0), idx_spec(1), idx_spec(2),
                  pl.BlockSpec(memory_space=pl.ANY),
                  pl.BlockSpec((tm, D_MODEL), lambda i: (i, 0)),
                  pl.BlockSpec((tm, LANES), lambda i: (i, 0)),
                  pl.BlockSpec((1, D_MODEL), lambda i: (0, 0)),
                  pl.BlockSpec((1, D_MODEL), lambda i: (0, 0))],
        out_specs=pl.BlockSpec((tm, D_MODEL), lambda i: (i, 0)),
        out_shape=jax.ShapeDtypeStruct((t, D_MODEL), jnp.float32),
        scratch_shapes=[pltpu.VMEM((MOE_TOP_K, tm * CHUNKS, LANES), jnp.float32)] * GATHER_BUFFERS
                       + [pltpu.SemaphoreType.DMA((GATHER_BUFFERS,))],
        compiler_params=_params(("arbitrary",)),
    )(idx, idx, idx, y_tiles, h, rw, ln_g, ln_b)


def _dispatch(route, counts, bm):
    t = route.shape[0]
    n_assign = t * MOE_TOP_K
    expert, rank = route[:, 0:MOE_TOP_K], route[:, MOE_TOP_K:2 * MOE_TOP_K]
    starts = jnp.cumsum(counts) - counts
    pcounts = (counts + bm - 1) // bm * bm
    pends = jnp.cumsum(pcounts)
    pstarts = pends - pcounts
    ids = jnp.arange(N_EXPERTS, dtype=jnp.int32)
    dest = jnp.sum(jnp.where(expert[..., None] == ids, pstarts, 0), axis=-1) + rank
    p = -(-(n_assign + N_EXPERTS * (bm - 1)) // bm) * bm
    nb = p // bm
    blk_e = jnp.minimum(jnp.sum(pends[None, :] <= (jnp.arange(nb, dtype=jnp.int32) * bm)[:, None], axis=1),
                        N_EXPERTS - 1).astype(jnp.int32)
    order = jnp.argsort(expert.reshape(n_assign), stable=True).astype(jnp.int32)
    row_e = jnp.repeat(blk_e, bm)
    k = jnp.arange(p, dtype=jnp.int32) - pstarts[row_e]
    src = jnp.clip(starts[row_e] + k, 0, n_assign - 1)
    tok_buf = jnp.where(k < counts[row_e], order[src] // MOE_TOP_K, 0).astype(jnp.int32)
    return tok_buf, dest.astype(jnp.int32), blk_e


def _split_bf16(w):
    hi = w.astype(jnp.bfloat16)
    return hi, (w - hi.astype(jnp.float32)).astype(jnp.bfloat16)


def kernel(x, w_in, cmp_pos_k, cmp_w1_k, cmp_w2_k, cmp_pos_v, cmp_w1_v, cmp_w2_v, w_branch_a, w_branch_b, w_out, ln1_g, ln1_b, w_coarse, b_coarse, w_fine, b_fine, w_gate_up, w_down, ln2_g, ln2_b):
    batch, seq, d = x.shape
    bf16 = jnp.bfloat16
    h = x.reshape(batch * seq, d)
    cols_bf, cols_f32, scale_bf = _in_proj_columns()
    half = CMP_STRIDE * HEAD_DIM
    for l in range(DEPTH):
        u_nat, *groups = _project_bf(h, (_permute_columns(w_in[l], cols_bf) * scale_bf).astype(bf16), batch, seq)
        u_f32, ckv = _project_f32(h, _permute_columns(w_in[l], cols_f32).astype(bf16), batch, seq)
        dil_outs = _dilated(u_nat, groups, batch, seq)
        pos = jnp.stack([cmp_pos_k[l], cmp_pos_v[l]]).reshape(2, 2, 1, half)
        w1 = jnp.stack([cmp_w1_k[l], cmp_w1_v[l]]).astype(bf16)
        w2 = jnp.stack([cmp_w2_k[l], cmp_w2_v[l]]).astype(bf16)
        cmp_kv = _compress(ckv, pos, w1, w2)
        y_b = _nsa(u_nat, u_f32, cmp_kv, batch, seq)
        w_r = jnp.concatenate([w_coarse[l], w_fine[l].transpose(1, 0, 2).reshape(d, N_EXPERTS)], axis=1)
        w_r = jnp.pad(w_r, ((0, 0), (0, LANES - w_r.shape[1])))
        b_r = jnp.pad(jnp.concatenate([b_coarse[l], b_fine[l].reshape(N_EXPERTS)]),
                      (0, LANES - N_GROUPS - N_EXPERTS)).reshape(1, LANES)
        wr_hi, wr_lo = _split_bf16(w_r)
        h1, h1_tiles, rw, route, counts = _merge(dil_outs, y_b, u_f32, h, seq, w_branch_a[l].astype(bf16),
                                       w_branch_b[l].astype(bf16), w_out[l].astype(bf16),
                                       ln1_g[l].reshape(1, d), ln1_b[l].reshape(1, d), wr_hi, wr_lo, b_r)
        bm = 512
        tok_buf, dest, blk_e = _dispatch(route, counts[0, :N_EXPERTS], bm)
        y_tiles = _experts(h1_tiles, tok_buf, blk_e, w_gate_up[l].astype(bf16), w_down[l].astype(bf16), bm)
        h = _combine(dest, y_tiles, h1, rw, ln2_g[l].reshape(1, d), ln2_b[l].reshape(1, d))
    return h.reshape(batch, seq, d)
```

```python
import functools

import numpy as np
import jax
import jax.numpy as jnp
from jax import lax
from jax.experimental import pallas as pl
from jax.experimental.pallas import tpu as pltpu

D_MODEL = 1024
HEAD_DIM = 64
Q_BLOCK = 128
DIL_GROUPS = ((128, 1), (512, 4), (2048, 16))
N_DIL = len(DIL_GROUPS)
A_HEADS_PER_GROUP = 4
A_HEADS = A_HEADS_PER_GROUP * N_DIL
A_WIDTH = A_HEADS * HEAD_DIM
A_OUT = A_HEADS_PER_GROUP * HEAD_DIM
B_HEADS = 8
B_KV_HEADS = 2
B_GQA = B_HEADS // B_KV_HEADS
B_WIDTH = B_HEADS * HEAD_DIM
B_KV_WIDTH = B_KV_HEADS * HEAD_DIM
CMP_LEN = 32
CMP_STRIDE = 16
CMP_HIDDEN = 256
SLC_LEN = 64
SLC_TOPK = 16
WIN = 512
N_GROUPS = 4
EXP_PER_GROUP = 8
N_EXPERTS = N_GROUPS * EXP_PER_GROUP
D_EXPERT = 512
MOE_TOP_K = 2
DEPTH = 1
ALPHA = (2.0 * DEPTH) ** 0.25
LN_EPS = 1e-5
FORCE_BONUS = 1e4
TINY = 1e-30
ATTN_SCALE = HEAD_DIM ** -0.5

LANES = 128
VMEM_LIMIT = 48 * 1024 * 1024

QKV_COLS = 3 * A_OUT
NAT_Q_B = QKV_COLS // A_OUT
NAT_SLC = (QKV_COLS + B_WIDTH) // LANES
NAT_WIN = NAT_SLC + B_KV_HEADS
NAT_COLS = QKV_COLS + B_WIDTH + 4 * B_KV_WIDTH
BF_COLS = NAT_COLS + (N_DIL - 1) * QKV_COLS
F_MGATE = 0
F_GATE = 2 * D_MODEL // LANES
F_NAT_COLS = 2 * D_MODEL + B_KV_HEADS * LANES
F_COLS = F_NAT_COLS + 2 * B_KV_WIDTH


def _params(semantics):
    return pltpu.CompilerParams(dimension_semantics=semantics, vmem_limit_bytes=VMEM_LIMIT)


def _in_proj_columns():
    kv_off = 3 * A_WIDTH + B_WIDTH
    gate_off = kv_off + 6 * B_KV_WIDTH
    mg_off = gate_off + 3 * B_HEADS
    def qkv_cols(g):
        return [part * A_WIDTH + g * A_OUT + c for part in range(3) for c in range(A_OUT)]

    bf = qkv_cols(0) + list(range(3 * A_WIDTH, 3 * A_WIDTH + B_WIDTH))
    for first in (2, 4):
        for g in range(B_KV_HEADS):
            for i in (first, first + 1):
                base = kv_off + i * B_KV_WIDTH + g * HEAD_DIM
                bf += list(range(base, base + HEAD_DIM))
    for g in range(1, N_DIL):
        bf += qkv_cols(g)
    f32 = list(range(mg_off, mg_off + 2 * D_MODEL))
    per_head = 3 * B_GQA
    for g in range(B_KV_HEADS):
        f32 += list(range(gate_off + g * per_head, gate_off + (g + 1) * per_head))
        f32 += [-1] * (LANES - per_head)
    f32 += list(range(kv_off, kv_off + 2 * B_KV_WIDTH))
    assert len(bf) == BF_COLS and len(f32) == F_COLS
    bf = np.asarray(bf)
    is_q = (bf < A_WIDTH) | ((bf >= 3 * A_WIDTH) & (bf < 3 * A_WIDTH + B_WIDTH))
    scale = np.where(is_q, ATTN_SCALE, 1.0).astype(np.float32)
    return bf, np.asarray(f32), scale


def _permute_columns(w, cols):
    pieces, start = [], 0
    for end in range(1, len(cols) + 1):
        run_ends = end == len(cols) or (cols[end] != cols[end - 1] + 1 if cols[end - 1] >= 0 else cols[end] >= 0)
        if run_ends:
            first = int(cols[start])
            pieces.append(jnp.zeros((w.shape[0], end - start), w.dtype) if first < 0
                          else w[:, first:first + end - start])
            start = end
    return jnp.concatenate(pieces, axis=1)


def _proj_f32_kernel(x_ref, w_ref, nat_ref, cmp_ref, stage):
    xb = x_ref[...].astype(jnp.bfloat16)
    tm = x_ref.shape[0]
    n_chunk = tm // CMP_STRIDE
    for c0 in range(0, F_NAT_COLS, A_OUT):
        nat_ref[:, c0:c0 + A_OUT] = jnp.dot(xb, w_ref[:, c0:c0 + A_OUT], preferred_element_type=jnp.float32)
    res = jnp.dot(xb, w_ref[:, F_NAT_COLS:F_COLS], preferred_element_type=jnp.float32)
    lower = lax.broadcasted_iota(jnp.int32, (n_chunk, LANES), 1) < HEAD_DIM
    for kv in range(2):
        stage[...] = res[:, kv * LANES:(kv + 1) * LANES]
        for p in range(0, CMP_STRIDE, 2):
            even = stage[pl.ds(p, n_chunk, stride=CMP_STRIDE), :]
            odd = stage[pl.ds(p + 1, n_chunk, stride=CMP_STRIDE), :]
            cols = slice(p * HEAD_DIM, (p + 2) * HEAD_DIM)
            cmp_ref[kv, 0, :, cols] = jnp.where(lower, even, pltpu.roll(odd, HEAD_DIM, 1))
            cmp_ref[kv, 1, :, cols] = jnp.where(lower, pltpu.roll(even, HEAD_DIM, 1), odd)


def _project_f32(x2d, w, batch, seq, tm=512):
    t, d = x2d.shape
    assert seq % tm == 0 and w.shape[1] == F_COLS and B_KV_HEADS * HEAD_DIM == LANES
    per_batch = seq // tm
    half = CMP_STRIDE * HEAD_DIM
    return pl.pallas_call(
        _proj_f32_kernel,
        grid=(t // tm,),
        in_specs=[pl.BlockSpec((tm, d), lambda i: (i, 0)),
                  pl.BlockSpec((d, F_COLS), lambda i: (0, 0))],
        out_specs=[pl.BlockSpec((tm, F_NAT_COLS), lambda i: (i, 0)),
                   pl.BlockSpec((2, None, B_KV_HEADS, tm // CMP_STRIDE, half),
                                lambda i: (0, i // per_batch, 0, i % per_batch, 0))],
        out_shape=[jax.ShapeDtypeStruct((t, F_NAT_COLS), jnp.float32),
                   jax.ShapeDtypeStruct((2, batch, B_KV_HEADS, seq // CMP_STRIDE, half), jnp.float32)],
        scratch_shapes=[pltpu.VMEM((tm, LANES), jnp.float32)],
        compiler_params=_params(("parallel",)),
    )(x2d, w)


def _proj_bf_kernel(x_ref, w_ref, nat_ref, *rest, dils):
    group_refs, stage = rest[:-1], rest[-1]
    xb = x_ref[...].astype(jnp.bfloat16)
    tm = x_ref.shape[0]
    for c0 in range(0, NAT_COLS, A_OUT):
        nat_ref[:, c0:c0 + A_OUT] = jnp.dot(
            xb, w_ref[:, c0:c0 + A_OUT], preferred_element_type=jnp.float32).astype(nat_ref.dtype)
    col = NAT_COLS
    for o_ref, dil in zip(group_refs, dils):
        for part in range(3):
            res = jnp.dot(xb, w_ref[:, col:col + A_OUT], preferred_element_type=jnp.float32)
            for j in range(A_OUT // LANES):
                stage[j] = res[:, j * LANES:(j + 1) * LANES]
                c0 = part * A_OUT + j * LANES
                for r in range(dil):
                    o_ref[r, :, c0:c0 + LANES] = stage[j, pl.ds(r, tm // dil, stride=dil), :].astype(o_ref.dtype)
            col += A_OUT


def _project_bf(x2d, w, batch, seq, tm=512):
    t, d = x2d.shape
    dils = tuple(dil for _, dil in DIL_GROUPS[1:])
    assert seq % tm == 0 and all(tm % (dil * 16) == 0 for dil in dils) and w.shape[1] == BF_COLS
    per_batch = seq // tm
    out_specs = [pl.BlockSpec((tm, NAT_COLS), lambda i: (i, 0))]
    out_shapes = [jax.ShapeDtypeStruct((t, NAT_COLS), jnp.bfloat16)]
    for dil in dils:
        out_specs.append(pl.BlockSpec((None, dil, tm // dil, QKV_COLS),
                                      lambda i: (i // per_batch, 0, i % per_batch, 0)))
        out_shapes.append(jax.ShapeDtypeStruct((batch, dil, seq // dil, QKV_COLS), jnp.bfloat16))
    return pl.pallas_call(
        functools.partial(_proj_bf_kernel, dils=dils),
        grid=(t // tm,),
        in_specs=[pl.BlockSpec((tm, d), lambda i: (i, 0)),
                  pl.BlockSpec((d, BF_COLS), lambda i: (0, 0))],
        out_specs=out_specs,
        out_shape=out_shapes,
        scratch_shapes=[pltpu.VMEM((A_OUT // LANES, tm, LANES), jnp.float32)],
        compiler_params=_params(("parallel",)),
    )(x2d, w)


def _dilated_kernel(*refs, nbs):
    bias_ref, ins, outs = refs[0], refs[1:1 + 5 * N_DIL], refs[1 + 5 * N_DIL:]
    i = pl.program_id(1)
    dn = (((1,), (1,)), ((), ()))
    f32, bf16 = jnp.float32, jnp.bfloat16
    lower = lax.broadcasted_iota(jnp.int32, (Q_BLOCK, LANES), 1) < HEAD_DIM
    heads = [(g, pair, half) for g in range(N_DIL) for pair in range(A_OUT // LANES) for half in range(2)]

    def col(ref, pair):
        return ref[:, pair * LANES:(pair + 1) * LANES]

    logits = []
    for g, pair, half in heads:
        q_ref, kp_ref, kc_ref = ins[5 * g:5 * g + 3]
        head = g * A_HEADS_PER_GROUP + 2 * pair + half
        q = col(q_ref, pair)
        q = jnp.where(lower if half == 0 else jnp.logical_not(lower), q, jnp.zeros_like(q))
        no_prev = jnp.where((i % nbs[g]) > 0, 0.0, -jnp.inf)
        s_p = lax.dot_general(q, col(kp_ref, pair), dn, preferred_element_type=f32)
        s_c = lax.dot_general(q, col(kc_ref, pair), dn, preferred_element_type=f32)
        logits.append((s_p + bias_ref[head, :, 0:Q_BLOCK] + no_prev, s_c + bias_ref[head, :, Q_BLOCK:2 * Q_BLOCK]))

    probs = []
    for s_p, s_c in logits:
        m = jnp.max(jnp.maximum(s_p, s_c), axis=1, keepdims=True)
        p_p, p_c = jnp.exp(s_p - m), jnp.exp(s_c - m)
        l = jnp.sum(p_p + p_c, axis=1, keepdims=True)
        probs.append((p_p.astype(bf16), p_c.astype(bf16), l, m + jnp.log(l)))

    for n_head in range(0, len(heads), 2):
        g, pair, _ = heads[n_head]
        vp_ref, vc_ref = ins[5 * g + 3:5 * g + 5]
        o_ref, lse_ref = outs[2 * g:2 * g + 2]
        out, lse = [], []
        for p_p, p_c, l, lse_h in probs[n_head:n_head + 2]:
            acc = jnp.dot(p_p, col(vp_ref, pair), preferred_element_type=f32)
            acc += jnp.dot(p_c, col(vc_ref, pair), preferred_element_type=f32)
            out.append(acc / l)
            lse.append(jnp.broadcast_to(lse_h, (Q_BLOCK, LANES)))
        o_ref[:, pair * LANES:(pair + 1) * LANES] = jnp.where(lower, out[0], out[1])
        lse_ref[:, pair * LANES:(pair + 1) * LANES] = jnp.where(lower, lse[0], lse[1])


def _dilated_bias():
    qi = Q_BLOCK + jnp.arange(Q_BLOCK)
    kj = jnp.arange(2 * Q_BLOCK)
    delta = qi[:, None] - kj[None, :]
    slopes = jnp.exp2(-8.0 * jnp.arange(1, A_HEADS + 1, dtype=jnp.float32) / A_HEADS)
    tables = []
    for g, (window, dil) in enumerate(DIL_GROUPS):
        on_band = (delta >= 0) & (delta <= window // dil)
        dist = (delta * dil).astype(jnp.float32)
        for h in range(A_HEADS_PER_GROUP):
            tables.append(jnp.where(on_band, -slopes[g * A_HEADS_PER_GROUP + h] * dist, -jnp.inf))
    return jnp.stack(tables)


def _dilated(u_nat, groups, batch, seq):
    steps = seq // Q_BLOCK
    arrays = [u_nat.reshape(batch, 1, seq, NAT_COLS)] + list(groups)
    bias = _dilated_bias()
    in_arrays, in_specs = [bias], [pl.BlockSpec(bias.shape, lambda b, i: (0, 0, 0))]
    out_specs, out_shapes, nbs = [], [], []
    blk = (None, None, Q_BLOCK, A_OUT)
    for g, (window, dil) in enumerate(DIL_GROUPS):
        length = seq // dil
        assert length % Q_BLOCK == 0 and window // dil == Q_BLOCK
        nb = length // Q_BLOCK
        nbs.append(nb)

        def cur(b, i, nb=nb, col=0):
            return (b, i // nb, i % nb, col)

        def prev(b, i, nb=nb, col=0):
            return (b, i // nb, jnp.maximum(i % nb - 1, 0), col)

        for fn, col in ((cur, 0), (prev, 1), (cur, 1), (prev, 2), (cur, 2)):
            in_arrays.append(arrays[g])
            in_specs.append(pl.BlockSpec(blk, functools.partial(fn, col=col)))
        for _ in range(2):
            out_specs.append(pl.BlockSpec(blk, functools.partial(cur, col=0)))
            out_shapes.append(jax.ShapeDtypeStruct((batch, dil, length, A_OUT), jnp.float32))
    return pl.pallas_call(
        functools.partial(_dilated_kernel, nbs=tuple(nbs)),
        grid=(batch, steps),
        in_specs=in_specs,
        out_specs=out_specs,
        out_shape=out_shapes,
        compiler_params=_params(("parallel", "parallel")),
    )(*in_arrays)


def _compress_kernel(x_ref, pos_ref, w1_ref, w2_ref, o_ref):
    half = CMP_STRIDE * HEAD_DIM
    x = x_ref[...]
    n_chunk = x.shape[0]
    lo = (x + pos_ref[0]).astype(jnp.bfloat16)
    hi = (x + pos_ref[1]).astype(jnp.bfloat16)
    h_lo = jnp.dot(lo, w1_ref[0:half, :], preferred_element_type=jnp.float32)
    h_hi = jnp.dot(hi, w1_ref[half:2 * half, :], preferred_element_type=jnp.float32)
    h = h_lo + pltpu.roll(h_hi, n_chunk - 1, 0)
    act = jax.nn.gelu(h, approximate=True)
    o_ref[...] = jnp.dot(act.astype(jnp.bfloat16), w2_ref[...],
                         preferred_element_type=jnp.float32).astype(o_ref.dtype)


def _compress(ckv, pos, w1, w2):
    _, batch, _, n_chunk, half = ckv.shape
    return pl.pallas_call(
        _compress_kernel,
        grid=(2, batch, B_KV_HEADS),
        in_specs=[pl.BlockSpec((None, None, None, n_chunk, half), lambda s, b, g: (s, b, g, 0, 0)),
                  pl.BlockSpec((None, 2, 1, half), lambda s, b, g: (s, 0, 0, 0)),
                  pl.BlockSpec((None, 2 * half, CMP_HIDDEN), lambda s, b, g: (s, 0, 0)),
                  pl.BlockSpec((None, CMP_HIDDEN, HEAD_DIM), lambda s, b, g: (s, 0, 0))],
        out_specs=pl.BlockSpec((None, None, None, n_chunk, HEAD_DIM), lambda s, b, g: (s, b, g, 0, 0)),
        out_shape=jax.ShapeDtypeStruct((2, batch, B_KV_HEADS, n_chunk, HEAD_DIM), jnp.bfloat16),
        compiler_params=_params(("parallel", "parallel", "parallel")),
    )(ckv, pos, w1, w2)


NEG = -2.0 ** 60
N_FEAT = HEAD_DIM


def _nsa_kernel(q_ref, kc_ref, vc_ref, skv_ref, wkv_ref, gate_ref, ovt_ref, kconst_ref, slope_ref, o_ref,
                kaug_sc, vaug_sc, kwin_sc, vwin_sc, m_sc, acc_sc, qaug_sc, sa_sc, sb_sc, *, n_sel, tk):
    n = pl.program_id(2)
    t0 = n * Q_BLOCK
    rows = B_GQA * Q_BLOCK
    seq = skv_ref.shape[0]
    dn = (((1,), (1,)), ((), ()))
    f32, bf16 = jnp.float32, jnp.bfloat16
    E = HEAD_DIM

    @pl.when(n == 0)
    def _build_keys():
        kaug_sc[:, 0:LANES] = kconst_ref[:, 0:LANES]
        kaug_sc[:, LANES:LANES + E] = skv_ref[:, 0:E]
        kaug_sc[:, LANES + E:2 * LANES] = kconst_ref[:, LANES:LANES + N_FEAT]
        vaug_sc[:, 0:E] = skv_ref[:, E:2 * E]
        vaug_sc[:, E:LANES] = kconst_ref[:, LANES + N_FEAT:2 * LANES]
        kwin_sc[:, 0:E] = wkv_ref[:, 0:E]
        kwin_sc[:, E:LANES] = kconst_ref[:, LANES:LANES + N_FEAT]
        vwin_sc[:, 0:E] = wkv_ref[:, E:2 * E]
        vwin_sc[:, E:LANES] = kconst_ref[:, LANES + N_FEAT:2 * LANES]

    q = q_ref[...]
    qs = jnp.concatenate([q[:, r * E:(r + 1) * E] for r in range(B_GQA)], axis=0)
    row = lax.broadcasted_iota(jnp.int32, (rows, 1), 0)
    within = row & (Q_BLOCK - 1)
    tq = t0 + within
    slope = slope_ref[...]
    lane = lax.broadcasted_iota(jnp.int32, (rows, N_FEAT), 1)
    feat = jnp.where(lane == 0, slope * SLC_LEN,
           jnp.where(lane == 1, slope,
           jnp.where(lane == 2, -slope * t0.astype(f32),
           jnp.where(lane == 3, -slope * within.astype(f32), 0.0))))
    q_feat = jnp.concatenate([qs, feat.astype(bf16)], axis=1)

    n_chunk = kc_ref.shape[0]
    s = lax.dot_general(qs, kc_ref[...], dn, preferred_element_type=f32)
    cmp_end = lax.broadcasted_iota(jnp.int32, (1, n_chunk), 1) * CMP_STRIDE + (CMP_LEN - 1)
    s = jnp.where(cmp_end <= tq, s, -jnp.inf)
    m = jnp.max(s, axis=1, keepdims=True)
    m = jnp.where(m == -jnp.inf, 0.0, m)
    p = jnp.exp(s - m)
    l = jnp.maximum(jnp.sum(p, axis=1, keepdims=True), TINY)
    p_c = (p / l).astype(bf16)
    o_c = jnp.dot(p_c, vc_ref[...], preferred_element_type=f32)

    span = min(WIN + Q_BLOCK, seq)
    start = pl.multiple_of(jnp.maximum(t0 + Q_BLOCK - span, 0), Q_BLOCK)
    s = lax.dot_general(q_feat, kwin_sc[pl.ds(start, span), :], dn, preferred_element_type=f32)
    dw = tq - (start + lax.broadcasted_iota(jnp.int32, (1, span), 1))
    s = jnp.where((dw >= 0) & (dw < WIN), s, -jnp.inf)
    m = jnp.max(s, axis=1, keepdims=True)
    p = jnp.exp(s - m).astype(bf16)
    acc = jnp.dot(p, vwin_sc[pl.ds(start, span), :], preferred_element_type=f32)
    o_w = acc[:, 0:E] / jnp.maximum(acc[:, E:E + 1], TINY)

    gates = 1.0 / (1.0 + jnp.exp(-gate_ref[...]))
    gate_of = lambda r, branch: jnp.broadcast_to(gates[:, 3 * r + branch:3 * r + branch + 1], (Q_BLOCK, E))
    gate_s = [gate_of(r, 1) for r in range(B_GQA)]
    gated_cw = [gate_of(r, 0) * o_c[r * Q_BLOCK:(r + 1) * Q_BLOCK] + gate_of(r, 2) * o_w[r * Q_BLOCK:(r + 1) * Q_BLOCK]
                for r in range(B_GQA)]

    imp = lax.dot_general(ovt_ref[...], p_c[0:Q_BLOCK], dn, preferred_element_type=f32)
    for r in range(1, B_GQA):
        imp += lax.dot_general(ovt_ref[...], p_c[r * Q_BLOCK:(r + 1) * Q_BLOCK], dn, preferred_element_type=f32)
    blk = lax.broadcasted_iota(jnp.int32, (LANES, Q_BLOCK), 0)
    t = t0 + lax.broadcasted_iota(jnp.int32, (LANES, Q_BLOCK), 1)
    cur = t >> 6
    forced = (blk == 0) | (blk == cur) | (blk == cur - 1)
    score = jnp.where(blk * SLC_LEN <= t, imp + jnp.where(forced, FORCE_BONUS, 0.0), -jnp.inf)

    def pick(_, carry):
        work, sel = carry
        best = jnp.max(work, axis=0, keepdims=True)
        idx = jnp.min(jnp.where(work == best, blk, LANES), axis=0, keepdims=True)
        hit = blk == idx
        finite = jnp.where(best > -jnp.inf, 1.0, 0.0)
        sel = jnp.where(hit, jnp.maximum(sel, finite), sel)
        return jnp.where(hit, -jnp.inf, work), sel

    _, sel = lax.fori_loop(0, n_sel, pick, (score, jnp.zeros((LANES, Q_BLOCK), f32)), unroll=True)
    bias = jnp.where(sel.T > 0.5, 0.0, NEG).astype(bf16)
    qaug_sc[...] = jnp.concatenate([jnp.concatenate([bias] * B_GQA, axis=0), q_feat], axis=1)

    m_sc[...] = jnp.full(m_sc.shape, -jnp.inf, f32)
    acc_sc[...] = jnp.zeros(acc_sc.shape, f32)

    def logits(j, s_ref):
        k0 = pl.multiple_of(j * tk, tk)
        s_ref[...] = lax.dot_general(qaug_sc[...], kaug_sc[pl.ds(k0, tk), :], dn, preferred_element_type=f32)

    def update(j, s_ref, diagonal):
        k0 = pl.multiple_of(j * tk, tk)
        s = s_ref[...]
        if diagonal:
            s = jnp.where(k0 + lax.broadcasted_iota(jnp.int32, (1, tk), 1) <= tq, s, -jnp.inf)
        m_old = m_sc[...]
        m_new = jnp.maximum(m_old, jnp.broadcast_to(jnp.max(s, axis=1, keepdims=True), m_old.shape))
        p = jnp.exp(s - jnp.concatenate([m_new] * (tk // LANES), axis=1)).astype(bf16)
        pv = jnp.dot(p, vaug_sc[pl.ds(k0, tk), :], preferred_element_type=f32)
        acc_sc[...] = jnp.exp(m_old - m_new) * acc_sc[...] + pv
        m_sc[...] = m_new

    j_last = t0 // tk
    logits(0, sa_sc)

    def tile_pairs(first, pairs):
        for k in range(pairs):
            j = first + 2 * k
            logits(j + 1, sb_sc)
            update(j, sa_sc, False)
            logits(j + 2, sa_sc)
            update(j + 1, sb_sc, False)

    n_quads = j_last // 4

    def four_tiles(i, carry):
        tile_pairs(4 * i, 2)
        return carry

    lax.fori_loop(0, n_quads, four_tiles, 0)

    @pl.when(j_last - 4 * n_quads >= 2)
    def _pair_left():
        tile_pairs(4 * n_quads, 1)

    odd = (j_last % 2) == 1

    @pl.when(odd)
    def _two_left():
        logits(j_last, sb_sc)
        update(j_last - 1, sa_sc, False)
        update(j_last, sb_sc, True)

    @pl.when(jnp.logical_not(odd))
    def _one_left():
        update(j_last, sa_sc, True)

    acc = acc_sc[...]
    o_s = acc[:, 0:E] / jnp.maximum(acc[:, E:E + 1], TINY)

    for r in range(B_GQA):
        out = gated_cw[r] + gate_s[r] * o_s[r * Q_BLOCK:(r + 1) * Q_BLOCK]
        o_ref[:, r * E:(r + 1) * E] = out.astype(o_ref.dtype)


def _nsa(u_nat, u_f32, cmp_kv, batch, seq):
    nq = seq // Q_BLOCK
    n_chunk = seq // CMP_STRIDE
    n_cmp = n_chunk - CMP_LEN // CMP_STRIDE + 1
    n_slc = seq // SLC_LEN
    n_sel = min(SLC_TOPK, n_slc)
    tk = min(512, seq)
    assert n_slc <= LANES and B_HEADS == 8 and seq % tk == 0
    c_start = np.arange(n_chunk) * CMP_STRIDE
    s_start = np.arange(LANES) * SLC_LEN
    overlap_t = ((c_start[None, :] < s_start[:, None] + SLC_LEN) & (c_start[None, :] + CMP_LEN > s_start[:, None])
                 & (np.arange(n_chunk)[None, :] < n_cmp) & (np.arange(LANES)[:, None] < n_slc))
    pos = np.arange(seq)
    kconst = np.zeros((seq, 2 * LANES), np.float32)
    kconst[pos, pos // SLC_LEN] = 1.0
    kconst[:, LANES + 0] = pos // SLC_LEN
    kconst[:, LANES + 1] = pos % SLC_LEN
    kconst[:, LANES + 2] = 1.0
    kconst[:, LANES + 3] = 1.0
    kconst[:, LANES + N_FEAT] = 1.0
    rows = B_GQA * Q_BLOCK
    slopes = jnp.exp2(-8.0 * jnp.arange(1, B_HEADS + 1, dtype=jnp.float32) / B_HEADS)
    slopes = jnp.repeat(slopes.reshape(B_KV_HEADS, B_GQA), Q_BLOCK, axis=1).reshape(B_KV_HEADS, rows, 1)
    kv_rows = pl.BlockSpec((seq, LANES), lambda b, g, n: (b, NAT_SLC + g))
    win_rows = pl.BlockSpec((seq, LANES), lambda b, g, n: (b, NAT_WIN + g))
    cmp_spec = lambda s: pl.BlockSpec((None, None, None, n_chunk, HEAD_DIM), lambda b, g, n: (s, b, g, 0, 0))
    return pl.pallas_call(
        functools.partial(_nsa_kernel, n_sel=n_sel, tk=tk),
        grid=(batch, B_KV_HEADS, nq),
        in_specs=[pl.BlockSpec((Q_BLOCK, B_GQA * HEAD_DIM), lambda b, g, n: (b * nq + n, NAT_Q_B + g)),
                  cmp_spec(0), cmp_spec(1), kv_rows, win_rows,
                  pl.BlockSpec((Q_BLOCK, LANES), lambda b, g, n: (b * nq + n, F_GATE + g)),
                  pl.BlockSpec((LANES, n_chunk), lambda b, g, n: (0, 0)),
                  pl.BlockSpec((seq, 2 * LANES), lambda b, g, n: (0, 0)),
                  pl.BlockSpec((None, rows, 1), lambda b, g, n: (g, 0, 0))],
        out_specs=pl.BlockSpec((Q_BLOCK, B_GQA * HEAD_DIM), lambda b, g, n: (b * nq + n, g)),
        out_shape=jax.ShapeDtypeStruct((batch * seq, B_WIDTH), jnp.bfloat16),
        scratch_shapes=[pltpu.VMEM((seq, 2 * LANES), jnp.bfloat16), pltpu.VMEM((seq, LANES), jnp.bfloat16),
                        pltpu.VMEM((seq, LANES), jnp.bfloat16), pltpu.VMEM((seq, LANES), jnp.bfloat16),
                        pltpu.VMEM((rows, LANES), jnp.float32), pltpu.VMEM((rows, LANES), jnp.float32),
                        pltpu.VMEM((rows, 2 * LANES), jnp.bfloat16),
                        pltpu.VMEM((rows, tk), jnp.float32), pltpu.VMEM((rows, tk), jnp.float32)],
        compiler_params=_params(("arbitrary", "arbitrary", "arbitrary")),
    )(u_nat, cmp_kv, cmp_kv, u_nat, u_nat, u_f32, jnp.asarray(overlap_t, jnp.bfloat16), jnp.asarray(kconst, jnp.bfloat16),
      slopes)


def _layer_norm(v, gain, bias):
    mu = jnp.mean(v, axis=1, keepdims=True)
    c = v - mu
    var = jnp.mean(c * c, axis=1, keepdims=True)
    return c * lax.rsqrt(var + LN_EPS) * gain + bias


def _sigmoid(v):
    return 1.0 / (1.0 + jnp.exp(-v))


MERGE_SPLIT = 2


def _token_major(ref, stage):
    dil, n = ref.shape[0], ref.shape[1]
    parts = stage.shape[0]
    for r in range(dil):
        for j in range(parts):
            stage[j, pl.ds(r, n, stride=dil), :] = ref[r, :, j * LANES:(j + 1) * LANES]
    return jnp.concatenate([stage[j] for j in range(parts)], axis=1)


def _merge_kernel(*refs):
    dil_refs, refs = refs[:2 * N_DIL], refs[2 * N_DIL:]
    (yb_ref, mga_ref, mgb_ref, x_ref, wa_ref, wb_ref, wo_ref, g_ref, b_ref, wr_hi_ref, wr_lo_ref, br_ref,
     h_ref, ht_ref, rw_ref, re_ref, cnt_ref) = refs[:17]
    stages, cnt_sc = refs[17:17 + 2 * (N_DIL - 1)], refs[17 + 2 * (N_DIL - 1)]
    f32 = jnp.float32
    outs, lse = [dil_refs[0][...]], [dil_refs[1][...]]
    for g in range(1, N_DIL):
        outs.append(_token_major(dil_refs[2 * g], stages[2 * g - 2]))
        lse.append(_token_major(dil_refs[2 * g + 1], stages[2 * g - 1]))
    def mixed_rows(rs):
        lse_r = [v[rs] for v in lse]
        top = jnp.maximum(jnp.maximum(lse_r[0], lse_r[1]), lse_r[2])
        e = [jnp.exp(v - top) for v in lse_r]
        y_a = (e[0] * outs[0][rs] + e[1] * outs[1][rs] + e[2] * outs[2][rs]) / (e[0] + e[1] + e[2])
        br_a = jnp.dot(y_a.astype(jnp.bfloat16), wa_ref[...], preferred_element_type=f32)
        br_b = jnp.dot(yb_ref[rs, :], wb_ref[...], preferred_element_type=f32)
        merged = _sigmoid(mga_ref[rs, :]) * br_a + _sigmoid(mgb_ref[rs, :]) * br_b
        mix = jnp.dot(merged.astype(jnp.bfloat16), wo_ref[...], preferred_element_type=f32)
        return _layer_norm(ALPHA * x_ref[rs, :] + mix, g_ref[...], b_ref[...])

    tm = x_ref.shape[0]
    part = tm // MERGE_SPLIT
    h = jnp.concatenate([mixed_rows(slice(k * part, (k + 1) * part)) for k in range(MERGE_SPLIT)], axis=0)
    h_ref[...] = h
    _to_token_tiles(ht_ref, h)

    h_hi = h.astype(jnp.bfloat16)
    h_lo = (h - h_hi.astype(f32)).astype(jnp.bfloat16)
    logits = (jnp.dot(h_hi, wr_hi_ref[...], preferred_element_type=f32)
              + jnp.dot(h_lo, wr_hi_ref[...], preferred_element_type=f32)
              + jnp.dot(h_hi, wr_lo_ref[...], preferred_element_type=f32)) + br_ref[...]
    lane = lax.broadcasted_iota(jnp.int32, logits.shape, 1)
    coarse = jnp.where(lane < N_GROUPS, logits, -jnp.inf)
    c_max = jnp.max(coarse, axis=1, keepdims=True)
    grp = jnp.min(jnp.where(coarse == c_max, lane, LANES), axis=1, keepdims=True)
    p_grp = 1.0 / jnp.sum(jnp.exp(coarse - c_max), axis=1, keepdims=True)
    lo_lane = N_GROUPS + EXP_PER_GROUP * grp
    fine = jnp.where((lane >= lo_lane) & (lane < lo_lane + EXP_PER_GROUP), logits, -jnp.inf)
    v1 = jnp.max(fine, axis=1, keepdims=True)
    i1 = jnp.min(jnp.where(fine == v1, lane, LANES), axis=1, keepdims=True)
    fine = jnp.where(lane == i1, -jnp.inf, fine)
    v2 = jnp.max(fine, axis=1, keepdims=True)
    i2 = jnp.min(jnp.where(fine == v2, lane, LANES), axis=1, keepdims=True)
    e2 = jnp.exp(v2 - v1)
    w1 = p_grp / (1.0 + e2)
    w2 = p_grp * e2 / (1.0 + e2)
    e_1, e_2 = i1 - N_GROUPS, i2 - N_GROUPS
    rw_ref[...] = jnp.where(lane == 0, w1, jnp.where(lane == 1, w2, 0.0))

    @pl.when(pl.program_id(0) == 0)
    def _zero_counts():
        cnt_sc[...] = jnp.zeros(cnt_sc.shape, f32)

    tm = logits.shape[0]
    onehot = jnp.where((lane == e_1) | (lane == e_2), 1.0, 0.0)
    earlier = (lax.broadcasted_iota(jnp.int32, (tm, tm), 0) > lax.broadcasted_iota(jnp.int32, (tm, tm), 1))
    before = jnp.dot(jnp.where(earlier, 1.0, 0.0).astype(jnp.bfloat16), onehot.astype(jnp.bfloat16),
                     preferred_element_type=f32) + cnt_sc[...]
    r_1 = jnp.sum(jnp.where(lane == e_1, before, 0.0), axis=1, keepdims=True).astype(jnp.int32)
    r_2 = jnp.sum(jnp.where(lane == e_2, before, 0.0), axis=1, keepdims=True).astype(jnp.int32)
    cnt_sc[...] = cnt_sc[...] + jnp.sum(onehot, axis=0, keepdims=True)
    cnt_ref[...] = cnt_sc[...].astype(jnp.int32)
    re_ref[...] = jnp.where(lane == 0, e_1, jnp.where(lane == 1, e_2,
                            jnp.where(lane == 2, r_1, jnp.where(lane == 3, r_2, 0))))


def _merge(dil_outs, y_b, u_f32, x2d, seq, wa, wb, wo, ln_g, ln_b, wr_hi, wr_lo, b_r, tm=256):
    t = x2d.shape[0]
    per_batch = seq // tm
    assert seq % tm == 0 and all(tm % (dil * 8) == 0 for _, dil in DIL_GROUPS)
    row = lambda w: pl.BlockSpec((tm, w), lambda i: (i, 0))
    full = lambda a: pl.BlockSpec(a.shape, lambda i: (0,) * a.ndim)
    dil_specs = [pl.BlockSpec((None, None, tm, A_OUT), lambda i: (i // per_batch, 0, i % per_batch, 0))] * 2
    stages = []
    for _, dil in DIL_GROUPS[1:]:
        dil_specs += [pl.BlockSpec((None, dil, tm // dil, A_OUT),
                                   lambda i: (i // per_batch, 0, i % per_batch, 0))] * 2
        stages += [pltpu.VMEM((A_OUT // LANES, tm, LANES), jnp.float32)] * 2
    return pl.pallas_call(
        _merge_kernel,
        grid=(t // tm,),
        in_specs=dil_specs + [row(B_WIDTH),
                  pl.BlockSpec((tm, D_MODEL), lambda i: (i, F_MGATE)),
                  pl.BlockSpec((tm, D_MODEL), lambda i: (i, F_MGATE + 1)),
                  row(D_MODEL), full(wa), full(wb), full(wo), full(ln_g), full(ln_b),
                  full(wr_hi), full(wr_lo), full(b_r)],
        out_specs=[row(D_MODEL), pl.BlockSpec((tm * CHUNKS, LANES), lambda i: (i, 0)), row(LANES), row(LANES),
                   pl.BlockSpec((1, LANES), lambda i: (0, 0))],
        out_shape=[jax.ShapeDtypeStruct((t, D_MODEL), jnp.float32),
                   jax.ShapeDtypeStruct((t * CHUNKS, LANES), jnp.float32),
                   jax.ShapeDtypeStruct((t, LANES), jnp.float32),
                   jax.ShapeDtypeStruct((t, LANES), jnp.int32),
                   jax.ShapeDtypeStruct((1, LANES), jnp.int32)],
        scratch_shapes=stages + [pltpu.VMEM((1, LANES), jnp.float32)],
        compiler_params=_params(("arbitrary",)),
    )(*dil_outs, y_b, u_f32, u_f32, x2d, wa, wb, wo, ln_g, ln_b, wr_hi, wr_lo, b_r)


CHUNKS = D_MODEL // LANES
GATHER_UNROLL = 8


def _to_token_tiles(ref2d, value):
    n = value.shape[0]
    for c in range(CHUNKS):
        ref2d[pl.ds(c, n, stride=CHUNKS), :] = value[:, c * LANES:(c + 1) * LANES]


def _from_token_tiles(ref2d):
    n = ref2d.shape[0] // CHUNKS
    return jnp.concatenate([ref2d[pl.ds(c, n, stride=CHUNKS), :] for c in range(CHUNKS)], axis=1)


def _token_copy(idx_ref, k, src2d, dst2d, r, sem):
    src_row = pl.multiple_of(idx_ref[0, 0, k] * CHUNKS, CHUNKS)
    dst_row = r * CHUNKS if isinstance(r, int) else pl.multiple_of(r * CHUNKS, CHUNKS)
    return pltpu.make_async_copy(src2d.at[pl.ds(src_row, CHUNKS), :], dst2d.at[pl.ds(dst_row, CHUNKS), :], sem)


def _start_token_gather(idx_ref, idx_of, src2d, dst2d, sem, unrolled, priority_of=lambda r: 0):
    n = dst2d.shape[0] // CHUNKS
    if unrolled:
        for r in range(n):
            _token_copy(idx_ref, idx_of(r), src2d, dst2d, r, sem).start(priority=priority_of(r))
    else:
        def start(r, carry):
            _token_copy(idx_ref, idx_of(r), src2d, dst2d, r, sem).start()
            return carry

        lax.fori_loop(0, n, start, 0, unroll=GATHER_UNROLL)


def _wait_token_gather(src2d, dst2d, sem):
    pltpu.make_async_copy(src2d.at[pl.ds(0, dst2d.shape[0]), :], dst2d, sem).wait()


GATHER_BUFFERS = 3


def _gather_ring(i, first_refs, ahead_ref, start, wait, compute):
    last = pl.num_programs(0) - 1

    @pl.when(i == 0)
    def _prologue():
        for k, ref in enumerate(first_refs):
            start(ref, k, unrolled=False)

    def step(k):
        ahead = (k + GATHER_BUFFERS - 1) % GATHER_BUFFERS
        wait(k)
        start(ahead_ref, ahead, unrolled=True)
        compute(k)

        @pl.when(i == last)
        def _drain():
            for other in range(GATHER_BUFFERS):
                if other != k:
                    wait(other)

    for k in range(GATHER_BUFFERS):
        pl.when(i % GATHER_BUFFERS == k)(functools.partial(step, k))


def _sort_tokens_kernel(tok_ref, tok_1_ref, tok_2_ref, ht_hbm, x_ref, *scratch):
    bufs, sems = scratch[:GATHER_BUFFERS], scratch[GATHER_BUFFERS]

    def start(idx_ref, k, unrolled):
        _start_token_gather(idx_ref, lambda r: r, ht_hbm, bufs[k], sems.at[k], unrolled, priority_of=lambda r: r % 2)

    def wait(k):
        _wait_token_gather(ht_hbm, bufs[k], sems.at[k])

    def compute(k):
        x_ref[...] = _from_token_tiles(bufs[k]).astype(x_ref.dtype)

    _gather_ring(pl.program_id(0), (tok_ref, tok_1_ref), tok_2_ref, start, wait, compute)


def _sort_tokens(h_tiles, tok_buf, rows):
    p = tok_buf.shape[0]
    assert p % rows == 0
    nb = p // rows
    tok = tok_buf.reshape(nb, 1, rows)
    idx_spec = lambda ahead: pl.BlockSpec((1, 1, rows), lambda i: (jnp.minimum(i + ahead, nb - 1), 0, 0),
                                          memory_space=pltpu.SMEM)
    return pl.pallas_call(
        _sort_tokens_kernel,
        grid=(nb,),
        in_specs=[idx_spec(0), idx_spec(1), idx_spec(2), pl.BlockSpec(memory_space=pl.ANY)],
        out_specs=pl.BlockSpec((rows, D_MODEL), lambda i: (i, 0)),
        out_shape=jax.ShapeDtypeStruct((p, D_MODEL), jnp.bfloat16),
        scratch_shapes=[pltpu.VMEM((rows * CHUNKS, LANES), jnp.float32)] * GATHER_BUFFERS
                       + [pltpu.SemaphoreType.DMA((GATHER_BUFFERS,))],
        compiler_params=_params(("arbitrary",)),
    )(tok, tok, tok, h_tiles)


def _expert_kernel(blk_e_ref, x_ref, wgu_ref, wd_ref, y_ref):
    gu = jnp.dot(x_ref[...], wgu_ref[...], preferred_element_type=jnp.float32)
    gate, up = gu[:, :D_EXPERT], gu[:, D_EXPERT:]
    act = (gate * _sigmoid(gate)) * up
    _to_token_tiles(y_ref, jnp.dot(act.astype(jnp.bfloat16), wd_ref[...], preferred_element_type=jnp.float32))


def _experts(x_sorted, blk_e, wgu, wd, bm):
    p = x_sorted.shape[0]
    nb = p // bm
    return pl.pallas_call(
        _expert_kernel,
        grid_spec=pltpu.PrefetchScalarGridSpec(
            num_scalar_prefetch=1,
            grid=(nb,),
            in_specs=[pl.BlockSpec((bm, D_MODEL), lambda i, e: (i, 0)),
                      pl.BlockSpec((None, D_MODEL, 2 * D_EXPERT), lambda i, e: (e[i], 0, 0)),
                      pl.BlockSpec((None, D_EXPERT, D_MODEL), lambda i, e: (e[i], 0, 0))],
            out_specs=pl.BlockSpec((bm * CHUNKS, LANES), lambda i, e: (i, 0)),
        ),
        out_shape=jax.ShapeDtypeStruct((p * CHUNKS, LANES), jnp.float32),
        compiler_params=_params(("parallel",)),
    )(blk_e, x_sorted, wgu, wd)


def _combine_kernel(dest_ref, dest_1_ref, dest_2_ref, yt_hbm, h_ref, rw_ref, g_ref, b_ref, o_ref, *scratch):
    bufs, sems = scratch[:GATHER_BUFFERS], scratch[GATHER_BUFFERS]

    def start(idx_ref, k, unrolled):
        for slot in range(MOE_TOP_K):
            _start_token_gather(idx_ref, lambda r, slot=slot: MOE_TOP_K * r + slot, yt_hbm, bufs[k].at[slot],
                                sems.at[k], unrolled, priority_of=lambda r: r % 2)

    def wait(k):
        for slot in range(MOE_TOP_K):
            _wait_token_gather(yt_hbm, bufs[k].at[slot], sems.at[k])

    def compute(k):
        rw = rw_ref[...]
        ffn = rw[:, 0:1] * _from_token_tiles(bufs[k].at[0]) + rw[:, 1:2] * _from_token_tiles(bufs[k].at[1])
        o_ref[...] = _layer_norm(ALPHA * h_ref[...] + ffn, g_ref[...], b_ref[...])

    _gather_ring(pl.program_id(0), (dest_ref, dest_1_ref), dest_2_ref, start, wait, compute)


def _combine(dest, y_tiles, h, rw, ln_g, ln_b, tm=256):
    t = h.shape[0]
    assert t % tm == 0
    nt = t // tm
    idx = dest.reshape(nt, 1, MOE_TOP_K * tm)
    idx_spec = lambda ahead: pl.BlockSpec((1, 1, MOE_TOP_K * tm), lambda i: (jnp.minimum(i + ahead, nt - 1), 0, 0),
                                          memory_space=pltpu.SMEM)
    return pl.pallas_call(
        _combine_kernel,
        grid=(nt,),
        in_specs=[idx_spec(0), idx_spec(1), idx_spec(2),
                  pl.BlockSpec(memory_space=pl.ANY),
                  pl.BlockSpec((tm, D_MODEL), lambda i: (i, 0)),
                  pl.BlockSpec((tm, LANES), lambda i: (i, 0)),
                  pl.BlockSpec((1, D_MODEL), lambda i: (0, 0)),
                  pl.BlockSpec((1, D_MODEL), lambda i: (0, 0))],
        out_specs=pl.BlockSpec((tm, D_MODEL), lambda i: (i, 0)),
        out_shape=jax.ShapeDtypeStruct((t, D_MODEL), jnp.float32),
        scratch_shapes=[pltpu.VMEM((MOE_TOP_K, tm * CHUNKS, LANES), jnp.float32)] * GATHER_BUFFERS
                       + [pltpu.SemaphoreType.DMA((GATHER_BUFFERS,))],
        compiler_params=_params(("arbitrary",)),
    )(idx, idx, idx, y_tiles, h, rw, ln_g, ln_b)


def _dispatch(route, counts, bm):
    t = route.shape[0]
    n_assign = t * MOE_TOP_K
    expert, rank = route[:, 0:MOE_TOP_K], route[:, MOE_TOP_K:2 * MOE_TOP_K]
    starts = jnp.cumsum(counts) - counts
    pcounts = (counts + bm - 1) // bm * bm
    pends = jnp.cumsum(pcounts)
    pstarts = pends - pcounts
    ids = jnp.arange(N_EXPERTS, dtype=jnp.int32)
    dest = jnp.sum(jnp.where(expert[..., None] == ids, pstarts, 0), axis=-1) + rank
    p = -(-(n_assign + N_EXPERTS * (bm - 1)) // bm) * bm
    nb = p // bm
    blk_e = jnp.minimum(jnp.sum(pends[None, :] <= (jnp.arange(nb, dtype=jnp.int32) * bm)[:, None], axis=1),
                        N_EXPERTS - 1).astype(jnp.int32)
    order = jnp.argsort(expert.reshape(n_assign), stable=True).astype(jnp.int32)
    row_e = jnp.repeat(blk_e, bm)
    k = jnp.arange(p, dtype=jnp.int32) - pstarts[row_e]
    src = jnp.clip(starts[row_e] + k, 0, n_assign - 1)
    tok_buf = jnp.where(k < counts[row_e], order[src] // MOE_TOP_K, 0).astype(jnp.int32)
    return tok_buf, dest.astype(jnp.int32), blk_e


def _split_bf16(w):
    hi = w.astype(jnp.bfloat16)
    return hi, (w - hi.astype(jnp.float32)).astype(jnp.bfloat16)


def kernel(x, w_in, cmp_pos_k, cmp_w1_k, cmp_w2_k, cmp_pos_v, cmp_w1_v, cmp_w2_v, w_branch_a, w_branch_b, w_out, ln1_g, ln1_b, w_coarse, b_coarse, w_fine, b_fine, w_gate_up, w_down, ln2_g, ln2_b):
    batch, seq, d = x.shape
    bf16 = jnp.bfloat16
    h = x.reshape(batch * seq, d)
    cols_bf, cols_f32, scale_bf = _in_proj_columns()
    half = CMP_STRIDE * HEAD_DIM
    for l in range(DEPTH):
        u_nat, *groups = _project_bf(h, (_permute_columns(w_in[l], cols_bf) * scale_bf).astype(bf16), batch, seq)
        u_f32, ckv = _project_f32(h, _permute_columns(w_in[l], cols_f32).astype(bf16), batch, seq)
        dil_outs = _dilated(u_nat, groups, batch, seq)
        pos = jnp.stack([cmp_pos_k[l], cmp_pos_v[l]]).reshape(2, 2, 1, half)
        w1 = jnp.stack([cmp_w1_k[l], cmp_w1_v[l]]).astype(bf16)
        w2 = jnp.stack([cmp_w2_k[l], cmp_w2_v[l]]).astype(bf16)
        cmp_kv = _compress(ckv, pos, w1, w2)
        y_b = _nsa(u_nat, u_f32, cmp_kv, batch, seq)
        w_r = jnp.concatenate([w_coarse[l], w_fine[l].transpose(1, 0, 2).reshape(d, N_EXPERTS)], axis=1)
        w_r = jnp.pad(w_r, ((0, 0), (0, LANES - w_r.shape[1])))
        b_r = jnp.pad(jnp.concatenate([b_coarse[l], b_fine[l].reshape(N_EXPERTS)]),
                      (0, LANES - N_GROUPS - N_EXPERTS)).reshape(1, LANES)
        wr_hi, wr_lo = _split_bf16(w_r)
        h1, h1_tiles, rw, route, counts = _merge(dil_outs, y_b, u_f32, h, seq, w_branch_a[l].astype(bf16),
                                       w_branch_b[l].astype(bf16), w_out[l].astype(bf16),
                                       ln1_g[l].reshape(1, d), ln1_b[l].reshape(1, d), wr_hi, wr_lo, b_r)
        bm = 256
        tok_buf, dest, blk_e = _dispatch(route, counts[0, :N_EXPERTS], bm)
        x_sorted = _sort_tokens(h1_tiles, tok_buf, rows=2 * bm if tok_buf.shape[0] % (2 * bm) == 0 else bm)
        y_tiles = _experts(x_sorted, blk_e, w_gate_up[l].astype(bf16), w_down[l].astype(bf16), bm)
        h = _combine(dest, y_tiles, h1, rw, ln2_g[l].reshape(1, d), ln2_b[l].reshape(1, d))
    return h.reshape(batch, seq, d)
```

```python
import functools

import numpy as np
import jax
import jax.numpy as jnp
from jax import lax
from jax.experimental import pallas as pl
from jax.experimental.pallas import tpu as pltpu

D_MODEL = 1024
HEAD_DIM = 64
Q_BLOCK = 128
DIL_GROUPS = ((128, 1), (512, 4), (2048, 16))
N_DIL = len(DIL_GROUPS)
A_HEADS_PER_GROUP = 4
A_HEADS = A_HEADS_PER_GROUP * N_DIL
A_WIDTH = A_HEADS * HEAD_DIM
A_OUT = A_HEADS_PER_GROUP * HEAD_DIM
B_HEADS = 8
B_KV_HEADS = 2
B_GQA = B_HEADS // B_KV_HEADS
B_WIDTH = B_HEADS * HEAD_DIM
B_KV_WIDTH = B_KV_HEADS * HEAD_DIM
CMP_LEN = 32
CMP_STRIDE = 16
CMP_HIDDEN = 256
SLC_LEN = 64
SLC_TOPK = 16
WIN = 512
N_GROUPS = 4
EXP_PER_GROUP = 8
N_EXPERTS = N_GROUPS * EXP_PER_GROUP
D_EXPERT = 512
MOE_TOP_K = 2
DEPTH = 1
ALPHA = (2.0 * DEPTH) ** 0.25
LN_EPS = 1e-5
FORCE_BONUS = 1e4
TINY = 1e-30
ATTN_SCALE = HEAD_DIM ** -0.5

LANES = 128
VMEM_LIMIT = 48 * 1024 * 1024

QKV_COLS = 3 * A_OUT
NAT_Q_B = QKV_COLS // A_OUT
NAT_SLC = (QKV_COLS + B_WIDTH) // LANES
NAT_WIN = NAT_SLC + B_KV_HEADS
NAT_COLS = QKV_COLS + B_WIDTH + 4 * B_KV_WIDTH
BF_COLS = NAT_COLS + (N_DIL - 1) * QKV_COLS
F_MGATE = 0
F_GATE = 2 * D_MODEL // LANES
F_NAT_COLS = 2 * D_MODEL + B_KV_HEADS * LANES
F_COLS = F_NAT_COLS + 2 * B_KV_WIDTH


def _params(semantics):
    return pltpu.CompilerParams(dimension_semantics=semantics, vmem_limit_bytes=VMEM_LIMIT)


def _in_proj_columns():
    kv_off = 3 * A_WIDTH + B_WIDTH
    gate_off = kv_off + 6 * B_KV_WIDTH
    mg_off = gate_off + 3 * B_HEADS
    def qkv_cols(g):
        return [part * A_WIDTH + g * A_OUT + c for part in range(3) for c in range(A_OUT)]

    bf = qkv_cols(0) + list(range(3 * A_WIDTH, 3 * A_WIDTH + B_WIDTH))
    for first in (2, 4):
        for g in range(B_KV_HEADS):
            for i in (first, first + 1):
                base = kv_off + i * B_KV_WIDTH + g * HEAD_DIM
                bf += list(range(base, base + HEAD_DIM))
    for g in range(1, N_DIL):
        bf += qkv_cols(g)
    f32 = list(range(mg_off, mg_off + 2 * D_MODEL))
    per_head = 3 * B_GQA
    for g in range(B_KV_HEADS):
        f32 += list(range(gate_off + g * per_head, gate_off + (g + 1) * per_head))
        f32 += [-1] * (LANES - per_head)
    f32 += list(range(kv_off, kv_off + 2 * B_KV_WIDTH))
    assert len(bf) == BF_COLS and len(f32) == F_COLS
    bf = np.asarray(bf)
    is_q = (bf < A_WIDTH) | ((bf >= 3 * A_WIDTH) & (bf < 3 * A_WIDTH + B_WIDTH))
    scale = np.where(is_q, ATTN_SCALE, 1.0).astype(np.float32)
    return bf, np.asarray(f32), scale


def _permute_columns(w, cols):
    pieces, start = [], 0
    for end in range(1, len(cols) + 1):
        run_ends = end == len(cols) or (cols[end] != cols[end - 1] + 1 if cols[end - 1] >= 0 else cols[end] >= 0)
        if run_ends:
            first = int(cols[start])
            pieces.append(jnp.zeros((w.shape[0], end - start), w.dtype) if first < 0
                          else w[:, first:first + end - start])
            start = end
    return jnp.concatenate(pieces, axis=1)


def _proj_f32_kernel(x_ref, w_ref, nat_ref, cmp_ref, stage):
    xb = x_ref[...].astype(jnp.bfloat16)
    tm = x_ref.shape[0]
    n_chunk = tm // CMP_STRIDE
    for c0 in range(0, F_NAT_COLS, A_OUT):
        nat_ref[:, c0:c0 + A_OUT] = jnp.dot(xb, w_ref[:, c0:c0 + A_OUT], preferred_element_type=jnp.float32)
    res = jnp.dot(xb, w_ref[:, F_NAT_COLS:F_COLS], preferred_element_type=jnp.float32)
    lower = lax.broadcasted_iota(jnp.int32, (n_chunk, LANES), 1) < HEAD_DIM
    for kv in range(2):
        stage[...] = res[:, kv * LANES:(kv + 1) * LANES]
        for p in range(0, CMP_STRIDE, 2):
            even = stage[pl.ds(p, n_chunk, stride=CMP_STRIDE), :]
            odd = stage[pl.ds(p + 1, n_chunk, stride=CMP_STRIDE), :]
            cols = slice(p * HEAD_DIM, (p + 2) * HEAD_DIM)
            cmp_ref[kv, 0, :, cols] = jnp.where(lower, even, pltpu.roll(odd, HEAD_DIM, 1))
            cmp_ref[kv, 1, :, cols] = jnp.where(lower, pltpu.roll(even, HEAD_DIM, 1), odd)


def _project_f32(x2d, w, batch, seq, tm=512):
    t, d = x2d.shape
    assert seq % tm == 0 and w.shape[1] == F_COLS and B_KV_HEADS * HEAD_DIM == LANES
    per_batch = seq // tm
    half = CMP_STRIDE * HEAD_DIM
    return pl.pallas_call(
        _proj_f32_kernel,
        grid=(t // tm,),
        in_specs=[pl.BlockSpec((tm, d), lambda i: (i, 0)),
                  pl.BlockSpec((d, F_COLS), lambda i: (0, 0))],
        out_specs=[pl.BlockSpec((tm, F_NAT_COLS), lambda i: (i, 0)),
                   pl.BlockSpec((2, None, B_KV_HEADS, tm // CMP_STRIDE, half),
                                lambda i: (0, i // per_batch, 0, i % per_batch, 0))],
        out_shape=[jax.ShapeDtypeStruct((t, F_NAT_COLS), jnp.float32),
                   jax.ShapeDtypeStruct((2, batch, B_KV_HEADS, seq // CMP_STRIDE, half), jnp.float32)],
        scratch_shapes=[pltpu.VMEM((tm, LANES), jnp.float32)],
        compiler_params=_params(("parallel",)),
    )(x2d, w)


def _proj_bf_kernel(x_ref, w_ref, nat_ref, *rest, dils):
    group_refs, stage = rest[:-1], rest[-1]
    xb = x_ref[...].astype(jnp.bfloat16)
    tm = x_ref.shape[0]
    for c0 in range(0, NAT_COLS, A_OUT):
        nat_ref[:, c0:c0 + A_OUT] = jnp.dot(
            xb, w_ref[:, c0:c0 + A_OUT], preferred_element_type=jnp.float32).astype(nat_ref.dtype)
    col = NAT_COLS
    for o_ref, dil in zip(group_refs, dils):
        for part in range(3):
            res = jnp.dot(xb, w_ref[:, col:col + A_OUT], preferred_element_type=jnp.float32)
            for j in range(A_OUT // LANES):
                stage[j] = res[:, j * LANES:(j + 1) * LANES]
                c0 = part * A_OUT + j * LANES
                for r in range(dil):
                    o_ref[r, :, c0:c0 + LANES] = stage[j, pl.ds(r, tm // dil, stride=dil), :].astype(o_ref.dtype)
            col += A_OUT


def _project_bf(x2d, w, batch, seq, tm=512):
    t, d = x2d.shape
    dils = tuple(dil for _, dil in DIL_GROUPS[1:])
    assert seq % tm == 0 and all(tm % (dil * 16) == 0 for dil in dils) and w.shape[1] == BF_COLS
    per_batch = seq // tm
    out_specs = [pl.BlockSpec((tm, NAT_COLS), lambda i: (i, 0))]
    out_shapes = [jax.ShapeDtypeStruct((t, NAT_COLS), jnp.bfloat16)]
    for dil in dils:
        out_specs.append(pl.BlockSpec((None, dil, tm // dil, QKV_COLS),
                                      lambda i: (i // per_batch, 0, i % per_batch, 0)))
        out_shapes.append(jax.ShapeDtypeStruct((batch, dil, seq // dil, QKV_COLS), jnp.bfloat16))
    return pl.pallas_call(
        functools.partial(_proj_bf_kernel, dils=dils),
        grid=(t // tm,),
        in_specs=[pl.BlockSpec((tm, d), lambda i: (i, 0)),
                  pl.BlockSpec((d, BF_COLS), lambda i: (0, 0))],
        out_specs=out_specs,
        out_shape=out_shapes,
        scratch_shapes=[pltpu.VMEM((A_OUT // LANES, tm, LANES), jnp.float32)],
        compiler_params=_params(("parallel",)),
    )(x2d, w)


def _dilated_kernel(*refs, nbs):
    bias_ref, ins, outs = refs[0], refs[1:1 + 5 * N_DIL], refs[1 + 5 * N_DIL:]
    i = pl.program_id(1)
    dn = (((1,), (1,)), ((), ()))
    f32, bf16 = jnp.float32, jnp.bfloat16
    lower = lax.broadcasted_iota(jnp.int32, (Q_BLOCK, LANES), 1) < HEAD_DIM
    heads = [(g, pair, half) for g in range(N_DIL) for pair in range(A_OUT // LANES) for half in range(2)]

    def col(ref, pair):
        return ref[:, pair * LANES:(pair + 1) * LANES]

    logits = []
    for g, pair, half in heads:
        q_ref, kp_ref, kc_ref = ins[5 * g:5 * g + 3]
        head = g * A_HEADS_PER_GROUP + 2 * pair + half
        q = col(q_ref, pair)
        q = jnp.where(lower if half == 0 else jnp.logical_not(lower), q, jnp.zeros_like(q))
        no_prev = jnp.where((i % nbs[g]) > 0, 0.0, -jnp.inf)
        s_p = lax.dot_general(q, col(kp_ref, pair), dn, preferred_element_type=f32)
        s_c = lax.dot_general(q, col(kc_ref, pair), dn, preferred_element_type=f32)
        logits.append((s_p + bias_ref[head, :, 0:Q_BLOCK] + no_prev, s_c + bias_ref[head, :, Q_BLOCK:2 * Q_BLOCK]))

    probs = []
    for s_p, s_c in logits:
        m = jnp.max(jnp.maximum(s_p, s_c), axis=1, keepdims=True)
        p_p, p_c = jnp.exp(s_p - m), jnp.exp(s_c - m)
        l = jnp.sum(p_p + p_c, axis=1, keepdims=True)
        probs.append((p_p.astype(bf16), p_c.astype(bf16), l, m + jnp.log(l)))

    for n_head in range(0, len(heads), 2):
        g, pair, _ = heads[n_head]
        vp_ref, vc_ref = ins[5 * g + 3:5 * g + 5]
        o_ref, lse_ref = outs[2 * g:2 * g + 2]
        out, lse = [], []
        for p_p, p_c, l, lse_h in probs[n_head:n_head + 2]:
            acc = jnp.dot(p_p, col(vp_ref, pair), preferred_element_type=f32)
            acc += jnp.dot(p_c, col(vc_ref, pair), preferred_element_type=f32)
            out.append(acc / l)
            lse.append(jnp.broadcast_to(lse_h, (Q_BLOCK, LANES)))
        o_ref[:, pair * LANES:(pair + 1) * LANES] = jnp.where(lower, out[0], out[1])
        lse_ref[:, pair * LANES:(pair + 1) * LANES] = jnp.where(lower, lse[0], lse[1])


def _dilated_bias():
    qi = Q_BLOCK + jnp.arange(Q_BLOCK)
    kj = jnp.arange(2 * Q_BLOCK)
    delta = qi[:, None] - kj[None, :]
    slopes = jnp.exp2(-8.0 * jnp.arange(1, A_HEADS + 1, dtype=jnp.float32) / A_HEADS)
    tables = []
    for g, (window, dil) in enumerate(DIL_GROUPS):
        on_band = (delta >= 0) & (delta <= window // dil)
        dist = (delta * dil).astype(jnp.float32)
        for h in range(A_HEADS_PER_GROUP):
            tables.append(jnp.where(on_band, -slopes[g * A_HEADS_PER_GROUP + h] * dist, -jnp.inf))
    return jnp.stack(tables)


def _dilated(u_nat, groups, batch, seq):
    steps = seq // Q_BLOCK
    arrays = [u_nat.reshape(batch, 1, seq, NAT_COLS)] + list(groups)
    bias = _dilated_bias()
    in_arrays, in_specs = [bias], [pl.BlockSpec(bias.shape, lambda b, i: (0, 0, 0))]
    out_specs, out_shapes, nbs = [], [], []
    blk = (None, None, Q_BLOCK, A_OUT)
    for g, (window, dil) in enumerate(DIL_GROUPS):
        length = seq // dil
        assert length % Q_BLOCK == 0 and window // dil == Q_BLOCK
        nb = length // Q_BLOCK
        nbs.append(nb)

        def cur(b, i, nb=nb, col=0):
            return (b, i // nb, i % nb, col)

        def prev(b, i, nb=nb, col=0):
            return (b, i // nb, jnp.maximum(i % nb - 1, 0), col)

        for fn, col in ((cur, 0), (prev, 1), (cur, 1), (prev, 2), (cur, 2)):
            in_arrays.append(arrays[g])
            in_specs.append(pl.BlockSpec(blk, functools.partial(fn, col=col)))
        for _ in range(2):
            out_specs.append(pl.BlockSpec(blk, functools.partial(cur, col=0)))
            out_shapes.append(jax.ShapeDtypeStruct((batch, dil, length, A_OUT), jnp.float32))
    return pl.pallas_call(
        functools.partial(_dilated_kernel, nbs=tuple(nbs)),
        grid=(batch, steps),
        in_specs=in_specs,
        out_specs=out_specs,
        out_shape=out_shapes,
        compiler_params=_params(("parallel", "parallel")),
    )(*in_arrays)


def _compress_kernel(x_ref, pos_ref, w1_ref, w2_ref, o_ref):
    half = CMP_STRIDE * HEAD_DIM
    x = x_ref[...]
    n_chunk = x.shape[0]
    lo = (x + pos_ref[0]).astype(jnp.bfloat16)
    hi = (x + pos_ref[1]).astype(jnp.bfloat16)
    h_lo = jnp.dot(lo, w1_ref[0:half, :], preferred_element_type=jnp.float32)
    h_hi = jnp.dot(hi, w1_ref[half:2 * half, :], preferred_element_type=jnp.float32)
    h = h_lo + pltpu.roll(h_hi, n_chunk - 1, 0)
    act = jax.nn.gelu(h, approximate=True)
    o_ref[...] = jnp.dot(act.astype(jnp.bfloat16), w2_ref[...],
                         preferred_element_type=jnp.float32).astype(o_ref.dtype)


def _compress(ckv, pos, w1, w2):
    _, batch, _, n_chunk, half = ckv.shape
    return pl.pallas_call(
        _compress_kernel,
        grid=(2, batch, B_KV_HEADS),
        in_specs=[pl.BlockSpec((None, None, None, n_chunk, half), lambda s, b, g: (s, b, g, 0, 0)),
                  pl.BlockSpec((None, 2, 1, half), lambda s, b, g: (s, 0, 0, 0)),
                  pl.BlockSpec((None, 2 * half, CMP_HIDDEN), lambda s, b, g: (s, 0, 0)),
                  pl.BlockSpec((None, CMP_HIDDEN, HEAD_DIM), lambda s, b, g: (s, 0, 0))],
        out_specs=pl.BlockSpec((None, None, None, n_chunk, HEAD_DIM), lambda s, b, g: (s, b, g, 0, 0)),
        out_shape=jax.ShapeDtypeStruct((2, batch, B_KV_HEADS, n_chunk, HEAD_DIM), jnp.bfloat16),
        compiler_params=_params(("parallel", "parallel", "parallel")),
    )(ckv, pos, w1, w2)


NEG = -2.0 ** 60
N_FEAT = HEAD_DIM


def _nsa_kernel(q_ref, kc_ref, vc_ref, skv_ref, wkv_ref, gate_ref, ovt_ref, kconst_ref, slope_ref, o_ref,
                kaug_sc, vaug_sc, kwin_sc, vwin_sc, m_sc, acc_sc, qaug_sc, sa_sc, sb_sc, *, n_sel, tk):
    n = pl.program_id(2)
    t0 = n * Q_BLOCK
    rows = B_GQA * Q_BLOCK
    seq = skv_ref.shape[0]
    dn = (((1,), (1,)), ((), ()))
    f32, bf16 = jnp.float32, jnp.bfloat16
    E = HEAD_DIM

    @pl.when(n == 0)
    def _build_keys():
        kaug_sc[:, 0:LANES] = kconst_ref[:, 0:LANES]
        kaug_sc[:, LANES:LANES + E] = skv_ref[:, 0:E]
        kaug_sc[:, LANES + E:2 * LANES] = kconst_ref[:, LANES:LANES + N_FEAT]
        vaug_sc[:, 0:E] = skv_ref[:, E:2 * E]
        vaug_sc[:, E:LANES] = kconst_ref[:, LANES + N_FEAT:2 * LANES]
        kwin_sc[:, 0:E] = wkv_ref[:, 0:E]
        kwin_sc[:, E:LANES] = kconst_ref[:, LANES:LANES + N_FEAT]
        vwin_sc[:, 0:E] = wkv_ref[:, E:2 * E]
        vwin_sc[:, E:LANES] = kconst_ref[:, LANES + N_FEAT:2 * LANES]

    q = q_ref[...]
    qs = jnp.concatenate([q[:, r * E:(r + 1) * E] for r in range(B_GQA)], axis=0)
    row = lax.broadcasted_iota(jnp.int32, (rows, 1), 0)
    within = row & (Q_BLOCK - 1)
    tq = t0 + within
    slope = slope_ref[...]
    lane = lax.broadcasted_iota(jnp.int32, (rows, N_FEAT), 1)
    feat = jnp.where(lane == 0, slope * SLC_LEN,
           jnp.where(lane == 1, slope,
           jnp.where(lane == 2, -slope * t0.astype(f32),
           jnp.where(lane == 3, -slope * within.astype(f32), 0.0))))
    q_feat = jnp.concatenate([qs, feat.astype(bf16)], axis=1)

    n_chunk = kc_ref.shape[0]
    s = lax.dot_general(qs, kc_ref[...], dn, preferred_element_type=f32)
    cmp_end = lax.broadcasted_iota(jnp.int32, (1, n_chunk), 1) * CMP_STRIDE + (CMP_LEN - 1)
    s = jnp.where(cmp_end <= tq, s, -jnp.inf)
    m = jnp.max(s, axis=1, keepdims=True)
    m = jnp.where(m == -jnp.inf, 0.0, m)
    p = jnp.exp(s - m)
    l = jnp.maximum(jnp.sum(p, axis=1, keepdims=True), TINY)
    p_c = (p / l).astype(bf16)
    o_c = jnp.dot(p_c, vc_ref[...], preferred_element_type=f32)

    span = min(WIN + Q_BLOCK, seq)
    start = pl.multiple_of(jnp.maximum(t0 + Q_BLOCK - span, 0), Q_BLOCK)
    s = lax.dot_general(q_feat, kwin_sc[pl.ds(start, span), :], dn, preferred_element_type=f32)
    dw = tq - (start + lax.broadcasted_iota(jnp.int32, (1, span), 1))
    s = jnp.where((dw >= 0) & (dw < WIN), s, -jnp.inf)
    m = jnp.max(s, axis=1, keepdims=True)
    p = jnp.exp(s - m).astype(bf16)
    acc = jnp.dot(p, vwin_sc[pl.ds(start, span), :], preferred_element_type=f32)
    o_w = acc[:, 0:E] / jnp.maximum(acc[:, E:E + 1], TINY)

    gates = 1.0 / (1.0 + jnp.exp(-gate_ref[...]))
    gate_of = lambda r, branch: jnp.broadcast_to(gates[:, 3 * r + branch:3 * r + branch + 1], (Q_BLOCK, E))
    gate_s = [gate_of(r, 1) for r in range(B_GQA)]
    gated_cw = [gate_of(r, 0) * o_c[r * Q_BLOCK:(r + 1) * Q_BLOCK] + gate_of(r, 2) * o_w[r * Q_BLOCK:(r + 1) * Q_BLOCK]
                for r in range(B_GQA)]

    imp = lax.dot_general(ovt_ref[...], p_c[0:Q_BLOCK], dn, preferred_element_type=f32)
    for r in range(1, B_GQA):
        imp += lax.dot_general(ovt_ref[...], p_c[r * Q_BLOCK:(r + 1) * Q_BLOCK], dn, preferred_element_type=f32)
    blk = lax.broadcasted_iota(jnp.int32, (LANES, Q_BLOCK), 0)
    t = t0 + lax.broadcasted_iota(jnp.int32, (LANES, Q_BLOCK), 1)
    cur = t >> 6
    forced = (blk == 0) | (blk == cur) | (blk == cur - 1)
    score = jnp.where(blk * SLC_LEN <= t, imp + jnp.where(forced, FORCE_BONUS, 0.0), -jnp.inf)

    def pick(_, carry):
        work, sel = carry
        best = jnp.max(work, axis=0, keepdims=True)
        idx = jnp.min(jnp.where(work == best, blk, LANES), axis=0, keepdims=True)
        hit = blk == idx
        finite = jnp.where(best > -jnp.inf, 1.0, 0.0)
        sel = jnp.where(hit, jnp.maximum(sel, finite), sel)
        return jnp.where(hit, -jnp.inf, work), sel

    _, sel = lax.fori_loop(0, n_sel, pick, (score, jnp.zeros((LANES, Q_BLOCK), f32)), unroll=True)
    bias = jnp.where(sel.T > 0.5, 0.0, NEG).astype(bf16)
    qaug_sc[...] = jnp.concatenate([jnp.concatenate([bias] * B_GQA, axis=0), q_feat], axis=1)

    m_sc[...] = jnp.full(m_sc.shape, -jnp.inf, f32)
    acc_sc[...] = jnp.zeros(acc_sc.shape, f32)

    def logits(j, s_ref):
        k0 = pl.multiple_of(j * tk, tk)
        s_ref[...] = lax.dot_general(qaug_sc[...], kaug_sc[pl.ds(k0, tk), :], dn, preferred_element_type=f32)

    def update(j, s_ref, diagonal):
        k0 = pl.multiple_of(j * tk, tk)
        s = s_ref[...]
        if diagonal:
            s = jnp.where(k0 + lax.broadcasted_iota(jnp.int32, (1, tk), 1) <= tq, s, -jnp.inf)
        m_old = m_sc[...]
        m_new = jnp.maximum(m_old, jnp.broadcast_to(jnp.max(s, axis=1, keepdims=True), m_old.shape))
        p = jnp.exp(s - jnp.concatenate([m_new] * (tk // LANES), axis=1)).astype(bf16)
        pv = jnp.dot(p, vaug_sc[pl.ds(k0, tk), :], preferred_element_type=f32)
        acc_sc[...] = jnp.exp(m_old - m_new) * acc_sc[...] + pv
        m_sc[...] = m_new

    j_last = t0 // tk
    logits(0, sa_sc)

    def tile_pairs(first, pairs):
        for k in range(pairs):
            j = first + 2 * k
            logits(j + 1, sb_sc)
            update(j, sa_sc, False)
            logits(j + 2, sa_sc)
            update(j + 1, sb_sc, False)

    n_quads = j_last // 4

    def four_tiles(i, carry):
        tile_pairs(4 * i, 2)
        return carry

    lax.fori_loop(0, n_quads, four_tiles, 0)

    @pl.when(j_last - 4 * n_quads >= 2)
    def _pair_left():
        tile_pairs(4 * n_quads, 1)

    odd = (j_last % 2) == 1

    @pl.when(odd)
    def _two_left():
        logits(j_last, sb_sc)
        update(j_last - 1, sa_sc, False)
        update(j_last, sb_sc, True)

    @pl.when(jnp.logical_not(odd))
    def _one_left():
        update(j_last, sa_sc, True)

    acc = acc_sc[...]
    o_s = acc[:, 0:E] / jnp.maximum(acc[:, E:E + 1], TINY)

    for r in range(B_GQA):
        out = gated_cw[r] + gate_s[r] * o_s[r * Q_BLOCK:(r + 1) * Q_BLOCK]
        o_ref[:, r * E:(r + 1) * E] = out.astype(o_ref.dtype)


def _nsa(u_nat, u_f32, cmp_kv, batch, seq):
    nq = seq // Q_BLOCK
    n_chunk = seq // CMP_STRIDE
    n_cmp = n_chunk - CMP_LEN // CMP_STRIDE + 1
    n_slc = seq // SLC_LEN
    n_sel = min(SLC_TOPK, n_slc)
    tk = min(512, seq)
    assert n_slc <= LANES and B_HEADS == 8 and seq % tk == 0
    c_start = np.arange(n_chunk) * CMP_STRIDE
    s_start = np.arange(LANES) * SLC_LEN
    overlap_t = ((c_start[None, :] < s_start[:, None] + SLC_LEN) & (c_start[None, :] + CMP_LEN > s_start[:, None])
                 & (np.arange(n_chunk)[None, :] < n_cmp) & (np.arange(LANES)[:, None] < n_slc))
    pos = np.arange(seq)
    kconst = np.zeros((seq, 2 * LANES), np.float32)
    kconst[pos, pos // SLC_LEN] = 1.0
    kconst[:, LANES + 0] = pos // SLC_LEN
    kconst[:, LANES + 1] = pos % SLC_LEN
    kconst[:, LANES + 2] = 1.0
    kconst[:, LANES + 3] = 1.0
    kconst[:, LANES + N_FEAT] = 1.0
    rows = B_GQA * Q_BLOCK
    slopes = jnp.exp2(-8.0 * jnp.arange(1, B_HEADS + 1, dtype=jnp.float32) / B_HEADS)
    slopes = jnp.repeat(slopes.reshape(B_KV_HEADS, B_GQA), Q_BLOCK, axis=1).reshape(B_KV_HEADS, rows, 1)
    kv_rows = pl.BlockSpec((seq, LANES), lambda b, g, n: (b, NAT_SLC + g))
    win_rows = pl.BlockSpec((seq, LANES), lambda b, g, n: (b, NAT_WIN + g))
    cmp_spec = lambda s: pl.BlockSpec((None, None, None, n_chunk, HEAD_DIM), lambda b, g, n: (s, b, g, 0, 0))
    return pl.pallas_call(
        functools.partial(_nsa_kernel, n_sel=n_sel, tk=tk),
        grid=(batch, B_KV_HEADS, nq),
        in_specs=[pl.BlockSpec((Q_BLOCK, B_GQA * HEAD_DIM), lambda b, g, n: (b * nq + n, NAT_Q_B + g)),
                  cmp_spec(0), cmp_spec(1), kv_rows, win_rows,
                  pl.BlockSpec((Q_BLOCK, LANES), lambda b, g, n: (b * nq + n, F_GATE + g)),
                  pl.BlockSpec((LANES, n_chunk), lambda b, g, n: (0, 0)),
                  pl.BlockSpec((seq, 2 * LANES), lambda b, g, n: (0, 0)),
                  pl.BlockSpec((None, rows, 1), lambda b, g, n: (g, 0, 0))],
        out_specs=pl.BlockSpec((Q_BLOCK, B_GQA * HEAD_DIM), lambda b, g, n: (b * nq + n, g)),
        out_shape=jax.ShapeDtypeStruct((batch * seq, B_WIDTH), jnp.bfloat16),
        scratch_shapes=[pltpu.VMEM((seq, 2 * LANES), jnp.bfloat16), pltpu.VMEM((seq, LANES), jnp.bfloat16),
                        pltpu.VMEM((seq, LANES), jnp.bfloat16), pltpu.VMEM((seq, LANES), jnp.bfloat16),
                        pltpu.VMEM((rows, LANES), jnp.float32), pltpu.VMEM((rows, LANES), jnp.float32),
                        pltpu.VMEM((rows, 2 * LANES), jnp.bfloat16),
                        pltpu.VMEM((rows, tk), jnp.float32), pltpu.VMEM((rows, tk), jnp.float32)],
        compiler_params=_params(("arbitrary", "arbitrary", "arbitrary")),
    )(u_nat, cmp_kv, cmp_kv, u_nat, u_nat, u_f32, jnp.asarray(overlap_t, jnp.bfloat16), jnp.asarray(kconst, jnp.bfloat16),
      slopes)


def _layer_norm(v, gain, bias):
    mu = jnp.mean(v, axis=1, keepdims=True)
    c = v - mu
    var = jnp.mean(c * c, axis=1, keepdims=True)
    return c * lax.rsqrt(var + LN_EPS) * gain + bias


def _sigmoid(v):
    return 1.0 / (1.0 + jnp.exp(-v))


MERGE_SPLIT = 2


def _token_major(ref, stage):
    dil, n = ref.shape[0], ref.shape[1]
    parts = stage.shape[0]
    for r in range(dil):
        for j in range(parts):
            stage[j, pl.ds(r, n, stride=dil), :] = ref[r, :, j * LANES:(j + 1) * LANES]
    return jnp.concatenate([stage[j] for j in range(parts)], axis=1)


def _merge_kernel(*refs):
    dil_refs, refs = refs[:2 * N_DIL], refs[2 * N_DIL:]
    (yb_ref, mga_ref, mgb_ref, x_ref, wa_ref, wb_ref, wo_ref, g_ref, b_ref, wr_hi_ref, wr_lo_ref, br_ref,
     h_ref, ht_ref, rw_ref, re_ref, cnt_ref) = refs[:17]
    stages, cnt_sc = refs[17:17 + 2 * (N_DIL - 1)], refs[17 + 2 * (N_DIL - 1)]
    f32 = jnp.float32
    outs, lse = [dil_refs[0][...]], [dil_refs[1][...]]
    for g in range(1, N_DIL):
        outs.append(_token_major(dil_refs[2 * g], stages[2 * g - 2]))
        lse.append(_token_major(dil_refs[2 * g + 1], stages[2 * g - 1]))
    def mixed_rows(rs):
        lse_r = [v[rs] for v in lse]
        top = jnp.maximum(jnp.maximum(lse_r[0], lse_r[1]), lse_r[2])
        e = [jnp.exp(v - top) for v in lse_r]
        y_a = (e[0] * outs[0][rs] + e[1] * outs[1][rs] + e[2] * outs[2][rs]) / (e[0] + e[1] + e[2])
        br_a = jnp.dot(y_a.astype(jnp.bfloat16), wa_ref[...], preferred_element_type=f32)
        br_b = jnp.dot(yb_ref[rs, :], wb_ref[...], preferred_element_type=f32)
        merged = _sigmoid(mga_ref[rs, :]) * br_a + _sigmoid(mgb_ref[rs, :]) * br_b
        mix = jnp.dot(merged.astype(jnp.bfloat16), wo_ref[...], preferred_element_type=f32)
        return _layer_norm(ALPHA * x_ref[rs, :] + mix, g_ref[...], b_ref[...])

    tm = x_ref.shape[0]
    part = tm // MERGE_SPLIT
    h = jnp.concatenate([mixed_rows(slice(k * part, (k + 1) * part)) for k in range(MERGE_SPLIT)], axis=0)
    h_ref[...] = h
    _to_token_tiles(ht_ref, h)

    h_hi = h.astype(jnp.bfloat16)
    h_lo = (h - h_hi.astype(f32)).astype(jnp.bfloat16)
    logits = (jnp.dot(h_hi, wr_hi_ref[...], preferred_element_type=f32)
              + jnp.dot(h_lo, wr_hi_ref[...], preferred_element_type=f32)
              + jnp.dot(h_hi, wr_lo_ref[...], preferred_element_type=f32)) + br_ref[...]
    lane = lax.broadcasted_iota(jnp.int32, logits.shape, 1)
    coarse = jnp.where(lane < N_GROUPS, logits, -jnp.inf)
    c_max = jnp.max(coarse, axis=1, keepdims=True)
    grp = jnp.min(jnp.where(coarse == c_max, lane, LANES), axis=1, keepdims=True)
    p_grp = 1.0 / jnp.sum(jnp.exp(coarse - c_max), axis=1, keepdims=True)
    lo_lane = N_GROUPS + EXP_PER_GROUP * grp
    fine = jnp.where((lane >= lo_lane) & (lane < lo_lane + EXP_PER_GROUP), logits, -jnp.inf)
    v1 = jnp.max(fine, axis=1, keepdims=True)
    i1 = jnp.min(jnp.where(fine == v1, lane, LANES), axis=1, keepdims=True)
    fine = jnp.where(lane == i1, -jnp.inf, fine)
    v2 = jnp.max(fine, axis=1, keepdims=True)
    i2 = jnp.min(jnp.where(fine == v2, lane, LANES), axis=1, keepdims=True)
    e2 = jnp.exp(v2 - v1)
    w1 = p_grp / (1.0 + e2)
    w2 = p_grp * e2 / (1.0 + e2)
    e_1, e_2 = i1 - N_GROUPS, i2 - N_GROUPS
    rw_ref[...] = jnp.where(lane == 0, w1, jnp.where(lane == 1, w2, 0.0))

    @pl.when(pl.program_id(0) == 0)
    def _zero_counts():
        cnt_sc[...] = jnp.zeros(cnt_sc.shape, f32)

    tm = logits.shape[0]
    onehot = jnp.where((lane == e_1) | (lane == e_2), 1.0, 0.0)
    earlier = (lax.broadcasted_iota(jnp.int32, (tm, tm), 0) > lax.broadcasted_iota(jnp.int32, (tm, tm), 1))
    before = jnp.dot(jnp.where(earlier, 1.0, 0.0).astype(jnp.bfloat16), onehot.astype(jnp.bfloat16),
                     preferred_element_type=f32) + cnt_sc[...]
    r_1 = jnp.sum(jnp.where(lane == e_1, before, 0.0), axis=1, keepdims=True).astype(jnp.int32)
    r_2 = jnp.sum(jnp.where(lane == e_2, before, 0.0), axis=1, keepdims=True).astype(jnp.int32)
    cnt_sc[...] = cnt_sc[...] + jnp.sum(onehot, axis=0, keepdims=True)
    cnt_ref[...] = cnt_sc[...].astype(jnp.int32)
    re_ref[...] = jnp.where(lane == 0, e_1, jnp.where(lane == 1, e_2,
                            jnp.where(lane == 2, r_1, jnp.where(lane == 3, r_2, 0))))


def _merge(dil_outs, y_b, u_f32, x2d, seq, wa, wb, wo, ln_g, ln_b, wr_hi, wr_lo, b_r, tm=256):
    t = x2d.shape[0]
    per_batch = seq // tm
    assert seq % tm == 0 and all(tm % (dil * 8) == 0 for _, dil in DIL_GROUPS)
    row = lambda w: pl.BlockSpec((tm, w), lambda i: (i, 0))
    full = lambda a: pl.BlockSpec(a.shape, lambda i: (0,) * a.ndim)
    dil_specs = [pl.BlockSpec((None, None, tm, A_OUT), lambda i: (i // per_batch, 0, i % per_batch, 0))] * 2
    stages = []
    for _, dil in DIL_GROUPS[1:]:
        dil_specs += [pl.BlockSpec((None, dil, tm // dil, A_OUT),
                                   lambda i: (i // per_batch, 0, i % per_batch, 0))] * 2
        stages += [pltpu.VMEM((A_OUT // LANES, tm, LANES), jnp.float32)] * 2
    return pl.pallas_call(
        _merge_kernel,
        grid=(t // tm,),
        in_specs=dil_specs + [row(B_WIDTH),
                  pl.BlockSpec((tm, D_MODEL), lambda i: (i, F_MGATE)),
                  pl.BlockSpec((tm, D_MODEL), lambda i: (i, F_MGATE + 1)),
                  row(D_MODEL), full(wa), full(wb), full(wo), full(ln_g), full(ln_b),
                  full(wr_hi), full(wr_lo), full(b_r)],
        out_specs=[row(D_MODEL), pl.BlockSpec((tm * CHUNKS, LANES), lambda i: (i, 0)), row(LANES), row(LANES),
                   pl.BlockSpec((1, LANES), lambda i: (0, 0))],
        out_shape=[jax.ShapeDtypeStruct((t, D_MODEL), jnp.float32),
                   jax.ShapeDtypeStruct((t * CHUNKS, LANES), jnp.float32),
                   jax.ShapeDtypeStruct((t, LANES), jnp.float32),
                   jax.ShapeDtypeStruct((t, LANES), jnp.int32),
                   jax.ShapeDtypeStruct((1, LANES), jnp.int32)],
        scratch_shapes=stages + [pltpu.VMEM((1, LANES), jnp.float32)],
        compiler_params=_params(("arbitrary",)),
    )(*dil_outs, y_b, u_f32, u_f32, x2d, wa, wb, wo, ln_g, ln_b, wr_hi, wr_lo, b_r)


CHUNKS = D_MODEL // LANES
GATHER_UNROLL = 8


def _to_token_tiles(ref2d, value):
    n = value.shape[0]
    for c in range(CHUNKS):
        ref2d[pl.ds(c, n, stride=CHUNKS), :] = value[:, c * LANES:(c + 1) * LANES]


def _from_token_tiles(ref2d):
    n = ref2d.shape[0] // CHUNKS
    return jnp.concatenate([ref2d[pl.ds(c, n, stride=CHUNKS), :] for c in range(CHUNKS)], axis=1)


def _token_copy(idx_ref, k, src2d, dst2d, r, sem):
    src_row = pl.multiple_of(idx_ref[0, 0, k] * CHUNKS, CHUNKS)
    dst_row = r * CHUNKS if isinstance(r, int) else pl.multiple_of(r * CHUNKS, CHUNKS)
    return pltpu.make_async_copy(src2d.at[pl.ds(src_row, CHUNKS), :], dst2d.at[pl.ds(dst_row, CHUNKS), :], sem)


SCATTER_STEP = 37


def _start_token_gather(idx_ref, idx_of, src2d, dst2d, sem, unrolled, priority_of=lambda r: 0, scattered=False):
    n = dst2d.shape[0] // CHUNKS
    if unrolled:
        assert not scattered or n & (n - 1) == 0
        for j in range(n):
            r = j * SCATTER_STEP % n if scattered else j
            _token_copy(idx_ref, idx_of(r), src2d, dst2d, r, sem).start(priority=priority_of(j))
    else:
        def start(r, carry):
            _token_copy(idx_ref, idx_of(r), src2d, dst2d, r, sem).start()
            return carry

        lax.fori_loop(0, n, start, 0, unroll=GATHER_UNROLL)


def _wait_token_gather(src2d, dst2d, sem):
    pltpu.make_async_copy(src2d.at[pl.ds(0, dst2d.shape[0]), :], dst2d, sem).wait()


GATHER_BUFFERS = 3


def _gather_ring(i, first_refs, ahead_ref, start, wait, compute):
    last = pl.num_programs(0) - 1

    @pl.when(i == 0)
    def _prologue():
        for k, ref in enumerate(first_refs):
            start(ref, k, unrolled=False)

    def step(k):
        ahead = (k + GATHER_BUFFERS - 1) % GATHER_BUFFERS
        wait(k)
        start(ahead_ref, ahead, unrolled=True)
        compute(k)

        @pl.when(i == last)
        def _drain():
            for other in range(GATHER_BUFFERS):
                if other != k:
                    wait(other)

    for k in range(GATHER_BUFFERS):
        pl.when(i % GATHER_BUFFERS == k)(functools.partial(step, k))


def _expert_kernel(blk_e_ref, tok_ref, tok_1_ref, tok_2_ref, ht_hbm, wgu_ref, wd_ref, y_ref, *scratch):
    bufs, sems = scratch[:GATHER_BUFFERS], scratch[GATHER_BUFFERS]

    def start(idx_ref, k, unrolled):
        _start_token_gather(idx_ref, lambda r: r, ht_hbm, bufs[k], sems.at[k], unrolled,
                            priority_of=lambda r: r % 2, scattered=True)

    def wait(k):
        _wait_token_gather(ht_hbm, bufs[k], sems.at[k])

    def compute(k):
        xb = _from_token_tiles(bufs[k]).astype(jnp.bfloat16)
        gu = jnp.dot(xb, wgu_ref[...], preferred_element_type=jnp.float32)
        gate, up = gu[:, :D_EXPERT], gu[:, D_EXPERT:]
        act = (gate * _sigmoid(gate)) * up
        _to_token_tiles(y_ref, jnp.dot(act.astype(jnp.bfloat16), wd_ref[...], preferred_element_type=jnp.float32))

    _gather_ring(pl.program_id(0), (tok_ref, tok_1_ref), tok_2_ref, start, wait, compute)


def _experts(h_tiles, tok_buf, blk_e, wgu, wd, bm):
    p = tok_buf.shape[0]
    nb = p // bm
    tok = tok_buf.reshape(nb, 1, bm)
    idx_spec = lambda ahead: pl.BlockSpec((1, 1, bm), lambda i, e: (jnp.minimum(i + ahead, nb - 1), 0, 0),
                                          memory_space=pltpu.SMEM)
    return pl.pallas_call(
        _expert_kernel,
        grid_spec=pltpu.PrefetchScalarGridSpec(
            num_scalar_prefetch=1,
            grid=(nb,),
            in_specs=[idx_spec(0), idx_spec(1), idx_spec(2),
                      pl.BlockSpec(memory_space=pl.ANY),
                      pl.BlockSpec((None, D_MODEL, 2 * D_EXPERT), lambda i, e: (e[i], 0, 0)),
                      pl.BlockSpec((None, D_EXPERT, D_MODEL), lambda i, e: (e[i], 0, 0))],
            out_specs=pl.BlockSpec((bm * CHUNKS, LANES), lambda i, e: (i, 0)),
            scratch_shapes=[pltpu.VMEM((bm * CHUNKS, LANES), jnp.float32)] * GATHER_BUFFERS
                           + [pltpu.SemaphoreType.DMA((GATHER_BUFFERS,))],
        ),
        out_shape=jax.ShapeDtypeStruct((p * CHUNKS, LANES), jnp.float32),
        compiler_params=_params(("arbitrary",)),
    )(blk_e, tok, tok, tok, h_tiles, wgu, wd)


def _combine_kernel(dest_ref, dest_1_ref, dest_2_ref, yt_hbm, h_ref, rw_ref, g_ref, b_ref, o_ref, *scratch):
    bufs, sems = scratch[:GATHER_BUFFERS], scratch[GATHER_BUFFERS]

    def start(idx_ref, k, unrolled):
        for slot in range(MOE_TOP_K):
            _start_token_gather(idx_ref, lambda r, slot=slot: MOE_TOP_K * r + slot, yt_hbm, bufs[k].at[slot],
                                sems.at[k], unrolled, priority_of=lambda r: r % 2)

    def wait(k):
        for slot in range(MOE_TOP_K):
            _wait_token_gather(yt_hbm, bufs[k].at[slot], sems.at[k])

    def compute(k):
        rw = rw_ref[...]
        ffn = rw[:, 0:1] * _from_token_tiles(bufs[k].at[0]) + rw[:, 1:2] * _from_token_tiles(bufs[k].at[1])
        o_ref[...] = _layer_norm(ALPHA * h_ref[...] + ffn, g_ref[...], b_ref[...])

    _gather_ring(pl.program_id(0), (dest_ref, dest_1_ref), dest_2_ref, start, wait, compute)


def _combine(dest, y_tiles, h, rw, ln_g, ln_b, tm=256):
    t = h.shape[0]
    assert t % tm == 0
    nt = t // tm
    idx = dest.reshape(nt, 1, MOE_TOP_K * tm)
    idx_spec = lambda ahead: pl.BlockSpec((1, 1, MOE_TOP_K * tm), lambda i: (jnp.minimum(i + ahead, nt - 1), 0, 0),
                                          memory_space=pltpu.SMEM)
    return pl.pallas_call(
        _combine_kernel,
        grid=(nt,),
        in_specs=[idx_spec(0), idx_spec(1), idx_spec(2),
                  pl.BlockSpec(memory_space=pl.ANY),
                  pl.BlockSpec((tm, D_MODEL), lambda i: (i, 0)),
                  pl.BlockSpec((tm, LANES), lambda i: (i, 0)),
                  pl.BlockSpec((1, D_MODEL), lambda i: (0, 0)),
                  pl.BlockSpec((1, D_MODEL), lambda i: (0, 0))],
        out_specs=pl.BlockSpec((tm, D_MODEL), lambda i: (i, 0)),
        out_shape=jax.ShapeDtypeStruct((t, D_MODEL), jnp.float32),
        scratch_shapes=[pltpu.VMEM((MOE_TOP_K, tm * CHUNKS, LANES), jnp.float32)] * GATHER_BUFFERS
                       + [pltpu.SemaphoreType.DMA((GATHER_BUFFERS,))],
        compiler_params=_params(("arbitrary",)),
    )(idx, idx, idx, y_tiles, h, rw, ln_g, ln_b)


def _dispatch(route, counts, bm):
    t = route.shape[0]
    n_assign = t * MOE_TOP_K
    expert, rank = route[:, 0:MOE_TOP_K], route[:, MOE_TOP_K:2 * MOE_TOP_K]
    starts = jnp.cumsum(counts) - counts
    pcounts = (counts + bm - 1) // bm * bm
    pends = jnp.cumsum(pcounts)
    pstarts = pends - pcounts
    ids = jnp.arange(N_EXPERTS, dtype=jnp.int32)
    dest = jnp.sum(jnp.where(expert[..., None] == ids, pstarts, 0), axis=-1) + rank
    p = -(-(n_assign + N_EXPERTS * (bm - 1)) // bm) * bm
    nb = p // bm
    blk_e = jnp.minimum(jnp.sum(pends[None, :] <= (jnp.arange(nb, dtype=jnp.int32) * bm)[:, None], axis=1),
                        N_EXPERTS - 1).astype(jnp.int32)
    order = jnp.argsort(expert.reshape(n_assign), stable=True).astype(jnp.int32)
    row_e = jnp.repeat(blk_e, bm)
    k = jnp.arange(p, dtype=jnp.int32) - pstarts[row_e]
    src = jnp.clip(starts[row_e] + k, 0, n_assign - 1)
    tok_buf = jnp.where(k < counts[row_e], order[src] // MOE_TOP_K, 0).astype(jnp.int32)
    return tok_buf, dest.astype(jnp.int32), blk_e


def _split_bf16(w):
    hi = w.astype(jnp.bfloat16)
    return hi, (w - hi.astype(jnp.float32)).astype(jnp.bfloat16)


def kernel(x, w_in, cmp_pos_k, cmp_w1_k, cmp_w2_k, cmp_pos_v, cmp_w1_v, cmp_w2_v, w_branch_a, w_branch_b, w_out, ln1_g, ln1_b, w_coarse, b_coarse, w_fine, b_fine, w_gate_up, w_down, ln2_g, ln2_b):
    batch, seq, d = x.shape
    bf16 = jnp.bfloat16
    h = x.reshape(batch * seq, d)
    cols_bf, cols_f32, scale_bf = _in_proj_columns()
    half = CMP_STRIDE * HEAD_DIM
    for l in range(DEPTH):
        u_nat, *groups = _project_bf(h, (_permute_columns(w_in[l], cols_bf) * scale_bf).astype(bf16), batch, seq)
        u_f32, ckv = _project_f32(h, _permute_columns(w_in[l], cols_f32).astype(bf16), batch, seq)
        dil_outs = _dilated(u_nat, groups, batch, seq)
        pos = jnp.stack([cmp_pos_k[l], cmp_pos_v[l]]).reshape(2, 2, 1, half)
        w1 = jnp.stack([cmp_w1_k[l], cmp_w1_v[l]]).astype(bf16)
        w2 = jnp.stack([cmp_w2_k[l], cmp_w2_v[l]]).astype(bf16)
        cmp_kv = _compress(ckv, pos, w1, w2)
        y_b = _nsa(u_nat, u_f32, cmp_kv, batch, seq)
        w_r = jnp.concatenate([w_coarse[l], w_fine[l].transpose(1, 0, 2).reshape(d, N_EXPERTS)], axis=1)
        w_r = jnp.pad(w_r, ((0, 0), (0, LANES - w_r.shape[1])))
        b_r = jnp.pad(jnp.concatenate([b_coarse[l], b_fine[l].reshape(N_EXPERTS)]),
                      (0, LANES - N_GROUPS - N_EXPERTS)).reshape(1, LANES)
        wr_hi, wr_lo = _split_bf16(w_r)
        h1, h1_tiles, rw, route, counts = _merge(dil_outs, y_b, u_f32, h, seq, w_branch_a[l].astype(bf16),
                                       w_branch_b[l].astype(bf16), w_out[l].astype(bf16),
                                       ln1_g[l].reshape(1, d), ln1_b[l].reshape(1, d), wr_hi, wr_lo, b_r)
        bm = 256
        tok_buf, dest, blk_e = _dispatch(route, counts[0, :N_EXPERTS], bm)
        y_tiles = _experts(h1_tiles, tok_buf, blk_e, w_gate_up[l].astype(bf16), w_down[l].astype(bf16), bm)
        h = _combine(dest, y_tiles, h1, rw, ln2_g[l].reshape(1, d), ln2_b[l].reshape(1, d))
    return h.reshape(batch, seq, d)
```

```python
import functools

import numpy as np
import jax
import jax.numpy as jnp
from jax import lax
from jax.experimental import pallas as pl
from jax.experimental.pallas import tpu as pltpu

D_MODEL = 1024
HEAD_DIM = 64
Q_BLOCK = 128
DIL_GROUPS = ((128, 1), (512, 4), (2048, 16))
N_DIL = len(DIL_GROUPS)
A_HEADS_PER_GROUP = 4
A_HEADS = A_HEADS_PER_GROUP * N_DIL
A_WIDTH = A_HEADS * HEAD_DIM
A_OUT = A_HEADS_PER_GROUP * HEAD_DIM
B_HEADS = 8
B_KV_HEADS = 2
B_GQA = B_HEADS // B_KV_HEADS
B_WIDTH = B_HEADS * HEAD_DIM
B_KV_WIDTH = B_KV_HEADS * HEAD_DIM
CMP_LEN = 32
CMP_STRIDE = 16
CMP_HIDDEN = 256
SLC_LEN = 64
SLC_TOPK = 16
WIN = 512
N_GROUPS = 4
EXP_PER_GROUP = 8
N_EXPERTS = N_GROUPS * EXP_PER_GROUP
D_EXPERT = 512
MOE_TOP_K = 2
DEPTH = 1
ALPHA = (2.0 * DEPTH) ** 0.25
LN_EPS = 1e-5
FORCE_BONUS = 1e4
TINY = 1e-30
ATTN_SCALE = HEAD_DIM ** -0.5

LANES = 128
VMEM_LIMIT = 48 * 1024 * 1024

QKV_COLS = 3 * A_OUT
NAT_Q_B = QKV_COLS // A_OUT
NAT_SLC = (QKV_COLS + B_WIDTH) // LANES
NAT_WIN = NAT_SLC + B_KV_HEADS
NAT_COLS = QKV_COLS + B_WIDTH + 4 * B_KV_WIDTH
BF_COLS = NAT_COLS + (N_DIL - 1) * QKV_COLS
F_MGATE = 0
F_GATE = 2 * D_MODEL // LANES
F_NAT_COLS = 2 * D_MODEL + B_KV_HEADS * LANES
F_COLS = F_NAT_COLS + 2 * B_KV_WIDTH


def _params(semantics):
    return pltpu.CompilerParams(dimension_semantics=semantics, vmem_limit_bytes=VMEM_LIMIT)


def _in_proj_columns():
    kv_off = 3 * A_WIDTH + B_WIDTH
    gate_off = kv_off + 6 * B_KV_WIDTH
    mg_off = gate_off + 3 * B_HEADS
    def qkv_cols(g):
        return [part * A_WIDTH + g * A_OUT + c for part in range(3) for c in range(A_OUT)]

    bf = qkv_cols(0) + list(range(3 * A_WIDTH, 3 * A_WIDTH + B_WIDTH))
    for first in (2, 4):
        for g in range(B_KV_HEADS):
            for i in (first, first + 1):
                base = kv_off + i * B_KV_WIDTH + g * HEAD_DIM
                bf += list(range(base, base + HEAD_DIM))
    for g in range(1, N_DIL):
        bf += qkv_cols(g)
    f32 = list(range(mg_off, mg_off + 2 * D_MODEL))
    per_head = 3 * B_GQA
    for g in range(B_KV_HEADS):
        f32 += list(range(gate_off + g * per_head, gate_off + (g + 1) * per_head))
        f32 += [-1] * (LANES - per_head)
    f32 += list(range(kv_off, kv_off + 2 * B_KV_WIDTH))
    assert len(bf) == BF_COLS and len(f32) == F_COLS
    bf = np.asarray(bf)
    is_q = (bf < A_WIDTH) | ((bf >= 3 * A_WIDTH) & (bf < 3 * A_WIDTH + B_WIDTH))
    scale = np.where(is_q, ATTN_SCALE, 1.0).astype(np.float32)
    return bf, np.asarray(f32), scale


def _permute_columns(w, cols):
    pieces, start = [], 0
    for end in range(1, len(cols) + 1):
        run_ends = end == len(cols) or (cols[end] != cols[end - 1] + 1 if cols[end - 1] >= 0 else cols[end] >= 0)
        if run_ends:
            first = int(cols[start])
            pieces.append(jnp.zeros((w.shape[0], end - start), w.dtype) if first < 0
                          else w[:, first:first + end - start])
            start = end
    return jnp.concatenate(pieces, axis=1)


def _proj_f32_kernel(x_ref, w_ref, nat_ref, cmp_ref, stage):
    xb = x_ref[...].astype(jnp.bfloat16)
    tm = x_ref.shape[0]
    n_chunk = tm // CMP_STRIDE
    for c0 in range(0, F_NAT_COLS, A_OUT):
        nat_ref[:, c0:c0 + A_OUT] = jnp.dot(xb, w_ref[:, c0:c0 + A_OUT], preferred_element_type=jnp.float32)
    res = jnp.dot(xb, w_ref[:, F_NAT_COLS:F_COLS], preferred_element_type=jnp.float32)
    lower = lax.broadcasted_iota(jnp.int32, (n_chunk, LANES), 1) < HEAD_DIM
    for kv in range(2):
        stage[...] = res[:, kv * LANES:(kv + 1) * LANES]
        for p in range(0, CMP_STRIDE, 2):
            even = stage[pl.ds(p, n_chunk, stride=CMP_STRIDE), :]
            odd = stage[pl.ds(p + 1, n_chunk, stride=CMP_STRIDE), :]
            cols = slice(p * HEAD_DIM, (p + 2) * HEAD_DIM)
            cmp_ref[kv, 0, :, cols] = jnp.where(lower, even, pltpu.roll(odd, HEAD_DIM, 1))
            cmp_ref[kv, 1, :, cols] = jnp.where(lower, pltpu.roll(even, HEAD_DIM, 1), odd)


def _project_f32(x2d, w, batch, seq, tm=512):
    t, d = x2d.shape
    assert seq % tm == 0 and w.shape[1] == F_COLS and B_KV_HEADS * HEAD_DIM == LANES
    per_batch = seq // tm
    half = CMP_STRIDE * HEAD_DIM
    return pl.pallas_call(
        _proj_f32_kernel,
        grid=(t // tm,),
        in_specs=[pl.BlockSpec((tm, d), lambda i: (i, 0)),
                  pl.BlockSpec((d, F_COLS), lambda i: (0, 0))],
        out_specs=[pl.BlockSpec((tm, F_NAT_COLS), lambda i: (i, 0)),
                   pl.BlockSpec((2, None, B_KV_HEADS, tm // CMP_STRIDE, half),
                                lambda i: (0, i // per_batch, 0, i % per_batch, 0))],
        out_shape=[jax.ShapeDtypeStruct((t, F_NAT_COLS), jnp.float32),
                   jax.ShapeDtypeStruct((2, batch, B_KV_HEADS, seq // CMP_STRIDE, half), jnp.float32)],
        scratch_shapes=[pltpu.VMEM((tm, LANES), jnp.float32)],
        compiler_params=_params(("parallel",)),
    )(x2d, w)


def _proj_bf_kernel(x_ref, w_ref, nat_ref, *rest, dils):
    group_refs, stage = rest[:-1], rest[-1]
    xb = x_ref[...].astype(jnp.bfloat16)
    tm = x_ref.shape[0]
    for c0 in range(0, NAT_COLS, A_OUT):
        nat_ref[:, c0:c0 + A_OUT] = jnp.dot(
            xb, w_ref[:, c0:c0 + A_OUT], preferred_element_type=jnp.float32).astype(nat_ref.dtype)
    col = NAT_COLS
    for o_ref, dil in zip(group_refs, dils):
        for part in range(3):
            res = jnp.dot(xb, w_ref[:, col:col + A_OUT], preferred_element_type=jnp.float32)
            for j in range(A_OUT // LANES):
                stage[j] = res[:, j * LANES:(j + 1) * LANES]
                c0 = part * A_OUT + j * LANES
                for r in range(dil):
                    o_ref[r, :, c0:c0 + LANES] = stage[j, pl.ds(r, tm // dil, stride=dil), :].astype(o_ref.dtype)
            col += A_OUT


def _project_bf(x2d, w, batch, seq, tm=512):
    t, d = x2d.shape
    dils = tuple(dil for _, dil in DIL_GROUPS[1:])
    assert seq % tm == 0 and all(tm % (dil * 16) == 0 for dil in dils) and w.shape[1] == BF_COLS
    per_batch = seq // tm
    out_specs = [pl.BlockSpec((tm, NAT_COLS), lambda i: (i, 0))]
    out_shapes = [jax.ShapeDtypeStruct((t, NAT_COLS), jnp.bfloat16)]
    for dil in dils:
        out_specs.append(pl.BlockSpec((None, dil, tm // dil, QKV_COLS),
                                      lambda i: (i // per_batch, 0, i % per_batch, 0)))
        out_shapes.append(jax.ShapeDtypeStruct((batch, dil, seq // dil, QKV_COLS), jnp.bfloat16))
    return pl.pallas_call(
        functools.partial(_proj_bf_kernel, dils=dils),
        grid=(t // tm,),
        in_specs=[pl.BlockSpec((tm, d), lambda i: (i, 0)),
                  pl.BlockSpec((d, BF_COLS), lambda i: (0, 0))],
        out_specs=out_specs,
        out_shape=out_shapes,
        scratch_shapes=[pltpu.VMEM((A_OUT // LANES, tm, LANES), jnp.float32)],
        compiler_params=_params(("parallel",)),
    )(x2d, w)


def _dilated_kernel(*refs, nbs):
    bias_ref, ins, outs = refs[0], refs[1:1 + 5 * N_DIL], refs[1 + 5 * N_DIL:]
    i = pl.program_id(1)
    dn = (((1,), (1,)), ((), ()))
    f32, bf16 = jnp.float32, jnp.bfloat16
    lower = lax.broadcasted_iota(jnp.int32, (Q_BLOCK, LANES), 1) < HEAD_DIM
    heads = [(g, pair, half) for g in range(N_DIL) for pair in range(A_OUT // LANES) for half in range(2)]

    def col(ref, pair):
        return ref[:, pair * LANES:(pair + 1) * LANES]

    logits = []
    for g, pair, half in heads:
        q_ref, kp_ref, kc_ref = ins[5 * g:5 * g + 3]
        head = g * A_HEADS_PER_GROUP + 2 * pair + half
        q = col(q_ref, pair)
        q = jnp.where(lower if half == 0 else jnp.logical_not(lower), q, jnp.zeros_like(q))
        no_prev = jnp.where((i % nbs[g]) > 0, 0.0, -jnp.inf)
        s_p = lax.dot_general(q, col(kp_ref, pair), dn, preferred_element_type=f32)
        s_c = lax.dot_general(q, col(kc_ref, pair), dn, preferred_element_type=f32)
        logits.append((s_p + bias_ref[head, :, 0:Q_BLOCK] + no_prev, s_c + bias_ref[head, :, Q_BLOCK:2 * Q_BLOCK]))

    probs = []
    for s_p, s_c in logits:
        m = jnp.max(jnp.maximum(s_p, s_c), axis=1, keepdims=True)
        p_p, p_c = jnp.exp(s_p - m), jnp.exp(s_c - m)
        l = jnp.sum(p_p + p_c, axis=1, keepdims=True)
        probs.append((p_p.astype(bf16), p_c.astype(bf16), l, m + jnp.log(l)))

    for n_head in range(0, len(heads), 2):
        g, pair, _ = heads[n_head]
        vp_ref, vc_ref = ins[5 * g + 3:5 * g + 5]
        o_ref, lse_ref = outs[2 * g:2 * g + 2]
        out, lse = [], []
        for p_p, p_c, l, lse_h in probs[n_head:n_head + 2]:
            acc = jnp.dot(p_p, col(vp_ref, pair), preferred_element_type=f32)
            acc += jnp.dot(p_c, col(vc_ref, pair), preferred_element_type=f32)
            out.append(acc / l)
            lse.append(jnp.broadcast_to(lse_h, (Q_BLOCK, LANES)))
        o_ref[:, pair * LANES:(pair + 1) * LANES] = jnp.where(lower, out[0], out[1])
        lse_ref[:, pair * LANES:(pair + 1) * LANES] = jnp.where(lower, lse[0], lse[1])


def _dilated_bias():
    qi = Q_BLOCK + jnp.arange(Q_BLOCK)
    kj = jnp.arange(2 * Q_BLOCK)
    delta = qi[:, None] - kj[None, :]
    slopes = jnp.exp2(-8.0 * jnp.arange(1, A_HEADS + 1, dtype=jnp.float32) / A_HEADS)
    tables = []
    for g, (window, dil) in enumerate(DIL_GROUPS):
        on_band = (delta >= 0) & (delta <= window // dil)
        dist = (delta * dil).astype(jnp.float32)
        for h in range(A_HEADS_PER_GROUP):
            tables.append(jnp.where(on_band, -slopes[g * A_HEADS_PER_GROUP + h] * dist, -jnp.inf))
    return jnp.stack(tables)


def _dilated(u_nat, groups, batch, seq):
    steps = seq // Q_BLOCK
    arrays = [u_nat.reshape(batch, 1, seq, NAT_COLS)] + list(groups)
    bias = _dilated_bias()
    in_arrays, in_specs = [bias], [pl.BlockSpec(bias.shape, lambda b, i: (0, 0, 0))]
    out_specs, out_shapes, nbs = [], [], []
    blk = (None, None, Q_BLOCK, A_OUT)
    for g, (window, dil) in enumerate(DIL_GROUPS):
        length = seq // dil
        assert length % Q_BLOCK == 0 and window // dil == Q_BLOCK
        nb = length // Q_BLOCK
        nbs.append(nb)

        def cur(b, i, nb=nb, col=0):
            return (b, i // nb, i % nb, col)

        def prev(b, i, nb=nb, col=0):
            return (b, i // nb, jnp.maximum(i % nb - 1, 0), col)

        for fn, col in ((cur, 0), (prev, 1), (cur, 1), (prev, 2), (cur, 2)):
            in_arrays.append(arrays[g])
            in_specs.append(pl.BlockSpec(blk, functools.partial(fn, col=col)))
        for _ in range(2):
            out_specs.append(pl.BlockSpec(blk, functools.partial(cur, col=0)))
            out_shapes.append(jax.ShapeDtypeStruct((batch, dil, length, A_OUT), jnp.float32))
    return pl.pallas_call(
        functools.partial(_dilated_kernel, nbs=tuple(nbs)),
        grid=(batch, steps),
        in_specs=in_specs,
        out_specs=out_specs,
        out_shape=out_shapes,
        compiler_params=_params(("parallel", "parallel")),
    )(*in_arrays)


def _compress_kernel(x_ref, pos_ref, w1_ref, w2_ref, o_ref):
    half = CMP_STRIDE * HEAD_DIM
    x = x_ref[...]
    n_chunk = x.shape[0]
    lo = (x + pos_ref[0]).astype(jnp.bfloat16)
    hi = (x + pos_ref[1]).astype(jnp.bfloat16)
    h_lo = jnp.dot(lo, w1_ref[0:half, :], preferred_element_type=jnp.float32)
    h_hi = jnp.dot(hi, w1_ref[half:2 * half, :], preferred_element_type=jnp.float32)
    h = h_lo + pltpu.roll(h_hi, n_chunk - 1, 0)
    act = jax.nn.gelu(h, approximate=True)
    o_ref[...] = jnp.dot(act.astype(jnp.bfloat16), w2_ref[...],
                         preferred_element_type=jnp.float32).astype(o_ref.dtype)


def _compress(ckv, pos, w1, w2):
    _, batch, _, n_chunk, half = ckv.shape
    return pl.pallas_call(
        _compress_kernel,
        grid=(2, batch, B_KV_HEADS),
        in_specs=[pl.BlockSpec((None, None, None, n_chunk, half), lambda s, b, g: (s, b, g, 0, 0)),
                  pl.BlockSpec((None, 2, 1, half), lambda s, b, g: (s, 0, 0, 0)),
                  pl.BlockSpec((None, 2 * half, CMP_HIDDEN), lambda s, b, g: (s, 0, 0)),
                  pl.BlockSpec((None, CMP_HIDDEN, HEAD_DIM), lambda s, b, g: (s, 0, 0))],
        out_specs=pl.BlockSpec((None, None, None, n_chunk, HEAD_DIM), lambda s, b, g: (s, b, g, 0, 0)),
        out_shape=jax.ShapeDtypeStruct((2, batch, B_KV_HEADS, n_chunk, HEAD_DIM), jnp.bfloat16),
        compiler_params=_params(("parallel", "parallel", "parallel")),
    )(ckv, pos, w1, w2)


NEG = -2.0 ** 60
N_FEAT = HEAD_DIM


def _nsa_kernel(q_ref, kc_ref, vc_ref, skv_ref, wkv_ref, gate_ref, ovt_ref, kconst_ref, slope_ref, o_ref,
                kaug_sc, vaug_sc, kwin_sc, vwin_sc, m_sc, acc_sc, qaug_sc, sa_sc, sb_sc, *, n_sel, tk):
    n = pl.program_id(2)
    t0 = n * Q_BLOCK
    rows = B_GQA * Q_BLOCK
    seq = skv_ref.shape[0]
    dn = (((1,), (1,)), ((), ()))
    f32, bf16 = jnp.float32, jnp.bfloat16
    E = HEAD_DIM

    @pl.when(n == 0)
    def _build_keys():
        kaug_sc[:, 0:LANES] = kconst_ref[:, 0:LANES]
        kaug_sc[:, LANES:LANES + E] = skv_ref[:, 0:E]
        kaug_sc[:, LANES + E:2 * LANES] = kconst_ref[:, LANES:LANES + N_FEAT]
        vaug_sc[:, 0:E] = skv_ref[:, E:2 * E]
        vaug_sc[:, E:LANES] = kconst_ref[:, LANES + N_FEAT:2 * LANES]
        kwin_sc[:, 0:E] = wkv_ref[:, 0:E]
        kwin_sc[:, E:LANES] = kconst_ref[:, LANES:LANES + N_FEAT]
        vwin_sc[:, 0:E] = wkv_ref[:, E:2 * E]
        vwin_sc[:, E:LANES] = kconst_ref[:, LANES + N_FEAT:2 * LANES]

    q = q_ref[...]
    qs = jnp.concatenate([q[:, r * E:(r + 1) * E] for r in range(B_GQA)], axis=0)
    row = lax.broadcasted_iota(jnp.int32, (rows, 1), 0)
    within = row & (Q_BLOCK - 1)
    tq = t0 + within
    slope = slope_ref[...]
    lane = lax.broadcasted_iota(jnp.int32, (rows, N_FEAT), 1)
    feat = jnp.where(lane == 0, slope * SLC_LEN,
           jnp.where(lane == 1, slope,
           jnp.where(lane == 2, -slope * t0.astype(f32),
           jnp.where(lane == 3, -slope * within.astype(f32), 0.0))))
    q_feat = jnp.concatenate([qs, feat.astype(bf16)], axis=1)

    n_chunk = kc_ref.shape[0]
    s = lax.dot_general(qs, kc_ref[...], dn, preferred_element_type=f32)
    cmp_end = lax.broadcasted_iota(jnp.int32, (1, n_chunk), 1) * CMP_STRIDE + (CMP_LEN - 1)
    s = jnp.where(cmp_end <= tq, s, -jnp.inf)
    m = jnp.max(s, axis=1, keepdims=True)
    m = jnp.where(m == -jnp.inf, 0.0, m)
    p = jnp.exp(s - m)
    l = jnp.maximum(jnp.sum(p, axis=1, keepdims=True), TINY)
    p_c = (p / l).astype(bf16)
    o_c = jnp.dot(p_c, vc_ref[...], preferred_element_type=f32)

    span = min(WIN + Q_BLOCK, seq)
    start = pl.multiple_of(jnp.maximum(t0 + Q_BLOCK - span, 0), Q_BLOCK)
    s = lax.dot_general(q_feat, kwin_sc[pl.ds(start, span), :], dn, preferred_element_type=f32)
    dw = tq - (start + lax.broadcasted_iota(jnp.int32, (1, span), 1))
    s = jnp.where((dw >= 0) & (dw < WIN), s, -jnp.inf)
    m = jnp.max(s, axis=1, keepdims=True)
    p = jnp.exp(s - m).astype(bf16)
    acc = jnp.dot(p, vwin_sc[pl.ds(start, span), :], preferred_element_type=f32)
    o_w = acc[:, 0:E] / jnp.maximum(acc[:, E:E + 1], TINY)

    gates = 1.0 / (1.0 + jnp.exp(-gate_ref[...]))
    gate_of = lambda r, branch: jnp.broadcast_to(gates[:, 3 * r + branch:3 * r + branch + 1], (Q_BLOCK, E))
    gate_s = [gate_of(r, 1) for r in range(B_GQA)]
    gated_cw = [gate_of(r, 0) * o_c[r * Q_BLOCK:(r + 1) * Q_BLOCK] + gate_of(r, 2) * o_w[r * Q_BLOCK:(r + 1) * Q_BLOCK]
                for r in range(B_GQA)]

    imp = lax.dot_general(ovt_ref[...], p_c[0:Q_BLOCK], dn, preferred_element_type=f32)
    for r in range(1, B_GQA):
        imp += lax.dot_general(ovt_ref[...], p_c[r * Q_BLOCK:(r + 1) * Q_BLOCK], dn, preferred_element_type=f32)
    blk = lax.broadcasted_iota(jnp.int32, (LANES, Q_BLOCK), 0)
    t = t0 + lax.broadcasted_iota(jnp.int32, (LANES, Q_BLOCK), 1)
    cur = t >> 6
    forced = (blk == 0) | (blk == cur) | (blk == cur - 1)
    score = jnp.where(blk * SLC_LEN <= t, imp + jnp.where(forced, FORCE_BONUS, 0.0), -jnp.inf)

    def pick(_, carry):
        work, sel = carry
        best = jnp.max(work, axis=0, keepdims=True)
        idx = jnp.min(jnp.where(work == best, blk, LANES), axis=0, keepdims=True)
        hit = blk == idx
        finite = jnp.where(best > -jnp.inf, 1.0, 0.0)
        sel = jnp.where(hit, jnp.maximum(sel, finite), sel)
        return jnp.where(hit, -jnp.inf, work), sel

    _, sel = lax.fori_loop(0, n_sel, pick, (score, jnp.zeros((LANES, Q_BLOCK), f32)), unroll=True)
    bias = jnp.where(sel.T > 0.5, 0.0, NEG).astype(bf16)
    qaug_sc[...] = jnp.concatenate([jnp.concatenate([bias] * B_GQA, axis=0), q_feat], axis=1)

    m_sc[...] = jnp.full(m_sc.shape, -jnp.inf, f32)
    acc_sc[...] = jnp.zeros(acc_sc.shape, f32)

    def logits(j, s_ref):
        k0 = pl.multiple_of(j * tk, tk)
        s_ref[...] = lax.dot_general(qaug_sc[...], kaug_sc[pl.ds(k0, tk), :], dn, preferred_element_type=f32)

    def update(j, s_ref, diagonal):
        k0 = pl.multiple_of(j * tk, tk)
        s = s_ref[...]
        if diagonal:
            s = jnp.where(k0 + lax.broadcasted_iota(jnp.int32, (1, tk), 1) <= tq, s, -jnp.inf)
        m_old = m_sc[...]
        m_new = jnp.maximum(m_old, jnp.broadcast_to(jnp.max(s, axis=1, keepdims=True), m_old.shape))
        p = jnp.exp(s - jnp.concatenate([m_new] * (tk // LANES), axis=1)).astype(bf16)
        pv = jnp.dot(p, vaug_sc[pl.ds(k0, tk), :], preferred_element_type=f32)
        acc_sc[...] = jnp.exp(m_old - m_new) * acc_sc[...] + pv
        m_sc[...] = m_new

    j_last = t0 // tk
    logits(0, sa_sc)

    def tile_pairs(first, pairs):
        for k in range(pairs):
            j = first + 2 * k
            logits(j + 1, sb_sc)
            update(j, sa_sc, False)
            logits(j + 2, sa_sc)
            update(j + 1, sb_sc, False)

    n_quads = j_last // 4

    def four_tiles(i, carry):
        tile_pairs(4 * i, 2)
        return carry

    lax.fori_loop(0, n_quads, four_tiles, 0)

    @pl.when(j_last - 4 * n_quads >= 2)
    def _pair_left():
        tile_pairs(4 * n_quads, 1)

    odd = (j_last % 2) == 1

    @pl.when(odd)
    def _two_left():
        logits(j_last, sb_sc)
        update(j_last - 1, sa_sc, False)
        update(j_last, sb_sc, True)

    @pl.when(jnp.logical_not(odd))
    def _one_left():
        update(j_last, sa_sc, True)

    acc = acc_sc[...]
    o_s = acc[:, 0:E] / jnp.maximum(acc[:, E:E + 1], TINY)

    for r in range(B_GQA):
        out = gated_cw[r] + gate_s[r] * o_s[r * Q_BLOCK:(r + 1) * Q_BLOCK]
        o_ref[:, r * E:(r + 1) * E] = out.astype(o_ref.dtype)


def _nsa(u_nat, u_f32, cmp_kv, batch, seq):
    nq = seq // Q_BLOCK
    n_chunk = seq // CMP_STRIDE
    n_cmp = n_chunk - CMP_LEN // CMP_STRIDE + 1
    n_slc = seq // SLC_LEN
    n_sel = min(SLC_TOPK, n_slc)
    tk = min(512, seq)
    assert n_slc <= LANES and B_HEADS == 8 and seq % tk == 0
    c_start = np.arange(n_chunk) * CMP_STRIDE
    s_start = np.arange(LANES) * SLC_LEN
    overlap_t = ((c_start[None, :] < s_start[:, None] + SLC_LEN) & (c_start[None, :] + CMP_LEN > s_start[:, None])
                 & (np.arange(n_chunk)[None, :] < n_cmp) & (np.arange(LANES)[:, None] < n_slc))
    pos = np.arange(seq)
    kconst = np.zeros((seq, 2 * LANES), np.float32)
    kconst[pos, pos // SLC_LEN] = 1.0
    kconst[:, LANES + 0] = pos // SLC_LEN
    kconst[:, LANES + 1] = pos % SLC_LEN
    kconst[:, LANES + 2] = 1.0
    kconst[:, LANES + 3] = 1.0
    kconst[:, LANES + N_FEAT] = 1.0
    rows = B_GQA * Q_BLOCK
    slopes = jnp.exp2(-8.0 * jnp.arange(1, B_HEADS + 1, dtype=jnp.float32) / B_HEADS)
    slopes = jnp.repeat(slopes.reshape(B_KV_HEADS, B_GQA), Q_BLOCK, axis=1).reshape(B_KV_HEADS, rows, 1)
    kv_rows = pl.BlockSpec((seq, LANES), lambda b, g, n: (b, NAT_SLC + g))
    win_rows = pl.BlockSpec((seq, LANES), lambda b, g, n: (b, NAT_WIN + g))
    cmp_spec = lambda s: pl.BlockSpec((None, None, None, n_chunk, HEAD_DIM), lambda b, g, n: (s, b, g, 0, 0))
    return pl.pallas_call(
        functools.partial(_nsa_kernel, n_sel=n_sel, tk=tk),
        grid=(batch, B_KV_HEADS, nq),
        in_specs=[pl.BlockSpec((Q_BLOCK, B_GQA * HEAD_DIM), lambda b, g, n: (b * nq + n, NAT_Q_B + g)),
                  cmp_spec(0), cmp_spec(1), kv_rows, win_rows,
                  pl.BlockSpec((Q_BLOCK, LANES), lambda b, g, n: (b * nq + n, F_GATE + g)),
                  pl.BlockSpec((LANES, n_chunk), lambda b, g, n: (0, 0)),
                  pl.BlockSpec((seq, 2 * LANES), lambda b, g, n: (0, 0)),
                  pl.BlockSpec((None, rows, 1), lambda b, g, n: (g, 0, 0))],
        out_specs=pl.BlockSpec((Q_BLOCK, B_GQA * HEAD_DIM), lambda b, g, n: (b * nq + n, g)),
        out_shape=jax.ShapeDtypeStruct((batch * seq, B_WIDTH), jnp.bfloat16),
        scratch_shapes=[pltpu.VMEM((seq, 2 * LANES), jnp.bfloat16), pltpu.VMEM((seq, LANES), jnp.bfloat16),
                        pltpu.VMEM((seq, LANES), jnp.bfloat16), pltpu.VMEM((seq, LANES), jnp.bfloat16),
                        pltpu.VMEM((rows, LANES), jnp.float32), pltpu.VMEM((rows, LANES), jnp.float32),
                        pltpu.VMEM((rows, 2 * LANES), jnp.bfloat16),
                        pltpu.VMEM((rows, tk), jnp.float32), pltpu.VMEM((rows, tk), jnp.float32)],
        compiler_params=_params(("arbitrary", "arbitrary", "arbitrary")),
    )(u_nat, cmp_kv, cmp_kv, u_nat, u_nat, u_f32, jnp.asarray(overlap_t, jnp.bfloat16), jnp.asarray(kconst, jnp.bfloat16),
      slopes)


def _layer_norm(v, gain, bias):
    mu = jnp.mean(v, axis=1, keepdims=True)
    c = v - mu
    var = jnp.mean(c * c, axis=1, keepdims=True)
    return c * lax.rsqrt(var + LN_EPS) * gain + bias


def _sigmoid(v):
    return 1.0 / (1.0 + jnp.exp(-v))


MERGE_SPLIT = 2


def _token_major(ref, stage):
    dil, n = ref.shape[0], ref.shape[1]
    parts = stage.shape[0]
    for r in range(dil):
        for j in range(parts):
            stage[j, pl.ds(r, n, stride=dil), :] = ref[r, :, j * LANES:(j + 1) * LANES]
    return jnp.concatenate([stage[j] for j in range(parts)], axis=1)


def _merge_kernel(*refs):
    dil_refs, refs = refs[:2 * N_DIL], refs[2 * N_DIL:]
    (yb_ref, mga_ref, mgb_ref, x_ref, wa_ref, wb_ref, wo_ref, g_ref, b_ref, wr_hi_ref, wr_lo_ref, br_ref,
     h_ref, ht_ref, rw_ref, re_ref, cnt_ref) = refs[:17]
    stages, cnt_sc = refs[17:17 + 2 * (N_DIL - 1)], refs[17 + 2 * (N_DIL - 1)]
    f32 = jnp.float32
    outs, lse = [dil_refs[0][...]], [dil_refs[1][...]]
    for g in range(1, N_DIL):
        outs.append(_token_major(dil_refs[2 * g], stages[2 * g - 2]))
        lse.append(_token_major(dil_refs[2 * g + 1], stages[2 * g - 1]))
    def mixed_rows(rs):
        lse_r = [v[rs] for v in lse]
        top = jnp.maximum(jnp.maximum(lse_r[0], lse_r[1]), lse_r[2])
        e = [jnp.exp(v - top) for v in lse_r]
        y_a = (e[0] * outs[0][rs] + e[1] * outs[1][rs] + e[2] * outs[2][rs]) / (e[0] + e[1] + e[2])
        br_a = jnp.dot(y_a.astype(jnp.bfloat16), wa_ref[...], preferred_element_type=f32)
        br_b = jnp.dot(yb_ref[rs, :], wb_ref[...], preferred_element_type=f32)
        merged = _sigmoid(mga_ref[rs, :]) * br_a + _sigmoid(mgb_ref[rs, :]) * br_b
        mix = jnp.dot(merged.astype(jnp.bfloat16), wo_ref[...], preferred_element_type=f32)
        return _layer_norm(ALPHA * x_ref[rs, :] + mix, g_ref[...], b_ref[...])

    tm = x_ref.shape[0]
    part = tm // MERGE_SPLIT
    h = jnp.concatenate([mixed_rows(slice(k * part, (k + 1) * part)) for k in range(MERGE_SPLIT)], axis=0)
    h_ref[...] = h
    for c in range(CHUNKS):
        ht_ref[:, c, :] = h[:, c * LANES:(c + 1) * LANES]

    h_hi = h.astype(jnp.bfloat16)
    h_lo = (h - h_hi.astype(f32)).astype(jnp.bfloat16)
    logits = (jnp.dot(h_hi, wr_hi_ref[...], preferred_element_type=f32)
              + jnp.dot(h_lo, wr_hi_ref[...], preferred_element_type=f32)
              + jnp.dot(h_hi, wr_lo_ref[...], preferred_element_type=f32)) + br_ref[...]
    lane = lax.broadcasted_iota(jnp.int32, logits.shape, 1)
    coarse = jnp.where(lane < N_GROUPS, logits, -jnp.inf)
    c_max = jnp.max(coarse, axis=1, keepdims=True)
    grp = jnp.min(jnp.where(coarse == c_max, lane, LANES), axis=1, keepdims=True)
    p_grp = 1.0 / jnp.sum(jnp.exp(coarse - c_max), axis=1, keepdims=True)
    lo_lane = N_GROUPS + EXP_PER_GROUP * grp
    fine = jnp.where((lane >= lo_lane) & (lane < lo_lane + EXP_PER_GROUP), logits, -jnp.inf)
    v1 = jnp.max(fine, axis=1, keepdims=True)
    i1 = jnp.min(jnp.where(fine == v1, lane, LANES), axis=1, keepdims=True)
    fine = jnp.where(lane == i1, -jnp.inf, fine)
    v2 = jnp.max(fine, axis=1, keepdims=True)
    i2 = jnp.min(jnp.where(fine == v2, lane, LANES), axis=1, keepdims=True)
    e2 = jnp.exp(v2 - v1)
    w1 = p_grp / (1.0 + e2)
    w2 = p_grp * e2 / (1.0 + e2)
    e_1, e_2 = i1 - N_GROUPS, i2 - N_GROUPS
    rw_ref[...] = jnp.where(lane == 0, w1, jnp.where(lane == 1, w2, 0.0))

    @pl.when(pl.program_id(0) == 0)
    def _zero_counts():
        cnt_sc[...] = jnp.zeros(cnt_sc.shape, f32)

    tm = logits.shape[0]
    onehot = jnp.where((lane == e_1) | (lane == e_2), 1.0, 0.0)
    earlier = (lax.broadcasted_iota(jnp.int32, (tm, tm), 0) > lax.broadcasted_iota(jnp.int32, (tm, tm), 1))
    before = jnp.dot(jnp.where(earlier, 1.0, 0.0).astype(jnp.bfloat16), onehot.astype(jnp.bfloat16),
                     preferred_element_type=f32) + cnt_sc[...]
    r_1 = jnp.sum(jnp.where(lane == e_1, before, 0.0), axis=1, keepdims=True).astype(jnp.int32)
    r_2 = jnp.sum(jnp.where(lane == e_2, before, 0.0), axis=1, keepdims=True).astype(jnp.int32)
    cnt_sc[...] = cnt_sc[...] + jnp.sum(onehot, axis=0, keepdims=True)
    cnt_ref[...] = cnt_sc[...].astype(jnp.int32)
    re_ref[...] = jnp.where(lane == 0, e_1, jnp.where(lane == 1, e_2,
                            jnp.where(lane == 2, r_1, jnp.where(lane == 3, r_2, 0))))


def _merge(dil_outs, y_b, u_f32, x2d, seq, wa, wb, wo, ln_g, ln_b, wr_hi, wr_lo, b_r, tm=256):
    t = x2d.shape[0]
    per_batch = seq // tm
    assert seq % tm == 0 and all(tm % (dil * 8) == 0 for _, dil in DIL_GROUPS)
    row = lambda w: pl.BlockSpec((tm, w), lambda i: (i, 0))
    full = lambda a: pl.BlockSpec(a.shape, lambda i: (0,) * a.ndim)
    dil_specs = [pl.BlockSpec((None, None, tm, A_OUT), lambda i: (i // per_batch, 0, i % per_batch, 0))] * 2
    stages = []
    for _, dil in DIL_GROUPS[1:]:
        dil_specs += [pl.BlockSpec((None, dil, tm // dil, A_OUT),
                                   lambda i: (i // per_batch, 0, i % per_batch, 0))] * 2
        stages += [pltpu.VMEM((A_OUT // LANES, tm, LANES), jnp.float32)] * 2
    return pl.pallas_call(
        _merge_kernel,
        grid=(t // tm,),
        in_specs=dil_specs + [row(B_WIDTH),
                  pl.BlockSpec((tm, D_MODEL), lambda i: (i, F_MGATE)),
                  pl.BlockSpec((tm, D_MODEL), lambda i: (i, F_MGATE + 1)),
                  row(D_MODEL), full(wa), full(wb), full(wo), full(ln_g), full(ln_b),
                  full(wr_hi), full(wr_lo), full(b_r)],
        out_specs=[row(D_MODEL), pl.BlockSpec((tm, CHUNKS, LANES), lambda i: (0, i, 0)), row(LANES), row(LANES),
                   pl.BlockSpec((1, LANES), lambda i: (0, 0))],
        out_shape=[jax.ShapeDtypeStruct((t, D_MODEL), jnp.float32),
                   jax.ShapeDtypeStruct((tm, (t // tm) * CHUNKS, LANES), jnp.float32),
                   jax.ShapeDtypeStruct((t, LANES), jnp.float32),
                   jax.ShapeDtypeStruct((t, LANES), jnp.int32),
                   jax.ShapeDtypeStruct((1, LANES), jnp.int32)],
        scratch_shapes=stages + [pltpu.VMEM((1, LANES), jnp.float32)],
        compiler_params=_params(("arbitrary",)),
    )(*dil_outs, y_b, u_f32, u_f32, x2d, wa, wb, wo, ln_g, ln_b, wr_hi, wr_lo, b_r)


CHUNKS = D_MODEL // LANES
GATHER_UNROLL = 8


def _to_token_tiles(ref2d, value):
    n = value.shape[0]
    for c in range(CHUNKS):
        ref2d[pl.ds(c, n, stride=CHUNKS), :] = value[:, c * LANES:(c + 1) * LANES]


def _from_token_tiles(ref2d):
    n = ref2d.shape[0] // CHUNKS
    return jnp.concatenate([ref2d[pl.ds(c, n, stride=CHUNKS), :] for c in range(CHUNKS)], axis=1)


def _token_copy(idx_ref, k, src2d, dst2d, r, sem):
    src_row = pl.multiple_of(idx_ref[0, 0, k] * CHUNKS, CHUNKS)
    dst_row = r * CHUNKS if isinstance(r, int) else pl.multiple_of(r * CHUNKS, CHUNKS)
    return pltpu.make_async_copy(src2d.at[pl.ds(src_row, CHUNKS), :], dst2d.at[pl.ds(dst_row, CHUNKS), :], sem)


def _start_token_gather(idx_ref, idx_of, src2d, dst2d, sem, unrolled, priority_of=lambda r: 0):
    n = dst2d.shape[0] // CHUNKS
    if unrolled:
        for r in range(n):
            _token_copy(idx_ref, idx_of(r), src2d, dst2d, r, sem).start(priority=priority_of(r))
    else:
        def start(r, carry):
            _token_copy(idx_ref, idx_of(r), src2d, dst2d, r, sem).start()
            return carry

        lax.fori_loop(0, n, start, 0, unroll=GATHER_UNROLL)


def _wait_token_gather(src2d, dst2d, sem):
    pltpu.make_async_copy(src2d.at[pl.ds(0, dst2d.shape[0]), :], dst2d, sem).wait()


GATHER_BUFFERS = 3


def _gather_ring(i, first_refs, ahead_ref, start, wait, compute):
    last = pl.num_programs(0) - 1

    @pl.when(i == 0)
    def _prologue():
        for k, ref in enumerate(first_refs):
            start(ref, k, unrolled=False)

    def step(k):
        ahead = (k + GATHER_BUFFERS - 1) % GATHER_BUFFERS
        wait(k)
        start(ahead_ref, ahead, unrolled=True)
        compute(k)

        @pl.when(i == last)
        def _drain():
            for other in range(GATHER_BUFFERS):
                if other != k:
                    wait(other)

    for k in range(GATHER_BUFFERS):
        pl.when(i % GATHER_BUFFERS == k)(functools.partial(step, k))


def _expert_kernel(blk_e_ref, tok_ref, tok_1_ref, tok_2_ref, hs_hbm, wgu_ref, wd_ref, y_ref, *scratch):
    bufs, sems = scratch[:GATHER_BUFFERS], scratch[GATHER_BUFFERS]
    rows = hs_hbm.shape[0]
    shift = rows.bit_length() - 1

    def token_copy(idx_ref, r, k):
        t = idx_ref[0, 0, r]
        first = pl.multiple_of((t >> shift) * CHUNKS, CHUNKS)
        dst_row = r * CHUNKS if isinstance(r, int) else pl.multiple_of(r * CHUNKS, CHUNKS)
        return pltpu.make_async_copy(hs_hbm.at[t & (rows - 1), pl.ds(first, CHUNKS), :],
                                     bufs[k].at[pl.ds(dst_row, CHUNKS), :], sems.at[k])

    def start(idx_ref, k, unrolled):
        n = bufs[k].shape[0] // CHUNKS
        if unrolled:
            for r in range(n):
                token_copy(idx_ref, r, k).start(priority=r % 2)
        else:
            def one(r, carry):
                token_copy(idx_ref, r, k).start()
                return carry

            lax.fori_loop(0, n, one, 0, unroll=GATHER_UNROLL)

    def wait(k):
        pltpu.make_async_copy(bufs[k], bufs[k], sems.at[k]).wait()

    def compute(k):
        xb = _from_token_tiles(bufs[k]).astype(jnp.bfloat16)
        gu = jnp.dot(xb, wgu_ref[...], preferred_element_type=jnp.float32)
        gate, up = gu[:, :D_EXPERT], gu[:, D_EXPERT:]
        act = (gate * _sigmoid(gate)) * up
        _to_token_tiles(y_ref, jnp.dot(act.astype(jnp.bfloat16), wd_ref[...], preferred_element_type=jnp.float32))

    _gather_ring(pl.program_id(0), (tok_ref, tok_1_ref), tok_2_ref, start, wait, compute)


def _experts(h_spread, tok_buf, blk_e, wgu, wd, bm):
    p = tok_buf.shape[0]
    nb = p // bm
    rows = h_spread.shape[0]
    assert rows & (rows - 1) == 0
    tok = tok_buf.reshape(nb, 1, bm)
    idx_spec = lambda ahead: pl.BlockSpec((1, 1, bm), lambda i, e: (jnp.minimum(i + ahead, nb - 1), 0, 0),
                                          memory_space=pltpu.SMEM)
    return pl.pallas_call(
        _expert_kernel,
        grid_spec=pltpu.PrefetchScalarGridSpec(
            num_scalar_prefetch=1,
            grid=(nb,),
            in_specs=[idx_spec(0), idx_spec(1), idx_spec(2),
                      pl.BlockSpec(memory_space=pl.ANY),
                      pl.BlockSpec((None, D_MODEL, 2 * D_EXPERT), lambda i, e: (e[i], 0, 0)),
                      pl.BlockSpec((None, D_EXPERT, D_MODEL), lambda i, e: (e[i], 0, 0))],
            out_specs=pl.BlockSpec((bm * CHUNKS, LANES), lambda i, e: (i, 0)),
            scratch_shapes=[pltpu.VMEM((bm * CHUNKS, LANES), jnp.float32)] * GATHER_BUFFERS
                           + [pltpu.SemaphoreType.DMA((GATHER_BUFFERS,))],
        ),
        out_shape=jax.ShapeDtypeStruct((p * CHUNKS, LANES), jnp.float32),
        compiler_params=_params(("arbitrary",)),
    )(blk_e, tok, tok, tok, h_spread, wgu, wd)


def _combine_kernel(dest_ref, dest_1_ref, dest_2_ref, yt_hbm, h_ref, rw_ref, g_ref, b_ref, o_ref, *scratch):
    bufs, sems = scratch[:GATHER_BUFFERS], scratch[GATHER_BUFFERS]

    def start(idx_ref, k, unrolled):
        for slot in range(MOE_TOP_K):
            _start_token_gather(idx_ref, lambda r, slot=slot: MOE_TOP_K * r + slot, yt_hbm, bufs[k].at[slot],
                                sems.at[k], unrolled, priority_of=lambda r: r % 2)

    def wait(k):
        for slot in range(MOE_TOP_K):
            _wait_token_gather(yt_hbm, bufs[k].at[slot], sems.at[k])

    def compute(k):
        rw = rw_ref[...]
        ffn = rw[:, 0:1] * _from_token_tiles(bufs[k].at[0]) + rw[:, 1:2] * _from_token_tiles(bufs[k].at[1])
        o_ref[...] = _layer_norm(ALPHA * h_ref[...] + ffn, g_ref[...], b_ref[...])

    _gather_ring(pl.program_id(0), (dest_ref, dest_1_ref), dest_2_ref, start, wait, compute)


def _combine(dest, y_tiles, h, rw, ln_g, ln_b, tm=256):
    t = h.shape[0]
    assert t % tm == 0
    nt = t // tm
    idx = dest.reshape(nt, 1, MOE_TOP_K * tm)
    idx_spec = lambda ahead: pl.BlockSpec((1, 1, MOE_TOP_K * tm), lambda i: (jnp.minimum(i + ahead, nt - 1), 0, 0),
                                          memory_space=pltpu.SMEM)
    return pl.pallas_call(
        _combine_kernel,
        grid=(nt,),
        in_specs=[idx_spec(0), idx_spec(1), idx_spec(2),
                  pl.BlockSpec(memory_space=pl.ANY),
                  pl.BlockSpec((tm, D_MODEL), lambda i: (i, 0)),
                  pl.BlockSpec((tm, LANES), lambda i: (i, 0)),
                  pl.BlockSpec((1, D_MODEL), lambda i: (0, 0)),
                  pl.BlockSpec((1, D_MODEL), lambda i: (0, 0))],
        out_specs=pl.BlockSpec((tm, D_MODEL), lambda i: (i, 0)),
        out_shape=jax.ShapeDtypeStruct((t, D_MODEL), jnp.float32),
        scratch_shapes=[pltpu.VMEM((MOE_TOP_K, tm * CHUNKS, LANES), jnp.float32)] * GATHER_BUFFERS
                       + [pltpu.SemaphoreType.DMA((GATHER_BUFFERS,))],
        compiler_params=_params(("arbitrary",)),
    )(idx, idx, idx, y_tiles, h, rw, ln_g, ln_b)


def _dispatch(route, counts, bm):
    t = route.shape[0]
    n_assign = t * MOE_TOP_K
    expert, rank = route[:, 0:MOE_TOP_K], route[:, MOE_TOP_K:2 * MOE_TOP_K]
    starts = jnp.cumsum(counts) - counts
    pcounts = (counts + bm - 1) // bm * bm
    pends = jnp.cumsum(pcounts)
    pstarts = pends - pcounts
    ids = jnp.arange(N_EXPERTS, dtype=jnp.int32)
    dest = jnp.sum(jnp.where(expert[..., None] == ids, pstarts, 0), axis=-1) + rank
    p = -(-(n_assign + N_EXPERTS * (bm - 1)) // bm) * bm
    nb = p // bm
    blk_e = jnp.minimum(jnp.sum(pends[None, :] <= (jnp.arange(nb, dtype=jnp.int32) * bm)[:, None], axis=1),
                        N_EXPERTS - 1).astype(jnp.int32)
    order = jnp.argsort(expert.reshape(n_assign), stable=True).astype(jnp.int32)
    row_e = jnp.repeat(blk_e, bm)
    k = jnp.arange(p, dtype=jnp.int32) - pstarts[row_e]
    src = jnp.clip(starts[row_e] + k, 0, n_assign - 1)
    tok_buf = jnp.where(k < counts[row_e], order[src] // MOE_TOP_K, 0).astype(jnp.int32)
    return tok_buf, dest.astype(jnp.int32), blk_e


def _split_bf16(w):
    hi = w.astype(jnp.bfloat16)
    return hi, (w - hi.astype(jnp.float32)).astype(jnp.bfloat16)


def kernel(x, w_in, cmp_pos_k, cmp_w1_k, cmp_w2_k, cmp_pos_v, cmp_w1_v, cmp_w2_v, w_branch_a, w_branch_b, w_out, ln1_g, ln1_b, w_coarse, b_coarse, w_fine, b_fine, w_gate_up, w_down, ln2_g, ln2_b):
    batch, seq, d = x.shape
    bf16 = jnp.bfloat16
    h = x.reshape(batch * seq, d)
    cols_bf, cols_f32, scale_bf = _in_proj_columns()
    half = CMP_STRIDE * HEAD_DIM
    for l in range(DEPTH):
        u_nat, *groups = _project_bf(h, (_permute_columns(w_in[l], cols_bf) * scale_bf).astype(bf16), batch, seq)
        u_f32, ckv = _project_f32(h, _permute_columns(w_in[l], cols_f32).astype(bf16), batch, seq)
        dil_outs = _dilated(u_nat, groups, batch, seq)
        pos = jnp.stack([cmp_pos_k[l], cmp_pos_v[l]]).reshape(2, 2, 1, half)
        w1 = jnp.stack([cmp_w1_k[l], cmp_w1_v[l]]).astype(bf16)
        w2 = jnp.stack([cmp_w2_k[l], cmp_w2_v[l]]).astype(bf16)
        cmp_kv = _compress(ckv, pos, w1, w2)
        y_b = _nsa(u_nat, u_f32, cmp_kv, batch, seq)
        w_r = jnp.concatenate([w_coarse[l], w_fine[l].transpose(1, 0, 2).reshape(d, N_EXPERTS)], axis=1)
        w_r = jnp.pad(w_r, ((0, 0), (0, LANES - w_r.shape[1])))
        b_r = jnp.pad(jnp.concatenate([b_coarse[l], b_fine[l].reshape(N_EXPERTS)]),
                      (0, LANES - N_GROUPS - N_EXPERTS)).reshape(1, LANES)
        wr_hi, wr_lo = _split_bf16(w_r)
        h1, h1_spread, rw, route, counts = _merge(dil_outs, y_b, u_f32, h, seq, w_branch_a[l].astype(bf16),
                                       w_branch_b[l].astype(bf16), w_out[l].astype(bf16),
                                       ln1_g[l].reshape(1, d), ln1_b[l].reshape(1, d), wr_hi, wr_lo, b_r)
        bm = 256
        tok_buf, dest, blk_e = _dispatch(route, counts[0, :N_EXPERTS], bm)
        y_tiles = _experts(h1_spread, tok_buf, blk_e, w_gate_up[l].astype(bf16), w_down[l].astype(bf16), bm)
        h = _combine(dest, y_tiles, h1, rw, ln2_g[l].reshape(1, d), ln2_b[l].reshape(1, d))
    return h.reshape(batch, seq, d)
```

```python
import functools

import numpy as np
import jax
import jax.numpy as jnp
from jax import lax
from jax.experimental import pallas as pl
from jax.experimental.pallas import tpu as pltpu

D_MODEL = 1024
HEAD_DIM = 64
Q_BLOCK = 128
DIL_GROUPS = ((128, 1), (512, 4), (2048, 16))
N_DIL = len(DIL_GROUPS)
A_HEADS_PER_GROUP = 4
A_HEADS = A_HEADS_PER_GROUP * N_DIL
A_WIDTH = A_HEADS * HEAD_DIM
A_OUT = A_HEADS_PER_GROUP * HEAD_DIM
B_HEADS = 8
B_KV_HEADS = 2
B_GQA = B_HEADS // B_KV_HEADS
B_WIDTH = B_HEADS * HEAD_DIM
B_KV_WIDTH = B_KV_HEADS * HEAD_DIM
CMP_LEN = 32
CMP_STRIDE = 16
CMP_HIDDEN = 256
SLC_LEN = 64
SLC_TOPK = 16
WIN = 512
N_GROUPS = 4
EXP_PER_GROUP = 8
N_EXPERTS = N_GROUPS * EXP_PER_GROUP
D_EXPERT = 512
MOE_TOP_K = 2
DEPTH = 1
ALPHA = (2.0 * DEPTH) ** 0.25
LN_EPS = 1e-5
FORCE_BONUS = 1e4
TINY = 1e-30
ATTN_SCALE = HEAD_DIM ** -0.5

LANES = 128
VMEM_LIMIT = 48 * 1024 * 1024

QKV_COLS = 3 * A_OUT
NAT_Q_B = QKV_COLS // A_OUT
NAT_SLC = (QKV_COLS + B_WIDTH) // LANES
NAT_WIN = NAT_SLC + B_KV_HEADS
NAT_COLS = QKV_COLS + B_WIDTH + 4 * B_KV_WIDTH
BF_COLS = NAT_COLS + (N_DIL - 1) * QKV_COLS
F_MGATE = 0
F_GATE = 2 * D_MODEL // LANES
F_NAT_COLS = 2 * D_MODEL + B_KV_HEADS * LANES
F_COLS = F_NAT_COLS + 2 * B_KV_WIDTH


def _params(semantics):
    return pltpu.CompilerParams(dimension_semantics=semantics, vmem_limit_bytes=VMEM_LIMIT)


def _in_proj_columns():
    kv_off = 3 * A_WIDTH + B_WIDTH
    gate_off = kv_off + 6 * B_KV_WIDTH
    mg_off = gate_off + 3 * B_HEADS
    def qkv_cols(g):
        return [part * A_WIDTH + g * A_OUT + c for part in range(3) for c in range(A_OUT)]

    bf = qkv_cols(0) + list(range(3 * A_WIDTH, 3 * A_WIDTH + B_WIDTH))
    for first in (2, 4):
        for g in range(B_KV_HEADS):
            for i in (first, first + 1):
                base = kv_off + i * B_KV_WIDTH + g * HEAD_DIM
                bf += list(range(base, base + HEAD_DIM))
    for g in range(1, N_DIL):
        bf += qkv_cols(g)
    f32 = list(range(mg_off, mg_off + 2 * D_MODEL))
    per_head = 3 * B_GQA
    for g in range(B_KV_HEADS):
        f32 += list(range(gate_off + g * per_head, gate_off + (g + 1) * per_head))
        f32 += [-1] * (LANES - per_head)
    f32 += list(range(kv_off, kv_off + 2 * B_KV_WIDTH))
    assert len(bf) == BF_COLS and len(f32) == F_COLS
    bf = np.asarray(bf)
    is_q = (bf < A_WIDTH) | ((bf >= 3 * A_WIDTH) & (bf < 3 * A_WIDTH + B_WIDTH))
    scale = np.where(is_q, ATTN_SCALE, 1.0).astype(np.float32)
    return bf, np.asarray(f32), scale


def _permute_columns(w, cols):
    taken = jnp.take(w, jnp.asarray(np.maximum(cols, 0)), axis=1)
    return jnp.where(jnp.asarray(cols >= 0)[None, :], taken, 0.0)


def _proj_f32_kernel(x_ref, w_ref, nat_ref, cmp_ref, stage):
    xb = x_ref[...].astype(jnp.bfloat16)
    tm = x_ref.shape[0]
    n_chunk = tm // CMP_STRIDE
    for c0 in range(0, F_NAT_COLS, A_OUT):
        nat_ref[:, c0:c0 + A_OUT] = jnp.dot(xb, w_ref[:, c0:c0 + A_OUT], preferred_element_type=jnp.float32)
    res = jnp.dot(xb, w_ref[:, F_NAT_COLS:F_COLS], preferred_element_type=jnp.float32)
    lower = lax.broadcasted_iota(jnp.int32, (n_chunk, LANES), 1) < HEAD_DIM
    for kv in range(2):
        stage[...] = res[:, kv * LANES:(kv + 1) * LANES]
        for p in range(0, CMP_STRIDE, 2):
            even = stage[pl.ds(p, n_chunk, stride=CMP_STRIDE), :]
            odd = stage[pl.ds(p + 1, n_chunk, stride=CMP_STRIDE), :]
            cols = slice(p * HEAD_DIM, (p + 2) * HEAD_DIM)
            cmp_ref[kv, 0, :, cols] = jnp.where(lower, even, pltpu.roll(odd, HEAD_DIM, 1))
            cmp_ref[kv, 1, :, cols] = jnp.where(lower, pltpu.roll(even, HEAD_DIM, 1), odd)


def _project_f32(x2d, w, batch, seq, tm=512):
    t, d = x2d.shape
    assert seq % tm == 0 and w.shape[1] == F_COLS and B_KV_HEADS * HEAD_DIM == LANES
    per_batch = seq // tm
    half = CMP_STRIDE * HEAD_DIM
    return pl.pallas_call(
        _proj_f32_kernel,
        grid=(t // tm,),
        in_specs=[pl.BlockSpec((tm, d), lambda i: (i, 0)),
                  pl.BlockSpec((d, F_COLS), lambda i: (0, 0))],
        out_specs=[pl.BlockSpec((tm, F_NAT_COLS), lambda i: (i, 0)),
                   pl.BlockSpec((2, None, B_KV_HEADS, tm // CMP_STRIDE, half),
                                lambda i: (0, i // per_batch, 0, i % per_batch, 0))],
        out_shape=[jax.ShapeDtypeStruct((t, F_NAT_COLS), jnp.float32),
                   jax.ShapeDtypeStruct((2, batch, B_KV_HEADS, seq // CMP_STRIDE, half), jnp.float32)],
        scratch_shapes=[pltpu.VMEM((tm, LANES), jnp.float32)],
        compiler_params=_params(("parallel",)),
    )(x2d, w)


def _proj_bf_kernel(x_ref, w_ref, nat_ref, *rest, dils):
    group_refs, stage = rest[:-1], rest[-1]
    xb = x_ref[...].astype(jnp.bfloat16)
    tm = x_ref.shape[0]
    for c0 in range(0, NAT_COLS, A_OUT):
        nat_ref[:, c0:c0 + A_OUT] = jnp.dot(
            xb, w_ref[:, c0:c0 + A_OUT], preferred_element_type=jnp.float32).astype(nat_ref.dtype)
    col = NAT_COLS
    for o_ref, dil in zip(group_refs, dils):
        for part in range(3):
            res = jnp.dot(xb, w_ref[:, col:col + A_OUT], preferred_element_type=jnp.float32)
            for j in range(A_OUT // LANES):
                stage[j] = res[:, j * LANES:(j + 1) * LANES]
                c0 = part * A_OUT + j * LANES
                for r in range(dil):
                    o_ref[r, :, c0:c0 + LANES] = stage[j, pl.ds(r, tm // dil, stride=dil), :].astype(o_ref.dtype)
            col += A_OUT


def _project_bf(x2d, w, batch, seq, tm=512):
    t, d = x2d.shape
    dils = tuple(dil for _, dil in DIL_GROUPS[1:])
    assert seq % tm == 0 and all(tm % (dil * 16) == 0 for dil in dils) and w.shape[1] == BF_COLS
    per_batch = seq // tm
    out_specs = [pl.BlockSpec((tm, NAT_COLS), lambda i: (i, 0))]
    out_shapes = [jax.ShapeDtypeStruct((t, NAT_COLS), jnp.bfloat16)]
    for dil in dils:
        out_specs.append(pl.BlockSpec((None, dil, tm // dil, QKV_COLS),
                                      lambda i: (i // per_batch, 0, i % per_batch, 0)))
        out_shapes.append(jax.ShapeDtypeStruct((batch, dil, seq // dil, QKV_COLS), jnp.bfloat16))
    return pl.pallas_call(
        functools.partial(_proj_bf_kernel, dils=dils),
        grid=(t // tm,),
        in_specs=[pl.BlockSpec((tm, d), lambda i: (i, 0)),
                  pl.BlockSpec((d, BF_COLS), lambda i: (0, 0))],
        out_specs=out_specs,
        out_shape=out_shapes,
        scratch_shapes=[pltpu.VMEM((A_OUT // LANES, tm, LANES), jnp.float32)],
        compiler_params=_params(("parallel",)),
    )(x2d, w)


def _dilated_kernel(*refs, nbs):
    bias_ref, ins, outs = refs[0], refs[1:1 + 5 * N_DIL], refs[1 + 5 * N_DIL:]
    i = pl.program_id(1)
    dn = (((1,), (1,)), ((), ()))
    f32, bf16 = jnp.float32, jnp.bfloat16
    lower = lax.broadcasted_iota(jnp.int32, (Q_BLOCK, LANES), 1) < HEAD_DIM
    heads = [(g, pair, half) for g in range(N_DIL) for pair in range(A_OUT // LANES) for half in range(2)]

    def col(ref, pair):
        return ref[:, pair * LANES:(pair + 1) * LANES]

    logits = []
    for g, pair, half in heads:
        q_ref, kp_ref, kc_ref = ins[5 * g:5 * g + 3]
        head = g * A_HEADS_PER_GROUP + 2 * pair + half
        q = col(q_ref, pair)
        q = jnp.where(lower if half == 0 else jnp.logical_not(lower), q, jnp.zeros_like(q))
        no_prev = jnp.where((i % nbs[g]) > 0, 0.0, -jnp.inf)
        s_p = lax.dot_general(q, col(kp_ref, pair), dn, preferred_element_type=f32)
        s_c = lax.dot_general(q, col(kc_ref, pair), dn, preferred_element_type=f32)
        logits.append((s_p + bias_ref[head, :, 0:Q_BLOCK] + no_prev, s_c + bias_ref[head, :, Q_BLOCK:2 * Q_BLOCK]))

    probs = []
    for s_p, s_c in logits:
        m = jnp.max(jnp.maximum(s_p, s_c), axis=1, keepdims=True)
        p_p, p_c = jnp.exp(s_p - m), jnp.exp(s_c - m)
        l = jnp.sum(p_p + p_c, axis=1, keepdims=True)
        probs.append((p_p.astype(bf16), p_c.astype(bf16), l, m + jnp.log(l)))

    for n_head in range(0, len(heads), 2):
        g, pair, _ = heads[n_head]
        vp_ref, vc_ref = ins[5 * g + 3:5 * g + 5]
        o_ref, lse_ref = outs[2 * g:2 * g + 2]
        out, lse = [], []
        for p_p, p_c, l, lse_h in probs[n_head:n_head + 2]:
            acc = jnp.dot(p_p, col(vp_ref, pair), preferred_element_type=f32)
            acc += jnp.dot(p_c, col(vc_ref, pair), preferred_element_type=f32)
            out.append(acc / l)
            lse.append(jnp.broadcast_to(lse_h, (Q_BLOCK, LANES)))
        o_ref[:, pair * LANES:(pair + 1) * LANES] = jnp.where(lower, out[0], out[1])
        lse_ref[:, pair * LANES:(pair + 1) * LANES] = jnp.where(lower, lse[0], lse[1])


def _dilated_bias():
    qi = Q_BLOCK + jnp.arange(Q_BLOCK)
    kj = jnp.arange(2 * Q_BLOCK)
    delta = qi[:, None] - kj[None, :]
    slopes = jnp.exp2(-8.0 * jnp.arange(1, A_HEADS + 1, dtype=jnp.float32) / A_HEADS)
    tables = []
    for g, (window, dil) in enumerate(DIL_GROUPS):
        on_band = (delta >= 0) & (delta <= window // dil)
        dist = (delta * dil).astype(jnp.float32)
        for h in range(A_HEADS_PER_GROUP):
            tables.append(jnp.where(on_band, -slopes[g * A_HEADS_PER_GROUP + h] * dist, -jnp.inf))
    return jnp.stack(tables)


def _dilated(u_nat, groups, batch, seq):
    steps = seq // Q_BLOCK
    arrays = [u_nat.reshape(batch, 1, seq, NAT_COLS)] + list(groups)
    bias = _dilated_bias()
    in_arrays, in_specs = [bias], [pl.BlockSpec(bias.shape, lambda b, i: (0, 0, 0))]
    out_specs, out_shapes, nbs = [], [], []
    blk = (None, None, Q_BLOCK, A_OUT)
    for g, (window, dil) in enumerate(DIL_GROUPS):
        length = seq // dil
        assert length % Q_BLOCK == 0 and window // dil == Q_BLOCK
        nb = length // Q_BLOCK
        nbs.append(nb)

        def cur(b, i, nb=nb, col=0):
            return (b, i // nb, i % nb, col)

        def prev(b, i, nb=nb, col=0):
            return (b, i // nb, jnp.maximum(i % nb - 1, 0), col)

        for fn, col in ((cur, 0), (prev, 1), (cur, 1), (prev, 2), (cur, 2)):
            in_arrays.append(arrays[g])
            in_specs.append(pl.BlockSpec(blk, functools.partial(fn, col=col)))
        for _ in range(2):
            out_specs.append(pl.BlockSpec(blk, functools.partial(cur, col=0)))
            out_shapes.append(jax.ShapeDtypeStruct((batch, dil, length, A_OUT), jnp.float32))
    return pl.pallas_call(
        functools.partial(_dilated_kernel, nbs=tuple(nbs)),
        grid=(batch, steps),
        in_specs=in_specs,
        out_specs=out_specs,
        out_shape=out_shapes,
        compiler_params=_params(("parallel", "parallel")),
    )(*in_arrays)


def _compress_kernel(x_ref, pos_ref, w1_ref, w2_ref, o_ref):
    half = CMP_STRIDE * HEAD_DIM
    x = x_ref[...]
    n_chunk = x.shape[0]
    lo = (x + pos_ref[0]).astype(jnp.bfloat16)
    hi = (x + pos_ref[1]).astype(jnp.bfloat16)
    h_lo = jnp.dot(lo, w1_ref[0:half, :], preferred_element_type=jnp.float32)
    h_hi = jnp.dot(hi, w1_ref[half:2 * half, :], preferred_element_type=jnp.float32)
    h = h_lo + pltpu.roll(h_hi, n_chunk - 1, 0)
    act = jax.nn.gelu(h, approximate=True)
    o_ref[...] = jnp.dot(act.astype(jnp.bfloat16), w2_ref[...],
                         preferred_element_type=jnp.float32).astype(o_ref.dtype)


def _compress(ckv, pos, w1, w2):
    _, batch, _, n_chunk, half = ckv.shape
    return pl.pallas_call(
        _compress_kernel,
        grid=(2, batch, B_KV_HEADS),
        in_specs=[pl.BlockSpec((None, None, None, n_chunk, half), lambda s, b, g: (s, b, g, 0, 0)),
                  pl.BlockSpec((None, 2, 1, half), lambda s, b, g: (s, 0, 0, 0)),
                  pl.BlockSpec((None, 2 * half, CMP_HIDDEN), lambda s, b, g: (s, 0, 0)),
                  pl.BlockSpec((None, CMP_HIDDEN, HEAD_DIM), lambda s, b, g: (s, 0, 0))],
        out_specs=pl.BlockSpec((None, None, None, n_chunk, HEAD_DIM), lambda s, b, g: (s, b, g, 0, 0)),
        out_shape=jax.ShapeDtypeStruct((2, batch, B_KV_HEADS, n_chunk, HEAD_DIM), jnp.bfloat16),
        compiler_params=_params(("parallel", "parallel", "parallel")),
    )(ckv, pos, w1, w2)


NEG = -2.0 ** 60
N_FEAT = HEAD_DIM


def _nsa_kernel(q_ref, kc_ref, vc_ref, skv_ref, wkv_ref, gate_ref, ovt_ref, kconst_ref, slope_ref, o_ref,
                kaug_sc, vaug_sc, kwin_sc, vwin_sc, m_sc, acc_sc, qaug_sc, sa_sc, sb_sc, *, n_sel, tk):
    n = pl.program_id(2)
    t0 = n * Q_BLOCK
    rows = B_GQA * Q_BLOCK
    seq = skv_ref.shape[0]
    dn = (((1,), (1,)), ((), ()))
    f32, bf16 = jnp.float32, jnp.bfloat16
    E = HEAD_DIM

    @pl.when(n == 0)
    def _build_keys():
        kaug_sc[:, 0:LANES] = kconst_ref[:, 0:LANES]
        kaug_sc[:, LANES:LANES + E] = skv_ref[:, 0:E]
        kaug_sc[:, LANES + E:2 * LANES] = kconst_ref[:, LANES:LANES + N_FEAT]
        vaug_sc[:, 0:E] = skv_ref[:, E:2 * E]
        vaug_sc[:, E:LANES] = kconst_ref[:, LANES + N_FEAT:2 * LANES]
        kwin_sc[:, 0:E] = wkv_ref[:, 0:E]
        kwin_sc[:, E:LANES] = kconst_ref[:, LANES:LANES + N_FEAT]
        vwin_sc[:, 0:E] = wkv_ref[:, E:2 * E]
        vwin_sc[:, E:LANES] = kconst_ref[:, LANES + N_FEAT:2 * LANES]

    q = q_ref[...]
    qs = jnp.concatenate([q[:, r * E:(r + 1) * E] for r in range(B_GQA)], axis=0)
    row = lax.broadcasted_iota(jnp.int32, (rows, 1), 0)
    within = row & (Q_BLOCK - 1)
    tq = t0 + within
    slope = slope_ref[...]
    lane = lax.broadcasted_iota(jnp.int32, (rows, N_FEAT), 1)
    feat = jnp.where(lane == 0, slope * SLC_LEN,
           jnp.where(lane == 1, slope,
           jnp.where(lane == 2, -slope * t0.astype(f32),
           jnp.where(lane == 3, -slope * within.astype(f32), 0.0))))
    q_feat = jnp.concatenate([qs, feat.astype(bf16)], axis=1)

    n_chunk = kc_ref.shape[0]
    s = lax.dot_general(qs, kc_ref[...], dn, preferred_element_type=f32)
    cmp_end = lax.broadcasted_iota(jnp.int32, (1, n_chunk), 1) * CMP_STRIDE + (CMP_LEN - 1)
    s = jnp.where(cmp_end <= tq, s, -jnp.inf)
    m = jnp.max(s, axis=1, keepdims=True)
    m = jnp.where(m == -jnp.inf, 0.0, m)
    p = jnp.exp(s - m)
    l = jnp.maximum(jnp.sum(p, axis=1, keepdims=True), TINY)
    p_c = (p / l).astype(bf16)
    o_c = jnp.dot(p_c, vc_ref[...], preferred_element_type=f32)

    span = min(WIN + Q_BLOCK, seq)
    start = pl.multiple_of(jnp.maximum(t0 + Q_BLOCK - span, 0), Q_BLOCK)
    s = lax.dot_general(q_feat, kwin_sc[pl.ds(start, span), :], dn, preferred_element_type=f32)
    dw = tq - (start + lax.broadcasted_iota(jnp.int32, (1, span), 1))
    s = jnp.where((dw >= 0) & (dw < WIN), s, -jnp.inf)
    m = jnp.max(s, axis=1, keepdims=True)
    p = jnp.exp(s - m).astype(bf16)
    acc = jnp.dot(p, vwin_sc[pl.ds(start, span), :], preferred_element_type=f32)
    o_w = acc[:, 0:E] / jnp.maximum(acc[:, E:E + 1], TINY)

    gates = 1.0 / (1.0 + jnp.exp(-gate_ref[...]))
    gate_of = lambda r, branch: jnp.broadcast_to(gates[:, 3 * r + branch:3 * r + branch + 1], (Q_BLOCK, E))
    gate_s = [gate_of(r, 1) for r in range(B_GQA)]
    gated_cw = [gate_of(r, 0) * o_c[r * Q_BLOCK:(r + 1) * Q_BLOCK] + gate_of(r, 2) * o_w[r * Q_BLOCK:(r + 1) * Q_BLOCK]
                for r in range(B_GQA)]

    imp = lax.dot_general(ovt_ref[...], p_c[0:Q_BLOCK], dn, preferred_element_type=f32)
    for r in range(1, B_GQA):
        imp += lax.dot_general(ovt_ref[...], p_c[r * Q_BLOCK:(r + 1) * Q_BLOCK], dn, preferred_element_type=f32)
    blk = lax.broadcasted_iota(jnp.int32, (LANES, Q_BLOCK), 0)
    t = t0 + lax.broadcasted_iota(jnp.int32, (LANES, Q_BLOCK), 1)
    cur = t >> 6
    forced = (blk == 0) | (blk == cur) | (blk == cur - 1)
    score = jnp.where(blk * SLC_LEN <= t, imp + jnp.where(forced, FORCE_BONUS, 0.0), -jnp.inf)

    def pick(_, carry):
        work, sel = carry
        best = jnp.max(work, axis=0, keepdims=True)
        idx = jnp.min(jnp.where(work == best, blk, LANES), axis=0, keepdims=True)
        hit = blk == idx
        finite = jnp.where(best > -jnp.inf, 1.0, 0.0)
        sel = jnp.where(hit, jnp.maximum(sel, finite), sel)
        return jnp.where(hit, -jnp.inf, work), sel

    _, sel = lax.fori_loop(0, n_sel, pick, (score, jnp.zeros((LANES, Q_BLOCK), f32)), unroll=True)
    bias = jnp.where(sel.T > 0.5, 0.0, NEG).astype(bf16)
    qaug_sc[...] = jnp.concatenate([jnp.concatenate([bias] * B_GQA, axis=0), q_feat], axis=1)

    m_sc[...] = jnp.full(m_sc.shape, -jnp.inf, f32)
    acc_sc[...] = jnp.zeros(acc_sc.shape, f32)

    def logits(j, s_ref):
        k0 = pl.multiple_of(j * tk, tk)
        s_ref[...] = lax.dot_general(qaug_sc[...], kaug_sc[pl.ds(k0, tk), :], dn, preferred_element_type=f32)

    def update(j, s_ref, diagonal):
        k0 = pl.multiple_of(j * tk, tk)
        s = s_ref[...]
        if diagonal:
            s = jnp.where(k0 + lax.broadcasted_iota(jnp.int32, (1, tk), 1) <= tq, s, -jnp.inf)
        m_old = m_sc[...]
        m_new = jnp.maximum(m_old, jnp.broadcast_to(jnp.max(s, axis=1, keepdims=True), m_old.shape))
        p = jnp.exp(s - jnp.concatenate([m_new] * (tk // LANES), axis=1)).astype(bf16)
        pv = jnp.dot(p, vaug_sc[pl.ds(k0, tk), :], preferred_element_type=f32)
        acc_sc[...] = jnp.exp(m_old - m_new) * acc_sc[...] + pv
        m_sc[...] = m_new

    j_last = t0 // tk
    logits(0, sa_sc)

    def tile_pairs(first, pairs):
        for k in range(pairs):
            j = first + 2 * k
            logits(j + 1, sb_sc)
            update(j, sa_sc, False)
            logits(j + 2, sa_sc)
            update(j + 1, sb_sc, False)

    n_quads = j_last // 4

    def four_tiles(i, carry):
        tile_pairs(4 * i, 2)
        return carry

    lax.fori_loop(0, n_quads, four_tiles, 0)

    @pl.when(j_last - 4 * n_quads >= 2)
    def _pair_left():
        tile_pairs(4 * n_quads, 1)

    odd = (j_last % 2) == 1

    @pl.when(odd)
    def _two_left():
        logits(j_last, sb_sc)
        update(j_last - 1, sa_sc, False)
        update(j_last, sb_sc, True)

    @pl.when(jnp.logical_not(odd))
    def _one_left():
        update(j_last, sa_sc, True)

    acc = acc_sc[...]
    o_s = acc[:, 0:E] / jnp.maximum(acc[:, E:E + 1], TINY)

    for r in range(B_GQA):
        out = gated_cw[r] + gate_s[r] * o_s[r * Q_BLOCK:(r + 1) * Q_BLOCK]
        o_ref[:, r * E:(r + 1) * E] = out.astype(o_ref.dtype)


def _nsa(u_nat, u_f32, cmp_kv, batch, seq):
    nq = seq // Q_BLOCK
    n_chunk = seq // CMP_STRIDE
    n_cmp = n_chunk - CMP_LEN // CMP_STRIDE + 1
    n_slc = seq // SLC_LEN
    n_sel = min(SLC_TOPK, n_slc)
    tk = min(512, seq)
    assert n_slc <= LANES and B_HEADS == 8 and seq % tk == 0
    c_start = np.arange(n_chunk) * CMP_STRIDE
    s_start = np.arange(LANES) * SLC_LEN
    overlap_t = ((c_start[None, :] < s_start[:, None] + SLC_LEN) & (c_start[None, :] + CMP_LEN > s_start[:, None])
                 & (np.arange(n_chunk)[None, :] < n_cmp) & (np.arange(LANES)[:, None] < n_slc))
    pos = np.arange(seq)
    kconst = np.zeros((seq, 2 * LANES), np.float32)
    kconst[pos, pos // SLC_LEN] = 1.0
    kconst[:, LANES + 0] = pos // SLC_LEN
    kconst[:, LANES + 1] = pos % SLC_LEN
    kconst[:, LANES + 2] = 1.0
    kconst[:, LANES + 3] = 1.0
    kconst[:, LANES + N_FEAT] = 1.0
    rows = B_GQA * Q_BLOCK
    slopes = jnp.exp2(-8.0 * jnp.arange(1, B_HEADS + 1, dtype=jnp.float32) / B_HEADS)
    slopes = jnp.repeat(slopes.reshape(B_KV_HEADS, B_GQA), Q_BLOCK, axis=1).reshape(B_KV_HEADS, rows, 1)
    kv_rows = pl.BlockSpec((seq, LANES), lambda b, g, n: (b, NAT_SLC + g))
    win_rows = pl.BlockSpec((seq, LANES), lambda b, g, n: (b, NAT_WIN + g))
    cmp_spec = lambda s: pl.BlockSpec((None, None, None, n_chunk, HEAD_DIM), lambda b, g, n: (s, b, g, 0, 0))
    return pl.pallas_call(
        functools.partial(_nsa_kernel, n_sel=n_sel, tk=tk),
        grid=(batch, B_KV_HEADS, nq),
        in_specs=[pl.BlockSpec((Q_BLOCK, B_GQA * HEAD_DIM), lambda b, g, n: (b * nq + n, NAT_Q_B + g)),
                  cmp_spec(0), cmp_spec(1), kv_rows, win_rows,
                  pl.BlockSpec((Q_BLOCK, LANES), lambda b, g, n: (b * nq + n, F_GATE + g)),
                  pl.BlockSpec((LANES, n_chunk), lambda b, g, n: (0, 0)),
                  pl.BlockSpec((seq, 2 * LANES), lambda b, g, n: (0, 0)),
                  pl.BlockSpec((None, rows, 1), lambda b, g, n: (g, 0, 0))],
        out_specs=pl.BlockSpec((Q_BLOCK, B_GQA * HEAD_DIM), lambda b, g, n: (b * nq + n, g)),
        out_shape=jax.ShapeDtypeStruct((batch * seq, B_WIDTH), jnp.bfloat16),
        scratch_shapes=[pltpu.VMEM((seq, 2 * LANES), jnp.bfloat16), pltpu.VMEM((seq, LANES), jnp.bfloat16),
                        pltpu.VMEM((seq, LANES), jnp.bfloat16), pltpu.VMEM((seq, LANES), jnp.bfloat16),
                        pltpu.VMEM((rows, LANES), jnp.float32), pltpu.VMEM((rows, LANES), jnp.float32),
                        pltpu.VMEM((rows, 2 * LANES), jnp.bfloat16),
                        pltpu.VMEM((rows, tk), jnp.float32), pltpu.VMEM((rows, tk), jnp.float32)],
        compiler_params=_params(("arbitrary", "arbitrary", "arbitrary")),
    )(u_nat, cmp_kv, cmp_kv, u_nat, u_nat, u_f32, jnp.asarray(overlap_t, jnp.bfloat16), jnp.asarray(kconst, jnp.bfloat16),
      slopes)


def _layer_norm(v, gain, bias):
    mu = jnp.mean(v, axis=1, keepdims=True)
    c = v - mu
    var = jnp.mean(c * c, axis=1, keepdims=True)
    return c * lax.rsqrt(var + LN_EPS) * gain + bias


def _sigmoid(v):
    return 1.0 / (1.0 + jnp.exp(-v))


MERGE_SPLIT = 2


def _token_major(ref, stage):
    dil, n = ref.shape[0], ref.shape[1]
    parts = stage.shape[0]
    for r in range(dil):
        for j in range(parts):
            stage[j, pl.ds(r, n, stride=dil), :] = ref[r, :, j * LANES:(j + 1) * LANES]
    return jnp.concatenate([stage[j] for j in range(parts)], axis=1)


def _merge_kernel(*refs):
    dil_refs, refs = refs[:2 * N_DIL], refs[2 * N_DIL:]
    (yb_ref, mga_ref, mgb_ref, x_ref, wa_ref, wb_ref, wo_ref, g_ref, b_ref, wr_hi_ref, wr_lo_ref, br_ref,
     h_ref, ht_ref, rw_ref, re_ref, cnt_ref) = refs[:17]
    stages, cnt_sc = refs[17:17 + 2 * (N_DIL - 1)], refs[17 + 2 * (N_DIL - 1)]
    f32 = jnp.float32
    outs, lse = [dil_refs[0][...]], [dil_refs[1][...]]
    for g in range(1, N_DIL):
        outs.append(_token_major(dil_refs[2 * g], stages[2 * g - 2]))
        lse.append(_token_major(dil_refs[2 * g + 1], stages[2 * g - 1]))
    def mixed_rows(rs):
        lse_r = [v[rs] for v in lse]
        top = jnp.maximum(jnp.maximum(lse_r[0], lse_r[1]), lse_r[2])
        e = [jnp.exp(v - top) for v in lse_r]
        y_a = (e[0] * outs[0][rs] + e[1] * outs[1][rs] + e[2] * outs[2][rs]) / (e[0] + e[1] + e[2])
        br_a = jnp.dot(y_a.astype(jnp.bfloat16), wa_ref[...], preferred_element_type=f32)
        br_b = jnp.dot(yb_ref[rs, :], wb_ref[...], preferred_element_type=f32)
        merged = _sigmoid(mga_ref[rs, :]) * br_a + _sigmoid(mgb_ref[rs, :]) * br_b
        mix = jnp.dot(merged.astype(jnp.bfloat16), wo_ref[...], preferred_element_type=f32)
        return _layer_norm(ALPHA * x_ref[rs, :] + mix, g_ref[...], b_ref[...])

    tm = x_ref.shape[0]
    part = tm // MERGE_SPLIT
    h = jnp.concatenate([mixed_rows(slice(k * part, (k + 1) * part)) for k in range(MERGE_SPLIT)], axis=0)
    h_ref[...] = h
    _to_token_tiles(ht_ref, h)

    h_hi = h.astype(jnp.bfloat16)
    h_lo = (h - h_hi.astype(f32)).astype(jnp.bfloat16)
    logits = (jnp.dot(h_hi, wr_hi_ref[...], preferred_element_type=f32)
              + jnp.dot(h_lo, wr_hi_ref[...], preferred_element_type=f32)
              + jnp.dot(h_hi, wr_lo_ref[...], preferred_element_type=f32)) + br_ref[...]
    lane = lax.broadcasted_iota(jnp.int32, logits.shape, 1)
    coarse = jnp.where(lane < N_GROUPS, logits, -jnp.inf)
    c_max = jnp.max(coarse, axis=1, keepdims=True)
    grp = jnp.min(jnp.where(coarse == c_max, lane, LANES), axis=1, keepdims=True)
    p_grp = 1.0 / jnp.sum(jnp.exp(coarse - c_max), axis=1, keepdims=True)
    lo_lane = N_GROUPS + EXP_PER_GROUP * grp
    fine = jnp.where((lane >= lo_lane) & (lane < lo_lane + EXP_PER_GROUP), logits, -jnp.inf)
    v1 = jnp.max(fine, axis=1, keepdims=True)
    i1 = jnp.min(jnp.where(fine == v1, lane, LANES), axis=1, keepdims=True)
    fine = jnp.where(lane == i1, -jnp.inf, fine)
    v2 = jnp.max(fine, axis=1, keepdims=True)
    i2 = jnp.min(jnp.where(fine == v2, lane, LANES), axis=1, keepdims=True)
    e2 = jnp.exp(v2 - v1)
    w1 = p_grp / (1.0 + e2)
    w2 = p_grp * e2 / (1.0 + e2)
    e_1, e_2 = i1 - N_GROUPS, i2 - N_GROUPS
    rw_ref[...] = jnp.where(lane == 0, w1, jnp.where(lane == 1, w2, 0.0))

    @pl.when(pl.program_id(0) == 0)
    def _zero_counts():
        cnt_sc[...] = jnp.zeros(cnt_sc.shape, f32)

    tm = logits.shape[0]
    onehot = jnp.where((lane == e_1) | (lane == e_2), 1.0, 0.0)
    earlier = (lax.broadcasted_iota(jnp.int32, (tm, tm), 0) > lax.broadcasted_iota(jnp.int32, (tm, tm), 1))
    before = jnp.dot(jnp.where(earlier, 1.0, 0.0).astype(jnp.bfloat16), onehot.astype(jnp.bfloat16),
                     preferred_element_type=f32) + cnt_sc[...]
    r_1 = jnp.sum(jnp.where(lane == e_1, before, 0.0), axis=1, keepdims=True).astype(jnp.int32)
    r_2 = jnp.sum(jnp.where(lane == e_2, before, 0.0), axis=1, keepdims=True).astype(jnp.int32)
    cnt_sc[...] = cnt_sc[...] + jnp.sum(onehot, axis=0, keepdims=True)
    cnt_ref[...] = cnt_sc[...].astype(jnp.int32)
    re_ref[...] = jnp.where(lane == 0, e_1, jnp.where(lane == 1, e_2,
                            jnp.where(lane == 2, r_1, jnp.where(lane == 3, r_2, 0))))


def _merge(dil_outs, y_b, u_f32, x2d, seq, wa, wb, wo, ln_g, ln_b, wr_hi, wr_lo, b_r, tm=256):
    t = x2d.shape[0]
    per_batch = seq // tm
    assert seq % tm == 0 and all(tm % (dil * 8) == 0 for _, dil in DIL_GROUPS)
    row = lambda w: pl.BlockSpec((tm, w), lambda i: (i, 0))
    full = lambda a: pl.BlockSpec(a.shape, lambda i: (0,) * a.ndim)
    dil_specs = [pl.BlockSpec((None, None, tm, A_OUT), lambda i: (i // per_batch, 0, i % per_batch, 0))] * 2
    stages = []
    for _, dil in DIL_GROUPS[1:]:
        dil_specs += [pl.BlockSpec((None, dil, tm // dil, A_OUT),
                                   lambda i: (i // per_batch, 0, i % per_batch, 0))] * 2
        stages += [pltpu.VMEM((A_OUT // LANES, tm, LANES), jnp.float32)] * 2
    return pl.pallas_call(
        _merge_kernel,
        grid=(t // tm,),
        in_specs=dil_specs + [row(B_WIDTH),
                  pl.BlockSpec((tm, D_MODEL), lambda i: (i, F_MGATE)),
                  pl.BlockSpec((tm, D_MODEL), lambda i: (i, F_MGATE + 1)),
                  row(D_MODEL), full(wa), full(wb), full(wo), full(ln_g), full(ln_b),
                  full(wr_hi), full(wr_lo), full(b_r)],
        out_specs=[row(D_MODEL), pl.BlockSpec((tm * CHUNKS, LANES), lambda i: (i, 0)), row(LANES), row(LANES),
                   pl.BlockSpec((1, LANES), lambda i: (0, 0))],
        out_shape=[jax.ShapeDtypeStruct((t, D_MODEL), jnp.float32),
                   jax.ShapeDtypeStruct((t * CHUNKS, LANES), jnp.float32),
                   jax.ShapeDtypeStruct((t, LANES), jnp.float32),
                   jax.ShapeDtypeStruct((t, LANES), jnp.int32),
                   jax.ShapeDtypeStruct((1, LANES), jnp.int32)],
        scratch_shapes=stages + [pltpu.VMEM((1, LANES), jnp.float32)],
        compiler_params=_params(("arbitrary",)),
    )(*dil_outs, y_b, u_f32, u_f32, x2d, wa, wb, wo, ln_g, ln_b, wr_hi, wr_lo, b_r)


CHUNKS = D_MODEL // LANES
GATHER_UNROLL = 8


def _to_token_tiles(ref2d, value):
    n = value.shape[0]
    for c in range(CHUNKS):
        ref2d[pl.ds(c, n, stride=CHUNKS), :] = value[:, c * LANES:(c + 1) * LANES]


def _from_token_tiles(ref2d):
    n = ref2d.shape[0] // CHUNKS
    return jnp.concatenate([ref2d[pl.ds(c, n, stride=CHUNKS), :] for c in range(CHUNKS)], axis=1)


def _token_copy(idx_ref, k, src2d, dst2d, r, sem):
    src_row = pl.multiple_of(idx_ref[0, 0, k] * CHUNKS, CHUNKS)
    dst_row = r * CHUNKS if isinstance(r, int) else pl.multiple_of(r * CHUNKS, CHUNKS)
    return pltpu.make_async_copy(src2d.at[pl.ds(src_row, CHUNKS), :], dst2d.at[pl.ds(dst_row, CHUNKS), :], sem)


def _start_token_gather(idx_ref, idx_of, src2d, dst2d, sem, unrolled, priority_of=lambda r: 0):
    n = dst2d.shape[0] // CHUNKS
    if unrolled:
        for r in range(n):
            _token_copy(idx_ref, idx_of(r), src2d, dst2d, r, sem).start(priority=priority_of(r))
    else:
        def start(r, carry):
            _token_copy(idx_ref, idx_of(r), src2d, dst2d, r, sem).start()
            return carry

        lax.fori_loop(0, n, start, 0, unroll=GATHER_UNROLL)


def _wait_token_gather(src2d, dst2d, sem):
    pltpu.make_async_copy(src2d.at[pl.ds(0, dst2d.shape[0]), :], dst2d, sem).wait()


GATHER_BUFFERS = 4


def _gather_ring(i, first_refs, ahead_ref, start, wait, compute):
    last = pl.num_programs(0) - 1

    @pl.when(i == 0)
    def _prologue():
        for k, ref in enumerate(first_refs):
            start(ref, k, unrolled=False)

    def step(k):
        ahead = (k + GATHER_BUFFERS - 1) % GATHER_BUFFERS
        wait(k)
        start(ahead_ref, ahead, unrolled=True)
        compute(k)

        @pl.when(i == last)
        def _drain():
            for other in range(GATHER_BUFFERS):
                if other != k:
                    wait(other)

    for k in range(GATHER_BUFFERS):
        pl.when(i % GATHER_BUFFERS == k)(functools.partial(step, k))


def _expert_kernel(blk_e_ref, *refs):
    tok_refs, (ht_hbm, wgu_ref, wd_ref, y_ref), scratch = (
        refs[:GATHER_BUFFERS], refs[GATHER_BUFFERS:GATHER_BUFFERS + 4], refs[GATHER_BUFFERS + 4:])
    bufs, sems = scratch[:GATHER_BUFFERS], scratch[GATHER_BUFFERS]

    def start(idx_ref, k, unrolled):
        _start_token_gather(idx_ref, lambda r: r, ht_hbm, bufs[k], sems.at[k], unrolled, priority_of=lambda r: r % 2)

    def wait(k):
        _wait_token_gather(ht_hbm, bufs[k], sems.at[k])

    def compute(k):
        xb = _from_token_tiles(bufs[k]).astype(jnp.bfloat16)
        gu = jnp.dot(xb, wgu_ref[...], preferred_element_type=jnp.float32)
        gate, up = gu[:, :D_EXPERT], gu[:, D_EXPERT:]
        act = (gate * _sigmoid(gate)) * up
        _to_token_tiles(y_ref, jnp.dot(act.astype(jnp.bfloat16), wd_ref[...], preferred_element_type=jnp.float32))

    _gather_ring(pl.program_id(0), tok_refs[:-1], tok_refs[-1], start, wait, compute)


def _experts(h_tiles, tok_buf, blk_e, wgu, wd, bm):
    p = tok_buf.shape[0]
    nb = p // bm
    tok = tok_buf.reshape(nb, 1, bm)
    idx_spec = lambda ahead: pl.BlockSpec((1, 1, bm), lambda i, e: (jnp.minimum(i + ahead, nb - 1), 0, 0),
                                          memory_space=pltpu.SMEM)
    return pl.pallas_call(
        _expert_kernel,
        grid_spec=pltpu.PrefetchScalarGridSpec(
            num_scalar_prefetch=1,
            grid=(nb,),
            in_specs=[idx_spec(ahead) for ahead in range(GATHER_BUFFERS)] + [
                      pl.BlockSpec(memory_space=pl.ANY),
                      pl.BlockSpec((None, D_MODEL, 2 * D_EXPERT), lambda i, e: (e[i], 0, 0)),
                      pl.BlockSpec((None, D_EXPERT, D_MODEL), lambda i, e: (e[i], 0, 0))],
            out_specs=pl.BlockSpec((bm * CHUNKS, LANES), lambda i, e: (i, 0)),
            scratch_shapes=[pltpu.VMEM((bm * CHUNKS, LANES), jnp.float32)] * GATHER_BUFFERS
                           + [pltpu.SemaphoreType.DMA((GATHER_BUFFERS,))],
        ),
        out_shape=jax.ShapeDtypeStruct((p * CHUNKS, LANES), jnp.float32),
        compiler_params=_params(("arbitrary",)),
    )(blk_e, *[tok] * GATHER_BUFFERS, h_tiles, wgu, wd)


def _combine_kernel(*refs):
    dest_refs, (yt_hbm, h_ref, rw_ref, g_ref, b_ref, o_ref), scratch = (
        refs[:GATHER_BUFFERS], refs[GATHER_BUFFERS:GATHER_BUFFERS + 6], refs[GATHER_BUFFERS + 6:])
    bufs, sems = scratch[:GATHER_BUFFERS], scratch[GATHER_BUFFERS]

    def start(idx_ref, k, unrolled):
        for slot in range(MOE_TOP_K):
            _start_token_gather(idx_ref, lambda r, slot=slot: MOE_TOP_K * r + slot, yt_hbm, bufs[k].at[slot],
                                sems.at[k], unrolled, priority_of=lambda r: r % 2)

    def wait(k):
        for slot in range(MOE_TOP_K):
            _wait_token_gather(yt_hbm, bufs[k].at[slot], sems.at[k])

    def compute(k):
        rw = rw_ref[...]
        ffn = rw[:, 0:1] * _from_token_tiles(bufs[k].at[0]) + rw[:, 1:2] * _from_token_tiles(bufs[k].at[1])
        o_ref[...] = _layer_norm(ALPHA * h_ref[...] + ffn, g_ref[...], b_ref[...])

    _gather_ring(pl.program_id(0), dest_refs[:-1], dest_refs[-1], start, wait, compute)


def _combine(dest, y_tiles, h, rw, ln_g, ln_b, tm=256):
    t = h.shape[0]
    assert t % tm == 0
    nt = t // tm
    idx = dest.reshape(nt, 1, MOE_TOP_K * tm)
    idx_spec = lambda ahead: pl.BlockSpec((1, 1, MOE_TOP_K * tm), lambda i: (jnp.minimum(i + ahead, nt - 1), 0, 0),
                                          memory_space=pltpu.SMEM)
    return pl.pallas_call(
        _combine_kernel,
        grid=(nt,),
        in_specs=[idx_spec(ahead) for ahead in range(GATHER_BUFFERS)] + [
                  pl.BlockSpec(memory_space=pl.ANY),
                  pl.BlockSpec((tm, D_MODEL), lambda i: (i, 0)),
                  pl.BlockSpec((tm, LANES), lambda i: (i, 0)),
                  pl.BlockSpec((1, D_MODEL), lambda i: (0, 0)),
                  pl.BlockSpec((1, D_MODEL), lambda i: (0, 0))],
        out_specs=pl.BlockSpec((tm, D_MODEL), lambda i: (i, 0)),
        out_shape=jax.ShapeDtypeStruct((t, D_MODEL), jnp.float32),
        scratch_shapes=[pltpu.VMEM((MOE_TOP_K, tm * CHUNKS, LANES), jnp.float32)] * GATHER_BUFFERS
                       + [pltpu.SemaphoreType.DMA((GATHER_BUFFERS,))],
        compiler_params=_params(("arbitrary",)),
    )(*[idx] * GATHER_BUFFERS, y_tiles, h, rw, ln_g, ln_b)


def _dispatch(route, counts, bm):
    t = route.shape[0]
    n_assign = t * MOE_TOP_K
    expert, rank = route[:, 0:MOE_TOP_K], route[:, MOE_TOP_K:2 * MOE_TOP_K]
    starts = jnp.cumsum(counts) - counts
    pcounts = (counts + bm - 1) // bm * bm
    pends = jnp.cumsum(pcounts)
    pstarts = pends - pcounts
    ids = jnp.arange(N_EXPERTS, dtype=jnp.int32)
    dest = jnp.sum(jnp.where(expert[..., None] == ids, pstarts, 0), axis=-1) + rank
    p = -(-(n_assign + N_EXPERTS * (bm - 1)) // bm) * bm
    nb = p // bm
    blk_e = jnp.minimum(jnp.sum(pends[None, :] <= (jnp.arange(nb, dtype=jnp.int32) * bm)[:, None], axis=1),
                        N_EXPERTS - 1).astype(jnp.int32)
    order = jnp.argsort(expert.reshape(n_assign), stable=True).astype(jnp.int32)
    row_e = jnp.repeat(blk_e, bm)
    k = jnp.arange(p, dtype=jnp.int32) - pstarts[row_e]
    src = jnp.clip(starts[row_e] + k, 0, n_assign - 1)
    tok_buf = jnp.where(k < counts[row_e], order[src] // MOE_TOP_K, 0).astype(jnp.int32)
    return tok_buf, dest.astype(jnp.int32), blk_e


def _split_bf16(w):
    hi = w.astype(jnp.bfloat16)
    return hi, (w - hi.astype(jnp.float32)).astype(jnp.bfloat16)


def kernel(x, w_in, cmp_pos_k, cmp_w1_k, cmp_w2_k, cmp_pos_v, cmp_w1_v, cmp_w2_v, w_branch_a, w_branch_b, w_out, ln1_g, ln1_b, w_coarse, b_coarse, w_fine, b_fine, w_gate_up, w_down, ln2_g, ln2_b):
    batch, seq, d = x.shape
    bf16 = jnp.bfloat16
    h = x.reshape(batch * seq, d)
    cols_bf, cols_f32, scale_bf = _in_proj_columns()
    half = CMP_STRIDE * HEAD_DIM
    for l in range(DEPTH):
        u_nat, *groups = _project_bf(h, (_permute_columns(w_in[l], cols_bf) * scale_bf).astype(bf16), batch, seq)
        u_f32, ckv = _project_f32(h, _permute_columns(w_in[l], cols_f32).astype(bf16), batch, seq)
        dil_outs = _dilated(u_nat, groups, batch, seq)
        pos = jnp.stack([cmp_pos_k[l], cmp_pos_v[l]]).reshape(2, 2, 1, half)
        w1 = jnp.stack([cmp_w1_k[l], cmp_w1_v[l]]).astype(bf16)
        w2 = jnp.stack([cmp_w2_k[l], cmp_w2_v[l]]).astype(bf16)
        cmp_kv = _compress(ckv, pos, w1, w2)
        y_b = _nsa(u_nat, u_f32, cmp_kv, batch, seq)
        w_r = jnp.concatenate([w_coarse[l], w_fine[l].transpose(1, 0, 2).reshape(d, N_EXPERTS)], axis=1)
        w_r = jnp.pad(w_r, ((0, 0), (0, LANES - w_r.shape[1])))
        b_r = jnp.pad(jnp.concatenate([b_coarse[l], b_fine[l].reshape(N_EXPERTS)]),
                      (0, LANES - N_GROUPS - N_EXPERTS)).reshape(1, LANES)
        wr_hi, wr_lo = _split_bf16(w_r)
        h1, h1_tiles, rw, route, counts = _merge(dil_outs, y_b, u_f32, h, seq, w_branch_a[l].astype(bf16),
                                       w_branch_b[l].astype(bf16), w_out[l].astype(bf16),
                                       ln1_g[l].reshape(1, d), ln1_b[l].reshape(1, d), wr_hi, wr_lo, b_r)
        bm = 256
        tok_buf, dest, blk_e = _dispatch(route, counts[0, :N_EXPERTS], bm)
        y_tiles = _experts(h1_tiles, tok_buf, blk_e, w_gate_up[l].astype(bf16), w_down[l].astype(bf16), bm)
        h = _combine(dest, y_tiles, h1, rw, ln2_g[l].reshape(1, d), ln2_b[l].reshape(1, d))
    return h.reshape(batch, seq, d)
```

```python
import functools

import numpy as np
import jax
import jax.numpy as jnp
from jax import lax
from jax.experimental import pallas as pl
from jax.experimental.pallas import tpu as pltpu

D_MODEL = 1024
HEAD_DIM = 64
Q_BLOCK = 128
DIL_GROUPS = ((128, 1), (512, 4), (2048, 16))
N_DIL = len(DIL_GROUPS)
A_HEADS_PER_GROUP = 4
A_HEADS = A_HEADS_PER_GROUP * N_DIL
A_WIDTH = A_HEADS * HEAD_DIM
A_OUT = A_HEADS_PER_GROUP * HEAD_DIM
B_HEADS = 8
B_KV_HEADS = 2
B_GQA = B_HEADS // B_KV_HEADS
B_WIDTH = B_HEADS * HEAD_DIM
B_KV_WIDTH = B_KV_HEADS * HEAD_DIM
CMP_LEN = 32
CMP_STRIDE = 16
CMP_HIDDEN = 256
SLC_LEN = 64
SLC_TOPK = 16
WIN = 512
N_GROUPS = 4
EXP_PER_GROUP = 8
N_EXPERTS = N_GROUPS * EXP_PER_GROUP
D_EXPERT = 512
MOE_TOP_K = 2
DEPTH = 1
ALPHA = (2.0 * DEPTH) ** 0.25
LN_EPS = 1e-5
FORCE_BONUS = 1e4
TINY = 1e-30
ATTN_SCALE = HEAD_DIM ** -0.5

LANES = 128
VMEM_LIMIT = 48 * 1024 * 1024

QKV_COLS = 3 * A_OUT
NAT_Q_B = QKV_COLS // A_OUT
NAT_SLC = (QKV_COLS + B_WIDTH) // LANES
NAT_WIN = NAT_SLC + B_KV_HEADS
NAT_COLS = QKV_COLS + B_WIDTH + 4 * B_KV_WIDTH
BF_COLS = NAT_COLS + (N_DIL - 1) * QKV_COLS
F_MGATE = 0
F_GATE = 2 * D_MODEL // LANES
F_NAT_COLS = 2 * D_MODEL + B_KV_HEADS * LANES
F_COLS = F_NAT_COLS + 2 * B_KV_WIDTH


def _params(semantics):
    return pltpu.CompilerParams(dimension_semantics=semantics, vmem_limit_bytes=VMEM_LIMIT)


def _in_proj_columns():
    kv_off = 3 * A_WIDTH + B_WIDTH
    gate_off = kv_off + 6 * B_KV_WIDTH
    mg_off = gate_off + 3 * B_HEADS
    def qkv_cols(g):
        return [part * A_WIDTH + g * A_OUT + c for part in range(3) for c in range(A_OUT)]

    bf = qkv_cols(0) + list(range(3 * A_WIDTH, 3 * A_WIDTH + B_WIDTH))
    for first in (2, 4):
        for g in range(B_KV_HEADS):
            for i in (first, first + 1):
                base = kv_off + i * B_KV_WIDTH + g * HEAD_DIM
                bf += list(range(base, base + HEAD_DIM))
    for g in range(1, N_DIL):
        bf += qkv_cols(g)
    f32 = list(range(mg_off, mg_off + 2 * D_MODEL))
    per_head = 3 * B_GQA
    for g in range(B_KV_HEADS):
        f32 += list(range(gate_off + g * per_head, gate_off + (g + 1) * per_head))
        f32 += [-1] * (LANES - per_head)
    f32 += list(range(kv_off, kv_off + 2 * B_KV_WIDTH))
    assert len(bf) == BF_COLS and len(f32) == F_COLS
    bf = np.asarray(bf)
    is_q = (bf < A_WIDTH) | ((bf >= 3 * A_WIDTH) & (bf < 3 * A_WIDTH + B_WIDTH))
    scale = np.where(is_q, ATTN_SCALE, 1.0).astype(np.float32)
    return bf, np.asarray(f32), scale


def _permute_columns(w, cols):
    taken = jnp.take(w, jnp.asarray(np.maximum(cols, 0)), axis=1)
    return jnp.where(jnp.asarray(cols >= 0)[None, :], taken, 0.0)


def _proj_f32_kernel(x_ref, w_ref, nat_ref, cmp_ref, stage):
    xb = x_ref[...].astype(jnp.bfloat16)
    tm = x_ref.shape[0]
    n_chunk = tm // CMP_STRIDE
    for c0 in range(0, F_NAT_COLS, A_OUT):
        nat_ref[:, c0:c0 + A_OUT] = jnp.dot(xb, w_ref[:, c0:c0 + A_OUT], preferred_element_type=jnp.float32)
    res = jnp.dot(xb, w_ref[:, F_NAT_COLS:F_COLS], preferred_element_type=jnp.float32)
    lower = lax.broadcasted_iota(jnp.int32, (n_chunk, LANES), 1) < HEAD_DIM
    for kv in range(2):
        stage[...] = res[:, kv * LANES:(kv + 1) * LANES]
        for p in range(0, CMP_STRIDE, 2):
            even = stage[pl.ds(p, n_chunk, stride=CMP_STRIDE), :]
            odd = stage[pl.ds(p + 1, n_chunk, stride=CMP_STRIDE), :]
            cols = slice(p * HEAD_DIM, (p + 2) * HEAD_DIM)
            cmp_ref[kv, 0, :, cols] = jnp.where(lower, even, pltpu.roll(odd, HEAD_DIM, 1))
            cmp_ref[kv, 1, :, cols] = jnp.where(lower, pltpu.roll(even, HEAD_DIM, 1), odd)


def _project_f32(x2d, w, batch, seq, tm=512):
    t, d = x2d.shape
    assert seq % tm == 0 and w.shape[1] == F_COLS and B_KV_HEADS * HEAD_DIM == LANES
    per_batch = seq // tm
    half = CMP_STRIDE * HEAD_DIM
    return pl.pallas_call(
        _proj_f32_kernel,
        grid=(t // tm,),
        in_specs=[pl.BlockSpec((tm, d), lambda i: (i, 0)),
                  pl.BlockSpec((d, F_COLS), lambda i: (0, 0))],
        out_specs=[pl.BlockSpec((tm, F_NAT_COLS), lambda i: (i, 0)),
                   pl.BlockSpec((2, None, B_KV_HEADS, tm // CMP_STRIDE, half),
                                lambda i: (0, i // per_batch, 0, i % per_batch, 0))],
        out_shape=[jax.ShapeDtypeStruct((t, F_NAT_COLS), jnp.float32),
                   jax.ShapeDtypeStruct((2, batch, B_KV_HEADS, seq // CMP_STRIDE, half), jnp.float32)],
        scratch_shapes=[pltpu.VMEM((tm, LANES), jnp.float32)],
        compiler_params=_params(("parallel",)),
    )(x2d, w)


def _proj_bf_kernel(x_ref, w_ref, nat_ref, *rest, dils):
    group_refs, stage = rest[:-1], rest[-1]
    xb = x_ref[...].astype(jnp.bfloat16)
    tm = x_ref.shape[0]
    for c0 in range(0, NAT_COLS, A_OUT):
        nat_ref[:, c0:c0 + A_OUT] = jnp.dot(
            xb, w_ref[:, c0:c0 + A_OUT], preferred_element_type=jnp.float32).astype(nat_ref.dtype)
    col = NAT_COLS
    for o_ref, dil in zip(group_refs, dils):
        for part in range(3):
            res = jnp.dot(xb, w_ref[:, col:col + A_OUT], preferred_element_type=jnp.float32)
            for j in range(A_OUT // LANES):
                stage[j] = res[:, j * LANES:(j + 1) * LANES]
                c0 = part * A_OUT + j * LANES
                for r in range(dil):
                    o_ref[r, :, c0:c0 + LANES] = stage[j, pl.ds(r, tm // dil, stride=dil), :].astype(o_ref.dtype)
            col += A_OUT


def _project_bf(x2d, w, batch, seq, tm=512):
    t, d = x2d.shape
    dils = tuple(dil for _, dil in DIL_GROUPS[1:])
    assert seq % tm == 0 and all(tm % (dil * 16) == 0 for dil in dils) and w.shape[1] == BF_COLS
    per_batch = seq // tm
    out_specs = [pl.BlockSpec((tm, NAT_COLS), lambda i: (i, 0))]
    out_shapes = [jax.ShapeDtypeStruct((t, NAT_COLS), jnp.bfloat16)]
    for dil in dils:
        out_specs.append(pl.BlockSpec((None, dil, tm // dil, QKV_COLS),
                                      lambda i: (i // per_batch, 0, i % per_batch, 0)))
        out_shapes.append(jax.ShapeDtypeStruct((batch, dil, seq // dil, QKV_COLS), jnp.bfloat16))
    return pl.pallas_call(
        functools.partial(_proj_bf_kernel, dils=dils),
        grid=(t // tm,),
        in_specs=[pl.BlockSpec((tm, d), lambda i: (i, 0)),
                  pl.BlockSpec((d, BF_COLS), lambda i: (0, 0))],
        out_specs=out_specs,
        out_shape=out_shapes,
        scratch_shapes=[pltpu.VMEM((A_OUT // LANES, tm, LANES), jnp.float32)],
        compiler_params=_params(("parallel",)),
    )(x2d, w)


def _dilated_kernel(*refs, nbs):
    bias_ref, ins, outs = refs[0], refs[1:1 + 5 * N_DIL], refs[1 + 5 * N_DIL:]
    i = pl.program_id(1)
    dn = (((1,), (1,)), ((), ()))
    f32, bf16 = jnp.float32, jnp.bfloat16
    lower = lax.broadcasted_iota(jnp.int32, (Q_BLOCK, LANES), 1) < HEAD_DIM
    heads = [(g, pair, half) for g in range(N_DIL) for pair in range(A_OUT // LANES) for half in range(2)]

    def col(ref, pair):
        return ref[:, pair * LANES:(pair + 1) * LANES]

    logits = []
    for g, pair, half in heads:
        q_ref, kp_ref, kc_ref = ins[5 * g:5 * g + 3]
        head = g * A_HEADS_PER_GROUP + 2 * pair + half
        q = col(q_ref, pair)
        q = jnp.where(lower if half == 0 else jnp.logical_not(lower), q, jnp.zeros_like(q))
        no_prev = jnp.where((i % nbs[g]) > 0, 0.0, -jnp.inf)
        s_p = lax.dot_general(q, col(kp_ref, pair), dn, preferred_element_type=f32)
        s_c = lax.dot_general(q, col(kc_ref, pair), dn, preferred_element_type=f32)
        logits.append((s_p + bias_ref[head, :, 0:Q_BLOCK] + no_prev, s_c + bias_ref[head, :, Q_BLOCK:2 * Q_BLOCK]))

    probs = []
    for s_p, s_c in logits:
        m = jnp.max(jnp.maximum(s_p, s_c), axis=1, keepdims=True)
        p_p, p_c = jnp.exp(s_p - m), jnp.exp(s_c - m)
        l = jnp.sum(p_p + p_c, axis=1, keepdims=True)
        probs.append((p_p.astype(bf16), p_c.astype(bf16), l, m + jnp.log(l)))

    for n_head in range(0, len(heads), 2):
        g, pair, _ = heads[n_head]
        vp_ref, vc_ref = ins[5 * g + 3:5 * g + 5]
        o_ref, lse_ref = outs[2 * g:2 * g + 2]
        out, lse = [], []
        for p_p, p_c, l, lse_h in probs[n_head:n_head + 2]:
            acc = jnp.dot(p_p, col(vp_ref, pair), preferred_element_type=f32)
            acc += jnp.dot(p_c, col(vc_ref, pair), preferred_element_type=f32)
            out.append(acc / l)
            lse.append(jnp.broadcast_to(lse_h, (Q_BLOCK, LANES)))
        o_ref[:, pair * LANES:(pair + 1) * LANES] = jnp.where(lower, out[0], out[1])
        lse_ref[:, pair * LANES:(pair + 1) * LANES] = jnp.where(lower, lse[0], lse[1])


def _dilated_bias():
    qi = Q_BLOCK + jnp.arange(Q_BLOCK)
    kj = jnp.arange(2 * Q_BLOCK)
    delta = qi[:, None] - kj[None, :]
    slopes = jnp.exp2(-8.0 * jnp.arange(1, A_HEADS + 1, dtype=jnp.float32) / A_HEADS)
    tables = []
    for g, (window, dil) in enumerate(DIL_GROUPS):
        on_band = (delta >= 0) & (delta <= window // dil)
        dist = (delta * dil).astype(jnp.float32)
        for h in range(A_HEADS_PER_GROUP):
            tables.append(jnp.where(on_band, -slopes[g * A_HEADS_PER_GROUP + h] * dist, -jnp.inf))
    return jnp.stack(tables)


def _dilated(u_nat, groups, batch, seq):
    steps = seq // Q_BLOCK
    arrays = [u_nat.reshape(batch, 1, seq, NAT_COLS)] + list(groups)
    bias = _dilated_bias()
    in_arrays, in_specs = [bias], [pl.BlockSpec(bias.shape, lambda b, i: (0, 0, 0))]
    out_specs, out_shapes, nbs = [], [], []
    blk = (None, None, Q_BLOCK, A_OUT)
    for g, (window, dil) in enumerate(DIL_GROUPS):
        length = seq // dil
        assert length % Q_BLOCK == 0 and window // dil == Q_BLOCK
        nb = length // Q_BLOCK
        nbs.append(nb)

        def cur(b, i, nb=nb, col=0):
            return (b, i // nb, i % nb, col)

        def prev(b, i, nb=nb, col=0):
            return (b, i // nb, jnp.maximum(i % nb - 1, 0), col)

        for fn, col in ((cur, 0), (prev, 1), (cur, 1), (prev, 2), (cur, 2)):
            in_arrays.append(arrays[g])
            in_specs.append(pl.BlockSpec(blk, functools.partial(fn, col=col)))
        for _ in range(2):
            out_specs.append(pl.BlockSpec(blk, functools.partial(cur, col=0)))
            out_shapes.append(jax.ShapeDtypeStruct((batch, dil, length, A_OUT), jnp.float32))
    return pl.pallas_call(
        functools.partial(_dilated_kernel, nbs=tuple(nbs)),
        grid=(batch, steps),
        in_specs=in_specs,
        out_specs=out_specs,
        out_shape=out_shapes,
        compiler_params=_params(("parallel", "parallel")),
    )(*in_arrays)


def _compress_kernel(x_ref, pos_ref, w1_ref, w2_ref, o_ref):
    half = CMP_STRIDE * HEAD_DIM
    x = x_ref[...]
    n_chunk = x.shape[0]
    lo = (x + pos_ref[0]).astype(jnp.bfloat16)
    hi = (x + pos_ref[1]).astype(jnp.bfloat16)
    h_lo = jnp.dot(lo, w1_ref[0:half, :], preferred_element_type=jnp.float32)
    h_hi = jnp.dot(hi, w1_ref[half:2 * half, :], preferred_element_type=jnp.float32)
    h = h_lo + pltpu.roll(h_hi, n_chunk - 1, 0)
    act = jax.nn.gelu(h, approximate=True)
    o_ref[...] = jnp.dot(act.astype(jnp.bfloat16), w2_ref[...],
                         preferred_element_type=jnp.float32).astype(o_ref.dtype)


def _compress(ckv, pos, w1, w2):
    _, batch, _, n_chunk, half = ckv.shape
    return pl.pallas_call(
        _compress_kernel,
        grid=(2, batch, B_KV_HEADS),
        in_specs=[pl.BlockSpec((None, None, None, n_chunk, half), lambda s, b, g: (s, b, g, 0, 0)),
                  pl.BlockSpec((None, 2, 1, half), lambda s, b, g: (s, 0, 0, 0)),
                  pl.BlockSpec((None, 2 * half, CMP_HIDDEN), lambda s, b, g: (s, 0, 0)),
                  pl.BlockSpec((None, CMP_HIDDEN, HEAD_DIM), lambda s, b, g: (s, 0, 0))],
        out_specs=pl.BlockSpec((None, None, None, n_chunk, HEAD_DIM), lambda s, b, g: (s, b, g, 0, 0)),
        out_shape=jax.ShapeDtypeStruct((2, batch, B_KV_HEADS, n_chunk, HEAD_DIM), jnp.bfloat16),
        compiler_params=_params(("parallel", "parallel", "parallel")),
    )(ckv, pos, w1, w2)


NEG = -2.0 ** 60
N_FEAT = HEAD_DIM


def _nsa_kernel(q_ref, kc_ref, vc_ref, skv_ref, wkv_ref, gate_ref, ovt_ref, kconst_ref, slope_ref, o_ref,
                kaug_sc, vaug_sc, kwin_sc, vwin_sc, m_sc, acc_sc, qaug_sc, sa_sc, sb_sc, *, n_sel, tk):
    n = pl.program_id(2)
    t0 = n * Q_BLOCK
    rows = B_GQA * Q_BLOCK
    seq = skv_ref.shape[0]
    dn = (((1,), (1,)), ((), ()))
    f32, bf16 = jnp.float32, jnp.bfloat16
    E = HEAD_DIM

    @pl.when(n == 0)
    def _build_keys():
        kaug_sc[:, 0:LANES] = kconst_ref[:, 0:LANES]
        kaug_sc[:, LANES:LANES + E] = skv_ref[:, 0:E]
        kaug_sc[:, LANES + E:2 * LANES] = kconst_ref[:, LANES:LANES + N_FEAT]
        vaug_sc[:, 0:E] = skv_ref[:, E:2 * E]
        vaug_sc[:, E:LANES] = kconst_ref[:, LANES + N_FEAT:2 * LANES]
        kwin_sc[:, 0:E] = wkv_ref[:, 0:E]
        kwin_sc[:, E:LANES] = kconst_ref[:, LANES:LANES + N_FEAT]
        vwin_sc[:, 0:E] = wkv_ref[:, E:2 * E]
        vwin_sc[:, E:LANES] = kconst_ref[:, LANES + N_FEAT:2 * LANES]

    q = q_ref[...]
    qs = jnp.concatenate([q[:, r * E:(r + 1) * E] for r in range(B_GQA)], axis=0)
    row = lax.broadcasted_iota(jnp.int32, (rows, 1), 0)
    within = row & (Q_BLOCK - 1)
    tq = t0 + within
    slope = slope_ref[...]
    lane = lax.broadcasted_iota(jnp.int32, (rows, N_FEAT), 1)
    feat = jnp.where(lane == 0, slope * SLC_LEN,
           jnp.where(lane == 1, slope,
           jnp.where(lane == 2, -slope * t0.astype(f32),
           jnp.where(lane == 3, -slope * within.astype(f32), 0.0))))
    q_feat = jnp.concatenate([qs, feat.astype(bf16)], axis=1)

    n_chunk = kc_ref.shape[0]
    s = lax.dot_general(qs, kc_ref[...], dn, preferred_element_type=f32)
    cmp_end = lax.broadcasted_iota(jnp.int32, (1, n_chunk), 1) * CMP_STRIDE + (CMP_LEN - 1)
    s = jnp.where(cmp_end <= tq, s, -jnp.inf)
    m = jnp.max(s, axis=1, keepdims=True)
    m = jnp.where(m == -jnp.inf, 0.0, m)
    p = jnp.exp(s - m)
    l = jnp.maximum(jnp.sum(p, axis=1, keepdims=True), TINY)
    p_c = (p / l).astype(bf16)
    o_c = jnp.dot(p_c, vc_ref[...], preferred_element_type=f32)

    span = min(WIN + Q_BLOCK, seq)
    start = pl.multiple_of(jnp.maximum(t0 + Q_BLOCK - span, 0), Q_BLOCK)
    s = lax.dot_general(q_feat, kwin_sc[pl.ds(start, span), :], dn, preferred_element_type=f32)
    dw = tq - (start + lax.broadcasted_iota(jnp.int32, (1, span), 1))
    s = jnp.where((dw >= 0) & (dw < WIN), s, -jnp.inf)
    m = jnp.max(s, axis=1, keepdims=True)
    p = jnp.exp(s - m).astype(bf16)
    acc = jnp.dot(p, vwin_sc[pl.ds(start, span), :], preferred_element_type=f32)
    o_w = acc[:, 0:E] / jnp.maximum(acc[:, E:E + 1], TINY)

    gates = 1.0 / (1.0 + jnp.exp(-gate_ref[...]))
    gate_of = lambda r, branch: jnp.broadcast_to(gates[:, 3 * r + branch:3 * r + branch + 1], (Q_BLOCK, E))
    gate_s = [gate_of(r, 1) for r in range(B_GQA)]
    gated_cw = [gate_of(r, 0) * o_c[r * Q_BLOCK:(r + 1) * Q_BLOCK] + gate_of(r, 2) * o_w[r * Q_BLOCK:(r + 1) * Q_BLOCK]
                for r in range(B_GQA)]

    imp = lax.dot_general(ovt_ref[...], p_c[0:Q_BLOCK], dn, preferred_element_type=f32)
    for r in range(1, B_GQA):
        imp += lax.dot_general(ovt_ref[...], p_c[r * Q_BLOCK:(r + 1) * Q_BLOCK], dn, preferred_element_type=f32)
    blk = lax.broadcasted_iota(jnp.int32, (LANES, Q_BLOCK), 0)
    t = t0 + lax.broadcasted_iota(jnp.int32, (LANES, Q_BLOCK), 1)
    cur = t >> 6
    forced = (blk == 0) | (blk == cur) | (blk == cur - 1)
    score = jnp.where(blk * SLC_LEN <= t, imp + jnp.where(forced, FORCE_BONUS, 0.0), -jnp.inf)

    def pick(_, carry):
        work, sel = carry
        best = jnp.max(work, axis=0, keepdims=True)
        idx = jnp.min(jnp.where(work == best, blk, LANES), axis=0, keepdims=True)
        hit = blk == idx
        finite = jnp.where(best > -jnp.inf, 1.0, 0.0)
        sel = jnp.where(hit, jnp.maximum(sel, finite), sel)
        return jnp.where(hit, -jnp.inf, work), sel

    _, sel = lax.fori_loop(0, n_sel, pick, (score, jnp.zeros((LANES, Q_BLOCK), f32)), unroll=True)
    bias = jnp.where(sel.T > 0.5, 0.0, NEG).astype(bf16)
    qaug_sc[...] = jnp.concatenate([jnp.concatenate([bias] * B_GQA, axis=0), q_feat], axis=1)

    m_sc[...] = jnp.full(m_sc.shape, -jnp.inf, f32)
    acc_sc[...] = jnp.zeros(acc_sc.shape, f32)

    def logits(j, s_ref):
        k0 = pl.multiple_of(j * tk, tk)
        s_ref[...] = lax.dot_general(qaug_sc[...], kaug_sc[pl.ds(k0, tk), :], dn, preferred_element_type=f32)

    def update(j, s_ref, diagonal):
        k0 = pl.multiple_of(j * tk, tk)
        s = s_ref[...]
        if diagonal:
            s = jnp.where(k0 + lax.broadcasted_iota(jnp.int32, (1, tk), 1) <= tq, s, -jnp.inf)
        m_old = m_sc[...]
        m_new = jnp.maximum(m_old, jnp.broadcast_to(jnp.max(s, axis=1, keepdims=True), m_old.shape))
        p = jnp.exp(s - jnp.concatenate([m_new] * (tk // LANES), axis=1)).astype(bf16)
        pv = jnp.dot(p, vaug_sc[pl.ds(k0, tk), :], preferred_element_type=f32)
        acc_sc[...] = jnp.exp(m_old - m_new) * acc_sc[...] + pv
        m_sc[...] = m_new

    j_last = t0 // tk
    logits(0, sa_sc)

    def tile_pairs(first, pairs):
        for k in range(pairs):
            j = first + 2 * k
            logits(j + 1, sb_sc)
            update(j, sa_sc, False)
            logits(j + 2, sa_sc)
            update(j + 1, sb_sc, False)

    n_quads = j_last // 4

    def four_tiles(i, carry):
        tile_pairs(4 * i, 2)
        return carry

    lax.fori_loop(0, n_quads, four_tiles, 0)

    @pl.when(j_last - 4 * n_quads >= 2)
    def _pair_left():
        tile_pairs(4 * n_quads, 1)

    odd = (j_last % 2) == 1

    @pl.when(odd)
    def _two_left():
        logits(j_last, sb_sc)
        update(j_last - 1, sa_sc, False)
        update(j_last, sb_sc, True)

    @pl.when(jnp.logical_not(odd))
    def _one_left():
        update(j_last, sa_sc, True)

    acc = acc_sc[...]
    o_s = acc[:, 0:E] / jnp.maximum(acc[:, E:E + 1], TINY)

    for r in range(B_GQA):
        out = gated_cw[r] + gate_s[r] * o_s[r * Q_BLOCK:(r + 1) * Q_BLOCK]
        o_ref[:, r * E:(r + 1) * E] = out.astype(o_ref.dtype)


def _nsa(u_nat, u_f32, cmp_kv, batch, seq):
    nq = seq // Q_BLOCK
    n_chunk = seq // CMP_STRIDE
    n_cmp = n_chunk - CMP_LEN // CMP_STRIDE + 1
    n_slc = seq // SLC_LEN
    n_sel = min(SLC_TOPK, n_slc)
    tk = min(512, seq)
    assert n_slc <= LANES and B_HEADS == 8 and seq % tk == 0
    c_start = np.arange(n_chunk) * CMP_STRIDE
    s_start = np.arange(LANES) * SLC_LEN
    overlap_t = ((c_start[None, :] < s_start[:, None] + SLC_LEN) & (c_start[None, :] + CMP_LEN > s_start[:, None])
                 & (np.arange(n_chunk)[None, :] < n_cmp) & (np.arange(LANES)[:, None] < n_slc))
    pos = np.arange(seq)
    kconst = np.zeros((seq, 2 * LANES), np.float32)
    kconst[pos, pos // SLC_LEN] = 1.0
    kconst[:, LANES + 0] = pos // SLC_LEN
    kconst[:, LANES + 1] = pos % SLC_LEN
    kconst[:, LANES + 2] = 1.0
    kconst[:, LANES + 3] = 1.0
    kconst[:, LANES + N_FEAT] = 1.0
    rows = B_GQA * Q_BLOCK
    slopes = jnp.exp2(-8.0 * jnp.arange(1, B_HEADS + 1, dtype=jnp.float32) / B_HEADS)
    slopes = jnp.repeat(slopes.reshape(B_KV_HEADS, B_GQA), Q_BLOCK, axis=1).reshape(B_KV_HEADS, rows, 1)
    kv_rows = pl.BlockSpec((seq, LANES), lambda b, g, n: (b, NAT_SLC + g))
    win_rows = pl.BlockSpec((seq, LANES), lambda b, g, n: (b, NAT_WIN + g))
    cmp_spec = lambda s: pl.BlockSpec((None, None, None, n_chunk, HEAD_DIM), lambda b, g, n: (s, b, g, 0, 0))
    return pl.pallas_call(
        functools.partial(_nsa_kernel, n_sel=n_sel, tk=tk),
        grid=(batch, B_KV_HEADS, nq),
        in_specs=[pl.BlockSpec((Q_BLOCK, B_GQA * HEAD_DIM), lambda b, g, n: (b * nq + n, NAT_Q_B + g)),
                  cmp_spec(0), cmp_spec(1), kv_rows, win_rows,
                  pl.BlockSpec((Q_BLOCK, LANES), lambda b, g, n: (b * nq + n, F_GATE + g)),
                  pl.BlockSpec((LANES, n_chunk), lambda b, g, n: (0, 0)),
                  pl.BlockSpec((seq, 2 * LANES), lambda b, g, n: (0, 0)),
                  pl.BlockSpec((None, rows, 1), lambda b, g, n: (g, 0, 0))],
        out_specs=pl.BlockSpec((Q_BLOCK, B_GQA * HEAD_DIM), lambda b, g, n: (b * nq + n, g)),
        out_shape=jax.ShapeDtypeStruct((batch * seq, B_WIDTH), jnp.bfloat16),
        scratch_shapes=[pltpu.VMEM((seq, 2 * LANES), jnp.bfloat16), pltpu.VMEM((seq, LANES), jnp.bfloat16),
                        pltpu.VMEM((seq, LANES), jnp.bfloat16), pltpu.VMEM((seq, LANES), jnp.bfloat16),
                        pltpu.VMEM((rows, LANES), jnp.float32), pltpu.VMEM((rows, LANES), jnp.float32),
                        pltpu.VMEM((rows, 2 * LANES), jnp.bfloat16),
                        pltpu.VMEM((rows, tk), jnp.float32), pltpu.VMEM((rows, tk), jnp.float32)],
        compiler_params=_params(("arbitrary", "arbitrary", "arbitrary")),
    )(u_nat, cmp_kv, cmp_kv, u_nat, u_nat, u_f32, jnp.asarray(overlap_t, jnp.bfloat16), jnp.asarray(kconst, jnp.bfloat16),
      slopes)


def _layer_norm(v, gain, bias):
    mu = jnp.mean(v, axis=1, keepdims=True)
    c = v - mu
    var = jnp.mean(c * c, axis=1, keepdims=True)
    return c * lax.rsqrt(var + LN_EPS) * gain + bias


def _sigmoid(v):
    return 1.0 / (1.0 + jnp.exp(-v))


MERGE_SPLIT = 2


def _token_major(ref, stage):
    dil, n = ref.shape[0], ref.shape[1]
    parts = stage.shape[0]
    for r in range(dil):
        for j in range(parts):
            stage[j, pl.ds(r, n, stride=dil), :] = ref[r, :, j * LANES:(j + 1) * LANES]
    return jnp.concatenate([stage[j] for j in range(parts)], axis=1)


def _merge_kernel(*refs):
    dil_refs, refs = refs[:2 * N_DIL], refs[2 * N_DIL:]
    (yb_ref, mga_ref, mgb_ref, x_ref, wa_ref, wb_ref, wo_ref, g_ref, b_ref, wr_hi_ref, wr_lo_ref, br_ref,
     ht_ref, rw_ref, re_ref, cnt_ref) = refs[:16]
    stages, cnt_sc = refs[16:16 + 2 * (N_DIL - 1)], refs[16 + 2 * (N_DIL - 1)]
    f32 = jnp.float32
    outs, lse = [dil_refs[0][...]], [dil_refs[1][...]]
    for g in range(1, N_DIL):
        outs.append(_token_major(dil_refs[2 * g], stages[2 * g - 2]))
        lse.append(_token_major(dil_refs[2 * g + 1], stages[2 * g - 1]))
    def mixed_rows(rs):
        lse_r = [v[rs] for v in lse]
        top = jnp.maximum(jnp.maximum(lse_r[0], lse_r[1]), lse_r[2])
        e = [jnp.exp(v - top) for v in lse_r]
        y_a = (e[0] * outs[0][rs] + e[1] * outs[1][rs] + e[2] * outs[2][rs]) / (e[0] + e[1] + e[2])
        br_a = jnp.dot(y_a.astype(jnp.bfloat16), wa_ref[...], preferred_element_type=f32)
        br_b = jnp.dot(yb_ref[rs, :], wb_ref[...], preferred_element_type=f32)
        merged = _sigmoid(mga_ref[rs, :]) * br_a + _sigmoid(mgb_ref[rs, :]) * br_b
        mix = jnp.dot(merged.astype(jnp.bfloat16), wo_ref[...], preferred_element_type=f32)
        return _layer_norm(ALPHA * x_ref[rs, :] + mix, g_ref[...], b_ref[...])

    tm = x_ref.shape[0]
    part = tm // MERGE_SPLIT
    h = jnp.concatenate([mixed_rows(slice(k * part, (k + 1) * part)) for k in range(MERGE_SPLIT)], axis=0)
    _to_token_tiles(ht_ref, h)

    h_hi = h.astype(jnp.bfloat16)
    h_lo = (h - h_hi.astype(f32)).astype(jnp.bfloat16)
    logits = (jnp.dot(h_hi, wr_hi_ref[...], preferred_element_type=f32)
              + jnp.dot(h_lo, wr_hi_ref[...], preferred_element_type=f32)
              + jnp.dot(h_hi, wr_lo_ref[...], preferred_element_type=f32)) + br_ref[...]
    lane = lax.broadcasted_iota(jnp.int32, logits.shape, 1)
    coarse = jnp.where(lane < N_GROUPS, logits, -jnp.inf)
    c_max = jnp.max(coarse, axis=1, keepdims=True)
    grp = jnp.min(jnp.where(coarse == c_max, lane, LANES), axis=1, keepdims=True)
    p_grp = 1.0 / jnp.sum(jnp.exp(coarse - c_max), axis=1, keepdims=True)
    lo_lane = N_GROUPS + EXP_PER_GROUP * grp
    fine = jnp.where((lane >= lo_lane) & (lane < lo_lane + EXP_PER_GROUP), logits, -jnp.inf)
    v1 = jnp.max(fine, axis=1, keepdims=True)
    i1 = jnp.min(jnp.where(fine == v1, lane, LANES), axis=1, keepdims=True)
    fine = jnp.where(lane == i1, -jnp.inf, fine)
    v2 = jnp.max(fine, axis=1, keepdims=True)
    i2 = jnp.min(jnp.where(fine == v2, lane, LANES), axis=1, keepdims=True)
    e2 = jnp.exp(v2 - v1)
    w1 = p_grp / (1.0 + e2)
    w2 = p_grp * e2 / (1.0 + e2)
    e_1, e_2 = i1 - N_GROUPS, i2 - N_GROUPS
    rw_ref[...] = jnp.where(lane == 0, w1, jnp.where(lane == 1, w2, 0.0))

    @pl.when(pl.program_id(0) == 0)
    def _zero_counts():
        cnt_sc[...] = jnp.zeros(cnt_sc.shape, f32)

    tm = logits.shape[0]
    onehot = jnp.where((lane == e_1) | (lane == e_2), 1.0, 0.0)
    earlier = (lax.broadcasted_iota(jnp.int32, (tm, tm), 0) > lax.broadcasted_iota(jnp.int32, (tm, tm), 1))
    before = jnp.dot(jnp.where(earlier, 1.0, 0.0).astype(jnp.bfloat16), onehot.astype(jnp.bfloat16),
                     preferred_element_type=f32) + cnt_sc[...]
    r_1 = jnp.sum(jnp.where(lane == e_1, before, 0.0), axis=1, keepdims=True).astype(jnp.int32)
    r_2 = jnp.sum(jnp.where(lane == e_2, before, 0.0), axis=1, keepdims=True).astype(jnp.int32)
    cnt_sc[...] = cnt_sc[...] + jnp.sum(onehot, axis=0, keepdims=True)
    cnt_ref[...] = cnt_sc[...].astype(jnp.int32)
    re_ref[...] = jnp.where(lane == 0, e_1, jnp.where(lane == 1, e_2,
                            jnp.where(lane == 2, r_1, jnp.where(lane == 3, r_2, 0))))


def _merge(dil_outs, y_b, u_f32, x2d, seq, wa, wb, wo, ln_g, ln_b, wr_hi, wr_lo, b_r, tm=256):
    t = x2d.shape[0]
    per_batch = seq // tm
    assert seq % tm == 0 and all(tm % (dil * 8) == 0 for _, dil in DIL_GROUPS)
    row = lambda w: pl.BlockSpec((tm, w), lambda i: (i, 0))
    full = lambda a: pl.BlockSpec(a.shape, lambda i: (0,) * a.ndim)
    dil_specs = [pl.BlockSpec((None, None, tm, A_OUT), lambda i: (i // per_batch, 0, i % per_batch, 0))] * 2
    stages = []
    for _, dil in DIL_GROUPS[1:]:
        dil_specs += [pl.BlockSpec((None, dil, tm // dil, A_OUT),
                                   lambda i: (i // per_batch, 0, i % per_batch, 0))] * 2
        stages += [pltpu.VMEM((A_OUT // LANES, tm, LANES), jnp.float32)] * 2
    return pl.pallas_call(
        _merge_kernel,
        grid=(t // tm,),
        in_specs=dil_specs + [row(B_WIDTH),
                  pl.BlockSpec((tm, D_MODEL), lambda i: (i, F_MGATE)),
                  pl.BlockSpec((tm, D_MODEL), lambda i: (i, F_MGATE + 1)),
                  row(D_MODEL), full(wa), full(wb), full(wo), full(ln_g), full(ln_b),
                  full(wr_hi), full(wr_lo), full(b_r)],
        out_specs=[pl.BlockSpec((tm * CHUNKS, LANES), lambda i: (i, 0)), row(LANES), row(LANES),
                   pl.BlockSpec((1, LANES), lambda i: (0, 0))],
        out_shape=[jax.ShapeDtypeStruct((t * CHUNKS, LANES), jnp.float32),
                   jax.ShapeDtypeStruct((t, LANES), jnp.float32),
                   jax.ShapeDtypeStruct((t, LANES), jnp.int32),
                   jax.ShapeDtypeStruct((1, LANES), jnp.int32)],
        scratch_shapes=stages + [pltpu.VMEM((1, LANES), jnp.float32)],
        compiler_params=_params(("arbitrary",)),
    )(*dil_outs, y_b, u_f32, u_f32, x2d, wa, wb, wo, ln_g, ln_b, wr_hi, wr_lo, b_r)


CHUNKS = D_MODEL // LANES
GATHER_UNROLL = 8


def _to_token_tiles(ref2d, value):
    n = value.shape[0]
    for c in range(CHUNKS):
        ref2d[pl.ds(c, n, stride=CHUNKS), :] = value[:, c * LANES:(c + 1) * LANES]


def _from_token_tiles(ref2d):
    n = ref2d.shape[0] // CHUNKS
    return jnp.concatenate([ref2d[pl.ds(c, n, stride=CHUNKS), :] for c in range(CHUNKS)], axis=1)


def _token_copy(idx_ref, k, src2d, dst2d, r, sem):
    src_row = pl.multiple_of(idx_ref[0, 0, k] * CHUNKS, CHUNKS)
    dst_row = r * CHUNKS if isinstance(r, int) else pl.multiple_of(r * CHUNKS, CHUNKS)
    return pltpu.make_async_copy(src2d.at[pl.ds(src_row, CHUNKS), :], dst2d.at[pl.ds(dst_row, CHUNKS), :], sem)


def _start_token_gather(idx_ref, idx_of, src2d, dst2d, sem, unrolled, priority_of=lambda r: 0):
    n = dst2d.shape[0] // CHUNKS
    if unrolled:
        for r in range(n):
            _token_copy(idx_ref, idx_of(r), src2d, dst2d, r, sem).start(priority=priority_of(r))
    else:
        def start(r, carry):
            _token_copy(idx_ref, idx_of(r), src2d, dst2d, r, sem).start()
            return carry

        lax.fori_loop(0, n, start, 0, unroll=GATHER_UNROLL)


def _wait_token_gather(src2d, dst2d, sem):
    pltpu.make_async_copy(src2d.at[pl.ds(0, dst2d.shape[0]), :], dst2d, sem).wait()


GATHER_BUFFERS = 4


def _gather_ring(i, first_refs, ahead_ref, start, wait, compute):
    last = pl.num_programs(0) - 1

    @pl.when(i == 0)
    def _prologue():
        for k, ref in enumerate(first_refs):
            start(ref, k, unrolled=False)

    def step(k):
        ahead = (k + GATHER_BUFFERS - 1) % GATHER_BUFFERS
        wait(k)
        start(ahead_ref, ahead, unrolled=True)
        compute(k)

        @pl.when(i == last)
        def _drain():
            for other in range(GATHER_BUFFERS):
                if other != k:
                    wait(other)

    for k in range(GATHER_BUFFERS):
        pl.when(i % GATHER_BUFFERS == k)(functools.partial(step, k))


def _expert_kernel(blk_e_ref, *refs):
    tok_refs, (ht_hbm, wgu_ref, wd_ref, y_ref), scratch = (
        refs[:GATHER_BUFFERS], refs[GATHER_BUFFERS:GATHER_BUFFERS + 4], refs[GATHER_BUFFERS + 4:])
    bufs, sems = scratch[:GATHER_BUFFERS], scratch[GATHER_BUFFERS]

    def start(idx_ref, k, unrolled):
        _start_token_gather(idx_ref, lambda r: r, ht_hbm, bufs[k], sems.at[k], unrolled, priority_of=lambda r: r % 2)

    def wait(k):
        _wait_token_gather(ht_hbm, bufs[k], sems.at[k])

    def compute(k):
        xb = _from_token_tiles(bufs[k]).astype(jnp.bfloat16)
        gu = jnp.dot(xb, wgu_ref[...], preferred_element_type=jnp.float32)
        gate, up = gu[:, :D_EXPERT], gu[:, D_EXPERT:]
        act = (gate * _sigmoid(gate)) * up
        _to_token_tiles(y_ref, jnp.dot(act.astype(jnp.bfloat16), wd_ref[...], preferred_element_type=jnp.float32))

    _gather_ring(pl.program_id(0), tok_refs[:-1], tok_refs[-1], start, wait, compute)


def _experts(h_tiles, tok_buf, blk_e, wgu, wd, bm):
    p = tok_buf.shape[0]
    nb = p // bm
    tok = tok_buf.reshape(nb, 1, bm)
    idx_spec = lambda ahead: pl.BlockSpec((1, 1, bm), lambda i, e: (jnp.minimum(i + ahead, nb - 1), 0, 0),
                                          memory_space=pltpu.SMEM)
    return pl.pallas_call(
        _expert_kernel,
        grid_spec=pltpu.PrefetchScalarGridSpec(
            num_scalar_prefetch=1,
            grid=(nb,),
            in_specs=[idx_spec(ahead) for ahead in range(GATHER_BUFFERS)] + [
                      pl.BlockSpec(memory_space=pl.ANY),
                      pl.BlockSpec((None, D_MODEL, 2 * D_EXPERT), lambda i, e: (e[i], 0, 0)),
                      pl.BlockSpec((None, D_EXPERT, D_MODEL), lambda i, e: (e[i], 0, 0))],
            out_specs=pl.BlockSpec((bm * CHUNKS, LANES), lambda i, e: (i, 0)),
            scratch_shapes=[pltpu.VMEM((bm * CHUNKS, LANES), jnp.float32)] * GATHER_BUFFERS
                           + [pltpu.SemaphoreType.DMA((GATHER_BUFFERS,))],
        ),
        out_shape=jax.ShapeDtypeStruct((p * CHUNKS, LANES), jnp.float32),
        compiler_params=_params(("arbitrary",)),
    )(blk_e, *[tok] * GATHER_BUFFERS, h_tiles, wgu, wd)


def _combine_kernel(*refs):
    dest_refs, (yt_hbm, h_ref, rw_ref, g_ref, b_ref, o_ref), scratch = (
        refs[:GATHER_BUFFERS], refs[GATHER_BUFFERS:GATHER_BUFFERS + 6], refs[GATHER_BUFFERS + 6:])
    bufs, sems = scratch[:GATHER_BUFFERS], scratch[GATHER_BUFFERS]

    def start(idx_ref, k, unrolled):
        for slot in range(MOE_TOP_K):
            _start_token_gather(idx_ref, lambda r, slot=slot: MOE_TOP_K * r + slot, yt_hbm, bufs[k].at[slot],
                                sems.at[k], unrolled, priority_of=lambda r: r % 2)

    def wait(k):
        for slot in range(MOE_TOP_K):
            _wait_token_gather(yt_hbm, bufs[k].at[slot], sems.at[k])

    def compute(k):
        rw = rw_ref[...]
        ffn = rw[:, 0:1] * _from_token_tiles(bufs[k].at[0]) + rw[:, 1:2] * _from_token_tiles(bufs[k].at[1])
        o_ref[...] = _layer_norm(ALPHA * _from_token_tiles(h_ref) + ffn, g_ref[...], b_ref[...])

    _gather_ring(pl.program_id(0), dest_refs[:-1], dest_refs[-1], start, wait, compute)


def _combine(dest, y_tiles, h_tiles, rw, ln_g, ln_b, tm=256):
    t = h_tiles.shape[0] // CHUNKS
    assert t % tm == 0
    nt = t // tm
    idx = dest.reshape(nt, 1, MOE_TOP_K * tm)
    idx_spec = lambda ahead: pl.BlockSpec((1, 1, MOE_TOP_K * tm), lambda i: (jnp.minimum(i + ahead, nt - 1), 0, 0),
                                          memory_space=pltpu.SMEM)
    return pl.pallas_call(
        _combine_kernel,
        grid=(nt,),
        in_specs=[idx_spec(ahead) for ahead in range(GATHER_BUFFERS)] + [
                  pl.BlockSpec(memory_space=pl.ANY),
                  pl.BlockSpec((tm * CHUNKS, LANES), lambda i: (i, 0)),
                  pl.BlockSpec((tm, LANES), lambda i: (i, 0)),
                  pl.BlockSpec((1, D_MODEL), lambda i: (0, 0)),
                  pl.BlockSpec((1, D_MODEL), lambda i: (0, 0))],
        out_specs=pl.BlockSpec((tm, D_MODEL), lambda i: (i, 0)),
        out_shape=jax.ShapeDtypeStruct((t, D_MODEL), jnp.float32),
        scratch_shapes=[pltpu.VMEM((MOE_TOP_K, tm * CHUNKS, LANES), jnp.float32)] * GATHER_BUFFERS
                       + [pltpu.SemaphoreType.DMA((GATHER_BUFFERS,))],
        compiler_params=_params(("arbitrary",)),
    )(*[idx] * GATHER_BUFFERS, y_tiles, h_tiles, rw, ln_g, ln_b)


def _dispatch(route, counts, bm):
    t = route.shape[0]
    n_assign = t * MOE_TOP_K
    expert, rank = route[:, 0:MOE_TOP_K], route[:, MOE_TOP_K:2 * MOE_TOP_K]
    starts = jnp.cumsum(counts) - counts
    pcounts = (counts + bm - 1) // bm * bm
    pends = jnp.cumsum(pcounts)
    pstarts = pends - pcounts
    ids = jnp.arange(N_EXPERTS, dtype=jnp.int32)
    dest = jnp.sum(jnp.where(expert[..., None] == ids, pstarts, 0), axis=-1) + rank
    p = -(-(n_assign + N_EXPERTS * (bm - 1)) // bm) * bm
    nb = p // bm
    blk_e = jnp.minimum(jnp.sum(pends[None, :] <= (jnp.arange(nb, dtype=jnp.int32) * bm)[:, None], axis=1),
                        N_EXPERTS - 1).astype(jnp.int32)
    order = jnp.argsort(expert.reshape(n_assign), stable=True).astype(jnp.int32)
    row_e = jnp.repeat(blk_e, bm)
    k = jnp.arange(p, dtype=jnp.int32) - pstarts[row_e]
    src = jnp.clip(starts[row_e] + k, 0, n_assign - 1)
    tok_buf = jnp.where(k < counts[row_e], order[src] // MOE_TOP_K, 0).astype(jnp.int32)
    return tok_buf, dest.astype(jnp.int32), blk_e


def _split_bf16(w):
    hi = w.astype(jnp.bfloat16)
    return hi, (w - hi.astype(jnp.float32)).astype(jnp.bfloat16)


def kernel(x, w_in, cmp_pos_k, cmp_w1_k, cmp_w2_k, cmp_pos_v, cmp_w1_v, cmp_w2_v, w_branch_a, w_branch_b, w_out, ln1_g, ln1_b, w_coarse, b_coarse, w_fine, b_fine, w_gate_up, w_down, ln2_g, ln2_b):
    batch, seq, d = x.shape
    bf16 = jnp.bfloat16
    h = x.reshape(batch * seq, d)
    cols_bf, cols_f32, scale_bf = _in_proj_columns()
    half = CMP_STRIDE * HEAD_DIM
    for l in range(DEPTH):
        u_nat, *groups = _project_bf(h, (_permute_columns(w_in[l], cols_bf) * scale_bf).astype(bf16), batch, seq)
        u_f32, ckv = _project_f32(h, _permute_columns(w_in[l], cols_f32).astype(bf16), batch, seq)
        dil_outs = _dilated(u_nat, groups, batch, seq)
        pos = jnp.stack([cmp_pos_k[l], cmp_pos_v[l]]).reshape(2, 2, 1, half)
        w1 = jnp.stack([cmp_w1_k[l], cmp_w1_v[l]]).astype(bf16)
        w2 = jnp.stack([cmp_w2_k[l], cmp_w2_v[l]]).astype(bf16)
        cmp_kv = _compress(ckv, pos, w1, w2)
        y_b = _nsa(u_nat, u_f32, cmp_kv, batch, seq)
        w_r = jnp.concatenate([w_coarse[l], w_fine[l].transpose(1, 0, 2).reshape(d, N_EXPERTS)], axis=1)
        w_r = jnp.pad(w_r, ((0, 0), (0, LANES - w_r.shape[1])))
        b_r = jnp.pad(jnp.concatenate([b_coarse[l], b_fine[l].reshape(N_EXPERTS)]),
                      (0, LANES - N_GROUPS - N_EXPERTS)).reshape(1, LANES)
        wr_hi, wr_lo = _split_bf16(w_r)
        h1_tiles, rw, route, counts = _merge(dil_outs, y_b, u_f32, h, seq, w_branch_a[l].astype(bf16),
                                       w_branch_b[l].astype(bf16), w_out[l].astype(bf16),
                                       ln1_g[l].reshape(1, d), ln1_b[l].reshape(1, d), wr_hi, wr_lo, b_r)
        bm = 256
        tok_buf, dest, blk_e = _dispatch(route, counts[0, :N_EXPERTS], bm)
        y_tiles = _experts(h1_tiles, tok_buf, blk_e, w_gate_up[l].astype(bf16), w_down[l].astype(bf16), bm)
        h = _combine(dest, y_tiles, h1_tiles, rw, ln2_g[l].reshape(1, d), ln2_b[l].reshape(1, d))
    return h.reshape(batch, seq, d)
```
